```python
import math
import numpy as np
import jax
import jax.numpy as jnp
from jax import lax

D_MODEL = 1024
BATCH = 8
SEQ = 2048
DEPTH = 4

MEM_LEN = 256
MIX_W = D_MODEL // 2
NSA_DH = 64
NSA_HEADS = MIX_W // NSA_DH
NSA_KV = 2
NSA_R = NSA_HEADS // NSA_KV
NSA_ROT = NSA_DH // 4
CMP_L = 32
CMP_D = 16
CMP_HID = 256
SEL_L = 64
N_SEL = 8
WINDOW = 256
QB = 128
HB = 4
HK = 128
HV = MIX_W // HB
HGRN_CHUNK = 64
HC = 8
NOPE = 64
ROPE_D = 32
VD = MIX_W // HC
Q_RANK = 384
KV_RANK = 256
XA_HEADS = 4
XA_DH = D_MODEL // XA_HEADS
D_FF = 256 * ((8 * D_MODEL // 3 + 255) // 256)
ROPE_THETA = 500000.0
LN_EPS = 1e-5
RMS_EPS = 1e-6
NEG = -1e30
BIG = 1e9
F_MIN = 1e-20
F32 = jnp.float32

IN_SIZES = (
    NSA_HEADS * NSA_DH,
    NSA_KV * NSA_DH, NSA_KV * NSA_DH,
    NSA_KV * NSA_DH, NSA_KV * NSA_DH,
    NSA_KV * NSA_DH, NSA_KV * NSA_DH,
    NSA_HEADS * 3,
    HB * HK, HB * HK,
    HB * HV, HB * HV,
    Q_RANK, KV_RANK, ROPE_D,
    3 * D_MODEL,
)
N_IN = sum(IN_SIZES)

kernel_name = 'hybrid_nsa_hgrn2_mla_deepnorm_macaron'


def layer_norm(x, g, b):
    xf = x.astype(F32)
    mu = jnp.mean(xf, -1, keepdims=True)
    var = jnp.mean(jnp.square(xf - mu), -1, keepdims=True)
    return ((xf - mu) * lax.rsqrt(var + LN_EPS) * g + b).astype(x.dtype)


def rms_norm(x, g):
    xf = x.astype(F32)
    return (xf * lax.rsqrt(jnp.mean(jnp.square(xf), -1, keepdims=True) + RMS_EPS) * g).astype(x.dtype)


def rope(x, pos, rot_dim):
    half = rot_dim // 2
    inv = ROPE_THETA ** (-jnp.arange(half, dtype=F32) / half)
    ang = pos.astype(F32)[..., None] * inv
    cos = jnp.cos(ang)[:, :, None, :]
    sin = jnp.sin(ang)[:, :, None, :]
    xr = x[..., :rot_dim].astype(F32)
    x1, x2 = xr[..., :half], xr[..., half:]
    rot = jnp.concatenate([x1 * cos - x2 * sin, x2 * cos + x1 * sin], -1).astype(x.dtype)
    return jnp.concatenate([rot, x[..., rot_dim:]], -1)


def masked_softmax(s, mask, axis=-1):
    p = jax.nn.softmax(jnp.where(mask, s, NEG), axis=axis)
    return jnp.where(mask, p, 0.0)


def swiglu(h, w1, w3, w2):
    return (jax.nn.silu(h @ w1) * (h @ w3)) @ w2


def nsa_mixer(q, kc, vc, ks, vs, kw, vw, gates, pos, cmp_pos, cmp_w1, cmp_w2):
    B, S = q.shape[:2]
    G, R, dh = NSA_KV, NSA_R, NSA_DH
    dt = q.dtype
    scale = dh ** -0.5
    t = jnp.arange(S)
    nq = S // QB
    qg = q.reshape(B, S, G, R, dh)
    qr = rope(q, pos, NSA_ROT).reshape(B, S, G, R, dh)
    ks = rope(ks, pos, NSA_ROT)
    kw = rope(kw, pos, NSA_ROT)

    n_cmp = (S - CMP_L) // CMP_D + 1
    start = np.arange(n_cmp) * CMP_D
    blk_idx = start[:, None] + np.arange(CMP_L)[None, :]

    def compress(z, pe, w1, w2):
        blocks = z[:, blk_idx] + pe[None, None, :, None, :]
        flat = blocks.transpose(0, 1, 3, 2, 4).reshape(B, n_cmp, G, CMP_L * dh)
        return jax.nn.gelu(flat @ w1) @ w2

    k_cmp = compress(kc, cmp_pos[0], cmp_w1[0], cmp_w2[0])
    v_cmp = compress(vc, cmp_pos[1], cmp_w1[1], cmp_w2[1])
    s_cmp = jnp.einsum('bsgrd,bngd->bgrsn', qg, k_cmp).astype(F32) * scale
    cmask = jnp.asarray(start + CMP_L - 1)[None, :] <= t[:, None]
    p_cmp = masked_softmax(s_cmp, cmask)
    o_cmp = jnp.einsum('bgrsn,bngd->bsgrd', p_cmp.astype(dt), v_cmp)

    n_slc = S // SEL_L
    j_np = np.arange(n_slc)
    overlap = np.clip(np.minimum(start[:, None] + CMP_L, (j_np[None, :] + 1) * SEL_L)
                      - np.maximum(start[:, None], j_np[None, :] * SEL_L), 0, None) / CMP_L
    imp = jnp.einsum('bgsn,nj->bgsj', p_cmp.sum(2), jnp.asarray(overlap, dtype=F32))
    j = jnp.arange(n_slc)
    cur = (t // SEL_L)[:, None]
    blk_valid = j[None, :] * SEL_L <= t[:, None]
    forced = (j[None, :] == 0) | (j[None, :] == cur) | (j[None, :] == cur - 1)
    score = jnp.where(blk_valid & forced, BIG, jnp.where(blk_valid, imp, -BIG))
    k_top = min(N_SEL, n_slc)
    top_val, top_idx = lax.top_k(score, k_top)
    top_ok = top_val > -0.5 * BIG

    k_blk = ks.reshape(B, n_slc, SEL_L, G, dh).transpose(0, 3, 1, 2, 4)
    v_blk = vs.reshape(B, n_slc, SEL_L, G, dh).transpose(0, 3, 1, 2, 4)
    bi = jnp.arange(B)[:, None, None, None]
    gi = jnp.arange(G)[None, :, None, None]
    q_ch = qr.reshape(B, nq, QB, G, R, dh).transpose(1, 0, 3, 2, 4, 5)
    idx_ch = top_idx.reshape(B, G, nq, QB, k_top).transpose(2, 0, 1, 3, 4)
    ok_ch = top_ok.reshape(B, G, nq, QB, k_top).transpose(2, 0, 1, 3, 4)
    t_ch = t.reshape(nq, QB)

    def sel_block(args):
        qc, ic, okc, tc = args
        kg = k_blk[bi, gi, ic]
        vg = v_blk[bi, gi, ic]
        s = jnp.einsum('bgtrd,bgtkld->bgtrkl', qc, kg).astype(F32) * scale
        kpos = ic[..., None] * SEL_L + jnp.arange(SEL_L)
        m = (kpos <= tc[None, None, :, None, None]) & okc[..., None]
        m = m[:, :, :, None].reshape(B, G, QB, 1, k_top * SEL_L)
        p = masked_softmax(s.reshape(B, G, QB, R, k_top * SEL_L), m).reshape(s.shape)
        return jnp.einsum('bgtrkl,bgtkld->bgtrd', p.astype(dt), vg)

    o_sel = lax.map(sel_block, (q_ch, idx_ch, ok_ch, t_ch))
    o_sel = o_sel.transpose(1, 0, 3, 2, 4, 5).reshape(B, S, G, R, dh)

    nwb = WINDOW // QB

    def band(z):
        zb = jnp.pad(z.reshape(B, nq, QB, G, dh), ((0, 0), (nwb, 0), (0, 0), (0, 0), (0, 0)))
        return jnp.concatenate([zb[:, o:o + nq] for o in range(nwb + 1)], axis=2)

    kwb, vwb = band(kw), band(vw)
    qpos = t.reshape(nq, QB)
    kpos = (jnp.arange(nq)[:, None] - nwb) * QB + jnp.arange((nwb + 1) * QB)[None, :]
    wmask = ((kpos[:, None, :] <= qpos[:, :, None]) & (kpos[:, None, :] > qpos[:, :, None] - WINDOW)
             & (kpos[:, None, :] >= 0))
    s_w = jnp.einsum('bnqgrd,bnkgd->bngrqk', qr.reshape(B, nq, QB, G, R, dh), kwb).astype(F32) * scale
    p_w = masked_softmax(s_w, wmask[None, :, None, None])
    o_win = jnp.einsum('bngrqk,bnkgd->bnqgrd', p_w.astype(dt), vwb).reshape(B, S, G, R, dh)

    gt = jax.nn.sigmoid(gates.astype(F32).reshape(B, S, G, R, 3)).astype(dt)
    o = gt[..., 0:1] * o_cmp + gt[..., 1:2] * o_sel + gt[..., 2:3] * o_win
    return o.reshape(B, S, G * R * dh)


def hgrn2_mixer(q, fz, i_in, g_out, lb, norm_g):
    B, S = q.shape[:2]
    H, dk, dv, C = HB, HK, HV, HGRN_CHUNK
    nc = S // C
    lb = lb.reshape(H, dk).astype(F32)
    zf = fz.reshape(B, S, H, dk).astype(F32)
    f = lb + (1.0 - lb) * jax.nn.sigmoid(zf)
    log_f = jnp.log(jnp.maximum(f, F_MIN))
    k = (1.0 - lb) * jax.nn.sigmoid(-zf)

    def chunks(z, d):
        return z.reshape(B, nc, C, H, d).transpose(1, 0, 3, 2, 4)

    qc = chunks(q.reshape(B, S, H, dk).astype(F32), dk)
    kc = chunks(k, dk)
    vc = chunks(i_in.reshape(B, S, H, dv).astype(F32), dv)
    lfc = chunks(log_f, dk)
    causal = jnp.tril(jnp.ones((C, C), dtype=bool))

    def step(state, inp):
        qt, kt, vt, lf = inp
        b = jnp.cumsum(lf, axis=2)
        inter = jnp.einsum('bhtd,bhde->bhte', qt * jnp.exp(b), state)
        diff = b[:, :, :, None, :] - b[:, :, None, :, :]
        decay = jnp.exp(jnp.where(causal[:, :, None], diff, NEG))
        a = jnp.einsum('bhtd,bhtsd,bhsd->bhts', qt, decay, kt)
        intra = jnp.einsum('bhts,bhse->bhte', a, vt)
        b_last = b[:, :, -1:, :]
        new_state = (jnp.exp(b_last[:, :, 0, :, None]) * state
                     + jnp.einsum('bhsd,bhse->bhde', kt * jnp.exp(b_last - b), vt))
        return new_state, inter + intra

    state0 = jnp.zeros((B, H, dk, dv), F32)
    _, o = lax.scan(step, state0, (qc, kc, vc, lfc))
    o = o.transpose(1, 0, 3, 2, 4).reshape(B, S, H, dv)
    o = rms_norm(o, norm_g) * jax.nn.silu(g_out.reshape(B, S, H, dv).astype(F32))
    return o.reshape(B, S, H * dv).astype(q.dtype)


def mla_mixer(cq, ckv, kr, pos, qn_g, w_uq, kvn_g, w_ukv):
    B, S = cq.shape[:2]
    q = (rms_norm(cq, qn_g) @ w_uq).reshape(B, S, HC, NOPE + ROPE_D)
    q_nope = q[..., :NOPE]
    q_pe = rope(q[..., NOPE:], pos, ROPE_D)
    kv = (rms_norm(ckv, kvn_g) @ w_ukv).reshape(B, S, HC, NOPE + VD)
    k_nope, v = kv[..., :NOPE], kv[..., NOPE:]
    k_pe = rope(kr[:, :, None, :], pos, ROPE_D)[:, :, 0]
    scale = (NOPE + ROPE_D) ** -0.5
    nq = S // QB
    t = jnp.arange(S)
    qn_ch = q_nope.reshape(B, nq, QB, HC, NOPE).transpose(1, 0, 2, 3, 4)
    qp_ch = q_pe.reshape(B, nq, QB, HC, ROPE_D).transpose(1, 0, 2, 3, 4)

    def attn_block(args):
        qn, qp, tc = args
        s = (jnp.einsum('bqhd,bkhd->bhqk', qn, k_nope)
             + jnp.einsum('bqhr,bkr->bhqk', qp, k_pe)).astype(F32) * scale
        p = masked_softmax(s, t[None, :] <= tc[:, None])
        return jnp.einsum('bhqk,bkhd->bqhd', p.astype(v.dtype), v)

    o = lax.map(attn_block, (qn_ch, qp_ch, t.reshape(nq, QB)))
    return o.transpose(1, 0, 2, 3, 4).reshape(B, S, HC * VD)


def memory_cross_attention(h, mem, wq, wk, wv, wo):
    B, S = h.shape[:2]
    q = (h @ wq).reshape(B, S, XA_HEADS, XA_DH)
    k = (mem @ wk).reshape(B, -1, XA_HEADS, XA_DH)
    v = (mem @ wv).reshape(B, -1, XA_HEADS, XA_DH)
    s = jnp.einsum('bshd,bmhd->bhsm', q, k).astype(F32) * (XA_DH ** -0.5)
    p = jax.nn.softmax(s, axis=-1).astype(h.dtype)
    return jnp.einsum('bhsm,bmhd->bshd', p, v).reshape(B, S, D_MODEL) @ wo


def setup_inputs(seed: int = 0) -> dict:
    key = jax.random.key(seed)
    ks = jax.random.split(key, 26)
    beta = (8.0 * DEPTH) ** -0.25

    def w(k, shape, fan_in, s=1.0):
        return jax.random.normal(k, shape, F32) * (s * fan_in ** -0.5)

    def gain(k, shape):
        return 1.0 + 0.02 * jax.random.normal(k, shape, F32)

    return {
        'x': jax.random.normal(ks[0], (BATCH, SEQ, D_MODEL), F32),
        'mem': jax.random.normal(ks[1], (BATCH, MEM_LEN, D_MODEL), F32),
        'positions': jnp.broadcast_to(jnp.arange(SEQ, dtype=jnp.int32), (BATCH, SEQ)),
        'ln_g': gain(ks[2], (DEPTH, 4, D_MODEL)),
        'ln_b': 0.02 * jax.random.normal(ks[3], (DEPTH, 4, D_MODEL), F32),
        'ffn_w1': w(ks[4], (DEPTH, 2, D_MODEL, D_FF), D_MODEL),
        'ffn_w3': w(ks[5], (DEPTH, 2, D_MODEL, D_FF), D_MODEL),
        'ffn_w2': w(ks[6], (DEPTH, 2, D_FF, D_MODEL), D_FF, beta),
        'w_in': w(ks[7], (DEPTH, D_MODEL, N_IN), D_MODEL),
        'nsa_cmp_pos': 0.1 * jax.random.normal(ks[8], (DEPTH, 2, CMP_L, NSA_DH), F32),
        'nsa_cmp_w1': w(ks[9], (DEPTH, 2, CMP_L * NSA_DH, CMP_HID), CMP_L * NSA_DH),
        'nsa_cmp_w2': w(ks[10], (DEPTH, 2, CMP_HID, NSA_DH), CMP_HID),
        'hgrn_lb_logits': 0.1 * jax.random.normal(ks[11], (DEPTH, HB * HK), F32),
        'hgrn_norm_g': gain(ks[12], (DEPTH, HV)),
        'mla_q_norm_g': gain(ks[13], (DEPTH, Q_RANK)),
        'mla_w_uq': w(ks[14], (DEPTH, Q_RANK, HC * (NOPE + ROPE_D)), Q_RANK),
        'mla_kv_norm_g': gain(ks[15], (DEPTH, KV_RANK)),
        'mla_w_ukv': w(ks[16], (DEPTH, KV_RANK, HC * (NOPE + VD)), KV_RANK),
        'w_branch': w(ks[17], (DEPTH, 3, MIX_W, D_MODEL), MIX_W, beta),
        'w_out': w(ks[18], (DEPTH, D_MODEL, D_MODEL), D_MODEL, beta),
        'xa_wq': w(ks[19], (DEPTH, D_MODEL, D_MODEL), D_MODEL),
        'xa_wk': w(ks[20], (DEPTH, D_MODEL, D_MODEL), D_MODEL),
        'xa_wv': w(ks[21], (DEPTH, D_MODEL, D_MODEL), D_MODEL),
        'xa_wo': w(ks[22], (DEPTH, D_MODEL, D_MODEL), D_MODEL, beta),
    }


def reference(x, mem, positions, ln_g, ln_b, ffn_w1, ffn_w3, ffn_w2, w_in, nsa_cmp_pos,
              nsa_cmp_w1, nsa_cmp_w2, hgrn_lb_logits, hgrn_norm_g, mla_q_norm_g, mla_w_uq,
              mla_kv_norm_g, mla_w_ukv, w_branch, w_out, xa_wq, xa_wk, xa_wv, xa_wo):
    B, S, _ = x.shape
    alpha = (2.0 * DEPTH) ** 0.25
    split_points = np.cumsum(IN_SIZES)[:-1].tolist()
    p_lb = jax.nn.softmax(hgrn_lb_logits.astype(F32), axis=0)
    lower_bounds = jnp.cumsum(p_lb, axis=0) - p_lb[0:1]
    for l in range(DEPTH):
        x = layer_norm(alpha * x + 0.5 * swiglu(x, ffn_w1[l, 0], ffn_w3[l, 0], ffn_w2[l, 0]),
                       ln_g[l, 0], ln_b[l, 0])
        (a_q, a_kc, a_vc, a_ks, a_vs, a_kw, a_vw, a_gate,
         b_q, b_f, b_i, b_g, c_q, c_kv, c_kr, merge) = jnp.split(x @ w_in[l], split_points, axis=-1)
        kvs = lambda z: z.reshape(B, S, NSA_KV, NSA_DH)
        y_a = nsa_mixer(a_q.reshape(B, S, NSA_HEADS, NSA_DH), kvs(a_kc), kvs(a_vc), kvs(a_ks),
                        kvs(a_vs), kvs(a_kw), kvs(a_vw), a_gate, positions,
                        nsa_cmp_pos[l], nsa_cmp_w1[l], nsa_cmp_w2[l])
        y_b = hgrn2_mixer(b_q, b_f, b_i, b_g, lower_bounds[l], hgrn_norm_g[l])
        y_c = mla_mixer(c_q, c_kv, c_kr, positions, mla_q_norm_g[l], mla_w_uq[l],
                        mla_kv_norm_g[l], mla_w_ukv[l])
        gates = jax.nn.sigmoid(merge.astype(F32).reshape(B, S, 3, D_MODEL)).astype(x.dtype)
        mixed = (gates[:, :, 0] * (y_a @ w_branch[l, 0]) + gates[:, :, 1] * (y_b @ w_branch[l, 1])
                 + gates[:, :, 2] * (y_c @ w_branch[l, 2]))
        x = layer_norm(alpha * x + mixed @ w_out[l], ln_g[l, 1], ln_b[l, 1])
        x = layer_norm(alpha * x + memory_cross_attention(x, mem, xa_wq[l], xa_wk[l], xa_wv[l], xa_wo[l]),
                       ln_g[l, 2], ln_b[l, 2])
        x = layer_norm(alpha * x + 0.5 * swiglu(x, ffn_w1[l, 1], ffn_w3[l, 1], ffn_w2[l, 1]),
                       ln_g[l, 3], ln_b[l, 3])
    return x
```

```python
import functools
import math

import numpy as np
import jax
import jax.numpy as jnp
from jax import lax
from jax.experimental import pallas as pl
from jax.experimental.pallas import tpu as pltpu

F32 = jnp.float32
BF16 = jnp.bfloat16

D_MODEL = 1024
MIX_W = D_MODEL // 2
NSA_DH = 64
NSA_HEADS = 8
NSA_KV = 2
NSA_R = 4
NSA_ROT = 16
CMP_L = 32
CMP_D = 16
CMP_HID = 256
SEL_L = 64
N_SEL = 8
WINDOW = 256
HB = 4
HK = 128
HV = 128
HGRN_CHUNK = 64
HGRN_SUB = 16
HC = 8
NOPE = 64
ROPE_D = 32
VD = 64
Q_RANK = 384
KV_RANK = 256
XA_HEADS = 4
XA_DH = D_MODEL // XA_HEADS
D_FF = 2816
ROPE_THETA = 500000.0
LN_EPS = 1e-5
RMS_EPS = 1e-6
NEG = -1e30
BIG = 1e9
F_MIN = 1e-20

LANE = 128
VMEM_LIMIT = 56 * 1024 * 1024

IN_SIZES = (512, 128, 128, 128, 128, 128, 128, 24, 512, 512, 512, 512, 384, 256, 32, 3072)
IN_OFF = tuple(int(v) for v in np.concatenate([[0], np.cumsum(IN_SIZES)]))


def _dot(a, b):
    return jnp.dot(a, b, preferred_element_type=F32)


def _dot_nt(a, b):
    return lax.dot_general(a, b, (((1,), (1,)), ((), ())), preferred_element_type=F32)


def _ln(y, g, b):
    mu = jnp.mean(y, axis=-1, keepdims=True)
    yc = y - mu
    var = jnp.mean(yc * yc, axis=-1, keepdims=True)
    return yc * lax.rsqrt(var + LN_EPS) * g + b


def _params(*sem):
    return pltpu.CompilerParams(dimension_semantics=sem, vmem_limit_bytes=VMEM_LIMIT)


def _resident(shape, index_map):
    return pl.BlockSpec(shape, index_map, pipeline_mode=pl.Buffered(1))


FFN_TM = 1024
FFN_TF = 256


def _ffn_kernel(x_ref, w1_ref, w3_ref, w2_ref, g_ref, b_ref, o_ref, acc_ref, xb_ref, *, alpha):
    j = pl.program_id(1)

    @pl.when(j == 0)
    def _():
        acc_ref[...] = jnp.zeros_like(acc_ref)
        xb_ref[...] = x_ref[...].astype(BF16)

    xb = xb_ref[...]
    h1 = _dot(xb, w1_ref[...])
    h3 = _dot(xb, w3_ref[...])
    h = (h1 * jax.nn.sigmoid(h1)) * h3
    acc_ref[...] += _dot(h.astype(BF16), w2_ref[...])

    @pl.when(j == pl.num_programs(1) - 1)
    def _():
        y = alpha * x_ref[...] + 0.5 * acc_ref[...]
        o_ref[...] = _ln(y, g_ref[...], b_ref[...])


def _ffn_ln(x2, w1, w3, w2, lng, lnb, l, which, ln_idx, alpha):
    n = x2.shape[0]
    tm = min(FFN_TM, n)
    grid = (n // tm, D_FF // FFN_TF)
    return pl.pallas_call(
        functools.partial(_ffn_kernel, alpha=alpha),
        grid=grid,
        in_specs=[
            pl.BlockSpec((tm, D_MODEL), lambda m, j: (m, 0)),
            pl.BlockSpec((None, None, D_MODEL, FFN_TF), lambda m, j: (l, which, 0, j)),
            pl.BlockSpec((None, None, D_MODEL, FFN_TF), lambda m, j: (l, which, 0, j)),
            pl.BlockSpec((None, None, FFN_TF, D_MODEL), lambda m, j: (l, which, j, 0)),
            pl.BlockSpec((None, None, 1, D_MODEL), lambda m, j: (l, ln_idx, 0, 0)),
            pl.BlockSpec((None, None, 1, D_MODEL), lambda m, j: (l, ln_idx, 0, 0)),
        ],
        out_specs=pl.BlockSpec((tm, D_MODEL), lambda m, j: (m, 0)),
        out_shape=jax.ShapeDtypeStruct((n, D_MODEL), F32),
        scratch_shapes=[pltpu.VMEM((tm, D_MODEL), F32), pltpu.VMEM((tm, D_MODEL), BF16)],
        compiler_params=_params("parallel", "arbitrary"),
        name="ffn_ln",
    )(x2, w1, w3, w2, lng, lnb)


def _proj_kernel(x_ref, w_ref, o_ref):
    o_ref[...] = _dot(x_ref[...].astype(BF16), w_ref[...]).astype(o_ref.dtype)


def _proj(x2, w, out_dtype, tm=512, tn=512):
    n, k = x2.shape
    c = w.shape[1]
    tm = min(tm, n)
    tn = min(tn, c)
    return pl.pallas_call(
        _proj_kernel,
        grid=(n // tm, c // tn),
        in_specs=[pl.BlockSpec((tm, k), lambda m, j: (m, 0)),
                  pl.BlockSpec((k, tn), lambda m, j: (0, j))],
        out_specs=pl.BlockSpec((tm, tn), lambda m, j: (m, j)),
        out_shape=jax.ShapeDtypeStruct((n, c), out_dtype),
        compiler_params=_params("parallel", "arbitrary"),
        name="proj",
    )(x2, w)


NSA_QW = NSA_HEADS * LANE
NSA_KW = NSA_KV * LANE
A_Q, A_QROT = 0, NSA_QW
A_KC = 2 * NSA_QW
A_VC = A_KC + NSA_KW
A_KS = A_VC + NSA_KW
A_KSROT = A_KS + NSA_KW
A_VS = A_KSROT + NSA_KW
A_KWN = A_VS + NSA_KW
A_KWROT = A_KWN + NSA_KW
A_VW = A_KWROT + NSA_KW
A_GATE = A_VW + NSA_KW
A_COLS = A_GATE + NSA_KW
PROJA_TM = 512


def _proja_kernel(x_ref, w_ref, cos_ref, sin_ref,
                  q_ref, qr_ref, kc_ref, vc_ref, ks_ref, vs_ref, kw_ref, vw_ref, g_ref):
    xb = x_ref[...].astype(BF16)
    cos = cos_ref[...]
    sin = sin_ref[...]
    cos2 = jnp.concatenate([cos, cos], axis=1)
    sin2 = jnp.concatenate([sin, sin], axis=1)

    def mm(c0, width):
        return _dot(xb, w_ref[:, c0:c0 + width])

    for p in range(NSA_QW // NSA_KW):
        c = p * NSA_KW
        y = mm(A_Q + c, NSA_KW)
        yr = mm(A_QROT + c, NSA_KW)
        q_ref[:, c:c + NSA_KW] = y.astype(BF16)
        qr_ref[:, c:c + NSA_KW] = (y * cos2 + yr * sin2).astype(BF16)

    def put(ref, y):
        for g in range(NSA_KV):
            ref[g] = y[:, g * LANE:(g + 1) * LANE].astype(BF16)

    put(kc_ref, mm(A_KC, NSA_KW))
    put(vc_ref, mm(A_VC, NSA_KW))
    put(ks_ref, mm(A_KS, NSA_KW) * cos2 + mm(A_KSROT, NSA_KW) * sin2)
    put(vs_ref, mm(A_VS, NSA_KW))
    put(kw_ref, mm(A_KWN, NSA_KW) * cos2 + mm(A_KWROT, NSA_KW) * sin2)
    put(vw_ref, mm(A_VW, NSA_KW))
    g_ref[...] = mm(A_GATE, NSA_KW)


def _proja(x3, wa, cos_a, sin_a):
    b, s, _ = x3.shape
    tm = PROJA_TM
    kv_shape = jax.ShapeDtypeStruct((b, NSA_KV, s, LANE), BF16)
    kv_spec = pl.BlockSpec((None, NSA_KV, tm, LANE), lambda bi, m: (bi, 0, m, 0))
    return pl.pallas_call(
        _proja_kernel,
        grid=(b, s // tm),
        in_specs=[
            pl.BlockSpec((None, tm, D_MODEL), lambda bi, m: (bi, m, 0)),
            _resident((D_MODEL, A_COLS), lambda bi, m: (0, 0)),
            pl.BlockSpec((None, tm, LANE), lambda bi, m: (bi, m, 0)),
            pl.BlockSpec((None, tm, LANE), lambda bi, m: (bi, m, 0)),
        ],
        out_specs=[
            pl.BlockSpec((None, tm, NSA_QW), lambda bi, m: (bi, m, 0)),
            pl.BlockSpec((None, tm, NSA_QW), lambda bi, m: (bi, m, 0)),
            kv_spec, kv_spec, kv_spec, kv_spec, kv_spec, kv_spec,
            pl.BlockSpec((None, tm, NSA_KW), lambda bi, m: (bi, m, 0)),
        ],
        out_shape=[
            jax.ShapeDtypeStruct((b, s, NSA_QW), BF16),
            jax.ShapeDtypeStruct((b, s, NSA_QW), BF16),
            kv_shape, kv_shape, kv_shape, kv_shape, kv_shape, kv_shape,
            jax.ShapeDtypeStruct((b, s, NSA_KW), F32),
        ],
        compiler_params=_params("parallel", "parallel"),
        name="nsa_proj",
    )(x3, wa, cos_a, sin_a)


N_CMP = 128
CMP_HALF = CMP_D * LANE


def _gelu_tanh(x):
    return 0.5 * x * (1.0 + jnp.tanh(0.7978845608028654 * (x + 0.044715 * x * x * x)))


def _cmp_kernel(zk_ref, zv_ref, w1_ref, pe_ref, w2_ref, shift_ref, ok_ref, ov_ref):
    shift = shift_ref[...]
    for which, (z_ref, o_ref) in enumerate(((zk_ref, ok_ref), (zv_ref, ov_ref))):
        z = z_ref[...]
        zn = _dot(shift, z).astype(BF16)
        bias = _dot(pe_ref[which], w1_ref[which])[0:1]
        pre = (_dot(z, w1_ref[which, 0:CMP_HALF, :])
               + _dot(zn, w1_ref[which, CMP_HALF:2 * CMP_HALF, :]) + bias)
        h = _gelu_tanh(pre)
        o_ref[...] = _dot(h.astype(BF16), w2_ref[which]).astype(BF16)


def _cmp(zk, zv, w1p, pe8, w2p, shift):
    b = zk.shape[0]
    z_spec = pl.BlockSpec((None, None, N_CMP, CMP_HALF), lambda bi, g: (bi, g, 0, 0))
    o_spec = pl.BlockSpec((None, None, N_CMP, LANE), lambda bi, g: (bi, g, 0, 0))
    o_shape = jax.ShapeDtypeStruct((b, NSA_KV, N_CMP, LANE), BF16)
    return pl.pallas_call(
        _cmp_kernel,
        grid=(b, NSA_KV),
        in_specs=[
            z_spec, z_spec,
            _resident((2, 2 * CMP_HALF, CMP_HID), lambda bi, g: (0, 0, 0)),
            _resident((2, 8, 2 * CMP_HALF), lambda bi, g: (0, 0, 0)),
            _resident((2, CMP_HID, LANE), lambda bi, g: (0, 0, 0)),
            _resident((N_CMP, N_CMP), lambda bi, g: (0, 0)),
        ],
        out_specs=[o_spec, o_spec],
        out_shape=[o_shape, o_shape],
        compiler_params=_params("parallel", "parallel"),
        name="nsa_cmp",
    )(zk, zv, w1p, pe8, w2p, shift)


NSA_TQ = 128
NSA_TK = 256
N_SLC = 32
WIN_SLAB = WINDOW + NSA_TQ


def _nsa_kernel(q_ref, qr_ref, kc_ref, vc_ref, ks_ref, vs_ref, kw_ref, vw_ref, gt_ref, ovl_ref,
                o_ref):
    tq, tk, r_heads = NSA_TQ, NSA_TK, NSA_R
    i = pl.program_id(2)
    t0 = i * tq
    tpos = t0 + lax.broadcasted_iota(jnp.int32, (tq, 1), 0)
    lane = lax.broadcasted_iota(jnp.int32, (1, LANE), 1)
    scale = NSA_DH ** -0.5

    cmask = (lane * CMP_D + (CMP_L - 1) <= tpos) & (lane < N_CMP - 1)
    kc = kc_ref[...]
    vc = vc_ref[...]
    psum = jnp.zeros((tq, LANE), F32)
    o_cmp = []
    for r in range(r_heads):
        q = q_ref[:, r * LANE:(r + 1) * LANE]
        s = jnp.where(cmask, _dot_nt(q, kc) * scale, NEG)
        m = jnp.max(s, axis=-1, keepdims=True)
        e = jnp.where(cmask, jnp.exp(s - m), 0.0)
        den = jnp.sum(e, axis=-1, keepdims=True)
        p = e / jnp.where(den > 0.0, den, 1.0)
        psum = psum + p
        o_cmp.append(_dot(p.astype(BF16), vc))

    imp = jnp.dot(psum, ovl_ref[...], precision=lax.Precision.HIGHEST, preferred_element_type=F32)
    cur = tpos // SEL_L
    valid = (lane * SEL_L <= tpos) & (lane < N_SLC)
    forced = (lane == 0) | (lane == cur) | (lane == cur - 1)
    score = jnp.where(valid & forced, BIG, jnp.where(valid, imp, -BIG))
    score = jnp.where(lane < N_SLC, score, -4.0 * BIG)
    rank = jnp.zeros((tq, LANE), F32)
    for j in range(N_SLC):
        sj = score[:, j:j + 1]
        rank = rank + jnp.where((sj > score) | ((sj == score) & (j < lane)), 1.0, 0.0)
    sel = jnp.where((rank < N_SEL) & valid, 1.0, 0.0).astype(BF16)

    q4 = jnp.concatenate([qr_ref[:, r * LANE:(r + 1) * LANE] for r in range(r_heads)], axis=0)

    def sel_body(c, carry):
        m, den, acc = carry
        k0 = pl.multiple_of(c * tk, tk)
        k = ks_ref[pl.ds(k0, tk), :]
        v = vs_ref[pl.ds(k0, tk), :]
        kpos = k0 + lax.broadcasted_iota(jnp.int32, (1, tk), 1)
        blk = (k0 + lax.broadcasted_iota(jnp.int32, (LANE, tk), 1)) // SEL_L
        expand = jnp.where(blk == lax.broadcasted_iota(jnp.int32, (LANE, tk), 0), 1.0, 0.0)
        chosen = _dot(sel, expand.astype(BF16)) > 0.5
        mask = (chosen & (kpos <= tpos))[None]
        s = (_dot_nt(q4, k) * scale).reshape(r_heads, tq, tk)
        s = jnp.where(mask, s, NEG)
        m_new = jnp.maximum(m, jnp.max(s, axis=-1, keepdims=True))
        a = jnp.exp(m - m_new)
        p = jnp.where(mask, jnp.exp(s - m_new), 0.0)
        den = a * den + jnp.sum(p, axis=-1, keepdims=True)
        pv = _dot(p.reshape(r_heads * tq, tk).astype(BF16), v).reshape(r_heads, tq, LANE)
        return m_new, den, a * acc + pv

    nk = (t0 + tq + tk - 1) // tk
    init = (jnp.full((r_heads, tq, 1), NEG, F32), jnp.zeros((r_heads, tq, 1), F32),
            jnp.zeros((r_heads, tq, LANE), F32))
    _, den_s, acc_s = lax.fori_loop(0, nk, sel_body, init)
    o_sel = acc_s / den_s

    w0 = pl.multiple_of(jnp.maximum(t0 - WINDOW, 0), LANE)
    kw = kw_ref[pl.ds(w0, WIN_SLAB), :]
    vw = vw_ref[pl.ds(w0, WIN_SLAB), :]
    kpos = w0 + lax.broadcasted_iota(jnp.int32, (1, WIN_SLAB), 1)
    wmask = ((kpos <= tpos) & (kpos > tpos - WINDOW))[None]
    s = (_dot_nt(q4, kw) * scale).reshape(r_heads, tq, WIN_SLAB)
    s = jnp.where(wmask, s, NEG)
    m = jnp.max(s, axis=-1, keepdims=True)
    p = jnp.where(wmask, jnp.exp(s - m), 0.0)
    den_w = jnp.sum(p, axis=-1, keepdims=True)
    o_win = _dot(p.reshape(r_heads * tq, WIN_SLAB).astype(BF16), vw).reshape(r_heads, tq, LANE)
    o_win = o_win / den_w

    gt = jax.nn.sigmoid(gt_ref[...])
    for r in range(r_heads):
        g0 = gt[:, 3 * r:3 * r + 1]
        g1 = gt[:, 3 * r + 1:3 * r + 2]
        g2 = gt[:, 3 * r + 2:3 * r + 3]
        o = g0 * o_cmp[r] + g1 * o_sel[r] + g2 * o_win[r]
        o_ref[:, r * LANE:(r + 1) * LANE] = o.astype(BF16)


def _nsa_attn(q, qr, kcmp, vcmp, ks, vs, kw, vw, gates, ovl):
    b, s, _ = q.shape
    tq = NSA_TQ
    gw = NSA_R * LANE
    q_spec = pl.BlockSpec((None, tq, gw), lambda bi, g, i: (bi, i, g))
    c_spec = pl.BlockSpec((None, None, N_CMP, LANE), lambda bi, g, i: (bi, g, 0, 0))
    kv_spec = pl.BlockSpec((None, None, s, LANE), lambda bi, g, i: (bi, g, 0, 0))
    return pl.pallas_call(
        _nsa_kernel,
        grid=(b, NSA_KV, s // tq),
        in_specs=[q_spec, q_spec, c_spec, c_spec, kv_spec, kv_spec, kv_spec, kv_spec,
                  pl.BlockSpec((None, tq, LANE), lambda bi, g, i: (bi, i, g)),
                  _resident((N_CMP, LANE), lambda bi, g, i: (0, 0))],
        out_specs=pl.BlockSpec((None, tq, gw), lambda bi, g, i: (bi, i, g)),
        out_shape=jax.ShapeDtypeStruct((b, s, NSA_QW), BF16),
        compiler_params=_params("parallel", "parallel", "arbitrary"),
        name="nsa_attn",
    )(q, qr, kcmp, vcmp, ks, vs, kw, vw, gates, ovl)


def _hgrn_kernel(q_ref, z_ref, v_ref, go_ref, lb_ref, ng_ref, o_ref, st_ref):
    s_len = q_ref.shape[0]
    c_len, sub_len = HGRN_CHUNK, HGRN_SUB
    nc = s_len // c_len
    n_sub = c_len // sub_len
    q = q_ref[...]
    z = z_ref[...]
    v = v_ref[...]
    lb = lb_ref[...]
    t = lax.broadcasted_iota(jnp.int32, (s_len, 1), 0)
    t_sub = t % sub_len
    t_chunk = t % c_len

    f = lb + (1.0 - lb) * jax.nn.sigmoid(z)
    lf = jnp.log(jnp.maximum(f, F_MIN))
    k = (1.0 - lb) * jax.nn.sigmoid(-z)

    bs = lf
    step = 1
    while step < sub_len:
        bs = bs + jnp.where(t_sub >= step, pltpu.roll(bs, step, axis=0), 0.0)
        step *= 2
    sub_id = t_chunk // sub_len
    sub_id3 = lax.broadcasted_iota(jnp.int32, (nc, c_len, 1), 1) // sub_len
    bs3 = bs.reshape(nc, c_len, HK)
    b3 = bs3
    for i in range(1, n_sub):
        total = bs3[:, sub_len * i - 1:sub_len * i, :]
        b3 = b3 + jnp.where(sub_id3 >= i, total, 0.0)
    b = b3.reshape(s_len, HK)
    k3 = k.reshape(nc, c_len, HK)
    vb3 = v.astype(BF16).reshape(nc, c_len, HV)

    lq = q * jnp.exp(bs)
    lhs, rhs = [], []
    for i in range(1, n_sub):
        ref_b = b3[:, sub_len * i - 1:sub_len * i, :]
        e = jnp.exp(jnp.minimum(ref_b - b3, 0.0)).reshape(s_len, HK)
        rhs.append(jnp.where(sub_id < i, k * e, 0.0).astype(BF16))
        lhs.append(jnp.where(sub_id == i, lq, 0.0).astype(BF16))
    width = (n_sub - 1) * HK
    lhs3 = jnp.concatenate(lhs, axis=1).reshape(nc, c_len, width)
    rhs3 = jnp.concatenate(rhs, axis=1).reshape(nc, c_len, width)
    a_off = jnp.einsum('ctk,csk->cts', lhs3, rhs3, preferred_element_type=F32)
    o = jnp.einsum('cts,csd->ctd', a_off.astype(BF16), vb3,
                   preferred_element_type=F32).reshape(s_len, HV)

    o = o + jnp.sum(q * k, axis=-1, keepdims=True) * v
    for d in range(1, sub_len):
        e = jnp.exp(jnp.where(t_sub >= d, bs - pltpu.roll(bs, d, axis=0), NEG))
        w = jnp.sum(q * pltpu.roll(k, d, axis=0) * e, axis=-1, keepdims=True)
        o = o + w * pltpu.roll(v, d, axis=0)

    b_last = b3[:, c_len - 1:c_len, :]
    kd3 = (k3 * jnp.exp(b_last - b3)).astype(BF16)
    upd = jnp.einsum('csv,csk->cvk', vb3, kd3, preferred_element_type=F32)
    dec = jnp.exp(b_last)
    state = jnp.zeros((HV, HK), F32)
    for c in range(nc):
        st_ref[c] = state.astype(BF16)
        state = state * dec[c] + upd[c]
    qe3 = (q * jnp.exp(b)).astype(BF16).reshape(nc, c_len, HK)
    o = o + jnp.einsum('ctk,cvk->ctv', qe3, st_ref[...],
                       preferred_element_type=F32).reshape(s_len, HV)

    o = o * lax.rsqrt(jnp.mean(o * o, axis=-1, keepdims=True) + RMS_EPS) * ng_ref[...]
    go = go_ref[...]
    o_ref[...] = (o * (go * jax.nn.sigmoid(go))).astype(BF16)


def _hgrn(pb, lb, ng):
    b, s, _ = pb.shape

    def col(part):
        return pl.BlockSpec((None, s, HK), lambda bi, h: (bi, 0, part * HB + h))

    return pl.pallas_call(
        _hgrn_kernel,
        grid=(b, HB),
        in_specs=[col(0), col(1), col(2), col(3),
                  pl.BlockSpec((None, 1, HK), lambda bi, h: (h, 0, 0)),
                  pl.BlockSpec((1, HV), lambda bi, h: (0, 0))],
        out_specs=pl.BlockSpec((None, s, HV), lambda bi, h: (bi, 0, h)),
        out_shape=jax.ShapeDtypeStruct((b, s, HB * HV), BF16),
        scratch_shapes=[pltpu.VMEM((s // HGRN_CHUNK, HV, HK), BF16)],
        compiler_params=_params("parallel", "parallel"),
        name="hgrn",
    )(pb, pb, pb, pb, lb, ng)


MLA_W = HC * LANE
C_CQ, C_CKV, C_KR, C_KRROT = 0, Q_RANK, Q_RANK + KV_RANK, Q_RANK + KV_RANK + LANE
C_COLS = C_KRROT + LANE
MLAP_TM = 512


def _rms(x, g):
    return x * lax.rsqrt(jnp.mean(x * x, axis=-1, keepdims=True) + RMS_EPS) * g


def _mlap_kernel(x_ref, wc_ref, qg_ref, kvg_ref, wuq_ref, wuqr_ref, wuk_ref, wuv_ref,
                 cos_ref, sin_ref, q_ref, k_ref, v_ref):
    xb = x_ref[...].astype(BF16)
    c = _dot(xb, wc_ref[...])
    cos = cos_ref[...]
    sin = sin_ref[...]
    nq = _rms(c[:, C_CQ:C_CKV], qg_ref[...]).astype(BF16)
    nkv = _rms(c[:, C_CKV:C_KR], kvg_ref[...]).astype(BF16)
    k_pe = c[:, C_KR:C_KRROT] * cos + c[:, C_KRROT:C_COLS] * sin
    for h in range(HC):
        hs = slice(h * LANE, (h + 1) * LANE)
        q_ref[:, hs] = (_dot(nq, wuq_ref[:, hs]) * cos + _dot(nq, wuqr_ref[:, hs]) * sin).astype(BF16)
        k_ref[:, hs] = (_dot(nkv, wuk_ref[:, hs]) + k_pe).astype(BF16)
        v_ref[:, hs] = _dot(nkv, wuv_ref[:, hs]).astype(BF16)


def _mlap(x3, wc, qg, kvg, wuq, wuqr, wuk, wuv, cos_c, sin_c):
    b, s, _ = x3.shape
    tm = MLAP_TM
    o_spec = pl.BlockSpec((None, tm, MLA_W), lambda bi, m: (bi, m, 0))
    o_shape = jax.ShapeDtypeStruct((b, s, MLA_W), BF16)
    t_spec = pl.BlockSpec((None, tm, LANE), lambda bi, m: (bi, m, 0))
    return pl.pallas_call(
        _mlap_kernel,
        grid=(b, s // tm),
        in_specs=[
            pl.BlockSpec((None, tm, D_MODEL), lambda bi, m: (bi, m, 0)),
            _resident((D_MODEL, C_COLS), lambda bi, m: (0, 0)),
            _resident((1, Q_RANK), lambda bi, m: (0, 0)),
            _resident((1, KV_RANK), lambda bi, m: (0, 0)),
            _resident((Q_RANK, MLA_W), lambda bi, m: (0, 0)),
            _resident((Q_RANK, MLA_W), lambda bi, m: (0, 0)),
            _resident((KV_RANK, MLA_W), lambda bi, m: (0, 0)),
            _resident((KV_RANK, MLA_W), lambda bi, m: (0, 0)),
            t_spec, t_spec,
        ],
        out_specs=[o_spec, o_spec, o_spec],
        out_shape=[o_shape, o_shape, o_shape],
        compiler_params=_params("parallel", "parallel"),
        name="mla_proj",
    )(x3, wc, qg, kvg, wuq, wuqr, wuk, wuv, cos_c, sin_c)


MLA_TQ = 256
MLA_TK = 256


def _mla_attn_kernel(q_ref, k_ref, v_ref, o_ref):
    tq, tk = MLA_TQ, MLA_TK
    i = pl.program_id(2)
    t0 = i * tq
    tpos = t0 + lax.broadcasted_iota(jnp.int32, (tq, 1), 0)
    scale = (NOPE + ROPE_D) ** -0.5
    q = q_ref[...]

    def body(c, carry):
        m, den, acc = carry
        k0 = pl.multiple_of(c * tk, tk)
        k = k_ref[pl.ds(k0, tk), :]
        v = v_ref[pl.ds(k0, tk), :]
        mask = k0 + lax.broadcasted_iota(jnp.int32, (1, tk), 1) <= tpos
        s = jnp.where(mask, _dot_nt(q, k) * scale, NEG)
        m_new = jnp.maximum(m, jnp.max(s, axis=-1, keepdims=True))
        a = jnp.exp(m - m_new)
        p = jnp.where(mask, jnp.exp(s - m_new), 0.0)
        den = a * den + jnp.sum(p, axis=-1, keepdims=True)
        return m_new, den, a * acc + _dot(p.astype(BF16), v)

    nk = (t0 + tq + tk - 1) // tk
    init = (jnp.full((tq, 1), NEG, F32), jnp.zeros((tq, 1), F32), jnp.zeros((tq, LANE), F32))
    _, den, acc = lax.fori_loop(0, nk, body, init)
    o_ref[...] = (acc / den).astype(BF16)


def _mla_attn(q, k, v):
    b, s, _ = q.shape
    tq = MLA_TQ
    return pl.pallas_call(
        _mla_attn_kernel,
        grid=(b, HC, s // tq),
        in_specs=[pl.BlockSpec((None, tq, LANE), lambda bi, h, i: (bi, i, h)),
                  pl.BlockSpec((None, s, LANE), lambda bi, h, i: (bi, 0, h)),
                  pl.BlockSpec((None, s, LANE), lambda bi, h, i: (bi, 0, h))],
        out_specs=pl.BlockSpec((None, tq, LANE), lambda bi, h, i: (bi, i, h)),
        out_shape=jax.ShapeDtypeStruct((b, s, MLA_W), BF16),
        compiler_params=_params("parallel", "parallel", "arbitrary"),
        name="mla_attn",
    )(q, k, v)


MERGE_TM = 512


def _merge_kernel(x_ref, ya_ref, yb_ref, yc_ref, wm_ref, wa_ref, wb_ref, wc_ref, wo_ref,
                  g_ref, b_ref, o_ref, *, alpha):
    x = x_ref[...]
    xb = x.astype(BF16)
    mixed = jnp.zeros(x.shape, F32)
    for idx, (y_ref, w_ref) in enumerate(((ya_ref, wa_ref), (yb_ref, wb_ref), (yc_ref, wc_ref))):
        gate = jax.nn.sigmoid(_dot(xb, wm_ref[:, idx * D_MODEL:(idx + 1) * D_MODEL]))
        mixed = mixed + gate * _dot(y_ref[...], w_ref[...])
    y = alpha * x + _dot(mixed.astype(BF16), wo_ref[...])
    o_ref[...] = _ln(y, g_ref[...], b_ref[...])


def _merge(x2, ya, yb, yc, wm, wa, wb, wc, wo, lng, lnb, l, alpha):
    n = x2.shape[0]
    tm = min(MERGE_TM, n)

    def rows(width):
        return pl.BlockSpec((tm, width), lambda m: (m, 0))

    return pl.pallas_call(
        functools.partial(_merge_kernel, alpha=alpha),
        grid=(n // tm,),
        in_specs=[rows(D_MODEL), rows(NSA_QW), rows(MIX_W), rows(MLA_W),
                  _resident((D_MODEL, 3 * D_MODEL), lambda m: (0, 0)),
                  _resident((NSA_QW, D_MODEL), lambda m: (0, 0)),
                  _resident((MIX_W, D_MODEL), lambda m: (0, 0)),
                  _resident((MLA_W, D_MODEL), lambda m: (0, 0)),
                  _resident((None, D_MODEL, D_MODEL), lambda m: (l, 0, 0)),
                  _resident((None, None, 1, D_MODEL), lambda m: (l, 1, 0, 0)),
                  _resident((None, None, 1, D_MODEL), lambda m: (l, 1, 0, 0))],
        out_specs=rows(D_MODEL),
        out_shape=jax.ShapeDtypeStruct((n, D_MODEL), F32),
        compiler_params=_params("parallel"),
        name="merge",
    )(x2, ya, yb, yc, wm, wa, wb, wc, wo, lng, lnb)


XA_TM = 512


def _xattn_kernel(x_ref, wq_ref, k_ref, v_ref, wo_ref, g_ref, b_ref, o_ref, *, alpha):
    x = x_ref[...]
    xb = x.astype(BF16)
    scale = XA_DH ** -0.5
    heads = []
    for h in range(XA_HEADS):
        hs = slice(h * XA_DH, (h + 1) * XA_DH)
        qh = _dot(xb, wq_ref[:, hs]).astype(BF16)
        s = _dot_nt(qh, k_ref[:, hs]) * scale
        e = jnp.exp(s - jnp.max(s, axis=-1, keepdims=True))
        p = e / jnp.sum(e, axis=-1, keepdims=True)
        heads.append(_dot(p.astype(BF16), v_ref[:, hs]).astype(BF16))
    att = jnp.concatenate(heads, axis=1)
    y = alpha * x + _dot(att, wo_ref[...])
    o_ref[...] = _ln(y, g_ref[...], b_ref[...])


def _xattn(x3, wq, k, v, wo, lng, lnb, l, alpha):
    b, s, _ = x3.shape
    m_len = k.shape[1]
    tm = XA_TM
    return pl.pallas_call(
        functools.partial(_xattn_kernel, alpha=alpha),
        grid=(b, s // tm),
        in_specs=[pl.BlockSpec((None, tm, D_MODEL), lambda bi, m: (bi, m, 0)),
                  _resident((None, D_MODEL, D_MODEL), lambda bi, m: (l, 0, 0)),
                  pl.BlockSpec((None, m_len, D_MODEL), lambda bi, m: (bi, 0, 0)),
                  pl.BlockSpec((None, m_len, D_MODEL), lambda bi, m: (bi, 0, 0)),
                  _resident((None, D_MODEL, D_MODEL), lambda bi, m: (l, 0, 0)),
                  _resident((None, None, 1, D_MODEL), lambda bi, m: (l, 2, 0, 0)),
                  _resident((None, None, 1, D_MODEL), lambda bi, m: (l, 2, 0, 0))],
        out_specs=pl.BlockSpec((None, tm, D_MODEL), lambda bi, m: (bi, m, 0)),
        out_shape=jax.ShapeDtypeStruct((b, s, D_MODEL), F32),
        compiler_params=_params("parallel", "parallel"),
        name="xattn",
    )(x3, wq, k, v, wo, lng, lnb)


def _pad_heads(w, n_heads, dh):
    k = w.shape[0]
    w = w.reshape(k, n_heads, dh)
    return jnp.pad(w, ((0, 0), (0, 0), (0, LANE - dh))).reshape(k, n_heads * LANE)


def _rot_cols(w, n_heads, dh, off, rot):
    k = w.shape[0]
    half = rot // 2
    w = w.reshape(k, n_heads, dh)
    x1 = w[:, :, off:off + half]
    x2 = w[:, :, off + half:off + rot]
    out = jnp.zeros_like(w)
    out = out.at[:, :, off:off + half].set(-x2)
    out = out.at[:, :, off + half:off + rot].set(x1)
    return out.reshape(k, n_heads * dh)


def _pad_rows(w, n_heads, dh):
    n = w.shape[1]
    w = w.reshape(n_heads, dh, n)
    return jnp.pad(w, ((0, 0), (0, LANE - dh), (0, 0))).reshape(n_heads * LANE, n)


def _rope_tables(positions, rot, off):
    half = rot // 2
    inv = ROPE_THETA ** (-jnp.arange(half, dtype=F32) / half)
    ang = positions.astype(F32)[..., None] * inv
    cos, sin = jnp.cos(ang), jnp.sin(ang)
    shape = positions.shape
    cos_t = jnp.concatenate([jnp.ones(shape + (off,), F32), cos, cos,
                             jnp.ones(shape + (LANE - off - rot,), F32)], axis=-1)
    sin_t = jnp.concatenate([jnp.zeros(shape + (off,), F32), sin, sin,
                             jnp.zeros(shape + (LANE - off - rot,), F32)], axis=-1)
    return cos_t, sin_t


def _overlap_matrix(s_len):
    n_cmp = (s_len - CMP_L) // CMP_D + 1
    n_slc = s_len // SEL_L
    start = np.arange(n_cmp) * CMP_D
    j = np.arange(n_slc)
    ov = np.clip(np.minimum(start[:, None] + CMP_L, (j[None, :] + 1) * SEL_L)
                 - np.maximum(start[:, None], j[None, :] * SEL_L), 0, None) / CMP_L
    out = np.zeros((N_CMP, LANE), np.float32)
    out[:n_cmp, :n_slc] = ov
    return jnp.asarray(out)


def _layer_weights(l, w_in, nsa_cmp_pos, nsa_cmp_w1, nsa_cmp_w2, mla_w_uq, mla_w_ukv, w_branch):
    wl = w_in[l]
    part = [wl[:, IN_OFF[i]:IN_OFF[i + 1]] for i in range(len(IN_SIZES))]
    (a_q, a_kc, a_vc, a_ks, a_vs, a_kw, a_vw, a_gate, b_q, b_f, b_i, b_g,
     c_q, c_kv, c_kr, merge) = part
    kv = lambda w: _pad_heads(w, NSA_KV, NSA_DH)
    kvr = lambda w: _pad_heads(_rot_cols(w, NSA_KV, NSA_DH, 0, NSA_ROT), NSA_KV, NSA_DH)
    gate = a_gate.reshape(D_MODEL, NSA_KV, NSA_R * 3)
    gate = jnp.pad(gate, ((0, 0), (0, 0), (0, LANE - NSA_R * 3))).reshape(D_MODEL, NSA_KW)
    wa = jnp.concatenate([
        _pad_heads(a_q, NSA_HEADS, NSA_DH),
        _pad_heads(_rot_cols(a_q, NSA_HEADS, NSA_DH, 0, NSA_ROT), NSA_HEADS, NSA_DH),
        kv(a_kc), kv(a_vc), kv(a_ks), kvr(a_ks), kv(a_vs), kv(a_kw), kvr(a_kw), kv(a_vw), gate,
    ], axis=1).astype(BF16)
    wb = jnp.concatenate([b_q, b_f, b_i, b_g], axis=1).astype(BF16)

    kr_pad = jnp.pad(c_kr, ((0, 0), (NOPE, LANE - NOPE - ROPE_D)))
    krr_pad = jnp.pad(_rot_cols(c_kr, 1, ROPE_D, 0, ROPE_D), ((0, 0), (NOPE, LANE - NOPE - ROPE_D)))
    wc = jnp.concatenate([c_q, c_kv, kr_pad, krr_pad], axis=1).astype(BF16)
    dq = NOPE + ROPE_D
    wuq = _pad_heads(mla_w_uq[l], HC, dq).astype(BF16)
    wuqr = _pad_heads(_rot_cols(mla_w_uq[l], HC, dq, NOPE, ROPE_D), HC, dq).astype(BF16)
    ukv = mla_w_ukv[l].reshape(KV_RANK, HC, NOPE + VD)
    wuk = _pad_heads(ukv[:, :, :NOPE].reshape(KV_RANK, HC * NOPE), HC, NOPE).astype(BF16)
    wuv = _pad_heads(ukv[:, :, NOPE:].reshape(KV_RANK, HC * VD), HC, VD).astype(BF16)

    w1 = nsa_cmp_w1[l].reshape(2, CMP_L, NSA_DH, CMP_HID)
    w1p = jnp.pad(w1, ((0, 0), (0, 0), (0, LANE - NSA_DH), (0, 0))).reshape(2, CMP_L * LANE, CMP_HID)
    pe = jnp.pad(nsa_cmp_pos[l], ((0, 0), (0, 0), (0, LANE - NSA_DH))).reshape(2, 1, CMP_L * LANE)
    pe8 = jnp.broadcast_to(pe, (2, 8, CMP_L * LANE))
    w2p = jnp.pad(nsa_cmp_w2[l], ((0, 0), (0, 0), (0, LANE - NSA_DH)))

    wba = _pad_rows(w_branch[l, 0], NSA_HEADS, NSA_DH).astype(BF16)
    wbb = w_branch[l, 1].astype(BF16)
    wbc = _pad_rows(w_branch[l, 2], HC, VD).astype(BF16)
    return dict(wa=wa, wb=wb, wc=wc, wuq=wuq, wuqr=wuqr, wuk=wuk, wuv=wuv,
                w1p=w1p.astype(BF16), pe8=pe8.astype(BF16), w2p=w2p.astype(BF16),
                wm=merge.astype(BF16), wba=wba, wbb=wbb, wbc=wbc)


def kernel(x, mem, positions, ln_g, ln_b, ffn_w1, ffn_w3, ffn_w2, w_in, nsa_cmp_pos,
           nsa_cmp_w1, nsa_cmp_w2, hgrn_lb_logits, hgrn_norm_g, mla_q_norm_g, mla_w_uq,
           mla_kv_norm_g, mla_w_ukv, w_branch, w_out, xa_wq, xa_wk, xa_wv, xa_wo):
    b, s, d = x.shape
    depth = ln_g.shape[0]
    n = b * s
    alpha = (2.0 * depth) ** 0.25

    lng = ln_g.reshape(depth, 4, 1, d)
    lnb = ln_b.reshape(depth, 4, 1, d)
    w1 = ffn_w1.astype(BF16)
    w3 = ffn_w3.astype(BF16)
    w2 = ffn_w2.astype(BF16)
    wo = w_out.astype(BF16)
    xq = xa_wq.astype(BF16)
    xk = xa_wk.astype(BF16)
    xv = xa_wv.astype(BF16)
    xo = xa_wo.astype(BF16)
    p_lb = jax.nn.softmax(hgrn_lb_logits.astype(F32), axis=0)
    lower = (jnp.cumsum(p_lb, axis=0) - p_lb[0:1]).reshape(depth, HB, 1, HK)

    cos_a, sin_a = _rope_tables(positions, NSA_ROT, 0)
    cos_c, sin_c = _rope_tables(positions, ROPE_D, NOPE)
    ovl = _overlap_matrix(s)
    shift = jnp.asarray(np.eye(N_CMP, k=1, dtype=np.float32)).astype(BF16)
    mem2 = mem.reshape(b * mem.shape[1], d)

    x2 = x.reshape(n, d)
    for l in range(depth):
        w = _layer_weights(l, w_in, nsa_cmp_pos, nsa_cmp_w1, nsa_cmp_w2, mla_w_uq, mla_w_ukv,
                           w_branch)
        x2 = _ffn_ln(x2, w1, w3, w2, lng, lnb, l, 0, 0, alpha)
        x3 = x2.reshape(b, s, d)

        q, qr, kc, vc, ks, vs, kw, vw, gates = _proja(x3, w["wa"], cos_a, sin_a)
        zk = kc.reshape(b, NSA_KV, N_CMP, CMP_HALF)
        zv = vc.reshape(b, NSA_KV, N_CMP, CMP_HALF)
        kcmp, vcmp = _cmp(zk, zv, w["w1p"], w["pe8"], w["w2p"], shift)
        ya = _nsa_attn(q, qr, kcmp, vcmp, ks, vs, kw, vw, gates, ovl)

        pb = _proj(x2, w["wb"], F32).reshape(b, s, 4 * HB * HK)
        yb = _hgrn(pb, lower[l], hgrn_norm_g[l].reshape(1, HV))

        mq, mk, mv = _mlap(x3, w["wc"], mla_q_norm_g[l].reshape(1, Q_RANK),
                           mla_kv_norm_g[l].reshape(1, KV_RANK), w["wuq"], w["wuqr"],
                           w["wuk"], w["wuv"], cos_c, sin_c)
        yc = _mla_attn(mq, mk, mv)

        x2 = _merge(x2, ya.reshape(n, NSA_QW), yb.reshape(n, MIX_W), yc.reshape(n, MLA_W),
                    w["wm"], w["wba"], w["wbb"], w["wbc"], wo, lng, lnb, l, alpha)

        xk_l = _proj(mem2, xk[l], BF16).reshape(b, -1, d)
        xv_l = _proj(mem2, xv[l], BF16).reshape(b, -1, d)
        x2 = _xattn(x2.reshape(b, s, d), xq, xk_l, xv_l, xo, lng, lnb, l, alpha).reshape(n, d)

        x2 = _ffn_ln(x2, w1, w3, w2, lng, lnb, l, 1, 3, alpha)
    return x2.reshape(b, s, d)
```

```python
import functools
import math

import numpy as np
import jax
import jax.numpy as jnp
from jax import lax
from jax.experimental import pallas as pl
from jax.experimental.pallas import tpu as pltpu

F32 = jnp.float32
BF16 = jnp.bfloat16

D_MODEL = 1024
MIX_W = D_MODEL // 2
NSA_DH = 64
NSA_HEADS = 8
NSA_KV = 2
NSA_R = 4
NSA_ROT = 16
CMP_L = 32
CMP_D = 16
CMP_HID = 256
SEL_L = 64
N_SEL = 8
WINDOW = 256
HB = 4
HK = 128
HV = 128
HGRN_CHUNK = 64
HGRN_SUB = 16
HC = 8
NOPE = 64
ROPE_D = 32
VD = 64
Q_RANK = 384
KV_RANK = 256
XA_HEADS = 4
XA_DH = D_MODEL // XA_HEADS
D_FF = 2816
ROPE_THETA = 500000.0
LN_EPS = 1e-5
RMS_EPS = 1e-6
NEG = -1e30
BIG = 1e9
F_MIN = 1e-20

LOG2E = 1.4426950408889634
LANE = 128
SEL_LANE0 = NSA_DH
DEN_LANE = 64
VMEM_LIMIT = 56 * 1024 * 1024

IN_SIZES = (512, 128, 128, 128, 128, 128, 128, 24, 512, 512, 512, 512, 384, 256, 32, 3072)
IN_OFF = tuple(int(v) for v in np.concatenate([[0], np.cumsum(IN_SIZES)]))


def _dot(a, b):
    return jnp.dot(a, b, preferred_element_type=F32)


def _dot_nt(a, b):
    return lax.dot_general(a, b, (((1,), (1,)), ((), ())), preferred_element_type=F32)


def _ln(y, g, b):
    mu = jnp.mean(y, axis=-1, keepdims=True)
    yc = y - mu
    var = jnp.mean(yc * yc, axis=-1, keepdims=True)
    return yc * lax.rsqrt(var + LN_EPS) * g + b


def _params(*sem):
    return pltpu.CompilerParams(dimension_semantics=sem, vmem_limit_bytes=VMEM_LIMIT)


def _resident(shape, index_map):
    return pl.BlockSpec(shape, index_map, pipeline_mode=pl.Buffered(1))


FFN_TM = 1024
FFN_TF = 256


def _ffn_kernel(x_ref, w1_ref, w3_ref, w2_ref, g_ref, b_ref, o_ref, acc_ref, xb_ref, *, alpha):
    j = pl.program_id(1)

    @pl.when(j == 0)
    def _():
        acc_ref[...] = jnp.zeros_like(acc_ref)
        xb_ref[...] = x_ref[...].astype(BF16)

    xb = xb_ref[...]
    h1 = _dot(xb, w1_ref[...])
    h3 = _dot(xb, w3_ref[...])
    h = (h1 * jax.nn.sigmoid(h1)) * h3
    acc_ref[...] += _dot(h.astype(BF16), w2_ref[...])

    @pl.when(j == pl.num_programs(1) - 1)
    def _():
        y = alpha * x_ref[...] + 0.5 * acc_ref[...]
        o_ref[...] = _ln(y, g_ref[...], b_ref[...])


def _ffn_ln(x2, w1, w3, w2, lng, lnb, l, which, ln_idx, alpha):
    n = x2.shape[0]
    tm = min(FFN_TM, n)
    grid = (n // tm, D_FF // FFN_TF)
    return pl.pallas_call(
        functools.partial(_ffn_kernel, alpha=alpha),
        grid=grid,
        in_specs=[
            pl.BlockSpec((tm, D_MODEL), lambda m, j: (m, 0)),
            pl.BlockSpec((None, None, D_MODEL, FFN_TF), lambda m, j: (l, which, 0, j)),
            pl.BlockSpec((None, None, D_MODEL, FFN_TF), lambda m, j: (l, which, 0, j)),
            pl.BlockSpec((None, None, FFN_TF, D_MODEL), lambda m, j: (l, which, j, 0)),
            pl.BlockSpec((None, None, 1, D_MODEL), lambda m, j: (l, ln_idx, 0, 0)),
            pl.BlockSpec((None, None, 1, D_MODEL), lambda m, j: (l, ln_idx, 0, 0)),
        ],
        out_specs=pl.BlockSpec((tm, D_MODEL), lambda m, j: (m, 0)),
        out_shape=jax.ShapeDtypeStruct((n, D_MODEL), F32),
        scratch_shapes=[pltpu.VMEM((tm, D_MODEL), F32), pltpu.VMEM((tm, D_MODEL), BF16)],
        compiler_params=_params("parallel", "arbitrary"),
        name="ffn_ln",
    )(x2, w1, w3, w2, lng, lnb)


def _proj_kernel(x_ref, w_ref, o_ref):
    o_ref[...] = _dot(x_ref[...].astype(BF16), w_ref[...]).astype(o_ref.dtype)


def _proj(x2, w, out_dtype, tm=512, tn=512):
    n, k = x2.shape
    c = w.shape[1]
    tm = min(tm, n)
    tn = min(tn, c)
    return pl.pallas_call(
        _proj_kernel,
        grid=(n // tm, c // tn),
        in_specs=[pl.BlockSpec((tm, k), lambda m, j: (m, 0)),
                  pl.BlockSpec((k, tn), lambda m, j: (0, j))],
        out_specs=pl.BlockSpec((tm, tn), lambda m, j: (m, j)),
        out_shape=jax.ShapeDtypeStruct((n, c), out_dtype),
        compiler_params=_params("parallel", "arbitrary"),
        name="proj",
    )(x2, w)


NSA_QW = NSA_HEADS * LANE
NSA_KW = NSA_KV * LANE
A_Q, A_QROT = 0, NSA_QW
A_KC = 2 * NSA_QW
A_VC = A_KC + NSA_KW
A_KS = A_VC + NSA_KW
A_KSROT = A_KS + NSA_KW
A_VS = A_KSROT + NSA_KW
A_KWN = A_VS + NSA_KW
A_KWROT = A_KWN + NSA_KW
A_VW = A_KWROT + NSA_KW
A_GATE = A_VW + NSA_KW
A_COLS = A_GATE + NSA_KW
PROJA_TM = 512


def _proja_kernel(x_ref, w_ref, cos_ref, sin_ref,
                  q_ref, qr_ref, kc_ref, vc_ref, ks_ref, vs_ref, kw_ref, vw_ref, g_ref):
    xb = x_ref[...].astype(BF16)
    cos = cos_ref[...]
    sin = sin_ref[...]
    cos2 = jnp.concatenate([cos, cos], axis=1)
    sin2 = jnp.concatenate([sin, sin], axis=1)

    def mm(c0, width):
        return _dot(xb, w_ref[:, c0:c0 + width])

    qs = NSA_DH ** -0.5 * LOG2E
    for p in range(NSA_QW // NSA_KW):
        c = p * NSA_KW
        y = mm(A_Q + c, NSA_KW)
        yr = mm(A_QROT + c, NSA_KW)
        q_ref[:, c:c + NSA_KW] = (y * qs).astype(BF16)
        qr_ref[:, c:c + NSA_KW] = ((y * cos2 + yr * sin2) * qs).astype(BF16)

    def put(ref, y):
        for g in range(NSA_KV):
            ref[g] = y[:, g * LANE:(g + 1) * LANE].astype(BF16)

    tm = x_ref.shape[0]
    lane2 = lax.broadcasted_iota(jnp.int32, (tm, NSA_KW), 1) % LANE
    tok = pl.program_id(1) * tm + lax.broadcasted_iota(jnp.int32, (tm, NSA_KW), 0)
    blk_flag = jnp.where(lane2 - SEL_LANE0 == tok // SEL_L, 1.0, 0.0)
    den_flag = jnp.where(lane2 == DEN_LANE, 1.0, 0.0)

    put(kc_ref, mm(A_KC, NSA_KW))
    put(vc_ref, mm(A_VC, NSA_KW))
    put(ks_ref, mm(A_KS, NSA_KW) * cos2 + mm(A_KSROT, NSA_KW) * sin2 + blk_flag)
    put(vs_ref, mm(A_VS, NSA_KW) + den_flag)
    put(kw_ref, mm(A_KWN, NSA_KW) * cos2 + mm(A_KWROT, NSA_KW) * sin2)
    put(vw_ref, mm(A_VW, NSA_KW) + den_flag)
    g_ref[...] = mm(A_GATE, NSA_KW)


def _proja(x3, wa, cos_a, sin_a):
    b, s, _ = x3.shape
    tm = PROJA_TM
    kv_shape = jax.ShapeDtypeStruct((b, NSA_KV, s, LANE), BF16)
    kv_spec = pl.BlockSpec((None, NSA_KV, tm, LANE), lambda bi, m: (bi, 0, m, 0))
    return pl.pallas_call(
        _proja_kernel,
        grid=(b, s // tm),
        in_specs=[
            pl.BlockSpec((None, tm, D_MODEL), lambda bi, m: (bi, m, 0)),
            _resident((D_MODEL, A_COLS), lambda bi, m: (0, 0)),
            pl.BlockSpec((None, tm, LANE), lambda bi, m: (bi, m, 0)),
            pl.BlockSpec((None, tm, LANE), lambda bi, m: (bi, m, 0)),
        ],
        out_specs=[
            pl.BlockSpec((None, tm, NSA_QW), lambda bi, m: (bi, m, 0)),
            pl.BlockSpec((None, tm, NSA_QW), lambda bi, m: (bi, m, 0)),
            kv_spec, kv_spec, kv_spec, kv_spec, kv_spec, kv_spec,
            pl.BlockSpec((None, tm, NSA_KW), lambda bi, m: (bi, m, 0)),
        ],
        out_shape=[
            jax.ShapeDtypeStruct((b, s, NSA_QW), BF16),
            jax.ShapeDtypeStruct((b, s, NSA_QW), BF16),
            kv_shape, kv_shape, kv_shape, kv_shape, kv_shape, kv_shape,
            jax.ShapeDtypeStruct((b, s, NSA_KW), F32),
        ],
        compiler_params=_params("parallel", "parallel"),
        name="nsa_proj",
    )(x3, wa, cos_a, sin_a)


N_CMP = 128
CMP_HALF = CMP_D * LANE


def _gelu_tanh(x):
    return 0.5 * x * (1.0 + jnp.tanh(0.7978845608028654 * (x + 0.044715 * x * x * x)))


def _cmp_kernel(zk_ref, zv_ref, w1_ref, pe_ref, w2_ref, shift_ref, ok_ref, ov_ref):
    shift = shift_ref[...]
    for which, (z_ref, o_ref) in enumerate(((zk_ref, ok_ref), (zv_ref, ov_ref))):
        z = z_ref[...]
        zn = _dot(shift, z).astype(BF16)
        bias = _dot(pe_ref[which], w1_ref[which])[0:1]
        pre = (_dot(z, w1_ref[which, 0:CMP_HALF, :])
               + _dot(zn, w1_ref[which, CMP_HALF:2 * CMP_HALF, :]) + bias)
        h = _gelu_tanh(pre)
        o_ref[...] = _dot(h.astype(BF16), w2_ref[which]).astype(BF16)


def _cmp(zk, zv, w1p, pe8, w2p, shift):
    b = zk.shape[0]
    z_spec = pl.BlockSpec((None, None, N_CMP, CMP_HALF), lambda bi, g: (bi, g, 0, 0))
    o_spec = pl.BlockSpec((None, None, N_CMP, LANE), lambda bi, g: (bi, g, 0, 0))
    o_shape = jax.ShapeDtypeStruct((b, NSA_KV, N_CMP, LANE), BF16)
    return pl.pallas_call(
        _cmp_kernel,
        grid=(b, NSA_KV),
        in_specs=[
            z_spec, z_spec,
            _resident((2, 2 * CMP_HALF, CMP_HID), lambda bi, g: (0, 0, 0)),
            _resident((2, 8, 2 * CMP_HALF), lambda bi, g: (0, 0, 0)),
            _resident((2, CMP_HID, LANE), lambda bi, g: (0, 0, 0)),
            _resident((N_CMP, N_CMP), lambda bi, g: (0, 0)),
        ],
        out_specs=[o_spec, o_spec],
        out_shape=[o_shape, o_shape],
        compiler_params=_params("parallel", "parallel"),
        name="nsa_cmp",
    )(zk, zv, w1p, pe8, w2p, shift)


NSA_TQ = 128
NSA_TK = 256
N_SLC = 32
WIN_SLAB = WINDOW + NSA_TQ


def _nsa_kernel(q_ref, qr_ref, kc_ref, vc_ref, ks_ref, vs_ref, kw_ref, vw_ref, gt_ref, ovl_ref,
                place_ref, o_ref):
    tq, tk, r_heads = NSA_TQ, NSA_TK, NSA_R
    i = pl.program_id(2)
    t0 = i * tq
    tpos = t0 + lax.broadcasted_iota(jnp.int32, (tq, 1), 0)
    lane = lax.broadcasted_iota(jnp.int32, (1, LANE), 1)

    cmask = (lane * CMP_D + (CMP_L - 1) <= tpos) & (lane < N_CMP - 1)
    kc = kc_ref[...]
    vc = vc_ref[...]
    psum = jnp.zeros((tq, LANE), F32)
    o_cmp = []
    for r in range(r_heads):
        q = q_ref[:, r * LANE:(r + 1) * LANE]
        s = jnp.where(cmask, _dot_nt(q, kc), NEG)
        m = jnp.max(s, axis=-1, keepdims=True)
        e = jnp.where(cmask, jnp.exp2(s - m), 0.0)
        den = jnp.sum(e, axis=-1, keepdims=True)
        p = e / jnp.where(den > 0.0, den, 1.0)
        psum = psum + p
        o_cmp.append(_dot(p.astype(BF16), vc))

    imp = lax.dot_general(ovl_ref[...], psum, (((1,), (1,)), ((), ())),
                          precision=lax.Precision.HIGHEST, preferred_element_type=F32)
    blk = lax.broadcasted_iota(jnp.int32, (N_SLC, 1), 0)
    tpos_l = t0 + lax.broadcasted_iota(jnp.int32, (1, tq), 1)
    cur = tpos_l // SEL_L
    valid = blk * SEL_L <= tpos_l
    forced = (blk == 0) | (blk == cur) | (blk == cur - 1)
    score = jnp.where(valid & forced, BIG, jnp.where(valid, imp, -BIG))
    rank = jnp.zeros((N_SLC, tq), F32)
    for j in range(N_SLC):
        sj = score[j:j + 1, :]
        rank = rank + jnp.where((sj > score) | ((sj == score) & (j < blk)), 1.0, 0.0)
    sel_t = jnp.where((rank < N_SEL) & valid, 1.0, 0.0)
    sel_q = lax.dot_general(sel_t, place_ref[...], (((0,), (0,)), ((), ())),
                            preferred_element_type=F32)
    in_flags = jnp.where((lane >= SEL_LANE0) & (lane < SEL_LANE0 + N_SLC), 1.0, 0.0)
    q_bias = (sel_q - in_flags) * (-NEG)
    q4 = jnp.concatenate(
        [(qr_ref[:, r * LANE:(r + 1) * LANE].astype(F32) + q_bias).astype(BF16)
         for r in range(r_heads)], axis=0)

    def sel_step(c, carry, causal):
        m, acc = carry
        k0 = pl.multiple_of(c * tk, tk)
        k = ks_ref[pl.ds(k0, tk), :]
        v = vs_ref[pl.ds(k0, tk), :]
        s = _dot_nt(q4, k).reshape(r_heads, tq, tk)
        if causal:
            kpos = k0 + lax.broadcasted_iota(jnp.int32, (1, tk), 1)
            s = jnp.where((kpos <= tpos)[None], s, NEG)
        m_new = jnp.maximum(m, jnp.max(s, axis=-1, keepdims=True))
        p = jnp.exp2(s - m_new)
        pv = _dot(p.reshape(r_heads * tq, tk).astype(BF16), v).reshape(r_heads, tq, LANE)
        return m_new, jnp.exp2(m - m_new) * acc + pv

    last = (t0 + tq - 1) // tk
    carry = (jnp.full((r_heads, tq, 1), NEG, F32), jnp.zeros((r_heads, tq, LANE), F32))
    carry = lax.fori_loop(0, last, functools.partial(sel_step, causal=False), carry)
    _, acc_s = sel_step(last, carry, True)
    o_sel = acc_s / acc_s[:, :, DEN_LANE:DEN_LANE + 1]

    w0 = pl.multiple_of(jnp.maximum(t0 - WINDOW, 0), LANE)
    kw = kw_ref[pl.ds(w0, WIN_SLAB), :]
    vw = vw_ref[pl.ds(w0, WIN_SLAB), :]
    kpos = w0 + lax.broadcasted_iota(jnp.int32, (1, WIN_SLAB), 1)
    w_bias = jnp.where((kpos <= tpos) & (kpos > tpos - WINDOW), 0.0, NEG)
    s = _dot_nt(q4, kw).reshape(r_heads, tq, WIN_SLAB) + w_bias[None]
    p = jnp.exp2(s - jnp.max(s, axis=-1, keepdims=True))
    o_win = _dot(p.reshape(r_heads * tq, WIN_SLAB).astype(BF16), vw).reshape(r_heads, tq, LANE)
    o_win = o_win / o_win[:, :, DEN_LANE:DEN_LANE + 1]

    gt = jax.nn.sigmoid(gt_ref[...])
    for r in range(r_heads):
        g0 = gt[:, 3 * r:3 * r + 1]
        g1 = gt[:, 3 * r + 1:3 * r + 2]
        g2 = gt[:, 3 * r + 2:3 * r + 3]
        o = g0 * o_cmp[r] + g1 * o_sel[r] + g2 * o_win[r]
        o_ref[:, r * LANE:(r + 1) * LANE] = o.astype(BF16)


def _nsa_attn(q, qr, kcmp, vcmp, ks, vs, kw, vw, gates, ovl, place):
    b, s, _ = q.shape
    tq = NSA_TQ
    gw = NSA_R * LANE
    q_spec = pl.BlockSpec((None, tq, gw), lambda bi, g, i: (bi, i, g))
    c_spec = pl.BlockSpec((None, None, N_CMP, LANE), lambda bi, g, i: (bi, g, 0, 0))
    kv_spec = pl.BlockSpec((None, None, s, LANE), lambda bi, g, i: (bi, g, 0, 0))
    return pl.pallas_call(
        _nsa_kernel,
        grid=(b, NSA_KV, s // tq),
        in_specs=[q_spec, q_spec, c_spec, c_spec, kv_spec, kv_spec, kv_spec, kv_spec,
                  pl.BlockSpec((None, tq, LANE), lambda bi, g, i: (bi, i, g)),
                  _resident((N_SLC, N_CMP), lambda bi, g, i: (0, 0)),
                  _resident((N_SLC, LANE), lambda bi, g, i: (0, 0))],
        out_specs=pl.BlockSpec((None, tq, gw), lambda bi, g, i: (bi, i, g)),
        out_shape=jax.ShapeDtypeStruct((b, s, NSA_QW), BF16),
        compiler_params=_params("parallel", "parallel", "arbitrary"),
        name="nsa_attn",
    )(q, qr, kcmp, vcmp, ks, vs, kw, vw, gates, ovl, place)


def _hgrn_kernel(q_ref, z_ref, v_ref, go_ref, lb_ref, ng_ref, o_ref, st_ref):
    s_len = q_ref.shape[0]
    c_len, sub_len = HGRN_CHUNK, HGRN_SUB
    nc = s_len // c_len
    n_sub = c_len // sub_len
    q = q_ref[...]
    z = z_ref[...]
    v = v_ref[...]
    lb = lb_ref[...]
    t = lax.broadcasted_iota(jnp.int32, (s_len, 1), 0)
    t_sub = t % sub_len
    t_chunk = t % c_len

    f = lb + (1.0 - lb) * jax.nn.sigmoid(z)
    lf = jnp.log(jnp.maximum(f, F_MIN))
    k = (1.0 - lb) * jax.nn.sigmoid(-z)

    bs = lf
    step = 1
    while step < sub_len:
        bs = bs + jnp.where(t_sub >= step, pltpu.roll(bs, step, axis=0), 0.0)
        step *= 2
    sub_id = t_chunk // sub_len
    sub_id3 = lax.broadcasted_iota(jnp.int32, (nc, c_len, 1), 1) // sub_len
    bs3 = bs.reshape(nc, c_len, HK)
    b3 = bs3
    for i in range(1, n_sub):
        total = bs3[:, sub_len * i - 1:sub_len * i, :]
        b3 = b3 + jnp.where(sub_id3 >= i, total, 0.0)
    b = b3.reshape(s_len, HK)
    k3 = k.reshape(nc, c_len, HK)
    vb3 = v.astype(BF16).reshape(nc, c_len, HV)

    lq = q * jnp.exp(bs)
    lhs, rhs = [], []
    for i in range(1, n_sub):
        ref_b = b3[:, sub_len * i - 1:sub_len * i, :]
        e = jnp.exp(jnp.minimum(ref_b - b3, 0.0)).reshape(s_len, HK)
        rhs.append(jnp.where(sub_id < i, k * e, 0.0).astype(BF16))
        lhs.append(jnp.where(sub_id == i, lq, 0.0).astype(BF16))
    width = (n_sub - 1) * HK
    lhs3 = jnp.concatenate(lhs, axis=1).reshape(nc, c_len, width)
    rhs3 = jnp.concatenate(rhs, axis=1).reshape(nc, c_len, width)
    a_off = jnp.einsum('ctk,csk->cts', lhs3, rhs3, preferred_element_type=F32)
    o = jnp.einsum('cts,csd->ctd', a_off.astype(BF16), vb3,
                   preferred_element_type=F32).reshape(s_len, HV)

    o = o + jnp.sum(q * k, axis=-1, keepdims=True) * v
    for d in range(1, sub_len):
        e = jnp.exp(jnp.where(t_sub >= d, bs - pltpu.roll(bs, d, axis=0), NEG))
        w = jnp.sum(q * pltpu.roll(k, d, axis=0) * e, axis=-1, keepdims=True)
        o = o + w * pltpu.roll(v, d, axis=0)

    b_last = b3[:, c_len - 1:c_len, :]
    kd3 = (k3 * jnp.exp(b_last - b3)).astype(BF16)
    upd = jnp.einsum('csv,csk->cvk', vb3, kd3, preferred_element_type=F32)
    dec = jnp.exp(b_last)
    state = jnp.zeros((HV, HK), F32)
    for c in range(nc):
        st_ref[c] = state.astype(BF16)
        state = state * dec[c] + upd[c]
    qe3 = (q * jnp.exp(b)).astype(BF16).reshape(nc, c_len, HK)
    o = o + jnp.einsum('ctk,cvk->ctv', qe3, st_ref[...],
                       preferred_element_type=F32).reshape(s_len, HV)

    o = o * lax.rsqrt(jnp.mean(o * o, axis=-1, keepdims=True) + RMS_EPS) * ng_ref[...]
    go = go_ref[...]
    o_ref[...] = (o * (go * jax.nn.sigmoid(go))).astype(BF16)


def _hgrn(pb, lb, ng):
    b, s, _ = pb.shape

    def col(part):
        return pl.BlockSpec((None, s, HK), lambda bi, h: (bi, 0, part * HB + h))

    return pl.pallas_call(
        _hgrn_kernel,
        grid=(b, HB),
        in_specs=[col(0), col(1), col(2), col(3),
                  pl.BlockSpec((None, 1, HK), lambda bi, h: (h, 0, 0)),
                  pl.BlockSpec((1, HV), lambda bi, h: (0, 0))],
        out_specs=pl.BlockSpec((None, s, HV), lambda bi, h: (bi, 0, h)),
        out_shape=jax.ShapeDtypeStruct((b, s, HB * HV), BF16),
        scratch_shapes=[pltpu.VMEM((s // HGRN_CHUNK, HV, HK), BF16)],
        compiler_params=_params("parallel", "parallel"),
        name="hgrn",
    )(pb, pb, pb, pb, lb, ng)


MLA_W = HC * LANE
C_CQ, C_CKV, C_KR, C_KRROT = 0, Q_RANK, Q_RANK + KV_RANK, Q_RANK + KV_RANK + LANE
C_COLS = C_KRROT + LANE
MLAP_TM = 512


def _rms(x, g):
    return x * lax.rsqrt(jnp.mean(x * x, axis=-1, keepdims=True) + RMS_EPS) * g


def _mlap_kernel(x_ref, wc_ref, qg_ref, kvg_ref, wuq_ref, wuqr_ref, wuk_ref, wuv_ref,
                 cos_ref, sin_ref, q_ref, k_ref, v_ref):
    xb = x_ref[...].astype(BF16)
    c = _dot(xb, wc_ref[...])
    cos = cos_ref[...]
    sin = sin_ref[...]
    nq = _rms(c[:, C_CQ:C_CKV], qg_ref[...]).astype(BF16)
    nkv = _rms(c[:, C_CKV:C_KR], kvg_ref[...]).astype(BF16)
    k_pe = c[:, C_KR:C_KRROT] * cos + c[:, C_KRROT:C_COLS] * sin
    qs = (NOPE + ROPE_D) ** -0.5 * LOG2E
    cos_q = cos * qs
    sin_q = sin * qs
    den_flag = jnp.where(lax.broadcasted_iota(jnp.int32, (1, LANE), 1) == DEN_LANE, 1.0, 0.0)
    for h in range(HC):
        hs = slice(h * LANE, (h + 1) * LANE)
        q_ref[:, hs] = (_dot(nq, wuq_ref[:, hs]) * cos_q
                        + _dot(nq, wuqr_ref[:, hs]) * sin_q).astype(BF16)
        k_ref[:, hs] = (_dot(nkv, wuk_ref[:, hs]) + k_pe).astype(BF16)
        v_ref[:, hs] = (_dot(nkv, wuv_ref[:, hs]) + den_flag).astype(BF16)


def _mlap(x3, wc, qg, kvg, wuq, wuqr, wuk, wuv, cos_c, sin_c):
    b, s, _ = x3.shape
    tm = MLAP_TM
    o_spec = pl.BlockSpec((None, tm, MLA_W), lambda bi, m: (bi, m, 0))
    o_shape = jax.ShapeDtypeStruct((b, s, MLA_W), BF16)
    t_spec = pl.BlockSpec((None, tm, LANE), lambda bi, m: (bi, m, 0))
    return pl.pallas_call(
        _mlap_kernel,
        grid=(b, s // tm),
        in_specs=[
            pl.BlockSpec((None, tm, D_MODEL), lambda bi, m: (bi, m, 0)),
            _resident((D_MODEL, C_COLS), lambda bi, m: (0, 0)),
            _resident((1, Q_RANK), lambda bi, m: (0, 0)),
            _resident((1, KV_RANK), lambda bi, m: (0, 0)),
            _resident((Q_RANK, MLA_W), lambda bi, m: (0, 0)),
            _resident((Q_RANK, MLA_W), lambda bi, m: (0, 0)),
            _resident((KV_RANK, MLA_W), lambda bi, m: (0, 0)),
            _resident((KV_RANK, MLA_W), lambda bi, m: (0, 0)),
            t_spec, t_spec,
        ],
        out_specs=[o_spec, o_spec, o_spec],
        out_shape=[o_shape, o_shape, o_shape],
        compiler_params=_params("parallel", "parallel"),
        name="mla_proj",
    )(x3, wc, qg, kvg, wuq, wuqr, wuk, wuv, cos_c, sin_c)


MLA_TQ = 256
MLA_TK = 256


def _mla_attn_kernel(q_ref, k_ref, v_ref, o_ref):
    tq, tk = MLA_TQ, MLA_TK
    i = pl.program_id(2)
    t0 = i * tq
    tpos = t0 + lax.broadcasted_iota(jnp.int32, (tq, 1), 0)
    q = q_ref[...]

    def step(c, carry, causal):
        m, acc = carry
        k0 = pl.multiple_of(c * tk, tk)
        k = k_ref[pl.ds(k0, tk), :]
        v = v_ref[pl.ds(k0, tk), :]
        s = _dot_nt(q, k)
        if causal:
            s = jnp.where(k0 + lax.broadcasted_iota(jnp.int32, (1, tk), 1) <= tpos, s, NEG)
        m_new = jnp.maximum(m, jnp.max(s, axis=-1, keepdims=True))
        p = jnp.exp2(s - m_new)
        return m_new, jnp.exp2(m - m_new) * acc + _dot(p.astype(BF16), v)

    last = (t0 + tq - 1) // tk
    carry = (jnp.full((tq, 1), NEG, F32), jnp.zeros((tq, LANE), F32))
    carry = lax.fori_loop(0, last, functools.partial(step, causal=False), carry)
    _, acc = step(last, carry, True)
    o_ref[...] = (acc / acc[:, DEN_LANE:DEN_LANE + 1]).astype(BF16)


def _mla_attn(q, k, v):
    b, s, _ = q.shape
    tq = MLA_TQ
    return pl.pallas_call(
        _mla_attn_kernel,
        grid=(b, HC, s // tq),
        in_specs=[pl.BlockSpec((None, tq, LANE), lambda bi, h, i: (bi, i, h)),
                  pl.BlockSpec((None, s, LANE), lambda bi, h, i: (bi, 0, h)),
                  pl.BlockSpec((None, s, LANE), lambda bi, h, i: (bi, 0, h))],
        out_specs=pl.BlockSpec((None, tq, LANE), lambda bi, h, i: (bi, i, h)),
        out_shape=jax.ShapeDtypeStruct((b, s, MLA_W), BF16),
        compiler_params=_params("parallel", "parallel", "arbitrary"),
        name="mla_attn",
    )(q, k, v)


MERGE_TM = 512


def _merge_kernel(x_ref, ya_ref, yb_ref, yc_ref, wm_ref, wa_ref, wb_ref, wc_ref, wo_ref,
                  g_ref, b_ref, o_ref, *, alpha):
    x = x_ref[...]
    xb = x.astype(BF16)
    mixed = jnp.zeros(x.shape, F32)
    for idx, (y_ref, w_ref) in enumerate(((ya_ref, wa_ref), (yb_ref, wb_ref), (yc_ref, wc_ref))):
        gate = jax.nn.sigmoid(_dot(xb, wm_ref[:, idx * D_MODEL:(idx + 1) * D_MODEL]))
        mixed = mixed + gate * _dot(y_ref[...], w_ref[...])
    y = alpha * x + _dot(mixed.astype(BF16), wo_ref[...])
    o_ref[...] = _ln(y, g_ref[...], b_ref[...])


def _merge(x2, ya, yb, yc, wm, wa, wb, wc, wo, lng, lnb, l, alpha):
    n = x2.shape[0]
    tm = min(MERGE_TM, n)

    def rows(width):
        return pl.BlockSpec((tm, width), lambda m: (m, 0))

    return pl.pallas_call(
        functools.partial(_merge_kernel, alpha=alpha),
        grid=(n // tm,),
        in_specs=[rows(D_MODEL), rows(NSA_QW), rows(MIX_W), rows(MLA_W),
                  _resident((D_MODEL, 3 * D_MODEL), lambda m: (0, 0)),
                  _resident((NSA_QW, D_MODEL), lambda m: (0, 0)),
                  _resident((MIX_W, D_MODEL), lambda m: (0, 0)),
                  _resident((MLA_W, D_MODEL), lambda m: (0, 0)),
                  _resident((None, D_MODEL, D_MODEL), lambda m: (l, 0, 0)),
                  _resident((None, None, 1, D_MODEL), lambda m: (l, 1, 0, 0)),
                  _resident((None, None, 1, D_MODEL), lambda m: (l, 1, 0, 0))],
        out_specs=rows(D_MODEL),
        out_shape=jax.ShapeDtypeStruct((n, D_MODEL), F32),
        compiler_params=_params("parallel"),
        name="merge",
    )(x2, ya, yb, yc, wm, wa, wb, wc, wo, lng, lnb)


XA_TM = 512


def _xattn_kernel(x_ref, wq_ref, k_ref, v_ref, wo_ref, g_ref, b_ref, o_ref, *, alpha):
    x = x_ref[...]
    xb = x.astype(BF16)
    scale = XA_DH ** -0.5
    heads = []
    for h in range(XA_HEADS):
        hs = slice(h * XA_DH, (h + 1) * XA_DH)
        qh = _dot(xb, wq_ref[:, hs]).astype(BF16)
        s = _dot_nt(qh, k_ref[:, hs]) * scale
        e = jnp.exp(s - jnp.max(s, axis=-1, keepdims=True))
        p = e / jnp.sum(e, axis=-1, keepdims=True)
        heads.append(_dot(p.astype(BF16), v_ref[:, hs]).astype(BF16))
    att = jnp.concatenate(heads, axis=1)
    y = alpha * x + _dot(att, wo_ref[...])
    o_ref[...] = _ln(y, g_ref[...], b_ref[...])


def _xattn(x3, wq, k, v, wo, lng, lnb, l, alpha):
    b, s, _ = x3.shape
    m_len = k.shape[1]
    tm = XA_TM
    return pl.pallas_call(
        functools.partial(_xattn_kernel, alpha=alpha),
        grid=(b, s // tm),
        in_specs=[pl.BlockSpec((None, tm, D_MODEL), lambda bi, m: (bi, m, 0)),
                  _resident((None, D_MODEL, D_MODEL), lambda bi, m: (l, 0, 0)),
                  pl.BlockSpec((None, m_len, D_MODEL), lambda bi, m: (bi, 0, 0)),
                  pl.BlockSpec((None, m_len, D_MODEL), lambda bi, m: (bi, 0, 0)),
                  _resident((None, D_MODEL, D_MODEL), lambda bi, m: (l, 0, 0)),
                  _resident((None, None, 1, D_MODEL), lambda bi, m: (l, 2, 0, 0)),
                  _resident((None, None, 1, D_MODEL), lambda bi, m: (l, 2, 0, 0))],
        out_specs=pl.BlockSpec((None, tm, D_MODEL), lambda bi, m: (bi, m, 0)),
        out_shape=jax.ShapeDtypeStruct((b, s, D_MODEL), F32),
        compiler_params=_params("parallel", "parallel"),
        name="xattn",
    )(x3, wq, k, v, wo, lng, lnb)


def _pad_heads(w, n_heads, dh):
    k = w.shape[0]
    w = w.reshape(k, n_heads, dh)
    return jnp.pad(w, ((0, 0), (0, 0), (0, LANE - dh))).reshape(k, n_heads * LANE)


def _rot_cols(w, n_heads, dh, off, rot):
    k = w.shape[0]
    half = rot // 2
    w = w.reshape(k, n_heads, dh)
    x1 = w[:, :, off:off + half]
    x2 = w[:, :, off + half:off + rot]
    out = jnp.zeros_like(w)
    out = out.at[:, :, off:off + half].set(-x2)
    out = out.at[:, :, off + half:off + rot].set(x1)
    return out.reshape(k, n_heads * dh)


def _pad_rows(w, n_heads, dh):
    n = w.shape[1]
    w = w.reshape(n_heads, dh, n)
    return jnp.pad(w, ((0, 0), (0, LANE - dh), (0, 0))).reshape(n_heads * LANE, n)


def _rope_tables(positions, rot, off):
    half = rot // 2
    inv = ROPE_THETA ** (-jnp.arange(half, dtype=F32) / half)
    ang = positions.astype(F32)[..., None] * inv
    cos, sin = jnp.cos(ang), jnp.sin(ang)
    shape = positions.shape
    cos_t = jnp.concatenate([jnp.ones(shape + (off,), F32), cos, cos,
                             jnp.ones(shape + (LANE - off - rot,), F32)], axis=-1)
    sin_t = jnp.concatenate([jnp.zeros(shape + (off,), F32), sin, sin,
                             jnp.zeros(shape + (LANE - off - rot,), F32)], axis=-1)
    return cos_t, sin_t


def _overlap_matrix(s_len):
    n_cmp = (s_len - CMP_L) // CMP_D + 1
    n_slc = s_len // SEL_L
    start = np.arange(n_cmp) * CMP_D
    j = np.arange(n_slc)
    ov = np.clip(np.minimum(start[:, None] + CMP_L, (j[None, :] + 1) * SEL_L)
                 - np.maximum(start[:, None], j[None, :] * SEL_L), 0, None) / CMP_L
    out = np.zeros((N_SLC, N_CMP), np.float32)
    out[:n_slc, :n_cmp] = ov.T
    place = np.zeros((N_SLC, LANE), np.float32)
    place[np.arange(N_SLC), SEL_LANE0 + np.arange(N_SLC)] = 1.0
    return jnp.asarray(out), jnp.asarray(place)


def _layer_weights(l, w_in, nsa_cmp_pos, nsa_cmp_w1, nsa_cmp_w2, mla_w_uq, mla_w_ukv, w_branch):
    wl = w_in[l]
    part = [wl[:, IN_OFF[i]:IN_OFF[i + 1]] for i in range(len(IN_SIZES))]
    (a_q, a_kc, a_vc, a_ks, a_vs, a_kw, a_vw, a_gate, b_q, b_f, b_i, b_g,
     c_q, c_kv, c_kr, merge) = part
    kv = lambda w: _pad_heads(w, NSA_KV, NSA_DH)
    kvr = lambda w: _pad_heads(_rot_cols(w, NSA_KV, NSA_DH, 0, NSA_ROT), NSA_KV, NSA_DH)
    gate = a_gate.reshape(D_MODEL, NSA_KV, NSA_R * 3)
    gate = jnp.pad(gate, ((0, 0), (0, 0), (0, LANE - NSA_R * 3))).reshape(D_MODEL, NSA_KW)
    wa = jnp.concatenate([
        _pad_heads(a_q, NSA_HEADS, NSA_DH),
        _pad_heads(_rot_cols(a_q, NSA_HEADS, NSA_DH, 0, NSA_ROT), NSA_HEADS, NSA_DH),
        kv(a_kc), kv(a_vc), kv(a_ks), kvr(a_ks), kv(a_vs), kv(a_kw), kvr(a_kw), kv(a_vw), gate,
    ], axis=1).astype(BF16)
    wb = jnp.concatenate([b_q, b_f, b_i, b_g], axis=1).astype(BF16)

    kr_pad = jnp.pad(c_kr, ((0, 0), (NOPE, LANE - NOPE - ROPE_D)))
    krr_pad = jnp.pad(_rot_cols(c_kr, 1, ROPE_D, 0, ROPE_D), ((0, 0), (NOPE, LANE - NOPE - ROPE_D)))
    wc = jnp.concatenate([c_q, c_kv, kr_pad, krr_pad], axis=1).astype(BF16)
    dq = NOPE + ROPE_D
    wuq = _pad_heads(mla_w_uq[l], HC, dq).astype(BF16)
    wuqr = _pad_heads(_rot_cols(mla_w_uq[l], HC, dq, NOPE, ROPE_D), HC, dq).astype(BF16)
    ukv = mla_w_ukv[l].reshape(KV_RANK, HC, NOPE + VD)
    wuk = _pad_heads(ukv[:, :, :NOPE].reshape(KV_RANK, HC * NOPE), HC, NOPE).astype(BF16)
    wuv = _pad_heads(ukv[:, :, NOPE:].reshape(KV_RANK, HC * VD), HC, VD).astype(BF16)

    w1 = nsa_cmp_w1[l].reshape(2, CMP_L, NSA_DH, CMP_HID)
    w1p = jnp.pad(w1, ((0, 0), (0, 0), (0, LANE - NSA_DH), (0, 0))).reshape(2, CMP_L * LANE, CMP_HID)
    pe = jnp.pad(nsa_cmp_pos[l], ((0, 0), (0, 0), (0, LANE - NSA_DH))).reshape(2, 1, CMP_L * LANE)
    pe8 = jnp.broadcast_to(pe, (2, 8, CMP_L * LANE))
    w2p = jnp.pad(nsa_cmp_w2[l], ((0, 0), (0, 0), (0, LANE - NSA_DH)))

    wba = _pad_rows(w_branch[l, 0], NSA_HEADS, NSA_DH).astype(BF16)
    wbb = w_branch[l, 1].astype(BF16)
    wbc = _pad_rows(w_branch[l, 2], HC, VD).astype(BF16)
    return dict(wa=wa, wb=wb, wc=wc, wuq=wuq, wuqr=wuqr, wuk=wuk, wuv=wuv,
                w1p=w1p.astype(BF16), pe8=pe8.astype(BF16), w2p=w2p.astype(BF16),
                wm=merge.astype(BF16), wba=wba, wbb=wbb, wbc=wbc)


def kernel(x, mem, positions, ln_g, ln_b, ffn_w1, ffn_w3, ffn_w2, w_in, nsa_cmp_pos,
           nsa_cmp_w1, nsa_cmp_w2, hgrn_lb_logits, hgrn_norm_g, mla_q_norm_g, mla_w_uq,
           mla_kv_norm_g, mla_w_ukv, w_branch, w_out, xa_wq, xa_wk, xa_wv, xa_wo):
    b, s, d = x.shape
    depth = ln_g.shape[0]
    n = b * s
    alpha = (2.0 * depth) ** 0.25

    lng = ln_g.reshape(depth, 4, 1, d)
    lnb = ln_b.reshape(depth, 4, 1, d)
    w1 = ffn_w1.astype(BF16)
    w3 = ffn_w3.astype(BF16)
    w2 = ffn_w2.astype(BF16)
    wo = w_out.astype(BF16)
    xq = xa_wq.astype(BF16)
    xk = xa_wk.astype(BF16)
    xv = xa_wv.astype(BF16)
    xo = xa_wo.astype(BF16)
    p_lb = jax.nn.softmax(hgrn_lb_logits.astype(F32), axis=0)
    lower = (jnp.cumsum(p_lb, axis=0) - p_lb[0:1]).reshape(depth, HB, 1, HK)

    cos_a, sin_a = _rope_tables(positions, NSA_ROT, 0)
    cos_c, sin_c = _rope_tables(positions, ROPE_D, NOPE)
    ovl, place = _overlap_matrix(s)
    shift = jnp.asarray(np.eye(N_CMP, k=1, dtype=np.float32)).astype(BF16)
    mem2 = mem.reshape(b * mem.shape[1], d)

    x2 = x.reshape(n, d)
    for l in range(depth):
        w = _layer_weights(l, w_in, nsa_cmp_pos, nsa_cmp_w1, nsa_cmp_w2, mla_w_uq, mla_w_ukv,
                           w_branch)
        x2 = _ffn_ln(x2, w1, w3, w2, lng, lnb, l, 0, 0, alpha)
        x3 = x2.reshape(b, s, d)

        q, qr, kc, vc, ks, vs, kw, vw, gates = _proja(x3, w["wa"], cos_a, sin_a)
        zk = kc.reshape(b, NSA_KV, N_CMP, CMP_HALF)
        zv = vc.reshape(b, NSA_KV, N_CMP, CMP_HALF)
        kcmp, vcmp = _cmp(zk, zv, w["w1p"], w["pe8"], w["w2p"], shift)
        ya = _nsa_attn(q, qr, kcmp, vcmp, ks, vs, kw, vw, gates, ovl, place)

        pb = _proj(x2, w["wb"], F32).reshape(b, s, 4 * HB * HK)
        yb = _hgrn(pb, lower[l], hgrn_norm_g[l].reshape(1, HV))

        mq, mk, mv = _mlap(x3, w["wc"], mla_q_norm_g[l].reshape(1, Q_RANK),
                           mla_kv_norm_g[l].reshape(1, KV_RANK), w["wuq"], w["wuqr"],
                           w["wuk"], w["wuv"], cos_c, sin_c)
        yc = _mla_attn(mq, mk, mv)

        x2 = _merge(x2, ya.reshape(n, NSA_QW), yb.reshape(n, MIX_W), yc.reshape(n, MLA_W),
                    w["wm"], w["wba"], w["wbb"], w["wbc"], wo, lng, lnb, l, alpha)

        xk_l = _proj(mem2, xk[l], BF16).reshape(b, -1, d)
        xv_l = _proj(mem2, xv[l], BF16).reshape(b, -1, d)
        x2 = _xattn(x2.reshape(b, s, d), xq, xk_l, xv_l, xo, lng, lnb, l, alpha).reshape(n, d)

        x2 = _ffn_ln(x2, w1, w3, w2, lng, lnb, l, 1, 3, alpha)
    return x2.reshape(b, s, d)
```

```python
import functools
import math

import numpy as np
import jax
import jax.numpy as jnp
from jax import lax
from jax.experimental import pallas as pl
from jax.experimental.pallas import tpu as pltpu

F32 = jnp.float32
BF16 = jnp.bfloat16

D_MODEL = 1024
MIX_W = D_MODEL // 2
NSA_DH = 64
NSA_HEADS = 8
NSA_KV = 2
NSA_R = 4
NSA_ROT = 16
CMP_L = 32
CMP_D = 16
CMP_HID = 256
SEL_L = 64
N_SEL = 8
WINDOW = 256
HB = 4
HK = 128
HV = 128
HGRN_CHUNK = 64
HGRN_SUB = 16
HC = 8
NOPE = 64
ROPE_D = 32
VD = 64
Q_RANK = 384
KV_RANK = 256
XA_HEADS = 4
XA_DH = D_MODEL // XA_HEADS
D_FF = 2816
ROPE_THETA = 500000.0
LN_EPS = 1e-5
RMS_EPS = 1e-6
NEG = -1e30
BIG = 1e9
F_MIN = 1e-20

LOG2E = 1.4426950408889634
LANE = 128
SEL_LANE0 = NSA_DH
DEN_LANE = 64
VMEM_LIMIT = 56 * 1024 * 1024

IN_SIZES = (512, 128, 128, 128, 128, 128, 128, 24, 512, 512, 512, 512, 384, 256, 32, 3072)
IN_OFF = tuple(int(v) for v in np.concatenate([[0], np.cumsum(IN_SIZES)]))


def _dot(a, b):
    return jnp.dot(a, b, preferred_element_type=F32)


def _dot_nt(a, b):
    return lax.dot_general(a, b, (((1,), (1,)), ((), ())), preferred_element_type=F32)


def _ln(y, g, b):
    mu = jnp.mean(y, axis=-1, keepdims=True)
    yc = y - mu
    var = jnp.mean(yc * yc, axis=-1, keepdims=True)
    return yc * lax.rsqrt(var + LN_EPS) * g + b


def _params(*sem):
    return pltpu.CompilerParams(dimension_semantics=sem, vmem_limit_bytes=VMEM_LIMIT)


def _resident(shape, index_map):
    return pl.BlockSpec(shape, index_map, pipeline_mode=pl.Buffered(1))


FFN_TM = 1024
FFN_TF = 256


def _ffn_kernel(x_ref, w1_ref, w3_ref, w2_ref, g_ref, b_ref, o_ref, acc_ref, xb_ref, *, alpha):
    j = pl.program_id(1)

    @pl.when(j == 0)
    def _():
        acc_ref[...] = jnp.zeros_like(acc_ref)
        xb_ref[...] = x_ref[...].astype(BF16)

    xb = xb_ref[...]
    h1 = _dot(xb, w1_ref[...])
    h3 = _dot(xb, w3_ref[...])
    h = (h1 * jax.nn.sigmoid(h1)) * h3
    acc_ref[...] += _dot(h.astype(BF16), w2_ref[...])

    @pl.when(j == pl.num_programs(1) - 1)
    def _():
        y = alpha * x_ref[...] + 0.5 * acc_ref[...]
        o_ref[...] = _ln(y, g_ref[...], b_ref[...])


def _ffn_ln(x2, w1, w3, w2, lng, lnb, l, which, ln_idx, alpha):
    n = x2.shape[0]
    tm = min(FFN_TM, n)
    grid = (n // tm, D_FF // FFN_TF)
    return pl.pallas_call(
        functools.partial(_ffn_kernel, alpha=alpha),
        grid=grid,
        in_specs=[
            pl.BlockSpec((tm, D_MODEL), lambda m, j: (m, 0)),
            pl.BlockSpec((None, None, D_MODEL, FFN_TF), lambda m, j: (l, which, 0, j)),
            pl.BlockSpec((None, None, D_MODEL, FFN_TF), lambda m, j: (l, which, 0, j)),
            pl.BlockSpec((None, None, FFN_TF, D_MODEL), lambda m, j: (l, which, j, 0)),
            pl.BlockSpec((None, None, 1, D_MODEL), lambda m, j: (l, ln_idx, 0, 0)),
            pl.BlockSpec((None, None, 1, D_MODEL), lambda m, j: (l, ln_idx, 0, 0)),
        ],
        out_specs=pl.BlockSpec((tm, D_MODEL), lambda m, j: (m, 0)),
        out_shape=jax.ShapeDtypeStruct((n, D_MODEL), F32),
        scratch_shapes=[pltpu.VMEM((tm, D_MODEL), F32), pltpu.VMEM((tm, D_MODEL), BF16)],
        compiler_params=_params("parallel", "arbitrary"),
        name="ffn_ln",
    )(x2, w1, w3, w2, lng, lnb)


def _proj_kernel(x_ref, w_ref, o_ref):
    o_ref[...] = _dot(x_ref[...].astype(BF16), w_ref[...]).astype(o_ref.dtype)


def _proj(x2, w, out_dtype, tm=512, tn=512):
    n, k = x2.shape
    c = w.shape[1]
    tm = min(tm, n)
    tn = min(tn, c)
    return pl.pallas_call(
        _proj_kernel,
        grid=(n // tm, c // tn),
        in_specs=[pl.BlockSpec((tm, k), lambda m, j: (m, 0)),
                  pl.BlockSpec((k, tn), lambda m, j: (0, j))],
        out_specs=pl.BlockSpec((tm, tn), lambda m, j: (m, j)),
        out_shape=jax.ShapeDtypeStruct((n, c), out_dtype),
        compiler_params=_params("parallel", "arbitrary"),
        name="proj",
    )(x2, w)


NSA_QW = NSA_HEADS * LANE
NSA_KW = NSA_KV * LANE
A_Q, A_QROT = 0, NSA_QW
A_KC = 2 * NSA_QW
A_VC = A_KC + NSA_KW
A_KS = A_VC + NSA_KW
A_KSROT = A_KS + NSA_KW
A_VS = A_KSROT + NSA_KW
A_KWN = A_VS + NSA_KW
A_KWROT = A_KWN + NSA_KW
A_VW = A_KWROT + NSA_KW
A_GATE = A_VW + NSA_KW
A_COLS = A_GATE + NSA_KW
PROJA_TM = 512


def _proja_kernel(x_ref, w_ref, cos_ref, sin_ref,
                  q_ref, qr_ref, kc_ref, vc_ref, ks_ref, vs_ref, kw_ref, vw_ref, g_ref):
    xb = x_ref[...].astype(BF16)
    cos = cos_ref[...]
    sin = sin_ref[...]
    cos2 = jnp.concatenate([cos, cos], axis=1)
    sin2 = jnp.concatenate([sin, sin], axis=1)

    def mm(c0, width):
        return _dot(xb, w_ref[:, c0:c0 + width])

    qs = NSA_DH ** -0.5 * LOG2E
    for p in range(NSA_QW // NSA_KW):
        c = p * NSA_KW
        y = mm(A_Q + c, NSA_KW)
        yr = mm(A_QROT + c, NSA_KW)
        q_ref[:, c:c + NSA_KW] = (y * qs).astype(BF16)
        qr_ref[:, c:c + NSA_KW] = ((y * cos2 + yr * sin2) * qs).astype(BF16)

    def put(ref, y):
        for g in range(NSA_KV):
            ref[g] = y[:, g * LANE:(g + 1) * LANE].astype(BF16)

    tm = x_ref.shape[0]
    lane2 = lax.broadcasted_iota(jnp.int32, (tm, NSA_KW), 1) % LANE
    tok = pl.program_id(1) * tm + lax.broadcasted_iota(jnp.int32, (tm, NSA_KW), 0)
    blk_flag = jnp.where(lane2 - SEL_LANE0 == tok // SEL_L, 1.0, 0.0)
    den_flag = jnp.where(lane2 == DEN_LANE, 1.0, 0.0)

    put(kc_ref, mm(A_KC, NSA_KW))
    put(vc_ref, mm(A_VC, NSA_KW))
    put(ks_ref, mm(A_KS, NSA_KW) * cos2 + mm(A_KSROT, NSA_KW) * sin2 + blk_flag)
    put(vs_ref, mm(A_VS, NSA_KW) + den_flag)
    put(kw_ref, mm(A_KWN, NSA_KW) * cos2 + mm(A_KWROT, NSA_KW) * sin2)
    put(vw_ref, mm(A_VW, NSA_KW) + den_flag)
    g_ref[...] = mm(A_GATE, NSA_KW)


def _proja(x3, wa, cos_a, sin_a):
    b, s, _ = x3.shape
    tm = PROJA_TM
    kv_shape = jax.ShapeDtypeStruct((b, NSA_KV, s, LANE), BF16)
    kv_spec = pl.BlockSpec((None, NSA_KV, tm, LANE), lambda bi, m: (bi, 0, m, 0))
    return pl.pallas_call(
        _proja_kernel,
        grid=(b, s // tm),
        in_specs=[
            pl.BlockSpec((None, tm, D_MODEL), lambda bi, m: (bi, m, 0)),
            _resident((D_MODEL, A_COLS), lambda bi, m: (0, 0)),
            pl.BlockSpec((None, tm, LANE), lambda bi, m: (bi, m, 0)),
            pl.BlockSpec((None, tm, LANE), lambda bi, m: (bi, m, 0)),
        ],
        out_specs=[
            pl.BlockSpec((None, tm, NSA_QW), lambda bi, m: (bi, m, 0)),
            pl.BlockSpec((None, tm, NSA_QW), lambda bi, m: (bi, m, 0)),
            kv_spec, kv_spec, kv_spec, kv_spec, kv_spec, kv_spec,
            pl.BlockSpec((None, tm, NSA_KW), lambda bi, m: (bi, m, 0)),
        ],
        out_shape=[
            jax.ShapeDtypeStruct((b, s, NSA_QW), BF16),
            jax.ShapeDtypeStruct((b, s, NSA_QW), BF16),
            kv_shape, kv_shape, kv_shape, kv_shape, kv_shape, kv_shape,
            jax.ShapeDtypeStruct((b, s, NSA_KW), F32),
        ],
        compiler_params=_params("parallel", "parallel"),
        name="nsa_proj",
    )(x3, wa, cos_a, sin_a)


N_CMP = 128
CMP_HALF = CMP_D * LANE


def _gelu_tanh(x):
    return 0.5 * x * (1.0 + jnp.tanh(0.7978845608028654 * (x + 0.044715 * x * x * x)))


def _cmp_kernel(zk_ref, zv_ref, w1_ref, pe_ref, w2_ref, shift_ref, ok_ref, ov_ref):
    shift = shift_ref[...]
    for which, (z_ref, o_ref) in enumerate(((zk_ref, ok_ref), (zv_ref, ov_ref))):
        z = z_ref[...]
        zn = _dot(shift, z).astype(BF16)
        bias = _dot(pe_ref[which], w1_ref[which])[0:1]
        pre = (_dot(z, w1_ref[which, 0:CMP_HALF, :])
               + _dot(zn, w1_ref[which, CMP_HALF:2 * CMP_HALF, :]) + bias)
        h = _gelu_tanh(pre)
        o_ref[...] = _dot(h.astype(BF16), w2_ref[which]).astype(BF16)


def _cmp(zk, zv, w1p, pe8, w2p, shift):
    b = zk.shape[0]
    z_spec = pl.BlockSpec((None, None, N_CMP, CMP_HALF), lambda bi, g: (bi, g, 0, 0))
    o_spec = pl.BlockSpec((None, None, N_CMP, LANE), lambda bi, g: (bi, g, 0, 0))
    o_shape = jax.ShapeDtypeStruct((b, NSA_KV, N_CMP, LANE), BF16)
    return pl.pallas_call(
        _cmp_kernel,
        grid=(b, NSA_KV),
        in_specs=[
            z_spec, z_spec,
            _resident((2, 2 * CMP_HALF, CMP_HID), lambda bi, g: (0, 0, 0)),
            _resident((2, 8, 2 * CMP_HALF), lambda bi, g: (0, 0, 0)),
            _resident((2, CMP_HID, LANE), lambda bi, g: (0, 0, 0)),
            _resident((N_CMP, N_CMP), lambda bi, g: (0, 0)),
        ],
        out_specs=[o_spec, o_spec],
        out_shape=[o_shape, o_shape],
        compiler_params=_params("parallel", "parallel"),
        name="nsa_cmp",
    )(zk, zv, w1p, pe8, w2p, shift)


NSA_TQ = 256
NSA_TK = 256
N_SLC = 32
WIN_SLAB = WINDOW + NSA_TQ


def _nsa_kernel(q_ref, qr_ref, kc_ref, vc_ref, ks_ref, vs_ref, kw_ref, vw_ref, gt_ref, ovl_ref,
                place_ref, o_ref):
    tq, tk, r_heads = NSA_TQ, NSA_TK, NSA_R
    i = pl.program_id(2)
    t0 = i * tq
    tpos = t0 + lax.broadcasted_iota(jnp.int32, (tq, 1), 0)
    lane = lax.broadcasted_iota(jnp.int32, (1, LANE), 1)

    def stack(ref):
        return jnp.concatenate([ref[:, r * LANE:(r + 1) * LANE] for r in range(r_heads)], axis=0)

    gt = jax.nn.sigmoid(gt_ref[...])

    def gate(c):
        return jnp.stack([gt[:, 3 * r + c:3 * r + c + 1] for r in range(r_heads)], axis=0)

    q4w = stack(qr_ref)
    w0 = pl.multiple_of(jnp.maximum(t0 - WINDOW, 0), LANE)
    kw = kw_ref[pl.ds(w0, WIN_SLAB), :]
    vw = vw_ref[pl.ds(w0, WIN_SLAB), :]
    kpos = w0 + lax.broadcasted_iota(jnp.int32, (1, WIN_SLAB), 1)
    w_bias = jnp.where((kpos <= tpos) & (kpos > tpos - WINDOW), 0.0, NEG)
    s = _dot_nt(q4w, kw).reshape(r_heads, tq, WIN_SLAB) + w_bias[None]
    p = jnp.exp2(s - jnp.max(s, axis=-1, keepdims=True))
    o_win = _dot(p.reshape(r_heads * tq, WIN_SLAB).astype(BF16), vw).reshape(r_heads, tq, LANE)
    o_part = gate(2) * (o_win / o_win[:, :, DEN_LANE:DEN_LANE + 1])

    cmask = ((lane * CMP_D + (CMP_L - 1) <= tpos) & (lane < N_CMP - 1))[None]
    s = _dot_nt(stack(q_ref), kc_ref[...]).reshape(r_heads, tq, N_CMP)
    s = jnp.where(cmask, s, NEG)
    e = jnp.where(cmask, jnp.exp2(s - jnp.max(s, axis=-1, keepdims=True)), 0.0)
    den = jnp.sum(e, axis=-1, keepdims=True)
    p = e / jnp.where(den > 0.0, den, 1.0)
    psum = jnp.sum(p, axis=0)
    o_cmp = _dot(p.reshape(r_heads * tq, N_CMP).astype(BF16), vc_ref[...])
    o_part = o_part + gate(0) * o_cmp.reshape(r_heads, tq, LANE)

    imp = lax.dot_general(ovl_ref[...], psum, (((1,), (1,)), ((), ())),
                          precision=lax.Precision.HIGHEST, preferred_element_type=F32)
    blk = lax.broadcasted_iota(jnp.int32, (N_SLC, 1), 0)
    tpos_l = t0 + lax.broadcasted_iota(jnp.int32, (1, tq), 1)
    cur = tpos_l // SEL_L
    valid = blk * SEL_L <= tpos_l
    forced = (blk == 0) | (blk == cur) | (blk == cur - 1)
    score = jnp.where(valid & forced, BIG, jnp.where(valid, imp, -BIG))
    beats = []
    for j in range(N_SLC):
        sj = score[j:j + 1, :]
        beats.append(jnp.where((sj > score) | ((sj == score) & (j < blk)), 1.0, 0.0))
    while len(beats) > 1:
        beats = [a + b for a, b in zip(beats[0::2], beats[1::2])]
    sel_t = jnp.where((beats[0] < N_SEL) & valid, 1.0, 0.0)
    sel_q = lax.dot_general(sel_t, place_ref[...], (((0,), (0,)), ((), ())),
                            preferred_element_type=F32)
    in_flags = jnp.where((lane >= SEL_LANE0) & (lane < SEL_LANE0 + N_SLC), 1.0, 0.0)
    q_bias = (sel_q - in_flags) * (-NEG)
    q4 = (q4w.astype(F32).reshape(r_heads, tq, LANE) + q_bias[None]).astype(BF16)
    q4 = q4.reshape(r_heads * tq, LANE)

    def sel_step(c, carry, causal):
        m, acc = carry
        k0 = pl.multiple_of(c * tk, tk)
        k = ks_ref[pl.ds(k0, tk), :]
        v = vs_ref[pl.ds(k0, tk), :]
        s = _dot_nt(q4, k).reshape(r_heads, tq, tk)
        if causal:
            kpos = k0 + lax.broadcasted_iota(jnp.int32, (1, tk), 1)
            s = jnp.where((kpos <= tpos)[None], s, NEG)
        m_new = jnp.maximum(m, jnp.max(s, axis=-1, keepdims=True))
        p = jnp.exp2(s - m_new)
        pv = _dot(p.reshape(r_heads * tq, tk).astype(BF16), v).reshape(r_heads, tq, LANE)
        return m_new, jnp.exp2(m - m_new) * acc + pv

    last = (t0 + tq - 1) // tk
    carry = (jnp.full((r_heads, tq, 1), NEG, F32), jnp.zeros((r_heads, tq, LANE), F32))
    carry = lax.fori_loop(0, last, functools.partial(sel_step, causal=False), carry)
    _, acc_s = sel_step(last, carry, True)
    o = o_part + gate(1) * (acc_s / acc_s[:, :, DEN_LANE:DEN_LANE + 1])
    for r in range(r_heads):
        o_ref[:, r * LANE:(r + 1) * LANE] = o[r].astype(BF16)


def _nsa_attn(q, qr, kcmp, vcmp, ks, vs, kw, vw, gates, ovl, place):
    b, s, _ = q.shape
    tq = NSA_TQ
    gw = NSA_R * LANE
    q_spec = pl.BlockSpec((None, tq, gw), lambda bi, g, i: (bi, i, g))
    c_spec = pl.BlockSpec((None, None, N_CMP, LANE), lambda bi, g, i: (bi, g, 0, 0))
    kv_spec = pl.BlockSpec((None, None, s, LANE), lambda bi, g, i: (bi, g, 0, 0))
    return pl.pallas_call(
        _nsa_kernel,
        grid=(b, NSA_KV, s // tq),
        in_specs=[q_spec, q_spec, c_spec, c_spec, kv_spec, kv_spec, kv_spec, kv_spec,
                  pl.BlockSpec((None, tq, LANE), lambda bi, g, i: (bi, i, g)),
                  _resident((N_SLC, N_CMP), lambda bi, g, i: (0, 0)),
                  _resident((N_SLC, LANE), lambda bi, g, i: (0, 0))],
        out_specs=pl.BlockSpec((None, tq, gw), lambda bi, g, i: (bi, i, g)),
        out_shape=jax.ShapeDtypeStruct((b, s, NSA_QW), BF16),
        compiler_params=_params("parallel", "parallel", "arbitrary"),
        name="nsa_attn",
    )(q, qr, kcmp, vcmp, ks, vs, kw, vw, gates, ovl, place)


def _hgrn_kernel(q_ref, z_ref, v_ref, go_ref, lb_ref, ng_ref, o_ref, st_ref):
    s_len = q_ref.shape[0]
    c_len, sub_len = HGRN_CHUNK, HGRN_SUB
    nc = s_len // c_len
    n_sub = c_len // sub_len
    q = q_ref[...]
    z = z_ref[...]
    v = v_ref[...]
    lb = lb_ref[...]
    t = lax.broadcasted_iota(jnp.int32, (s_len, 1), 0)
    t_sub = t % sub_len
    t_chunk = t % c_len

    f = lb + (1.0 - lb) * jax.nn.sigmoid(z)
    lf = jnp.log(jnp.maximum(f, F_MIN))
    k = (1.0 - lb) * jax.nn.sigmoid(-z)

    bs = lf
    step = 1
    while step < sub_len:
        bs = bs + jnp.where(t_sub >= step, pltpu.roll(bs, step, axis=0), 0.0)
        step *= 2
    sub_id = t_chunk // sub_len
    sub_id3 = lax.broadcasted_iota(jnp.int32, (nc, c_len, 1), 1) // sub_len
    bs3 = bs.reshape(nc, c_len, HK)
    b3 = bs3
    for i in range(1, n_sub):
        total = bs3[:, sub_len * i - 1:sub_len * i, :]
        b3 = b3 + jnp.where(sub_id3 >= i, total, 0.0)
    b = b3.reshape(s_len, HK)
    k3 = k.reshape(nc, c_len, HK)
    vb3 = v.astype(BF16).reshape(nc, c_len, HV)

    lq = q * jnp.exp(bs)
    lhs, rhs = [], []
    for i in range(1, n_sub):
        ref_b = b3[:, sub_len * i - 1:sub_len * i, :]
        e = jnp.exp(jnp.minimum(ref_b - b3, 0.0)).reshape(s_len, HK)
        rhs.append(jnp.where(sub_id < i, k * e, 0.0).astype(BF16))
        lhs.append(jnp.where(sub_id == i, lq, 0.0).astype(BF16))
    width = (n_sub - 1) * HK
    lhs3 = jnp.concatenate(lhs, axis=1).reshape(nc, c_len, width)
    rhs3 = jnp.concatenate(rhs, axis=1).reshape(nc, c_len, width)
    a_off = jnp.einsum('ctk,csk->cts', lhs3, rhs3, preferred_element_type=F32)
    o = jnp.einsum('cts,csd->ctd', a_off.astype(BF16), vb3,
                   preferred_element_type=F32).reshape(s_len, HV)

    o = o + jnp.sum(q * k, axis=-1, keepdims=True) * v
    for d in range(1, sub_len):
        e = jnp.exp(jnp.where(t_sub >= d, bs - pltpu.roll(bs, d, axis=0), NEG))
        w = jnp.sum(q * pltpu.roll(k, d, axis=0) * e, axis=-1, keepdims=True)
        o = o + w * pltpu.roll(v, d, axis=0)

    b_last = b3[:, c_len - 1:c_len, :]
    kd3 = (k3 * jnp.exp(b_last - b3)).astype(BF16)
    upd = jnp.einsum('csv,csk->cvk', vb3, kd3, preferred_element_type=F32)
    dec = jnp.exp(b_last)
    state = jnp.zeros((HV, HK), F32)
    for c in range(nc):
        st_ref[c] = state.astype(BF16)
        state = state * dec[c] + upd[c]
    qe3 = (q * jnp.exp(b)).astype(BF16).reshape(nc, c_len, HK)
    o = o + jnp.einsum('ctk,cvk->ctv', qe3, st_ref[...],
                       preferred_element_type=F32).reshape(s_len, HV)

    o = o * lax.rsqrt(jnp.mean(o * o, axis=-1, keepdims=True) + RMS_EPS) * ng_ref[...]
    go = go_ref[...]
    o_ref[...] = (o * (go * jax.nn.sigmoid(go))).astype(BF16)


def _hgrn(pb, lb, ng):
    b, s, _ = pb.shape

    def col(part):
        return pl.BlockSpec((None, s, HK), lambda bi, h: (bi, 0, part * HB + h))

    return pl.pallas_call(
        _hgrn_kernel,
        grid=(b, HB),
        in_specs=[col(0), col(1), col(2), col(3),
                  pl.BlockSpec((None, 1, HK), lambda bi, h: (h, 0, 0)),
                  pl.BlockSpec((1, HV), lambda bi, h: (0, 0))],
        out_specs=pl.BlockSpec((None, s, HV), lambda bi, h: (bi, 0, h)),
        out_shape=jax.ShapeDtypeStruct((b, s, HB * HV), BF16),
        scratch_shapes=[pltpu.VMEM((s // HGRN_CHUNK, HV, HK), BF16)],
        compiler_params=_params("parallel", "parallel"),
        name="hgrn",
    )(pb, pb, pb, pb, lb, ng)


MLA_W = HC * LANE
C_CQ, C_CKV, C_KR, C_KRROT = 0, Q_RANK, Q_RANK + KV_RANK, Q_RANK + KV_RANK + LANE
C_COLS = C_KRROT + LANE
MLAP_TM = 512


def _rms(x, g):
    return x * lax.rsqrt(jnp.mean(x * x, axis=-1, keepdims=True) + RMS_EPS) * g


def _mlap_kernel(x_ref, wc_ref, qg_ref, kvg_ref, wuq_ref, wuqr_ref, wuk_ref, wuv_ref,
                 cos_ref, sin_ref, q_ref, k_ref, v_ref):
    xb = x_ref[...].astype(BF16)
    c = _dot(xb, wc_ref[...])
    cos = cos_ref[...]
    sin = sin_ref[...]
    nq = _rms(c[:, C_CQ:C_CKV], qg_ref[...]).astype(BF16)
    nkv = _rms(c[:, C_CKV:C_KR], kvg_ref[...]).astype(BF16)
    k_pe = c[:, C_KR:C_KRROT] * cos + c[:, C_KRROT:C_COLS] * sin
    qs = (NOPE + ROPE_D) ** -0.5 * LOG2E
    cos_q = cos * qs
    sin_q = sin * qs
    den_flag = jnp.where(lax.broadcasted_iota(jnp.int32, (1, LANE), 1) == DEN_LANE, 1.0, 0.0)
    for h in range(HC):
        hs = slice(h * LANE, (h + 1) * LANE)
        q_ref[:, hs] = (_dot(nq, wuq_ref[:, hs]) * cos_q
                        + _dot(nq, wuqr_ref[:, hs]) * sin_q).astype(BF16)
        k_ref[:, hs] = (_dot(nkv, wuk_ref[:, hs]) + k_pe).astype(BF16)
        v_ref[:, hs] = (_dot(nkv, wuv_ref[:, hs]) + den_flag).astype(BF16)


def _mlap(x3, wc, qg, kvg, wuq, wuqr, wuk, wuv, cos_c, sin_c):
    b, s, _ = x3.shape
    tm = MLAP_TM
    o_spec = pl.BlockSpec((None, tm, MLA_W), lambda bi, m: (bi, m, 0))
    o_shape = jax.ShapeDtypeStruct((b, s, MLA_W), BF16)
    t_spec = pl.BlockSpec((None, tm, LANE), lambda bi, m: (bi, m, 0))
    return pl.pallas_call(
        _mlap_kernel,
        grid=(b, s // tm),
        in_specs=[
            pl.BlockSpec((None, tm, D_MODEL), lambda bi, m: (bi, m, 0)),
            _resident((D_MODEL, C_COLS), lambda bi, m: (0, 0)),
            _resident((1, Q_RANK), lambda bi, m: (0, 0)),
            _resident((1, KV_RANK), lambda bi, m: (0, 0)),
            _resident((Q_RANK, MLA_W), lambda bi, m: (0, 0)),
            _resident((Q_RANK, MLA_W), lambda bi, m: (0, 0)),
            _resident((KV_RANK, MLA_W), lambda bi, m: (0, 0)),
            _resident((KV_RANK, MLA_W), lambda bi, m: (0, 0)),
            t_spec, t_spec,
        ],
        out_specs=[o_spec, o_spec, o_spec],
        out_shape=[o_shape, o_shape, o_shape],
        compiler_params=_params("parallel", "parallel"),
        name="mla_proj",
    )(x3, wc, qg, kvg, wuq, wuqr, wuk, wuv, cos_c, sin_c)


MLA_TQ = 256
MLA_TK = 256


MLA_HPS = 4


def _mla_attn_kernel(q_ref, k_ref, v_ref, o_ref):
    tq, tk = MLA_TQ, MLA_TK
    i = pl.program_id(2)
    t0 = i * tq
    tpos = t0 + lax.broadcasted_iota(jnp.int32, (tq, 1), 0)
    heads = [slice(h * LANE, (h + 1) * LANE) for h in range(MLA_HPS)]
    qs = [q_ref[:, hs] for hs in heads]

    def step(c, carry, causal):
        k0 = pl.multiple_of(c * tk, tk)
        if causal:
            keep = k0 + lax.broadcasted_iota(jnp.int32, (1, tk), 1) <= tpos
        out = []
        for h, hs in enumerate(heads):
            m, acc = carry[h]
            s = _dot_nt(qs[h], k_ref[pl.ds(k0, tk), hs])
            if causal:
                s = jnp.where(keep, s, NEG)
            m_new = jnp.maximum(m, jnp.max(s, axis=-1, keepdims=True))
            p = jnp.exp2(s - m_new)
            pv = _dot(p.astype(BF16), v_ref[pl.ds(k0, tk), hs])
            out.append((m_new, jnp.exp2(m - m_new) * acc + pv))
        return tuple(out)

    last = (t0 + tq - 1) // tk
    carry = tuple((jnp.full((tq, 1), NEG, F32), jnp.zeros((tq, LANE), F32)) for _ in heads)
    carry = lax.fori_loop(0, last, functools.partial(step, causal=False), carry)
    carry = step(last, carry, True)
    for (_, acc), hs in zip(carry, heads):
        o_ref[:, hs] = (acc / acc[:, DEN_LANE:DEN_LANE + 1]).astype(BF16)


def _mla_attn(q, k, v):
    b, s, _ = q.shape
    tq = MLA_TQ
    gw = MLA_HPS * LANE
    return pl.pallas_call(
        _mla_attn_kernel,
        grid=(b, HC // MLA_HPS, s // tq),
        in_specs=[pl.BlockSpec((None, tq, gw), lambda bi, h, i: (bi, i, h)),
                  pl.BlockSpec((None, s, gw), lambda bi, h, i: (bi, 0, h)),
                  pl.BlockSpec((None, s, gw), lambda bi, h, i: (bi, 0, h))],
        out_specs=pl.BlockSpec((None, tq, gw), lambda bi, h, i: (bi, i, h)),
        out_shape=jax.ShapeDtypeStruct((b, s, MLA_W), BF16),
        compiler_params=_params("parallel", "parallel", "arbitrary"),
        name="mla_attn",
    )(q, k, v)


MERGE_TM = 512


def _merge_kernel(x_ref, ya_ref, yb_ref, yc_ref, wm_ref, wa_ref, wb_ref, wc_ref, wo_ref,
                  g_ref, b_ref, o_ref, *, alpha):
    x = x_ref[...]
    xb = x.astype(BF16)
    mixed = jnp.zeros(x.shape, F32)
    for idx, (y_ref, w_ref) in enumerate(((ya_ref, wa_ref), (yb_ref, wb_ref), (yc_ref, wc_ref))):
        gate = jax.nn.sigmoid(_dot(xb, wm_ref[:, idx * D_MODEL:(idx + 1) * D_MODEL]))
        mixed = mixed + gate * _dot(y_ref[...], w_ref[...])
    y = alpha * x + _dot(mixed.astype(BF16), wo_ref[...])
    o_ref[...] = _ln(y, g_ref[...], b_ref[...])


def _merge(x2, ya, yb, yc, wm, wa, wb, wc, wo, lng, lnb, l, alpha):
    n = x2.shape[0]
    tm = min(MERGE_TM, n)

    def rows(width):
        return pl.BlockSpec((tm, width), lambda m: (m, 0))

    return pl.pallas_call(
        functools.partial(_merge_kernel, alpha=alpha),
        grid=(n // tm,),
        in_specs=[rows(D_MODEL), rows(NSA_QW), rows(MIX_W), rows(MLA_W),
                  _resident((D_MODEL, 3 * D_MODEL), lambda m: (0, 0)),
                  _resident((NSA_QW, D_MODEL), lambda m: (0, 0)),
                  _resident((MIX_W, D_MODEL), lambda m: (0, 0)),
                  _resident((MLA_W, D_MODEL), lambda m: (0, 0)),
                  _resident((None, D_MODEL, D_MODEL), lambda m: (l, 0, 0)),
                  _resident((None, None, 1, D_MODEL), lambda m: (l, 1, 0, 0)),
                  _resident((None, None, 1, D_MODEL), lambda m: (l, 1, 0, 0))],
        out_specs=rows(D_MODEL),
        out_shape=jax.ShapeDtypeStruct((n, D_MODEL), F32),
        compiler_params=_params("parallel"),
        name="merge",
    )(x2, ya, yb, yc, wm, wa, wb, wc, wo, lng, lnb)


XA_TM = 512


def _xattn_kernel(x_ref, wq_ref, k_ref, v_ref, wo_ref, g_ref, b_ref, o_ref, *, alpha):
    x = x_ref[...]
    xb = x.astype(BF16)
    scale = XA_DH ** -0.5
    heads = []
    for h in range(XA_HEADS):
        hs = slice(h * XA_DH, (h + 1) * XA_DH)
        qh = _dot(xb, wq_ref[:, hs]).astype(BF16)
        s = _dot_nt(qh, k_ref[:, hs]) * scale
        e = jnp.exp(s - jnp.max(s, axis=-1, keepdims=True))
        p = e / jnp.sum(e, axis=-1, keepdims=True)
        heads.append(_dot(p.astype(BF16), v_ref[:, hs]).astype(BF16))
    att = jnp.concatenate(heads, axis=1)
    y = alpha * x + _dot(att, wo_ref[...])
    o_ref[...] = _ln(y, g_ref[...], b_ref[...])


def _xattn(x3, wq, k, v, wo, lng, lnb, l, alpha):
    b, s, _ = x3.shape
    m_len = k.shape[1]
    tm = XA_TM
    return pl.pallas_call(
        functools.partial(_xattn_kernel, alpha=alpha),
        grid=(b, s // tm),
        in_specs=[pl.BlockSpec((None, tm, D_MODEL), lambda bi, m: (bi, m, 0)),
                  _resident((None, D_MODEL, D_MODEL), lambda bi, m: (l, 0, 0)),
                  pl.BlockSpec((None, m_len, D_MODEL), lambda bi, m: (bi, 0, 0)),
                  pl.BlockSpec((None, m_len, D_MODEL), lambda bi, m: (bi, 0, 0)),
                  _resident((None, D_MODEL, D_MODEL), lambda bi, m: (l, 0, 0)),
                  _resident((None, None, 1, D_MODEL), lambda bi, m: (l, 2, 0, 0)),
                  _resident((None, None, 1, D_MODEL), lambda bi, m: (l, 2, 0, 0))],
        out_specs=pl.BlockSpec((None, tm, D_MODEL), lambda bi, m: (bi, m, 0)),
        out_shape=jax.ShapeDtypeStruct((b, s, D_MODEL), F32),
        compiler_params=_params("parallel", "parallel"),
        name="xattn",
    )(x3, wq, k, v, wo, lng, lnb)


def _pad_heads(w, n_heads, dh):
    k = w.shape[0]
    w = w.reshape(k, n_heads, dh)
    return jnp.pad(w, ((0, 0), (0, 0), (0, LANE - dh))).reshape(k, n_heads * LANE)


def _rot_cols(w, n_heads, dh, off, rot):
    k = w.shape[0]
    half = rot // 2
    w = w.reshape(k, n_heads, dh)
    x1 = w[:, :, off:off + half]
    x2 = w[:, :, off + half:off + rot]
    out = jnp.zeros_like(w)
    out = out.at[:, :, off:off + half].set(-x2)
    out = out.at[:, :, off + half:off + rot].set(x1)
    return out.reshape(k, n_heads * dh)


def _pad_rows(w, n_heads, dh):
    n = w.shape[1]
    w = w.reshape(n_heads, dh, n)
    return jnp.pad(w, ((0, 0), (0, LANE - dh), (0, 0))).reshape(n_heads * LANE, n)


def _rope_tables(positions, rot, off):
    half = rot // 2
    inv = ROPE_THETA ** (-jnp.arange(half, dtype=F32) / half)
    ang = positions.astype(F32)[..., None] * inv
    cos, sin = jnp.cos(ang), jnp.sin(ang)
    shape = positions.shape
    cos_t = jnp.concatenate([jnp.ones(shape + (off,), F32), cos, cos,
                             jnp.ones(shape + (LANE - off - rot,), F32)], axis=-1)
    sin_t = jnp.concatenate([jnp.zeros(shape + (off,), F32), sin, sin,
                             jnp.zeros(shape + (LANE - off - rot,), F32)], axis=-1)
    return cos_t, sin_t


def _overlap_matrix(s_len):
    n_cmp = (s_len - CMP_L) // CMP_D + 1
    n_slc = s_len // SEL_L
    start = np.arange(n_cmp) * CMP_D
    j = np.arange(n_slc)
    ov = np.clip(np.minimum(start[:, None] + CMP_L, (j[None, :] + 1) * SEL_L)
                 - np.maximum(start[:, None], j[None, :] * SEL_L), 0, None) / CMP_L
    out = np.zeros((N_SLC, N_CMP), np.float32)
    out[:n_slc, :n_cmp] = ov.T
    place = np.zeros((N_SLC, LANE), np.float32)
    place[np.arange(N_SLC), SEL_LANE0 + np.arange(N_SLC)] = 1.0
    return jnp.asarray(out), jnp.asarray(place)


def _layer_weights(l, w_in, nsa_cmp_pos, nsa_cmp_w1, nsa_cmp_w2, mla_w_uq, mla_w_ukv, w_branch):
    wl = w_in[l]
    part = [wl[:, IN_OFF[i]:IN_OFF[i + 1]] for i in range(len(IN_SIZES))]
    (a_q, a_kc, a_vc, a_ks, a_vs, a_kw, a_vw, a_gate, b_q, b_f, b_i, b_g,
     c_q, c_kv, c_kr, merge) = part
    kv = lambda w: _pad_heads(w, NSA_KV, NSA_DH)
    kvr = lambda w: _pad_heads(_rot_cols(w, NSA_KV, NSA_DH, 0, NSA_ROT), NSA_KV, NSA_DH)
    gate = a_gate.reshape(D_MODEL, NSA_KV, NSA_R * 3)
    gate = jnp.pad(gate, ((0, 0), (0, 0), (0, LANE - NSA_R * 3))).reshape(D_MODEL, NSA_KW)
    wa = jnp.concatenate([
        _pad_heads(a_q, NSA_HEADS, NSA_DH),
        _pad_heads(_rot_cols(a_q, NSA_HEADS, NSA_DH, 0, NSA_ROT), NSA_HEADS, NSA_DH),
        kv(a_kc), kv(a_vc), kv(a_ks), kvr(a_ks), kv(a_vs), kv(a_kw), kvr(a_kw), kv(a_vw), gate,
    ], axis=1).astype(BF16)
    wb = jnp.concatenate([b_q, b_f, b_i, b_g], axis=1).astype(BF16)

    kr_pad = jnp.pad(c_kr, ((0, 0), (NOPE, LANE - NOPE - ROPE_D)))
    krr_pad = jnp.pad(_rot_cols(c_kr, 1, ROPE_D, 0, ROPE_D), ((0, 0), (NOPE, LANE - NOPE - ROPE_D)))
    wc = jnp.concatenate([c_q, c_kv, kr_pad, krr_pad], axis=1).astype(BF16)
    dq = NOPE + ROPE_D
    wuq = _pad_heads(mla_w_uq[l], HC, dq).astype(BF16)
    wuqr = _pad_heads(_rot_cols(mla_w_uq[l], HC, dq, NOPE, ROPE_D), HC, dq).astype(BF16)
    ukv = mla_w_ukv[l].reshape(KV_RANK, HC, NOPE + VD)
    wuk = _pad_heads(ukv[:, :, :NOPE].reshape(KV_RANK, HC * NOPE), HC, NOPE).astype(BF16)
    wuv = _pad_heads(ukv[:, :, NOPE:].reshape(KV_RANK, HC * VD), HC, VD).astype(BF16)

    w1 = nsa_cmp_w1[l].reshape(2, CMP_L, NSA_DH, CMP_HID)
    w1p = jnp.pad(w1, ((0, 0), (0, 0), (0, LANE - NSA_DH), (0, 0))).reshape(2, CMP_L * LANE, CMP_HID)
    pe = jnp.pad(nsa_cmp_pos[l], ((0, 0), (0, 0), (0, LANE - NSA_DH))).reshape(2, 1, CMP_L * LANE)
    pe8 = jnp.broadcast_to(pe, (2, 8, CMP_L * LANE))
    w2p = jnp.pad(nsa_cmp_w2[l], ((0, 0), (0, 0), (0, LANE - NSA_DH)))

    wba = _pad_rows(w_branch[l, 0], NSA_HEADS, NSA_DH).astype(BF16)
    wbb = w_branch[l, 1].astype(BF16)
    wbc = _pad_rows(w_branch[l, 2], HC, VD).astype(BF16)
    return dict(wa=wa, wb=wb, wc=wc, wuq=wuq, wuqr=wuqr, wuk=wuk, wuv=wuv,
                w1p=w1p.astype(BF16), pe8=pe8.astype(BF16), w2p=w2p.astype(BF16),
                wm=merge.astype(BF16), wba=wba, wbb=wbb, wbc=wbc)


def kernel(x, mem, positions, ln_g, ln_b, ffn_w1, ffn_w3, ffn_w2, w_in, nsa_cmp_pos,
           nsa_cmp_w1, nsa_cmp_w2, hgrn_lb_logits, hgrn_norm_g, mla_q_norm_g, mla_w_uq,
           mla_kv_norm_g, mla_w_ukv, w_branch, w_out, xa_wq, xa_wk, xa_wv, xa_wo):
    b, s, d = x.shape
    depth = ln_g.shape[0]
    n = b * s
    alpha = (2.0 * depth) ** 0.25

    lng = ln_g.reshape(depth, 4, 1, d)
    lnb = ln_b.reshape(depth, 4, 1, d)
    w1 = ffn_w1.astype(BF16)
    w3 = ffn_w3.astype(BF16)
    w2 = ffn_w2.astype(BF16)
    wo = w_out.astype(BF16)
    xq = xa_wq.astype(BF16)
    xk = xa_wk.astype(BF16)
    xv = xa_wv.astype(BF16)
    xo = xa_wo.astype(BF16)
    p_lb = jax.nn.softmax(hgrn_lb_logits.astype(F32), axis=0)
    lower = (jnp.cumsum(p_lb, axis=0) - p_lb[0:1]).reshape(depth, HB, 1, HK)

    cos_a, sin_a = _rope_tables(positions, NSA_ROT, 0)
    cos_c, sin_c = _rope_tables(positions, ROPE_D, NOPE)
    ovl, place = _overlap_matrix(s)
    shift = jnp.asarray(np.eye(N_CMP, k=1, dtype=np.float32)).astype(BF16)
    mem2 = mem.reshape(b * mem.shape[1], d)

    x2 = x.reshape(n, d)
    for l in range(depth):
        w = _layer_weights(l, w_in, nsa_cmp_pos, nsa_cmp_w1, nsa_cmp_w2, mla_w_uq, mla_w_ukv,
                           w_branch)
        x2 = _ffn_ln(x2, w1, w3, w2, lng, lnb, l, 0, 0, alpha)
        x3 = x2.reshape(b, s, d)

        q, qr, kc, vc, ks, vs, kw, vw, gates = _proja(x3, w["wa"], cos_a, sin_a)
        zk = kc.reshape(b, NSA_KV, N_CMP, CMP_HALF)
        zv = vc.reshape(b, NSA_KV, N_CMP, CMP_HALF)
        kcmp, vcmp = _cmp(zk, zv, w["w1p"], w["pe8"], w["w2p"], shift)
        ya = _nsa_attn(q, qr, kcmp, vcmp, ks, vs, kw, vw, gates, ovl, place)

        pb = _proj(x2, w["wb"], F32).reshape(b, s, 4 * HB * HK)
        yb = _hgrn(pb, lower[l], hgrn_norm_g[l].reshape(1, HV))

        mq, mk, mv = _mlap(x3, w["wc"], mla_q_norm_g[l].reshape(1, Q_RANK),
                           mla_kv_norm_g[l].reshape(1, KV_RANK), w["wuq"], w["wuqr"],
                           w["wuk"], w["wuv"], cos_c, sin_c)
        yc = _mla_attn(mq, mk, mv)

        x2 = _merge(x2, ya.reshape(n, NSA_QW), yb.reshape(n, MIX_W), yc.reshape(n, MLA_W),
                    w["wm"], w["wba"], w["wbb"], w["wbc"], wo, lng, lnb, l, alpha)

        xk_l = _proj(mem2, xk[l], BF16).reshape(b, -1, d)
        xv_l = _proj(mem2, xv[l], BF16).reshape(b, -1, d)
        x2 = _xattn(x2.reshape(b, s, d), xq, xk_l, xv_l, xo, lng, lnb, l, alpha).reshape(n, d)

        x2 = _ffn_ln(x2, w1, w3, w2, lng, lnb, l, 1, 3, alpha)
    return x2.reshape(b, s, d)
```

```python
import functools
import math

import numpy as np
import jax
import jax.numpy as jnp
from jax import lax
from jax.experimental import pallas as pl
from jax.experimental.pallas import tpu as pltpu

F32 = jnp.float32
BF16 = jnp.bfloat16

D_MODEL = 1024
MIX_W = D_MODEL // 2
NSA_DH = 64
NSA_HEADS = 8
NSA_KV = 2
NSA_R = 4
NSA_ROT = 16
CMP_L = 32
CMP_D = 16
CMP_HID = 256
SEL_L = 64
N_SEL = 8
WINDOW = 256
HB = 4
HK = 128
HV = 128
HGRN_CHUNK = 64
HGRN_SUB = 4
HC = 8
NOPE = 64
ROPE_D = 32
VD = 64
Q_RANK = 384
KV_RANK = 256
XA_HEADS = 4
XA_DH = D_MODEL // XA_HEADS
D_FF = 2816
ROPE_THETA = 500000.0
LN_EPS = 1e-5
RMS_EPS = 1e-6
NEG = -1e30
BIG = 1e9
F_MIN = 1e-20

LOG2E = 1.4426950408889634
LANE = 128
SEL_LANE0 = NSA_DH
DEN_LANE = 64
VMEM_LIMIT = 56 * 1024 * 1024

IN_SIZES = (512, 128, 128, 128, 128, 128, 128, 24, 512, 512, 512, 512, 384, 256, 32, 3072)
IN_OFF = tuple(int(v) for v in np.concatenate([[0], np.cumsum(IN_SIZES)]))


def _dot(a, b):
    return jnp.dot(a, b, preferred_element_type=F32)


def _dot_nt(a, b):
    return lax.dot_general(a, b, (((1,), (1,)), ((), ())), preferred_element_type=F32)


def _ln(y, g, b):
    mu = jnp.mean(y, axis=-1, keepdims=True)
    yc = y - mu
    var = jnp.mean(yc * yc, axis=-1, keepdims=True)
    return yc * lax.rsqrt(var + LN_EPS) * g + b


def _params(*sem):
    return pltpu.CompilerParams(dimension_semantics=sem, vmem_limit_bytes=VMEM_LIMIT)


def _resident(shape, index_map):
    return pl.BlockSpec(shape, index_map, pipeline_mode=pl.Buffered(1))


FFN_TM = 1024
FFN_TF = 256


def _ffn_kernel(x_ref, w1_ref, w3_ref, w2_ref, g_ref, b_ref, o_ref, acc_ref, xb_ref, *, alpha):
    j = pl.program_id(1)

    @pl.when(j == 0)
    def _():
        acc_ref[...] = jnp.zeros_like(acc_ref)
        xb_ref[...] = x_ref[...].astype(BF16)

    xb = xb_ref[...]
    h1 = _dot(xb, w1_ref[...])
    h3 = _dot(xb, w3_ref[...])
    h = (h1 * jax.nn.sigmoid(h1)) * h3
    acc_ref[...] += _dot(h.astype(BF16), w2_ref[...])

    @pl.when(j == pl.num_programs(1) - 1)
    def _():
        y = alpha * x_ref[...] + 0.5 * acc_ref[...]
        o_ref[...] = _ln(y, g_ref[...], b_ref[...])


def _ffn_ln(x2, w1, w3, w2, lng, lnb, l, which, ln_idx, alpha):
    n = x2.shape[0]
    tm = min(FFN_TM, n)
    grid = (n // tm, D_FF // FFN_TF)
    return pl.pallas_call(
        functools.partial(_ffn_kernel, alpha=alpha),
        grid=grid,
        in_specs=[
            pl.BlockSpec((tm, D_MODEL), lambda m, j: (m, 0)),
            pl.BlockSpec((None, None, D_MODEL, FFN_TF), lambda m, j: (l, which, 0, j)),
            pl.BlockSpec((None, None, D_MODEL, FFN_TF), lambda m, j: (l, which, 0, j)),
            pl.BlockSpec((None, None, FFN_TF, D_MODEL), lambda m, j: (l, which, j, 0)),
            pl.BlockSpec((None, None, 1, D_MODEL), lambda m, j: (l, ln_idx, 0, 0)),
            pl.BlockSpec((None, None, 1, D_MODEL), lambda m, j: (l, ln_idx, 0, 0)),
        ],
        out_specs=pl.BlockSpec((tm, D_MODEL), lambda m, j: (m, 0)),
        out_shape=jax.ShapeDtypeStruct((n, D_MODEL), F32),
        scratch_shapes=[pltpu.VMEM((tm, D_MODEL), F32), pltpu.VMEM((tm, D_MODEL), BF16)],
        compiler_params=_params("parallel", "arbitrary"),
        name="ffn_ln",
    )(x2, w1, w3, w2, lng, lnb)


def _proj_kernel(x_ref, w_ref, o_ref):
    o_ref[...] = _dot(x_ref[...].astype(BF16), w_ref[...]).astype(o_ref.dtype)


def _proj(x2, w, out_dtype, tm=1024, tn=2048):
    n, k = x2.shape
    c = w.shape[1]
    tm = min(tm, n)
    tn = min(tn, c)
    return pl.pallas_call(
        _proj_kernel,
        grid=(n // tm, c // tn),
        in_specs=[pl.BlockSpec((tm, k), lambda m, j: (m, 0)),
                  pl.BlockSpec((k, tn), lambda m, j: (0, j))],
        out_specs=pl.BlockSpec((tm, tn), lambda m, j: (m, j)),
        out_shape=jax.ShapeDtypeStruct((n, c), out_dtype),
        compiler_params=_params("parallel", "arbitrary"),
        name="proj",
    )(x2, w)


NSA_QW = NSA_HEADS * LANE
NSA_KW = NSA_KV * LANE
A_Q, A_QROT = 0, NSA_QW
A_KC = 2 * NSA_QW
A_VC = A_KC + NSA_KW
A_KS = A_VC + NSA_KW
A_KSROT = A_KS + NSA_KW
A_VS = A_KSROT + NSA_KW
A_KWN = A_VS + NSA_KW
A_KWROT = A_KWN + NSA_KW
A_VW = A_KWROT + NSA_KW
A_GATE = A_VW + NSA_KW
A_COLS = A_GATE + NSA_KW
PROJA_TM = 512


def _proja_kernel(x_ref, w_ref, cos_ref, sin_ref,
                  q_ref, qr_ref, kc_ref, vc_ref, ks_ref, vs_ref, kw_ref, vw_ref, g_ref):
    xb = x_ref[...].astype(BF16)
    cos = cos_ref[...]
    sin = sin_ref[...]
    cos2 = jnp.concatenate([cos, cos], axis=1)
    sin2 = jnp.concatenate([sin, sin], axis=1)

    def mm(c0, width):
        return _dot(xb, w_ref[:, c0:c0 + width])

    qs = NSA_DH ** -0.5 * LOG2E
    for p in range(NSA_QW // NSA_KW):
        c = p * NSA_KW
        y = mm(A_Q + c, NSA_KW)
        yr = mm(A_QROT + c, NSA_KW)
        q_ref[:, c:c + NSA_KW] = (y * qs).astype(BF16)
        qr_ref[:, c:c + NSA_KW] = ((y * cos2 + yr * sin2) * qs).astype(BF16)

    def put(ref, y):
        for g in range(NSA_KV):
            ref[g] = y[:, g * LANE:(g + 1) * LANE].astype(BF16)

    tm = x_ref.shape[0]
    lane2 = lax.broadcasted_iota(jnp.int32, (tm, NSA_KW), 1) % LANE
    tok = pl.program_id(1) * tm + lax.broadcasted_iota(jnp.int32, (tm, NSA_KW), 0)
    blk_flag = jnp.where(lane2 - SEL_LANE0 == tok // SEL_L, 1.0, 0.0)
    den_flag = jnp.where(lane2 == DEN_LANE, 1.0, 0.0)

    put(kc_ref, mm(A_KC, NSA_KW))
    put(vc_ref, mm(A_VC, NSA_KW))
    put(ks_ref, mm(A_KS, NSA_KW) * cos2 + mm(A_KSROT, NSA_KW) * sin2 + blk_flag)
    put(vs_ref, mm(A_VS, NSA_KW) + den_flag)
    put(kw_ref, mm(A_KWN, NSA_KW) * cos2 + mm(A_KWROT, NSA_KW) * sin2)
    put(vw_ref, mm(A_VW, NSA_KW) + den_flag)
    g_ref[...] = mm(A_GATE, NSA_KW)


def _proja(x3, wa, cos_a, sin_a):
    b, s, _ = x3.shape
    tm = PROJA_TM
    kv_shape = jax.ShapeDtypeStruct((b, NSA_KV, s, LANE), BF16)
    kv_spec = pl.BlockSpec((None, NSA_KV, tm, LANE), lambda bi, m: (bi, 0, m, 0))
    return pl.pallas_call(
        _proja_kernel,
        grid=(b, s // tm),
        in_specs=[
            pl.BlockSpec((None, tm, D_MODEL), lambda bi, m: (bi, m, 0)),
            _resident((D_MODEL, A_COLS), lambda bi, m: (0, 0)),
            pl.BlockSpec((None, tm, LANE), lambda bi, m: (bi, m, 0)),
            pl.BlockSpec((None, tm, LANE), lambda bi, m: (bi, m, 0)),
        ],
        out_specs=[
            pl.BlockSpec((None, tm, NSA_QW), lambda bi, m: (bi, m, 0)),
            pl.BlockSpec((None, tm, NSA_QW), lambda bi, m: (bi, m, 0)),
            kv_spec, kv_spec, kv_spec, kv_spec, kv_spec, kv_spec,
            pl.BlockSpec((None, tm, NSA_KW), lambda bi, m: (bi, m, 0)),
        ],
        out_shape=[
            jax.ShapeDtypeStruct((b, s, NSA_QW), BF16),
            jax.ShapeDtypeStruct((b, s, NSA_QW), BF16),
            kv_shape, kv_shape, kv_shape, kv_shape, kv_shape, kv_shape,
            jax.ShapeDtypeStruct((b, s, NSA_KW), F32),
        ],
        compiler_params=_params("parallel", "parallel"),
        name="nsa_proj",
    )(x3, wa, cos_a, sin_a)


N_CMP = 128
CMP_HALF = CMP_D * LANE


def _gelu_tanh(x):
    return 0.5 * x * (1.0 + jnp.tanh(0.7978845608028654 * (x + 0.044715 * x * x * x)))


def _cmp_kernel(zk_ref, zv_ref, w1_ref, pe_ref, w2_ref, shift_ref, ok_ref, ov_ref):
    shift = shift_ref[...]
    for which, (z_ref, o_ref) in enumerate(((zk_ref, ok_ref), (zv_ref, ov_ref))):
        z = z_ref[...]
        zn = _dot(shift, z).astype(BF16)
        bias = _dot(pe_ref[which], w1_ref[which])[0:1]
        pre = (_dot(z, w1_ref[which, 0:CMP_HALF, :])
               + _dot(zn, w1_ref[which, CMP_HALF:2 * CMP_HALF, :]) + bias)
        h = _gelu_tanh(pre)
        o_ref[...] = _dot(h.astype(BF16), w2_ref[which]).astype(BF16)


def _cmp(zk, zv, w1p, pe8, w2p, shift):
    b = zk.shape[0]
    z_spec = pl.BlockSpec((None, None, N_CMP, CMP_HALF), lambda bi, g: (bi, g, 0, 0))
    o_spec = pl.BlockSpec((None, None, N_CMP, LANE), lambda bi, g: (bi, g, 0, 0))
    o_shape = jax.ShapeDtypeStruct((b, NSA_KV, N_CMP, LANE), BF16)
    return pl.pallas_call(
        _cmp_kernel,
        grid=(b, NSA_KV),
        in_specs=[
            z_spec, z_spec,
            _resident((2, 2 * CMP_HALF, CMP_HID), lambda bi, g: (0, 0, 0)),
            _resident((2, 8, 2 * CMP_HALF), lambda bi, g: (0, 0, 0)),
            _resident((2, CMP_HID, LANE), lambda bi, g: (0, 0, 0)),
            _resident((N_CMP, N_CMP), lambda bi, g: (0, 0)),
        ],
        out_specs=[o_spec, o_spec],
        out_shape=[o_shape, o_shape],
        compiler_params=_params("parallel", "parallel"),
        name="nsa_cmp",
    )(zk, zv, w1p, pe8, w2p, shift)


NSA_TQ = 256
NSA_TK = 256
N_SLC = 32
WIN_SLAB = WINDOW + NSA_TQ


def _nsa_kernel(q_ref, qr_ref, kc_ref, vc_ref, ks_ref, vs_ref, kw_ref, vw_ref, gt_ref, ovl_ref,
                place_ref, o_ref):
    tq, tk, r_heads = NSA_TQ, NSA_TK, NSA_R
    i = pl.program_id(2)
    t0 = i * tq
    tpos = t0 + lax.broadcasted_iota(jnp.int32, (tq, 1), 0)
    lane = lax.broadcasted_iota(jnp.int32, (1, LANE), 1)

    def stack(ref):
        return jnp.concatenate([ref[:, r * LANE:(r + 1) * LANE] for r in range(r_heads)], axis=0)

    gt = jax.nn.sigmoid(gt_ref[...])

    def gate(c):
        return jnp.stack([gt[:, 3 * r + c:3 * r + c + 1] for r in range(r_heads)], axis=0)

    q4w = stack(qr_ref)
    w0 = pl.multiple_of(jnp.maximum(t0 - WINDOW, 0), LANE)
    kw = kw_ref[pl.ds(w0, WIN_SLAB), :]
    vw = vw_ref[pl.ds(w0, WIN_SLAB), :]
    kpos = w0 + lax.broadcasted_iota(jnp.int32, (1, WIN_SLAB), 1)
    w_bias = jnp.where((kpos <= tpos) & (kpos > tpos - WINDOW), 0.0, NEG)
    s = _dot_nt(q4w, kw).reshape(r_heads, tq, WIN_SLAB) + w_bias[None]
    p = jnp.exp2(s - jnp.max(s, axis=-1, keepdims=True))
    o_win = _dot(p.reshape(r_heads * tq, WIN_SLAB).astype(BF16), vw).reshape(r_heads, tq, LANE)
    o_part = gate(2) * (o_win / o_win[:, :, DEN_LANE:DEN_LANE + 1])

    cmask = ((lane * CMP_D + (CMP_L - 1) <= tpos) & (lane < N_CMP - 1))[None]
    s = _dot_nt(stack(q_ref), kc_ref[...]).reshape(r_heads, tq, N_CMP)
    s = jnp.where(cmask, s, NEG)
    e = jnp.where(cmask, jnp.exp2(s - jnp.max(s, axis=-1, keepdims=True)), 0.0)
    den = jnp.sum(e, axis=-1, keepdims=True)
    p = e / jnp.where(den > 0.0, den, 1.0)
    psum = jnp.sum(p, axis=0)
    o_cmp = _dot(p.reshape(r_heads * tq, N_CMP).astype(BF16), vc_ref[...])
    o_part = o_part + gate(0) * o_cmp.reshape(r_heads, tq, LANE)

    imp = lax.dot_general(ovl_ref[...], psum, (((1,), (1,)), ((), ())),
                          precision=lax.Precision.HIGHEST, preferred_element_type=F32)
    blk = lax.broadcasted_iota(jnp.int32, (N_SLC, 1), 0)
    tpos_l = t0 + lax.broadcasted_iota(jnp.int32, (1, tq), 1)
    cur = tpos_l // SEL_L
    valid = blk * SEL_L <= tpos_l
    forced = (blk == 0) | (blk == cur) | (blk == cur - 1)
    score = jnp.where(valid & forced, BIG, jnp.where(valid, imp, -BIG))
    beats = []
    for j in range(N_SLC):
        sj = score[j:j + 1, :]
        beats.append(jnp.where((sj > score) | ((sj == score) & (j < blk)), 1.0, 0.0))
    while len(beats) > 1:
        beats = [a + b for a, b in zip(beats[0::2], beats[1::2])]
    sel_t = jnp.where((beats[0] < N_SEL) & valid, 1.0, 0.0)
    sel_q = lax.dot_general(sel_t, place_ref[...], (((0,), (0,)), ((), ())),
                            preferred_element_type=F32)
    in_flags = jnp.where((lane >= SEL_LANE0) & (lane < SEL_LANE0 + N_SLC), 1.0, 0.0)
    q_bias = (sel_q - in_flags) * (-NEG)
    q4 = (q4w.astype(F32).reshape(r_heads, tq, LANE) + q_bias[None]).astype(BF16)
    q4 = q4.reshape(r_heads * tq, LANE)

    def sel_step(c, carry, causal):
        m, acc = carry
        k0 = pl.multiple_of(c * tk, tk)
        k = ks_ref[pl.ds(k0, tk), :]
        v = vs_ref[pl.ds(k0, tk), :]
        s = _dot_nt(q4, k).reshape(r_heads, tq, tk)
        if causal:
            kpos = k0 + lax.broadcasted_iota(jnp.int32, (1, tk), 1)
            s = jnp.where((kpos <= tpos)[None], s, NEG)
        m_new = jnp.maximum(m, jnp.max(s, axis=-1, keepdims=True))
        p = jnp.exp2(s - m_new)
        pv = _dot(p.reshape(r_heads * tq, tk).astype(BF16), v).reshape(r_heads, tq, LANE)
        return m_new, jnp.exp2(m - m_new) * acc + pv

    last = (t0 + tq - 1) // tk
    carry = (jnp.full((r_heads, tq, 1), NEG, F32), jnp.zeros((r_heads, tq, LANE), F32))
    carry = lax.fori_loop(0, last, functools.partial(sel_step, causal=False), carry)
    _, acc_s = sel_step(last, carry, True)
    o = o_part + gate(1) * (acc_s / acc_s[:, :, DEN_LANE:DEN_LANE + 1])
    for r in range(r_heads):
        o_ref[:, r * LANE:(r + 1) * LANE] = o[r].astype(BF16)


def _nsa_attn(q, qr, kcmp, vcmp, ks, vs, kw, vw, gates, ovl, place):
    b, s, _ = q.shape
    tq = NSA_TQ
    gw = NSA_R * LANE
    q_spec = pl.BlockSpec((None, tq, gw), lambda bi, g, i: (bi, i, g))
    c_spec = pl.BlockSpec((None, None, N_CMP, LANE), lambda bi, g, i: (bi, g, 0, 0))
    kv_spec = pl.BlockSpec((None, None, s, LANE), lambda bi, g, i: (bi, g, 0, 0))
    return pl.pallas_call(
        _nsa_kernel,
        grid=(b, NSA_KV, s // tq),
        in_specs=[q_spec, q_spec, c_spec, c_spec, kv_spec, kv_spec, kv_spec, kv_spec,
                  pl.BlockSpec((None, tq, LANE), lambda bi, g, i: (bi, i, g)),
                  _resident((N_SLC, N_CMP), lambda bi, g, i: (0, 0)),
                  _resident((N_SLC, LANE), lambda bi, g, i: (0, 0))],
        out_specs=pl.BlockSpec((None, tq, gw), lambda bi, g, i: (bi, i, g)),
        out_shape=jax.ShapeDtypeStruct((b, s, NSA_QW), BF16),
        compiler_params=_params("parallel", "parallel", "arbitrary"),
        name="nsa_attn",
    )(q, qr, kcmp, vcmp, ks, vs, kw, vw, gates, ovl, place)


def _hgrn_kernel(q_ref, z_ref, v_ref, go_ref, lb_ref, ng_ref, o_ref, st_ref):
    s_len = q_ref.shape[0]
    c_len, sub_len = HGRN_CHUNK, HGRN_SUB
    nc = s_len // c_len
    q = q_ref[...]
    z = z_ref[...]
    v = v_ref[...]
    lb = lb_ref[...]
    t = lax.broadcasted_iota(jnp.int32, (s_len, 1), 0)
    t_sub = t % sub_len

    f = lb + (1.0 - lb) * jax.nn.sigmoid(z)
    lf = jnp.log(jnp.maximum(f, F_MIN))
    k = (1.0 - lb) * jax.nn.sigmoid(-z)

    row = lax.broadcasted_iota(jnp.int32, (nc, c_len, c_len), 1)
    col = lax.broadcasted_iota(jnp.int32, (nc, c_len, c_len), 2)
    b3 = jnp.einsum('cts,csd->ctd', jnp.where(col <= row, 1.0, 0.0), lf.reshape(nc, c_len, HK),
                    precision=lax.Precision.HIGHEST, preferred_element_type=F32)
    b = b3.reshape(s_len, HK)
    k3 = k.reshape(nc, c_len, HK)
    vb3 = v.astype(BF16).reshape(nc, c_len, HV)

    a_intra = None
    h = c_len // 2
    while h >= sub_len:
        blk = b.reshape(s_len // (2 * h), 2 * h, HK)
        e = jnp.exp(-jnp.abs(blk[:, h - 1:h, :] - blk)).reshape(s_len, HK)
        upper = (t // h) % 2 == 1
        lq = jnp.where(upper, q * e, 0.0).astype(BF16).reshape(nc, c_len, HK)
        rk = jnp.where(upper, 0.0, k * e).astype(BF16).reshape(nc, c_len, HK)
        a = jnp.einsum('ctk,csk->cts', lq, rk, preferred_element_type=F32)
        if 2 * h < c_len:
            a = jnp.where(row // (2 * h) == col // (2 * h), a, 0.0)
        a_intra = a if a_intra is None else a_intra + a
        h //= 2
    o = jnp.einsum('cts,csd->ctd', a_intra.astype(BF16), vb3,
                   preferred_element_type=F32).reshape(s_len, HV)

    o = o + jnp.sum(q * k, axis=-1, keepdims=True) * v
    for d in range(1, sub_len):
        e = jnp.exp(jnp.where(t_sub >= d, b - pltpu.roll(b, d, axis=0), NEG))
        w = jnp.sum(q * pltpu.roll(k, d, axis=0) * e, axis=-1, keepdims=True)
        o = o + w * pltpu.roll(v, d, axis=0)

    b_last = b3[:, c_len - 1:c_len, :]
    kd3 = (k3 * jnp.exp(b_last - b3)).astype(BF16)
    upd = jnp.einsum('csv,csk->cvk', vb3, kd3, preferred_element_type=F32)
    dec = jnp.exp(b_last)
    state = jnp.zeros((HV, HK), F32)
    for c in range(nc):
        st_ref[c] = state.astype(BF16)
        state = state * dec[c] + upd[c]
    qe3 = (q * jnp.exp(b)).astype(BF16).reshape(nc, c_len, HK)
    o = o + jnp.einsum('ctk,cvk->ctv', qe3, st_ref[...],
                       preferred_element_type=F32).reshape(s_len, HV)

    o = o * lax.rsqrt(jnp.mean(o * o, axis=-1, keepdims=True) + RMS_EPS) * ng_ref[...]
    go = go_ref[...]
    o_ref[...] = (o * (go * jax.nn.sigmoid(go))).astype(BF16)


def _hgrn(pb, lb, ng):
    b, s, _ = pb.shape

    def col(part):
        return pl.BlockSpec((None, s, HK), lambda bi, h: (bi, 0, part * HB + h))

    return pl.pallas_call(
        _hgrn_kernel,
        grid=(b, HB),
        in_specs=[col(0), col(1), col(2), col(3),
                  pl.BlockSpec((None, 1, HK), lambda bi, h: (h, 0, 0)),
                  pl.BlockSpec((1, HV), lambda bi, h: (0, 0))],
        out_specs=pl.BlockSpec((None, s, HV), lambda bi, h: (bi, 0, h)),
        out_shape=jax.ShapeDtypeStruct((b, s, HB * HV), BF16),
        scratch_shapes=[pltpu.VMEM((s // HGRN_CHUNK, HV, HK), BF16)],
        compiler_params=_params("parallel", "parallel"),
        name="hgrn",
    )(pb, pb, pb, pb, lb, ng)


MLA_W = HC * LANE
C_CQ, C_CKV, C_KR, C_KRROT = 0, Q_RANK, Q_RANK + KV_RANK, Q_RANK + KV_RANK + LANE
C_COLS = C_KRROT + LANE
MLAP_TM = 512


def _rms(x, g):
    return x * lax.rsqrt(jnp.mean(x * x, axis=-1, keepdims=True) + RMS_EPS) * g


def _mlap_kernel(x_ref, wc_ref, qg_ref, kvg_ref, wuq_ref, wuqr_ref, wuk_ref, wuv_ref,
                 cos_ref, sin_ref, q_ref, k_ref, v_ref):
    xb = x_ref[...].astype(BF16)
    c = _dot(xb, wc_ref[...])
    cos = cos_ref[...]
    sin = sin_ref[...]
    nq = _rms(c[:, C_CQ:C_CKV], qg_ref[...]).astype(BF16)
    nkv = _rms(c[:, C_CKV:C_KR], kvg_ref[...]).astype(BF16)
    k_pe = c[:, C_KR:C_KRROT] * cos + c[:, C_KRROT:C_COLS] * sin
    qs = (NOPE + ROPE_D) ** -0.5 * LOG2E
    cos_q = cos * qs
    sin_q = sin * qs
    den_flag = jnp.where(lax.broadcasted_iota(jnp.int32, (1, LANE), 1) == DEN_LANE, 1.0, 0.0)
    for h in range(HC):
        hs = slice(h * LANE, (h + 1) * LANE)
        q_ref[:, hs] = (_dot(nq, wuq_ref[:, hs]) * cos_q
                        + _dot(nq, wuqr_ref[:, hs]) * sin_q).astype(BF16)
        k_ref[:, hs] = (_dot(nkv, wuk_ref[:, hs]) + k_pe).astype(BF16)
        v_ref[:, hs] = (_dot(nkv, wuv_ref[:, hs]) + den_flag).astype(BF16)


def _mlap(x3, wc, qg, kvg, wuq, wuqr, wuk, wuv, cos_c, sin_c):
    b, s, _ = x3.shape
    tm = MLAP_TM
    o_spec = pl.BlockSpec((None, tm, MLA_W), lambda bi, m: (bi, m, 0))
    o_shape = jax.ShapeDtypeStruct((b, s, MLA_W), BF16)
    t_spec = pl.BlockSpec((None, tm, LANE), lambda bi, m: (bi, m, 0))
    return pl.pallas_call(
        _mlap_kernel,
        grid=(b, s // tm),
        in_specs=[
            pl.BlockSpec((None, tm, D_MODEL), lambda bi, m: (bi, m, 0)),
            _resident((D_MODEL, C_COLS), lambda bi, m: (0, 0)),
            _resident((1, Q_RANK), lambda bi, m: (0, 0)),
            _resident((1, KV_RANK), lambda bi, m: (0, 0)),
            _resident((Q_RANK, MLA_W), lambda bi, m: (0, 0)),
            _resident((Q_RANK, MLA_W), lambda bi, m: (0, 0)),
            _resident((KV_RANK, MLA_W), lambda bi, m: (0, 0)),
            _resident((KV_RANK, MLA_W), lambda bi, m: (0, 0)),
            t_spec, t_spec,
        ],
        out_specs=[o_spec, o_spec, o_spec],
        out_shape=[o_shape, o_shape, o_shape],
        compiler_params=_params("parallel", "parallel"),
        name="mla_proj",
    )(x3, wc, qg, kvg, wuq, wuqr, wuk, wuv, cos_c, sin_c)


MLA_TQ = 512
MLA_TK = 512


MLA_HPS = 4


def _mla_attn_kernel(q_ref, k_ref, v_ref, o_ref):
    tq, tk = MLA_TQ, MLA_TK
    i = pl.program_id(2)
    t0 = i * tq
    tpos = t0 + lax.broadcasted_iota(jnp.int32, (tq, 1), 0)
    heads = [slice(h * LANE, (h + 1) * LANE) for h in range(MLA_HPS)]
    qs = [q_ref[:, hs] for hs in heads]

    def step(c, carry, causal):
        k0 = pl.multiple_of(c * tk, tk)
        if causal:
            keep = k0 + lax.broadcasted_iota(jnp.int32, (1, tk), 1) <= tpos
        out = []
        for h, hs in enumerate(heads):
            m, acc = carry[h]
            s = _dot_nt(qs[h], k_ref[pl.ds(k0, tk), hs])
            if causal:
                s = jnp.where(keep, s, NEG)
            m_new = jnp.maximum(m, jnp.max(s, axis=-1, keepdims=True))
            p = jnp.exp2(s - m_new)
            pv = _dot(p.astype(BF16), v_ref[pl.ds(k0, tk), hs])
            out.append((m_new, jnp.exp2(m - m_new) * acc + pv))
        return tuple(out)

    last = (t0 + tq - 1) // tk
    carry = tuple((jnp.full((tq, 1), NEG, F32), jnp.zeros((tq, LANE), F32)) for _ in heads)
    carry = lax.fori_loop(0, last, functools.partial(step, causal=False), carry)
    carry = step(last, carry, True)
    for (_, acc), hs in zip(carry, heads):
        o_ref[:, hs] = (acc / acc[:, DEN_LANE:DEN_LANE + 1]).astype(BF16)


def _mla_attn(q, k, v):
    b, s, _ = q.shape
    tq = MLA_TQ
    gw = MLA_HPS * LANE
    return pl.pallas_call(
        _mla_attn_kernel,
        grid=(b, HC // MLA_HPS, s // tq),
        in_specs=[pl.BlockSpec((None, tq, gw), lambda bi, h, i: (bi, i, h)),
                  pl.BlockSpec((None, s, gw), lambda bi, h, i: (bi, 0, h)),
                  pl.BlockSpec((None, s, gw), lambda bi, h, i: (bi, 0, h))],
        out_specs=pl.BlockSpec((None, tq, gw), lambda bi, h, i: (bi, i, h)),
        out_shape=jax.ShapeDtypeStruct((b, s, MLA_W), BF16),
        compiler_params=_params("parallel", "parallel", "arbitrary"),
        name="mla_attn",
    )(q, k, v)


MERGE_TM = 512


def _merge_kernel(x_ref, ya_ref, yb_ref, yc_ref, wm_ref, wa_ref, wb_ref, wc_ref, wo_ref,
                  g_ref, b_ref, o_ref, *, alpha):
    x = x_ref[...]
    xb = x.astype(BF16)
    mixed = jnp.zeros(x.shape, F32)
    for idx, (y_ref, w_ref) in enumerate(((ya_ref, wa_ref), (yb_ref, wb_ref), (yc_ref, wc_ref))):
        gate = jax.nn.sigmoid(_dot(xb, wm_ref[:, idx * D_MODEL:(idx + 1) * D_MODEL]))
        mixed = mixed + gate * _dot(y_ref[...], w_ref[...])
    y = alpha * x + _dot(mixed.astype(BF16), wo_ref[...])
    o_ref[...] = _ln(y, g_ref[...], b_ref[...])


def _merge(x2, ya, yb, yc, wm, wa, wb, wc, wo, lng, lnb, l, alpha):
    n = x2.shape[0]
    tm = min(MERGE_TM, n)

    def rows(width):
        return pl.BlockSpec((tm, width), lambda m: (m, 0))

    return pl.pallas_call(
        functools.partial(_merge_kernel, alpha=alpha),
        grid=(n // tm,),
        in_specs=[rows(D_MODEL), rows(NSA_QW), rows(MIX_W), rows(MLA_W),
                  _resident((D_MODEL, 3 * D_MODEL), lambda m: (0, 0)),
                  _resident((NSA_QW, D_MODEL), lambda m: (0, 0)),
                  _resident((MIX_W, D_MODEL), lambda m: (0, 0)),
                  _resident((MLA_W, D_MODEL), lambda m: (0, 0)),
                  _resident((None, D_MODEL, D_MODEL), lambda m: (l, 0, 0)),
                  _resident((None, None, 1, D_MODEL), lambda m: (l, 1, 0, 0)),
                  _resident((None, None, 1, D_MODEL), lambda m: (l, 1, 0, 0))],
        out_specs=rows(D_MODEL),
        out_shape=jax.ShapeDtypeStruct((n, D_MODEL), F32),
        compiler_params=_params("parallel"),
        name="merge",
    )(x2, ya, yb, yc, wm, wa, wb, wc, wo, lng, lnb)


XA_TM = 512


def _xattn_kernel(x_ref, wq_ref, k_ref, v_ref, wo_ref, g_ref, b_ref, o_ref, *, alpha):
    x = x_ref[...]
    xb = x.astype(BF16)
    scale = XA_DH ** -0.5
    heads = []
    for h in range(XA_HEADS):
        hs = slice(h * XA_DH, (h + 1) * XA_DH)
        qh = _dot(xb, wq_ref[:, hs]).astype(BF16)
        s = _dot_nt(qh, k_ref[:, hs]) * scale
        e = jnp.exp(s - jnp.max(s, axis=-1, keepdims=True))
        p = e / jnp.sum(e, axis=-1, keepdims=True)
        heads.append(_dot(p.astype(BF16), v_ref[:, hs]).astype(BF16))
    att = jnp.concatenate(heads, axis=1)
    y = alpha * x + _dot(att, wo_ref[...])
    o_ref[...] = _ln(y, g_ref[...], b_ref[...])


def _xattn(x3, wq, k, v, wo, lng, lnb, l, alpha):
    b, s, _ = x3.shape
    m_len = k.shape[1]
    tm = XA_TM
    return pl.pallas_call(
        functools.partial(_xattn_kernel, alpha=alpha),
        grid=(b, s // tm),
        in_specs=[pl.BlockSpec((None, tm, D_MODEL), lambda bi, m: (bi, m, 0)),
                  _resident((None, D_MODEL, D_MODEL), lambda bi, m: (l, 0, 0)),
                  pl.BlockSpec((None, m_len, D_MODEL), lambda bi, m: (bi, 0, 0)),
                  pl.BlockSpec((None, m_len, D_MODEL), lambda bi, m: (bi, 0, 0)),
                  _resident((None, D_MODEL, D_MODEL), lambda bi, m: (l, 0, 0)),
                  _resident((None, None, 1, D_MODEL), lambda bi, m: (l, 2, 0, 0)),
                  _resident((None, None, 1, D_MODEL), lambda bi, m: (l, 2, 0, 0))],
        out_specs=pl.BlockSpec((None, tm, D_MODEL), lambda bi, m: (bi, m, 0)),
        out_shape=jax.ShapeDtypeStruct((b, s, D_MODEL), F32),
        compiler_params=_params("parallel", "parallel"),
        name="xattn",
    )(x3, wq, k, v, wo, lng, lnb)


def _pad_heads(w, n_heads, dh):
    k = w.shape[0]
    w = w.reshape(k, n_heads, dh)
    return jnp.pad(w, ((0, 0), (0, 0), (0, LANE - dh))).reshape(k, n_heads * LANE)


def _rot_cols(w, n_heads, dh, off, rot):
    k = w.shape[0]
    half = rot // 2
    w = w.reshape(k, n_heads, dh)
    x1 = w[:, :, off:off + half]
    x2 = w[:, :, off + half:off + rot]
    out = jnp.zeros_like(w)
    out = out.at[:, :, off:off + half].set(-x2)
    out = out.at[:, :, off + half:off + rot].set(x1)
    return out.reshape(k, n_heads * dh)


def _pad_rows(w, n_heads, dh):
    n = w.shape[1]
    w = w.reshape(n_heads, dh, n)
    return jnp.pad(w, ((0, 0), (0, LANE - dh), (0, 0))).reshape(n_heads * LANE, n)


def _rope_tables(positions, rot, off):
    half = rot // 2
    inv = ROPE_THETA ** (-jnp.arange(half, dtype=F32) / half)
    ang = positions.astype(F32)[..., None] * inv
    cos, sin = jnp.cos(ang), jnp.sin(ang)
    shape = positions.shape
    cos_t = jnp.concatenate([jnp.ones(shape + (off,), F32), cos, cos,
                             jnp.ones(shape + (LANE - off - rot,), F32)], axis=-1)
    sin_t = jnp.concatenate([jnp.zeros(shape + (off,), F32), sin, sin,
                             jnp.zeros(shape + (LANE - off - rot,), F32)], axis=-1)
    return cos_t, sin_t


def _overlap_matrix(s_len):
    n_cmp = (s_len - CMP_L) // CMP_D + 1
    n_slc = s_len // SEL_L
    start = np.arange(n_cmp) * CMP_D
    j = np.arange(n_slc)
    ov = np.clip(np.minimum(start[:, None] + CMP_L, (j[None, :] + 1) * SEL_L)
                 - np.maximum(start[:, None], j[None, :] * SEL_L), 0, None) / CMP_L
    out = np.zeros((N_SLC, N_CMP), np.float32)
    out[:n_slc, :n_cmp] = ov.T
    place = np.zeros((N_SLC, LANE), np.float32)
    place[np.arange(N_SLC), SEL_LANE0 + np.arange(N_SLC)] = 1.0
    return jnp.asarray(out), jnp.asarray(place)


def _layer_weights(l, w_in, nsa_cmp_pos, nsa_cmp_w1, nsa_cmp_w2, mla_w_uq, mla_w_ukv, w_branch):
    wl = w_in[l]
    part = [wl[:, IN_OFF[i]:IN_OFF[i + 1]] for i in range(len(IN_SIZES))]
    (a_q, a_kc, a_vc, a_ks, a_vs, a_kw, a_vw, a_gate, b_q, b_f, b_i, b_g,
     c_q, c_kv, c_kr, merge) = part
    kv = lambda w: _pad_heads(w, NSA_KV, NSA_DH)
    kvr = lambda w: _pad_heads(_rot_cols(w, NSA_KV, NSA_DH, 0, NSA_ROT), NSA_KV, NSA_DH)
    gate = a_gate.reshape(D_MODEL, NSA_KV, NSA_R * 3)
    gate = jnp.pad(gate, ((0, 0), (0, 0), (0, LANE - NSA_R * 3))).reshape(D_MODEL, NSA_KW)
    wa = jnp.concatenate([
        _pad_heads(a_q, NSA_HEADS, NSA_DH),
        _pad_heads(_rot_cols(a_q, NSA_HEADS, NSA_DH, 0, NSA_ROT), NSA_HEADS, NSA_DH),
        kv(a_kc), kv(a_vc), kv(a_ks), kvr(a_ks), kv(a_vs), kv(a_kw), kvr(a_kw), kv(a_vw), gate,
    ], axis=1).astype(BF16)
    wb = jnp.concatenate([b_q, b_f, b_i, b_g], axis=1).astype(BF16)

    kr_pad = jnp.pad(c_kr, ((0, 0), (NOPE, LANE - NOPE - ROPE_D)))
    krr_pad = jnp.pad(_rot_cols(c_kr, 1, ROPE_D, 0, ROPE_D), ((0, 0), (NOPE, LANE - NOPE - ROPE_D)))
    wc = jnp.concatenate([c_q, c_kv, kr_pad, krr_pad], axis=1).astype(BF16)
    dq = NOPE + ROPE_D
    wuq = _pad_heads(mla_w_uq[l], HC, dq).astype(BF16)
    wuqr = _pad_heads(_rot_cols(mla_w_uq[l], HC, dq, NOPE, ROPE_D), HC, dq).astype(BF16)
    ukv = mla_w_ukv[l].reshape(KV_RANK, HC, NOPE + VD)
    wuk = _pad_heads(ukv[:, :, :NOPE].reshape(KV_RANK, HC * NOPE), HC, NOPE).astype(BF16)
    wuv = _pad_heads(ukv[:, :, NOPE:].reshape(KV_RANK, HC * VD), HC, VD).astype(BF16)

    w1 = nsa_cmp_w1[l].reshape(2, CMP_L, NSA_DH, CMP_HID)
    w1p = jnp.pad(w1, ((0, 0), (0, 0), (0, LANE - NSA_DH), (0, 0))).reshape(2, CMP_L * LANE, CMP_HID)
    pe = jnp.pad(nsa_cmp_pos[l], ((0, 0), (0, 0), (0, LANE - NSA_DH))).reshape(2, 1, CMP_L * LANE)
    pe8 = jnp.broadcast_to(pe, (2, 8, CMP_L * LANE))
    w2p = jnp.pad(nsa_cmp_w2[l], ((0, 0), (0, 0), (0, LANE - NSA_DH)))

    wba = _pad_rows(w_branch[l, 0], NSA_HEADS, NSA_DH).astype(BF16)
    wbb = w_branch[l, 1].astype(BF16)
    wbc = _pad_rows(w_branch[l, 2], HC, VD).astype(BF16)
    return dict(wa=wa, wb=wb, wc=wc, wuq=wuq, wuqr=wuqr, wuk=wuk, wuv=wuv,
                w1p=w1p.astype(BF16), pe8=pe8.astype(BF16), w2p=w2p.astype(BF16),
                wm=merge.astype(BF16), wba=wba, wbb=wbb, wbc=wbc)


def kernel(x, mem, positions, ln_g, ln_b, ffn_w1, ffn_w3, ffn_w2, w_in, nsa_cmp_pos,
           nsa_cmp_w1, nsa_cmp_w2, hgrn_lb_logits, hgrn_norm_g, mla_q_norm_g, mla_w_uq,
           mla_kv_norm_g, mla_w_ukv, w_branch, w_out, xa_wq, xa_wk, xa_wv, xa_wo):
    b, s, d = x.shape
    depth = ln_g.shape[0]
    n = b * s
    alpha = (2.0 * depth) ** 0.25

    lng = ln_g.reshape(depth, 4, 1, d)
    lnb = ln_b.reshape(depth, 4, 1, d)
    w1 = ffn_w1.astype(BF16)
    w3 = ffn_w3.astype(BF16)
    w2 = ffn_w2.astype(BF16)
    wo = w_out.astype(BF16)
    xq = xa_wq.astype(BF16)
    xk = xa_wk.astype(BF16)
    xv = xa_wv.astype(BF16)
    xo = xa_wo.astype(BF16)
    p_lb = jax.nn.softmax(hgrn_lb_logits.astype(F32), axis=0)
    lower = (jnp.cumsum(p_lb, axis=0) - p_lb[0:1]).reshape(depth, HB, 1, HK)

    cos_a, sin_a = _rope_tables(positions, NSA_ROT, 0)
    cos_c, sin_c = _rope_tables(positions, ROPE_D, NOPE)
    ovl, place = _overlap_matrix(s)
    shift = jnp.asarray(np.eye(N_CMP, k=1, dtype=np.float32)).astype(BF16)
    mem2 = mem.reshape(b * mem.shape[1], d)

    x2 = x.reshape(n, d)
    for l in range(depth):
        w = _layer_weights(l, w_in, nsa_cmp_pos, nsa_cmp_w1, nsa_cmp_w2, mla_w_uq, mla_w_ukv,
                           w_branch)
        x2 = _ffn_ln(x2, w1, w3, w2, lng, lnb, l, 0, 0, alpha)
        x3 = x2.reshape(b, s, d)

        q, qr, kc, vc, ks, vs, kw, vw, gates = _proja(x3, w["wa"], cos_a, sin_a)
        zk = kc.reshape(b, NSA_KV, N_CMP, CMP_HALF)
        zv = vc.reshape(b, NSA_KV, N_CMP, CMP_HALF)
        kcmp, vcmp = _cmp(zk, zv, w["w1p"], w["pe8"], w["w2p"], shift)
        ya = _nsa_attn(q, qr, kcmp, vcmp, ks, vs, kw, vw, gates, ovl, place)

        pb = _proj(x2, w["wb"], F32).reshape(b, s, 4 * HB * HK)
        yb = _hgrn(pb, lower[l], hgrn_norm_g[l].reshape(1, HV))

        mq, mk, mv = _mlap(x3, w["wc"], mla_q_norm_g[l].reshape(1, Q_RANK),
                           mla_kv_norm_g[l].reshape(1, KV_RANK), w["wuq"], w["wuqr"],
                           w["wuk"], w["wuv"], cos_c, sin_c)
        yc = _mla_attn(mq, mk, mv)

        x2 = _merge(x2, ya.reshape(n, NSA_QW), yb.reshape(n, MIX_W), yc.reshape(n, MLA_W),
                    w["wm"], w["wba"], w["wbb"], w["wbc"], wo, lng, lnb, l, alpha)

        xk_l = _proj(mem2, xk[l], BF16).reshape(b, -1, d)
        xv_l = _proj(mem2, xv[l], BF16).reshape(b, -1, d)
        x2 = _xattn(x2.reshape(b, s, d), xq, xk_l, xv_l, xo, lng, lnb, l, alpha).reshape(n, d)

        x2 = _ffn_ln(x2, w1, w3, w2, lng, lnb, l, 1, 3, alpha)
    return x2.reshape(b, s, d)
```

```python
import functools
import math

import numpy as np
import jax
import jax.numpy as jnp
from jax import lax
from jax.experimental import pallas as pl
from jax.experimental.pallas import tpu as pltpu

F32 = jnp.float32
BF16 = jnp.bfloat16

D_MODEL = 1024
MIX_W = D_MODEL // 2
NSA_DH = 64
NSA_HEADS = 8
NSA_KV = 2
NSA_R = 4
NSA_ROT = 16
CMP_L = 32
CMP_D = 16
CMP_HID = 256
SEL_L = 64
N_SEL = 8
WINDOW = 256
HB = 4
HK = 128
HV = 128
HGRN_CHUNK = 64
HGRN_SUB = 4
HC = 8
NOPE = 64
ROPE_D = 32
VD = 64
Q_RANK = 384
KV_RANK = 256
XA_HEADS = 4
XA_DH = D_MODEL // XA_HEADS
D_FF = 2816
ROPE_THETA = 500000.0
LN_EPS = 1e-5
RMS_EPS = 1e-6
NEG = -1e30
BIG = 1e9
F_MIN = 1e-20

LOG2E = 1.4426950408889634
LANE = 128
SEL_LANE0 = NSA_DH
DEN_LANE = 64
VMEM_LIMIT = 56 * 1024 * 1024

IN_SIZES = (512, 128, 128, 128, 128, 128, 128, 24, 512, 512, 512, 512, 384, 256, 32, 3072)
IN_OFF = tuple(int(v) for v in np.concatenate([[0], np.cumsum(IN_SIZES)]))


def _dot(a, b):
    return jnp.dot(a, b, preferred_element_type=F32)


def _dot_nt(a, b):
    return lax.dot_general(a, b, (((1,), (1,)), ((), ())), preferred_element_type=F32)


def _ln(y, g, b):
    mu = jnp.mean(y, axis=-1, keepdims=True)
    yc = y - mu
    var = jnp.mean(yc * yc, axis=-1, keepdims=True)
    return yc * lax.rsqrt(var + LN_EPS) * g + b


def _params(*sem):
    return pltpu.CompilerParams(dimension_semantics=sem, vmem_limit_bytes=VMEM_LIMIT)


def _resident(shape, index_map):
    return pl.BlockSpec(shape, index_map, pipeline_mode=pl.Buffered(1))


FFN_TM = 1024
FFN_TF = 256


def _ffn_kernel(x_ref, w1_ref, w3_ref, w2_ref, g_ref, b_ref, o_ref, acc_ref, xb_ref, *, alpha):
    j = pl.program_id(1)

    @pl.when(j == 0)
    def _():
        acc_ref[...] = jnp.zeros_like(acc_ref)
        xb_ref[...] = x_ref[...].astype(BF16)

    xb = xb_ref[...]
    h1 = _dot(xb, w1_ref[...])
    h3 = _dot(xb, w3_ref[...])
    h = (h1 * jax.nn.sigmoid(h1)) * h3
    acc_ref[...] += _dot(h.astype(BF16), w2_ref[...])

    @pl.when(j == pl.num_programs(1) - 1)
    def _():
        y = alpha * x_ref[...] + 0.5 * acc_ref[...]
        o_ref[...] = _ln(y, g_ref[...], b_ref[...])


def _ffn_ln(x2, w1, w3, w2, lng, lnb, l, which, ln_idx, alpha):
    n = x2.shape[0]
    tm = min(FFN_TM, n)
    grid = (n // tm, D_FF // FFN_TF)
    return pl.pallas_call(
        functools.partial(_ffn_kernel, alpha=alpha),
        grid=grid,
        in_specs=[
            pl.BlockSpec((tm, D_MODEL), lambda m, j: (m, 0)),
            pl.BlockSpec((None, None, None, D_MODEL, FFN_TF), lambda m, j: (l, which, j, 0, 0)),
            pl.BlockSpec((None, None, None, D_MODEL, FFN_TF), lambda m, j: (l, which, j, 0, 0)),
            pl.BlockSpec((None, None, FFN_TF, D_MODEL), lambda m, j: (l, which, j, 0)),
            pl.BlockSpec((None, None, 1, D_MODEL), lambda m, j: (l, ln_idx, 0, 0)),
            pl.BlockSpec((None, None, 1, D_MODEL), lambda m, j: (l, ln_idx, 0, 0)),
        ],
        out_specs=pl.BlockSpec((tm, D_MODEL), lambda m, j: (m, 0)),
        out_shape=jax.ShapeDtypeStruct((n, D_MODEL), F32),
        scratch_shapes=[pltpu.VMEM((tm, D_MODEL), F32), pltpu.VMEM((tm, D_MODEL), BF16)],
        compiler_params=_params("parallel", "arbitrary"),
        name="ffn_ln",
    )(x2, w1, w3, w2, lng, lnb)


def _proj_kernel(x_ref, w_ref, o_ref):
    o_ref[...] = _dot(x_ref[...].astype(BF16), w_ref[...]).astype(o_ref.dtype)


def _proj(x2, w, out_dtype, tm=1024, tn=2048):
    n, k = x2.shape
    c = w.shape[1]
    tm = min(tm, n)
    tn = min(tn, c)
    return pl.pallas_call(
        _proj_kernel,
        grid=(n // tm, c // tn),
        in_specs=[pl.BlockSpec((tm, k), lambda m, j: (m, 0)),
                  pl.BlockSpec((k, tn), lambda m, j: (0, j))],
        out_specs=pl.BlockSpec((tm, tn), lambda m, j: (m, j)),
        out_shape=jax.ShapeDtypeStruct((n, c), out_dtype),
        compiler_params=_params("parallel", "arbitrary"),
        name="proj",
    )(x2, w)


NSA_QW = NSA_HEADS * LANE
NSA_KW = NSA_KV * LANE
A_Q, A_QROT = 0, NSA_QW
A_KC = 2 * NSA_QW
A_VC = A_KC + NSA_KW
A_KS = A_VC + NSA_KW
A_KSROT = A_KS + NSA_KW
A_VS = A_KSROT + NSA_KW
A_KWN = A_VS + NSA_KW
A_KWROT = A_KWN + NSA_KW
A_VW = A_KWROT + NSA_KW
A_GATE = A_VW + NSA_KW
A_COLS = A_GATE + NSA_KW
PROJA_TM = 512


def _proja_kernel(x_ref, w_ref, cos_ref, sin_ref,
                  q_ref, qr_ref, kc_ref, vc_ref, ks_ref, vs_ref, kw_ref, vw_ref, g_ref):
    xb = x_ref[...].astype(BF16)
    cos = cos_ref[...]
    sin = sin_ref[...]
    cos2 = jnp.concatenate([cos, cos], axis=1)
    sin2 = jnp.concatenate([sin, sin], axis=1)

    def mm(c0, width):
        return _dot(xb, w_ref[:, c0:c0 + width])

    qs = NSA_DH ** -0.5 * LOG2E
    for p in range(NSA_QW // NSA_KW):
        c = p * NSA_KW
        y = mm(A_Q + c, NSA_KW)
        yr = mm(A_QROT + c, NSA_KW)
        q_ref[:, c:c + NSA_KW] = (y * qs).astype(BF16)
        qr_ref[:, c:c + NSA_KW] = ((y * cos2 + yr * sin2) * qs).astype(BF16)

    def put(ref, y):
        for g in range(NSA_KV):
            ref[g] = y[:, g * LANE:(g + 1) * LANE].astype(BF16)

    tm = x_ref.shape[0]
    lane2 = lax.broadcasted_iota(jnp.int32, (tm, NSA_KW), 1) % LANE
    tok = pl.program_id(1) * tm + lax.broadcasted_iota(jnp.int32, (tm, NSA_KW), 0)
    blk_flag = jnp.where(lane2 - SEL_LANE0 == tok // SEL_L, 1.0, 0.0)
    den_flag = jnp.where(lane2 == DEN_LANE, 1.0, 0.0)

    put(kc_ref, mm(A_KC, NSA_KW))
    put(vc_ref, mm(A_VC, NSA_KW))
    put(ks_ref, mm(A_KS, NSA_KW) * cos2 + mm(A_KSROT, NSA_KW) * sin2 + blk_flag)
    put(vs_ref, mm(A_VS, NSA_KW) + den_flag)
    put(kw_ref, mm(A_KWN, NSA_KW) * cos2 + mm(A_KWROT, NSA_KW) * sin2)
    put(vw_ref, mm(A_VW, NSA_KW) + den_flag)
    g_ref[...] = mm(A_GATE, NSA_KW)


def _proja(x3, wa, cos_a, sin_a):
    b, s, _ = x3.shape
    tm = PROJA_TM
    kv_shape = jax.ShapeDtypeStruct((b, NSA_KV, s, LANE), BF16)
    kv_spec = pl.BlockSpec((None, NSA_KV, tm, LANE), lambda bi, m: (bi, 0, m, 0))
    return pl.pallas_call(
        _proja_kernel,
        grid=(b, s // tm),
        in_specs=[
            pl.BlockSpec((None, tm, D_MODEL), lambda bi, m: (bi, m, 0)),
            _resident((D_MODEL, A_COLS), lambda bi, m: (0, 0)),
            pl.BlockSpec((None, tm, LANE), lambda bi, m: (bi, m, 0)),
            pl.BlockSpec((None, tm, LANE), lambda bi, m: (bi, m, 0)),
        ],
        out_specs=[
            pl.BlockSpec((None, tm, NSA_QW), lambda bi, m: (bi, m, 0)),
            pl.BlockSpec((None, tm, NSA_QW), lambda bi, m: (bi, m, 0)),
            kv_spec, kv_spec, kv_spec, kv_spec, kv_spec, kv_spec,
            pl.BlockSpec((None, tm, NSA_KW), lambda bi, m: (bi, m, 0)),
        ],
        out_shape=[
            jax.ShapeDtypeStruct((b, s, NSA_QW), BF16),
            jax.ShapeDtypeStruct((b, s, NSA_QW), BF16),
            kv_shape, kv_shape, kv_shape, kv_shape, kv_shape, kv_shape,
            jax.ShapeDtypeStruct((b, s, NSA_KW), F32),
        ],
        compiler_params=_params("parallel", "parallel"),
        name="nsa_proj",
    )(x3, wa, cos_a, sin_a)


N_CMP = 128
CMP_HALF = CMP_D * LANE


def _gelu_tanh(x):
    return 0.5 * x * (1.0 + jnp.tanh(0.7978845608028654 * (x + 0.044715 * x * x * x)))


def _cmp_kernel(zk_ref, zv_ref, w1_ref, pe_ref, w2_ref, shift_ref, ok_ref, ov_ref):
    shift = shift_ref[...]
    for which, (z_ref, o_ref) in enumerate(((zk_ref, ok_ref), (zv_ref, ov_ref))):
        z = z_ref[...]
        zn = _dot(shift, z).astype(BF16)
        bias = _dot(pe_ref[which], w1_ref[which])[0:1]
        pre = (_dot(z, w1_ref[which, 0:CMP_HALF, :])
               + _dot(zn, w1_ref[which, CMP_HALF:2 * CMP_HALF, :]) + bias)
        h = _gelu_tanh(pre)
        o_ref[...] = _dot(h.astype(BF16), w2_ref[which]).astype(BF16)


def _cmp(zk, zv, w1p, pe8, w2p, shift):
    b = zk.shape[0]
    z_spec = pl.BlockSpec((None, None, N_CMP, CMP_HALF), lambda bi, g: (bi, g, 0, 0))
    o_spec = pl.BlockSpec((None, None, N_CMP, LANE), lambda bi, g: (bi, g, 0, 0))
    o_shape = jax.ShapeDtypeStruct((b, NSA_KV, N_CMP, LANE), BF16)
    return pl.pallas_call(
        _cmp_kernel,
        grid=(b, NSA_KV),
        in_specs=[
            z_spec, z_spec,
            _resident((2, 2 * CMP_HALF, CMP_HID), lambda bi, g: (0, 0, 0)),
            _resident((2, 8, 2 * CMP_HALF), lambda bi, g: (0, 0, 0)),
            _resident((2, CMP_HID, LANE), lambda bi, g: (0, 0, 0)),
            _resident((N_CMP, N_CMP), lambda bi, g: (0, 0)),
        ],
        out_specs=[o_spec, o_spec],
        out_shape=[o_shape, o_shape],
        compiler_params=_params("parallel", "parallel"),
        name="nsa_cmp",
    )(zk, zv, w1p, pe8, w2p, shift)


NSA_TQ = 256
NSA_TK = 512
N_SLC = 32
WIN_SLAB = WINDOW + NSA_TQ


def _nsa_kernel(q_ref, qr_ref, kc_ref, vc_ref, ks_ref, vs_ref, kw_ref, vw_ref, gt_ref, ovl_ref,
                place_ref, o_ref):
    tq, tk, r_heads = NSA_TQ, NSA_TK, NSA_R
    i = pl.program_id(2)
    t0 = i * tq
    tpos = t0 + lax.broadcasted_iota(jnp.int32, (tq, 1), 0)
    lane = lax.broadcasted_iota(jnp.int32, (1, LANE), 1)

    def stack(ref):
        return jnp.concatenate([ref[:, r * LANE:(r + 1) * LANE] for r in range(r_heads)], axis=0)

    gt = jax.nn.sigmoid(gt_ref[...])

    def gate(c):
        return jnp.stack([gt[:, 3 * r + c:3 * r + c + 1] for r in range(r_heads)], axis=0)

    q4w = stack(qr_ref)
    w0 = pl.multiple_of(jnp.maximum(t0 - WINDOW, 0), LANE)
    kw = kw_ref[pl.ds(w0, WIN_SLAB), :]
    vw = vw_ref[pl.ds(w0, WIN_SLAB), :]
    kpos = w0 + lax.broadcasted_iota(jnp.int32, (1, WIN_SLAB), 1)
    w_bias = jnp.where((kpos <= tpos) & (kpos > tpos - WINDOW), 0.0, NEG)
    s = _dot_nt(q4w, kw).reshape(r_heads, tq, WIN_SLAB) + w_bias[None]
    p = jnp.exp2(s - jnp.max(s, axis=-1, keepdims=True))
    o_win = _dot(p.reshape(r_heads * tq, WIN_SLAB).astype(BF16), vw).reshape(r_heads, tq, LANE)
    o_part = gate(2) * (o_win / o_win[:, :, DEN_LANE:DEN_LANE + 1])

    cmask = ((lane * CMP_D + (CMP_L - 1) <= tpos) & (lane < N_CMP - 1))[None]
    s = _dot_nt(stack(q_ref), kc_ref[...]).reshape(r_heads, tq, N_CMP)
    s = jnp.where(cmask, s, NEG)
    e = jnp.where(cmask, jnp.exp2(s - jnp.max(s, axis=-1, keepdims=True)), 0.0)
    den = jnp.sum(e, axis=-1, keepdims=True)
    p = e / jnp.where(den > 0.0, den, 1.0)
    psum = jnp.sum(p, axis=0)
    o_cmp = _dot(p.reshape(r_heads * tq, N_CMP).astype(BF16), vc_ref[...])
    o_part = o_part + gate(0) * o_cmp.reshape(r_heads, tq, LANE)

    imp = lax.dot_general(ovl_ref[...], psum, (((1,), (1,)), ((), ())),
                          precision=lax.Precision.HIGHEST, preferred_element_type=F32)
    blk = lax.broadcasted_iota(jnp.int32, (N_SLC, 1), 0)
    tpos_l = t0 + lax.broadcasted_iota(jnp.int32, (1, tq), 1)
    cur = tpos_l // SEL_L
    valid = blk * SEL_L <= tpos_l
    forced = (blk == 0) | (blk == cur) | (blk == cur - 1)
    score = jnp.where(valid & forced, BIG, jnp.where(valid, imp, -BIG))
    beats = []
    for j in range(N_SLC):
        sj = score[j:j + 1, :]
        beats.append(jnp.where((sj > score) | ((sj == score) & (j < blk)), 1.0, 0.0))
    while len(beats) > 1:
        beats = [a + b for a, b in zip(beats[0::2], beats[1::2])]
    sel_t = jnp.where((beats[0] < N_SEL) & valid, 1.0, 0.0)
    sel_q = lax.dot_general(sel_t, place_ref[...], (((0,), (0,)), ((), ())),
                            preferred_element_type=F32)
    in_flags = jnp.where((lane >= SEL_LANE0) & (lane < SEL_LANE0 + N_SLC), 1.0, 0.0)
    q_bias = (sel_q - in_flags) * (-NEG)
    q4 = (q4w.astype(F32).reshape(r_heads, tq, LANE) + q_bias[None]).astype(BF16)
    q4 = q4.reshape(r_heads * tq, LANE)

    def sel_step(c, carry, causal):
        m, acc = carry
        k0 = pl.multiple_of(c * tk, tk)
        k = ks_ref[pl.ds(k0, tk), :]
        v = vs_ref[pl.ds(k0, tk), :]
        s = _dot_nt(q4, k).reshape(r_heads, tq, tk)
        if causal:
            kpos = k0 + lax.broadcasted_iota(jnp.int32, (1, tk), 1)
            s = jnp.where((kpos <= tpos)[None], s, NEG)
        m_new = jnp.maximum(m, jnp.max(s, axis=-1, keepdims=True))
        p = jnp.exp2(s - m_new)
        pv = _dot(p.reshape(r_heads * tq, tk).astype(BF16), v).reshape(r_heads, tq, LANE)
        return m_new, jnp.exp2(m - m_new) * acc + pv

    last = (t0 + tq - 1) // tk
    carry = (jnp.full((r_heads, tq, 1), NEG, F32), jnp.zeros((r_heads, tq, LANE), F32))
    carry = lax.fori_loop(0, last, functools.partial(sel_step, causal=False), carry)
    _, acc_s = sel_step(last, carry, True)
    o = o_part + gate(1) * (acc_s / acc_s[:, :, DEN_LANE:DEN_LANE + 1])
    for r in range(r_heads):
        o_ref[:, r * LANE:(r + 1) * LANE] = o[r].astype(BF16)


def _nsa_attn(q, qr, kcmp, vcmp, ks, vs, kw, vw, gates, ovl, place):
    b, s, _ = q.shape
    tq = NSA_TQ
    gw = NSA_R * LANE
    q_spec = pl.BlockSpec((None, tq, gw), lambda bi, g, i: (bi, i, g))
    c_spec = pl.BlockSpec((None, None, N_CMP, LANE), lambda bi, g, i: (bi, g, 0, 0))
    kv_spec = pl.BlockSpec((None, None, s, LANE), lambda bi, g, i: (bi, g, 0, 0))
    return pl.pallas_call(
        _nsa_kernel,
        grid=(b, NSA_KV, s // tq),
        in_specs=[q_spec, q_spec, c_spec, c_spec, kv_spec, kv_spec, kv_spec, kv_spec,
                  pl.BlockSpec((None, tq, LANE), lambda bi, g, i: (bi, i, g)),
                  _resident((N_SLC, N_CMP), lambda bi, g, i: (0, 0)),
                  _resident((N_SLC, LANE), lambda bi, g, i: (0, 0))],
        out_specs=pl.BlockSpec((None, tq, gw), lambda bi, g, i: (bi, i, g)),
        out_shape=jax.ShapeDtypeStruct((b, s, NSA_QW), BF16),
        compiler_params=_params("parallel", "parallel", "arbitrary"),
        name="nsa_attn",
    )(q, qr, kcmp, vcmp, ks, vs, kw, vw, gates, ovl, place)


def _hgrn_kernel(q_ref, z_ref, v_ref, go_ref, lb_ref, ng_ref, o_ref, st_ref):
    s_len = q_ref.shape[0]
    c_len, sub_len = HGRN_CHUNK, HGRN_SUB
    nc = s_len // c_len
    q = q_ref[...]
    z = z_ref[...]
    v = v_ref[...]
    lb = lb_ref[...]
    t = lax.broadcasted_iota(jnp.int32, (s_len, 1), 0)
    t_sub = t % sub_len

    f = lb + (1.0 - lb) * jax.nn.sigmoid(z)
    lf = jnp.log(jnp.maximum(f, F_MIN))
    k = (1.0 - lb) * jax.nn.sigmoid(-z)

    row = lax.broadcasted_iota(jnp.int32, (nc, c_len, c_len), 1)
    col = lax.broadcasted_iota(jnp.int32, (nc, c_len, c_len), 2)
    b3 = jnp.einsum('cts,csd->ctd', jnp.where(col <= row, 1.0, 0.0), lf.reshape(nc, c_len, HK),
                    precision=lax.Precision.HIGHEST, preferred_element_type=F32)
    b = b3.reshape(s_len, HK)
    k3 = k.reshape(nc, c_len, HK)
    vb3 = v.astype(BF16).reshape(nc, c_len, HV)

    a_intra = None
    h = c_len // 2
    while h >= sub_len:
        blk = b.reshape(s_len // (2 * h), 2 * h, HK)
        e = jnp.exp(-jnp.abs(blk[:, h - 1:h, :] - blk)).reshape(s_len, HK)
        upper = (t // h) % 2 == 1
        lq = jnp.where(upper, q * e, 0.0).astype(BF16).reshape(nc, c_len, HK)
        rk = jnp.where(upper, 0.0, k * e).astype(BF16).reshape(nc, c_len, HK)
        a = jnp.einsum('ctk,csk->cts', lq, rk, preferred_element_type=F32)
        if 2 * h < c_len:
            a = jnp.where(row // (2 * h) == col // (2 * h), a, 0.0)
        a_intra = a if a_intra is None else a_intra + a
        h //= 2
    o = jnp.einsum('cts,csd->ctd', a_intra.astype(BF16), vb3,
                   preferred_element_type=F32).reshape(s_len, HV)

    o = o + jnp.sum(q * k, axis=-1, keepdims=True) * v
    for d in range(1, sub_len):
        e = jnp.exp(jnp.where(t_sub >= d, b - pltpu.roll(b, d, axis=0), NEG))
        w = jnp.sum(q * pltpu.roll(k, d, axis=0) * e, axis=-1, keepdims=True)
        o = o + w * pltpu.roll(v, d, axis=0)

    b_last = b3[:, c_len - 1:c_len, :]
    kd3 = (k3 * jnp.exp(b_last - b3)).astype(BF16)
    upd = jnp.einsum('csv,csk->cvk', vb3, kd3, preferred_element_type=F32)
    dec = jnp.exp(b_last)
    state = jnp.zeros((HV, HK), F32)
    for c in range(nc):
        st_ref[c] = state.astype(BF16)
        state = state * dec[c] + upd[c]
    qe3 = (q * jnp.exp(b)).astype(BF16).reshape(nc, c_len, HK)
    o = o + jnp.einsum('ctk,cvk->ctv', qe3, st_ref[...],
                       preferred_element_type=F32).reshape(s_len, HV)

    o = o * lax.rsqrt(jnp.mean(o * o, axis=-1, keepdims=True) + RMS_EPS) * ng_ref[...]
    go = go_ref[...]
    o_ref[...] = (o * (go * jax.nn.sigmoid(go))).astype(BF16)


def _hgrn(pb, lb, ng):
    b, s, _ = pb.shape

    def col(part):
        return pl.BlockSpec((None, s, HK), lambda bi, h: (bi, 0, part * HB + h))

    return pl.pallas_call(
        _hgrn_kernel,
        grid=(b, HB),
        in_specs=[col(0), col(1), col(2), col(3),
                  pl.BlockSpec((None, 1, HK), lambda bi, h: (h, 0, 0)),
                  pl.BlockSpec((1, HV), lambda bi, h: (0, 0))],
        out_specs=pl.BlockSpec((None, s, HV), lambda bi, h: (bi, 0, h)),
        out_shape=jax.ShapeDtypeStruct((b, s, HB * HV), BF16),
        scratch_shapes=[pltpu.VMEM((s // HGRN_CHUNK, HV, HK), BF16)],
        compiler_params=_params("parallel", "parallel"),
        name="hgrn",
    )(pb, pb, pb, pb, lb, ng)


MLA_W = HC * LANE
C_CQ, C_CKV, C_KR, C_KRROT = 0, Q_RANK, Q_RANK + KV_RANK, Q_RANK + KV_RANK + LANE
C_COLS = C_KRROT + LANE
MLAP_TM = 512


def _rms(x, g):
    return x * lax.rsqrt(jnp.mean(x * x, axis=-1, keepdims=True) + RMS_EPS) * g


def _mlap_kernel(x_ref, wc_ref, qg_ref, kvg_ref, wuq_ref, wuqr_ref, wuk_ref, wuv_ref,
                 cos_ref, sin_ref, q_ref, k_ref, v_ref):
    xb = x_ref[...].astype(BF16)
    c = _dot(xb, wc_ref[...])
    cos = cos_ref[...]
    sin = sin_ref[...]
    nq = _rms(c[:, C_CQ:C_CKV], qg_ref[...]).astype(BF16)
    nkv = _rms(c[:, C_CKV:C_KR], kvg_ref[...]).astype(BF16)
    k_pe = c[:, C_KR:C_KRROT] * cos + c[:, C_KRROT:C_COLS] * sin
    qs = (NOPE + ROPE_D) ** -0.5 * LOG2E
    cos_q = cos * qs
    sin_q = sin * qs
    den_flag = jnp.where(lax.broadcasted_iota(jnp.int32, (1, LANE), 1) == DEN_LANE, 1.0, 0.0)
    for h in range(HC):
        hs = slice(h * LANE, (h + 1) * LANE)
        q_ref[:, hs] = (_dot(nq, wuq_ref[:, hs]) * cos_q
                        + _dot(nq, wuqr_ref[:, hs]) * sin_q).astype(BF16)
        k_ref[:, hs] = (_dot(nkv, wuk_ref[:, hs]) + k_pe).astype(BF16)
        v_ref[:, hs] = (_dot(nkv, wuv_ref[:, hs]) + den_flag).astype(BF16)


def _mlap(x3, wc, qg, kvg, wuq, wuqr, wuk, wuv, cos_c, sin_c):
    b, s, _ = x3.shape
    tm = MLAP_TM
    o_spec = pl.BlockSpec((None, tm, MLA_W), lambda bi, m: (bi, m, 0))
    o_shape = jax.ShapeDtypeStruct((b, s, MLA_W), BF16)
    t_spec = pl.BlockSpec((None, tm, LANE), lambda bi, m: (bi, m, 0))
    return pl.pallas_call(
        _mlap_kernel,
        grid=(b, s // tm),
        in_specs=[
            pl.BlockSpec((None, tm, D_MODEL), lambda bi, m: (bi, m, 0)),
            _resident((D_MODEL, C_COLS), lambda bi, m: (0, 0)),
            _resident((1, Q_RANK), lambda bi, m: (0, 0)),
            _resident((1, KV_RANK), lambda bi, m: (0, 0)),
            _resident((Q_RANK, MLA_W), lambda bi, m: (0, 0)),
            _resident((Q_RANK, MLA_W), lambda bi, m: (0, 0)),
            _resident((KV_RANK, MLA_W), lambda bi, m: (0, 0)),
            _resident((KV_RANK, MLA_W), lambda bi, m: (0, 0)),
            t_spec, t_spec,
        ],
        out_specs=[o_spec, o_spec, o_spec],
        out_shape=[o_shape, o_shape, o_shape],
        compiler_params=_params("parallel", "parallel"),
        name="mla_proj",
    )(x3, wc, qg, kvg, wuq, wuqr, wuk, wuv, cos_c, sin_c)


MLA_TQ = 512
MLA_TK = 512


MLA_HPS = 4


def _mla_attn_kernel(q_ref, k_ref, v_ref, o_ref):
    tq, tk = MLA_TQ, MLA_TK
    i = pl.program_id(2)
    t0 = i * tq
    tpos = t0 + lax.broadcasted_iota(jnp.int32, (tq, 1), 0)
    heads = [slice(h * LANE, (h + 1) * LANE) for h in range(MLA_HPS)]
    qs = [q_ref[:, hs] for hs in heads]

    def step(c, carry, causal):
        k0 = pl.multiple_of(c * tk, tk)
        if causal:
            keep = k0 + lax.broadcasted_iota(jnp.int32, (1, tk), 1) <= tpos
        out = []
        for h, hs in enumerate(heads):
            m, acc = carry[h]
            s = _dot_nt(qs[h], k_ref[pl.ds(k0, tk), hs])
            if causal:
                s = jnp.where(keep, s, NEG)
            m_new = jnp.maximum(m, jnp.max(s, axis=-1, keepdims=True))
            p = jnp.exp2(s - m_new)
            pv = _dot(p.astype(BF16), v_ref[pl.ds(k0, tk), hs])
            out.append((m_new, jnp.exp2(m - m_new) * acc + pv))
        return tuple(out)

    last = (t0 + tq - 1) // tk
    carry = tuple((jnp.full((tq, 1), NEG, F32), jnp.zeros((tq, LANE), F32)) for _ in heads)
    carry = lax.fori_loop(0, last, functools.partial(step, causal=False), carry)
    carry = step(last, carry, True)
    for (_, acc), hs in zip(carry, heads):
        o_ref[:, hs] = (acc / acc[:, DEN_LANE:DEN_LANE + 1]).astype(BF16)


def _mla_attn(q, k, v):
    b, s, _ = q.shape
    tq = MLA_TQ
    gw = MLA_HPS * LANE
    return pl.pallas_call(
        _mla_attn_kernel,
        grid=(b, HC // MLA_HPS, s // tq),
        in_specs=[pl.BlockSpec((None, tq, gw), lambda bi, h, i: (bi, i, h)),
                  pl.BlockSpec((None, s, gw), lambda bi, h, i: (bi, 0, h)),
                  pl.BlockSpec((None, s, gw), lambda bi, h, i: (bi, 0, h))],
        out_specs=pl.BlockSpec((None, tq, gw), lambda bi, h, i: (bi, i, h)),
        out_shape=jax.ShapeDtypeStruct((b, s, MLA_W), BF16),
        compiler_params=_params("parallel", "parallel", "arbitrary"),
        name="mla_attn",
    )(q, k, v)


MERGE_TM = 512


def _merge_kernel(x_ref, ya_ref, yb_ref, yc_ref, wm_ref, wa_ref, wb_ref, wc_ref, wo_ref,
                  g_ref, b_ref, o_ref, *, alpha):
    x = x_ref[...]
    xb = x.astype(BF16)
    mixed = jnp.zeros(x.shape, F32)
    for idx, (y_ref, w_ref) in enumerate(((ya_ref, wa_ref), (yb_ref, wb_ref), (yc_ref, wc_ref))):
        gate = jax.nn.sigmoid(_dot(xb, wm_ref[:, idx * D_MODEL:(idx + 1) * D_MODEL]))
        mixed = mixed + gate * _dot(y_ref[...], w_ref[...])
    y = alpha * x + _dot(mixed.astype(BF16), wo_ref[...])
    o_ref[...] = _ln(y, g_ref[...], b_ref[...])


def _merge(x2, ya, yb, yc, wm, wa, wb, wc, wo, lng, lnb, l, alpha):
    n = x2.shape[0]
    tm = min(MERGE_TM, n)

    def rows(width):
        return pl.BlockSpec((tm, width), lambda m: (m, 0))

    return pl.pallas_call(
        functools.partial(_merge_kernel, alpha=alpha),
        grid=(n // tm,),
        in_specs=[rows(D_MODEL), rows(NSA_QW), rows(MIX_W), rows(MLA_W),
                  _resident((D_MODEL, 3 * D_MODEL), lambda m: (0, 0)),
                  _resident((NSA_QW, D_MODEL), lambda m: (0, 0)),
                  _resident((MIX_W, D_MODEL), lambda m: (0, 0)),
                  _resident((MLA_W, D_MODEL), lambda m: (0, 0)),
                  _resident((None, D_MODEL, D_MODEL), lambda m: (l, 0, 0)),
                  _resident((None, None, 1, D_MODEL), lambda m: (l, 1, 0, 0)),
                  _resident((None, None, 1, D_MODEL), lambda m: (l, 1, 0, 0))],
        out_specs=rows(D_MODEL),
        out_shape=jax.ShapeDtypeStruct((n, D_MODEL), F32),
        compiler_params=_params("parallel"),
        name="merge",
    )(x2, ya, yb, yc, wm, wa, wb, wc, wo, lng, lnb)


XA_TM = 512


def _xattn_kernel(x_ref, wq_ref, k_ref, v_ref, wo_ref, g_ref, b_ref, o_ref, *, alpha):
    x = x_ref[...]
    xb = x.astype(BF16)
    scale = XA_DH ** -0.5
    heads = []
    for h in range(XA_HEADS):
        hs = slice(h * XA_DH, (h + 1) * XA_DH)
        qh = _dot(xb, wq_ref[:, hs]).astype(BF16)
        s = _dot_nt(qh, k_ref[:, hs]) * scale
        e = jnp.exp(s - jnp.max(s, axis=-1, keepdims=True))
        p = e / jnp.sum(e, axis=-1, keepdims=True)
        heads.append(_dot(p.astype(BF16), v_ref[:, hs]).astype(BF16))
    att = jnp.concatenate(heads, axis=1)
    y = alpha * x + _dot(att, wo_ref[...])
    o_ref[...] = _ln(y, g_ref[...], b_ref[...])


def _xattn(x3, wq, k, v, wo, lng, lnb, l, alpha):
    b, s, _ = x3.shape
    m_len = k.shape[1]
    tm = XA_TM
    return pl.pallas_call(
        functools.partial(_xattn_kernel, alpha=alpha),
        grid=(b, s // tm),
        in_specs=[pl.BlockSpec((None, tm, D_MODEL), lambda bi, m: (bi, m, 0)),
                  _resident((None, D_MODEL, D_MODEL), lambda bi, m: (l, 0, 0)),
                  pl.BlockSpec((None, m_len, D_MODEL), lambda bi, m: (bi, 0, 0)),
                  pl.BlockSpec((None, m_len, D_MODEL), lambda bi, m: (bi, 0, 0)),
                  _resident((None, D_MODEL, D_MODEL), lambda bi, m: (l, 0, 0)),
                  _resident((None, None, 1, D_MODEL), lambda bi, m: (l, 2, 0, 0)),
                  _resident((None, None, 1, D_MODEL), lambda bi, m: (l, 2, 0, 0))],
        out_specs=pl.BlockSpec((None, tm, D_MODEL), lambda bi, m: (bi, m, 0)),
        out_shape=jax.ShapeDtypeStruct((b, s, D_MODEL), F32),
        compiler_params=_params("parallel", "parallel"),
        name="xattn",
    )(x3, wq, k, v, wo, lng, lnb)


def _pad_heads(w, n_heads, dh):
    k = w.shape[0]
    w = w.reshape(k, n_heads, dh)
    return jnp.pad(w, ((0, 0), (0, 0), (0, LANE - dh))).reshape(k, n_heads * LANE)


def _rot_cols(w, n_heads, dh, off, rot):
    k = w.shape[0]
    half = rot // 2
    w = w.reshape(k, n_heads, dh)
    x1 = w[:, :, off:off + half]
    x2 = w[:, :, off + half:off + rot]
    out = jnp.zeros_like(w)
    out = out.at[:, :, off:off + half].set(-x2)
    out = out.at[:, :, off + half:off + rot].set(x1)
    return out.reshape(k, n_heads * dh)


def _pad_rows(w, n_heads, dh):
    n = w.shape[1]
    w = w.reshape(n_heads, dh, n)
    return jnp.pad(w, ((0, 0), (0, LANE - dh), (0, 0))).reshape(n_heads * LANE, n)


def _rope_tables(positions, rot, off):
    half = rot // 2
    inv = ROPE_THETA ** (-jnp.arange(half, dtype=F32) / half)
    ang = positions.astype(F32)[..., None] * inv
    cos, sin = jnp.cos(ang), jnp.sin(ang)
    shape = positions.shape
    cos_t = jnp.concatenate([jnp.ones(shape + (off,), F32), cos, cos,
                             jnp.ones(shape + (LANE - off - rot,), F32)], axis=-1)
    sin_t = jnp.concatenate([jnp.zeros(shape + (off,), F32), sin, sin,
                             jnp.zeros(shape + (LANE - off - rot,), F32)], axis=-1)
    return cos_t, sin_t


def _overlap_matrix(s_len):
    n_cmp = (s_len - CMP_L) // CMP_D + 1
    n_slc = s_len // SEL_L
    start = np.arange(n_cmp) * CMP_D
    j = np.arange(n_slc)
    ov = np.clip(np.minimum(start[:, None] + CMP_L, (j[None, :] + 1) * SEL_L)
                 - np.maximum(start[:, None], j[None, :] * SEL_L), 0, None) / CMP_L
    out = np.zeros((N_SLC, N_CMP), np.float32)
    out[:n_slc, :n_cmp] = ov.T
    place = np.zeros((N_SLC, LANE), np.float32)
    place[np.arange(N_SLC), SEL_LANE0 + np.arange(N_SLC)] = 1.0
    return jnp.asarray(out), jnp.asarray(place)


def _layer_weights(l, w_in, nsa_cmp_pos, nsa_cmp_w1, nsa_cmp_w2, mla_w_uq, mla_w_ukv, w_branch):
    wl = w_in[l]
    part = [wl[:, IN_OFF[i]:IN_OFF[i + 1]] for i in range(len(IN_SIZES))]
    (a_q, a_kc, a_vc, a_ks, a_vs, a_kw, a_vw, a_gate, b_q, b_f, b_i, b_g,
     c_q, c_kv, c_kr, merge) = part
    kv = lambda w: _pad_heads(w, NSA_KV, NSA_DH)
    kvr = lambda w: _pad_heads(_rot_cols(w, NSA_KV, NSA_DH, 0, NSA_ROT), NSA_KV, NSA_DH)
    gate = a_gate.reshape(D_MODEL, NSA_KV, NSA_R * 3)
    gate = jnp.pad(gate, ((0, 0), (0, 0), (0, LANE - NSA_R * 3))).reshape(D_MODEL, NSA_KW)
    wa = jnp.concatenate([
        _pad_heads(a_q, NSA_HEADS, NSA_DH),
        _pad_heads(_rot_cols(a_q, NSA_HEADS, NSA_DH, 0, NSA_ROT), NSA_HEADS, NSA_DH),
        kv(a_kc), kv(a_vc), kv(a_ks), kvr(a_ks), kv(a_vs), kv(a_kw), kvr(a_kw), kv(a_vw), gate,
    ], axis=1).astype(BF16)
    wb = jnp.concatenate([b_q, b_f, b_i, b_g], axis=1).astype(BF16)

    kr_pad = jnp.pad(c_kr, ((0, 0), (NOPE, LANE - NOPE - ROPE_D)))
    krr_pad = jnp.pad(_rot_cols(c_kr, 1, ROPE_D, 0, ROPE_D), ((0, 0), (NOPE, LANE - NOPE - ROPE_D)))
    wc = jnp.concatenate([c_q, c_kv, kr_pad, krr_pad], axis=1).astype(BF16)
    dq = NOPE + ROPE_D
    wuq = _pad_heads(mla_w_uq[l], HC, dq).astype(BF16)
    wuqr = _pad_heads(_rot_cols(mla_w_uq[l], HC, dq, NOPE, ROPE_D), HC, dq).astype(BF16)
    ukv = mla_w_ukv[l].reshape(KV_RANK, HC, NOPE + VD)
    wuk = _pad_heads(ukv[:, :, :NOPE].reshape(KV_RANK, HC * NOPE), HC, NOPE).astype(BF16)
    wuv = _pad_heads(ukv[:, :, NOPE:].reshape(KV_RANK, HC * VD), HC, VD).astype(BF16)

    w1 = nsa_cmp_w1[l].reshape(2, CMP_L, NSA_DH, CMP_HID)
    w1p = jnp.pad(w1, ((0, 0), (0, 0), (0, LANE - NSA_DH), (0, 0))).reshape(2, CMP_L * LANE, CMP_HID)
    pe = jnp.pad(nsa_cmp_pos[l], ((0, 0), (0, 0), (0, LANE - NSA_DH))).reshape(2, 1, CMP_L * LANE)
    pe8 = jnp.broadcast_to(pe, (2, 8, CMP_L * LANE))
    w2p = jnp.pad(nsa_cmp_w2[l], ((0, 0), (0, 0), (0, LANE - NSA_DH)))

    wba = _pad_rows(w_branch[l, 0], NSA_HEADS, NSA_DH).astype(BF16)
    wbb = w_branch[l, 1].astype(BF16)
    wbc = _pad_rows(w_branch[l, 2], HC, VD).astype(BF16)
    return dict(wa=wa, wb=wb, wc=wc, wuq=wuq, wuqr=wuqr, wuk=wuk, wuv=wuv,
                w1p=w1p.astype(BF16), pe8=pe8.astype(BF16), w2p=w2p.astype(BF16),
                wm=merge.astype(BF16), wba=wba, wbb=wbb, wbc=wbc)


def kernel(x, mem, positions, ln_g, ln_b, ffn_w1, ffn_w3, ffn_w2, w_in, nsa_cmp_pos,
           nsa_cmp_w1, nsa_cmp_w2, hgrn_lb_logits, hgrn_norm_g, mla_q_norm_g, mla_w_uq,
           mla_kv_norm_g, mla_w_ukv, w_branch, w_out, xa_wq, xa_wk, xa_wv, xa_wo):
    b, s, d = x.shape
    depth = ln_g.shape[0]
    n = b * s
    alpha = (2.0 * depth) ** 0.25

    lng = ln_g.reshape(depth, 4, 1, d)
    lnb = ln_b.reshape(depth, 4, 1, d)
    def col_blocks(w):
        w = w.astype(BF16).reshape(depth, 2, d, D_FF // FFN_TF, FFN_TF)
        return w.transpose(0, 1, 3, 2, 4)

    w1 = col_blocks(ffn_w1)
    w3 = col_blocks(ffn_w3)
    w2 = ffn_w2.astype(BF16)
    wo = w_out.astype(BF16)
    xq = xa_wq.astype(BF16)
    xk = xa_wk.astype(BF16)
    xv = xa_wv.astype(BF16)
    xo = xa_wo.astype(BF16)
    p_lb = jax.nn.softmax(hgrn_lb_logits.astype(F32), axis=0)
    lower = (jnp.cumsum(p_lb, axis=0) - p_lb[0:1]).reshape(depth, HB, 1, HK)

    cos_a, sin_a = _rope_tables(positions, NSA_ROT, 0)
    cos_c, sin_c = _rope_tables(positions, ROPE_D, NOPE)
    ovl, place = _overlap_matrix(s)
    shift = jnp.asarray(np.eye(N_CMP, k=1, dtype=np.float32)).astype(BF16)
    mem2 = mem.reshape(b * mem.shape[1], d)

    x2 = x.reshape(n, d)
    for l in range(depth):
        w = _layer_weights(l, w_in, nsa_cmp_pos, nsa_cmp_w1, nsa_cmp_w2, mla_w_uq, mla_w_ukv,
                           w_branch)
        x2 = _ffn_ln(x2, w1, w3, w2, lng, lnb, l, 0, 0, alpha)
        x3 = x2.reshape(b, s, d)

        q, qr, kc, vc, ks, vs, kw, vw, gates = _proja(x3, w["wa"], cos_a, sin_a)
        zk = kc.reshape(b, NSA_KV, N_CMP, CMP_HALF)
        zv = vc.reshape(b, NSA_KV, N_CMP, CMP_HALF)
        kcmp, vcmp = _cmp(zk, zv, w["w1p"], w["pe8"], w["w2p"], shift)
        ya = _nsa_attn(q, qr, kcmp, vcmp, ks, vs, kw, vw, gates, ovl, place)

        pb = _proj(x2, w["wb"], F32).reshape(b, s, 4 * HB * HK)
        yb = _hgrn(pb, lower[l], hgrn_norm_g[l].reshape(1, HV))

        mq, mk, mv = _mlap(x3, w["wc"], mla_q_norm_g[l].reshape(1, Q_RANK),
                           mla_kv_norm_g[l].reshape(1, KV_RANK), w["wuq"], w["wuqr"],
                           w["wuk"], w["wuv"], cos_c, sin_c)
        yc = _mla_attn(mq, mk, mv)

        x2 = _merge(x2, ya.reshape(n, NSA_QW), yb.reshape(n, MIX_W), yc.reshape(n, MLA_W),
                    w["wm"], w["wba"], w["wbb"], w["wbc"], wo, lng, lnb, l, alpha)

        xk_l = _proj(mem2, xk[l], BF16).reshape(b, -1, d)
        xv_l = _proj(mem2, xv[l], BF16).reshape(b, -1, d)
        x2 = _xattn(x2.reshape(b, s, d), xq, xk_l, xv_l, xo, lng, lnb, l, alpha).reshape(n, d)

        x2 = _ffn_ln(x2, w1, w3, w2, lng, lnb, l, 1, 3, alpha)
    return x2.reshape(b, s, d)
```

```python
import functools
import math

import numpy as np
import jax
import jax.numpy as jnp
from jax import lax
from jax.experimental import pallas as pl
from jax.experimental.pallas import tpu as pltpu

F32 = jnp.float32
BF16 = jnp.bfloat16

D_MODEL = 1024
MIX_W = D_MODEL // 2
NSA_DH = 64
NSA_HEADS = 8
NSA_KV = 2
NSA_R = 4
NSA_ROT = 16
CMP_L = 32
CMP_D = 16
CMP_HID = 256
SEL_L = 64
N_SEL = 8
WINDOW = 256
HB = 4
HK = 128
HV = 128
HGRN_CHUNK = 64
HGRN_SUB = 4
HC = 8
NOPE = 64
ROPE_D = 32
VD = 64
Q_RANK = 384
KV_RANK = 256
XA_HEADS = 4
XA_DH = D_MODEL // XA_HEADS
D_FF = 2816
ROPE_THETA = 500000.0
LN_EPS = 1e-5
RMS_EPS = 1e-6
NEG = -1e30
BIG = 1e9
F_MIN = 1e-20

LOG2E = 1.4426950408889634
LANE = 128
SEL_LANE0 = NSA_DH
DEN_LANE = 64
VMEM_LIMIT = 56 * 1024 * 1024

IN_SIZES = (512, 128, 128, 128, 128, 128, 128, 24, 512, 512, 512, 512, 384, 256, 32, 3072)
IN_OFF = tuple(int(v) for v in np.concatenate([[0], np.cumsum(IN_SIZES)]))


def _dot(a, b):
    return jnp.dot(a, b, preferred_element_type=F32)


def _dot_nt(a, b):
    return lax.dot_general(a, b, (((1,), (1,)), ((), ())), preferred_element_type=F32)


def _ln(y, g, b):
    mu = jnp.mean(y, axis=-1, keepdims=True)
    yc = y - mu
    var = jnp.mean(yc * yc, axis=-1, keepdims=True)
    return yc * lax.rsqrt(var + LN_EPS) * g + b


def _pack_pair(a, b):
    lane = lax.broadcasted_iota(jnp.int32, a.shape, a.ndim - 1)
    return jnp.where(lane < LANE // 2, a, pltpu.roll(b, LANE // 2, axis=a.ndim - 1))


def _params(*sem):
    return pltpu.CompilerParams(dimension_semantics=sem, vmem_limit_bytes=VMEM_LIMIT)


def _resident(shape, index_map):
    return pl.BlockSpec(shape, index_map, pipeline_mode=pl.Buffered(1))


FFN_TM = 1024
FFN_TF = 256


def _ffn_kernel(x_ref, w1_ref, w3_ref, w2_ref, g_ref, b_ref, o_ref, acc_ref, xb_ref, *, alpha):
    j = pl.program_id(1)

    @pl.when(j == 0)
    def _():
        acc_ref[...] = jnp.zeros_like(acc_ref)
        xb_ref[...] = x_ref[...].astype(BF16)

    xb = xb_ref[...]
    h1 = _dot(xb, w1_ref[...])
    h3 = _dot(xb, w3_ref[...])
    h = (h1 * jax.nn.sigmoid(h1)) * h3
    acc_ref[...] += _dot(h.astype(BF16), w2_ref[...])

    @pl.when(j == pl.num_programs(1) - 1)
    def _():
        y = alpha * x_ref[...] + 0.5 * acc_ref[...]
        o_ref[...] = _ln(y, g_ref[...], b_ref[...])


def _ffn_ln(x2, w1, w3, w2, lng, lnb, l, which, ln_idx, alpha):
    n = x2.shape[0]
    tm = min(FFN_TM, n)
    grid = (n // tm, D_FF // FFN_TF)
    return pl.pallas_call(
        functools.partial(_ffn_kernel, alpha=alpha),
        grid=grid,
        in_specs=[
            pl.BlockSpec((tm, D_MODEL), lambda m, j: (m, 0)),
            pl.BlockSpec((None, None, None, D_MODEL, FFN_TF), lambda m, j: (l, which, j, 0, 0)),
            pl.BlockSpec((None, None, None, D_MODEL, FFN_TF), lambda m, j: (l, which, j, 0, 0)),
            pl.BlockSpec((None, None, FFN_TF, D_MODEL), lambda m, j: (l, which, j, 0)),
            pl.BlockSpec((None, None, 1, D_MODEL), lambda m, j: (l, ln_idx, 0, 0)),
            pl.BlockSpec((None, None, 1, D_MODEL), lambda m, j: (l, ln_idx, 0, 0)),
        ],
        out_specs=pl.BlockSpec((tm, D_MODEL), lambda m, j: (m, 0)),
        out_shape=jax.ShapeDtypeStruct((n, D_MODEL), F32),
        scratch_shapes=[pltpu.VMEM((tm, D_MODEL), F32), pltpu.VMEM((tm, D_MODEL), BF16)],
        compiler_params=_params("parallel", "arbitrary"),
        name="ffn_ln",
    )(x2, w1, w3, w2, lng, lnb)


def _proj_kernel(x_ref, w_ref, o_ref):
    o_ref[...] = _dot(x_ref[...].astype(BF16), w_ref[...]).astype(o_ref.dtype)


def _proj(x2, w, out_dtype, tm=1024, tn=2048):
    n, k = x2.shape
    c = w.shape[1]
    tm = min(tm, n)
    tn = min(tn, c)
    return pl.pallas_call(
        _proj_kernel,
        grid=(n // tm, c // tn),
        in_specs=[pl.BlockSpec((tm, k), lambda m, j: (m, 0)),
                  pl.BlockSpec((k, tn), lambda m, j: (0, j))],
        out_specs=pl.BlockSpec((tm, tn), lambda m, j: (m, j)),
        out_shape=jax.ShapeDtypeStruct((n, c), out_dtype),
        compiler_params=_params("parallel", "arbitrary"),
        name="proj",
    )(x2, w)


NSA_QW = NSA_HEADS * LANE
NSA_KW = NSA_KV * LANE
A_Q = 0
A_KV = NSA_QW
A_KV_ORDER = ("kc", "vc", "ks", "vs", "kw", "vw")
A_GATE = A_KV + len(A_KV_ORDER) * LANE
A_COLS = A_GATE + NSA_KW
PROJA_TM = 512


def _rope_lanes(y, cos, sin_lo, sin_hi, half):
    return (y * cos + pltpu.roll(y, LANE - half, axis=1) * sin_lo
            + pltpu.roll(y, half, axis=1) * sin_hi)


def _proja_kernel(x_ref, w_ref, cos_ref, slo_ref, shi_ref,
                  q_ref, qr_ref, kc_ref, vc_ref, ks_ref, vs_ref, kw_ref, vw_ref, g_ref):
    xb = x_ref[...].astype(BF16)
    cos = cos_ref[...]
    sin_lo = slo_ref[...]
    sin_hi = shi_ref[...]
    half = NSA_ROT // 2
    tm = x_ref.shape[0]

    def mm(c0, width):
        return _dot(xb, w_ref[:, c0:c0 + width])

    qs = NSA_DH ** -0.5 * LOG2E
    for p in range(NSA_QW // NSA_KW):
        c = p * NSA_KW
        y = mm(A_Q + c, NSA_KW)
        q_ref[:, c:c + NSA_KW] = (y * qs).astype(BF16)
        for hh in range(NSA_KW // LANE):
            yh = y[:, hh * LANE:(hh + 1) * LANE]
            qr_ref[:, c + hh * LANE:c + (hh + 1) * LANE] = (
                _rope_lanes(yh, cos, sin_lo, sin_hi, half) * qs).astype(BF16)

    lane = lax.broadcasted_iota(jnp.int32, (tm, LANE), 1)
    first = lane < NSA_DH
    tok = pl.program_id(1) * tm + lax.broadcasted_iota(jnp.int32, (tm, LANE), 0)
    blk_flag = jnp.where(lane - SEL_LANE0 == tok // SEL_L, 1.0, 0.0)
    den_flag = jnp.where(lane == DEN_LANE, 1.0, 0.0)

    def packed(table):
        return jnp.where(first, table, pltpu.roll(table, NSA_DH, axis=1))

    def put(ref, y, flag=None):
        for g, yg in enumerate((y, pltpu.roll(y, NSA_DH, axis=1))):
            yg = jnp.where(first, yg, 0.0)
            ref[g] = (yg if flag is None else yg + flag).astype(BF16)

    kv = mm(A_KV, len(A_KV_ORDER) * LANE)
    part = {name: kv[:, i * LANE:(i + 1) * LANE] for i, name in enumerate(A_KV_ORDER)}
    cos_p, lo_p, hi_p = packed(cos), packed(sin_lo), packed(sin_hi)
    put(kc_ref, part["kc"])
    put(vc_ref, part["vc"])
    put(ks_ref, _rope_lanes(part["ks"], cos_p, lo_p, hi_p, half), blk_flag)
    put(vs_ref, part["vs"], den_flag)
    put(kw_ref, _rope_lanes(part["kw"], cos_p, lo_p, hi_p, half))
    put(vw_ref, part["vw"], den_flag)
    g_ref[...] = mm(A_GATE, NSA_KW)


def _proja(x3, wa, cos_a, slo_a, shi_a):
    b, s, _ = x3.shape
    tm = PROJA_TM
    kv_shape = jax.ShapeDtypeStruct((b, NSA_KV, s, LANE), BF16)
    kv_spec = pl.BlockSpec((None, NSA_KV, tm, LANE), lambda bi, m: (bi, 0, m, 0))
    return pl.pallas_call(
        _proja_kernel,
        grid=(b, s // tm),
        in_specs=[
            pl.BlockSpec((None, tm, D_MODEL), lambda bi, m: (bi, m, 0)),
            _resident((D_MODEL, A_COLS), lambda bi, m: (0, 0)),
            pl.BlockSpec((None, tm, LANE), lambda bi, m: (bi, m, 0)),
            pl.BlockSpec((None, tm, LANE), lambda bi, m: (bi, m, 0)),
            pl.BlockSpec((None, tm, LANE), lambda bi, m: (bi, m, 0)),
        ],
        out_specs=[
            pl.BlockSpec((None, tm, NSA_QW), lambda bi, m: (bi, m, 0)),
            pl.BlockSpec((None, tm, NSA_QW), lambda bi, m: (bi, m, 0)),
            kv_spec, kv_spec, kv_spec, kv_spec, kv_spec, kv_spec,
            pl.BlockSpec((None, tm, NSA_KW), lambda bi, m: (bi, m, 0)),
        ],
        out_shape=[
            jax.ShapeDtypeStruct((b, s, NSA_QW), BF16),
            jax.ShapeDtypeStruct((b, s, NSA_QW), BF16),
            kv_shape, kv_shape, kv_shape, kv_shape, kv_shape, kv_shape,
            jax.ShapeDtypeStruct((b, s, NSA_KW), F32),
        ],
        compiler_params=_params("parallel", "parallel"),
        name="nsa_proj",
    )(x3, wa, cos_a, slo_a, shi_a)


N_CMP = 128
CMP_HALF = CMP_D * LANE


def _gelu_tanh(x):
    return 0.5 * x * (1.0 + jnp.tanh(0.7978845608028654 * (x + 0.044715 * x * x * x)))


def _cmp_kernel(zk_ref, zv_ref, w1_ref, pe_ref, w2_ref, shift_ref, ok_ref, ov_ref):
    shift = shift_ref[...]
    for which, (z_ref, o_ref) in enumerate(((zk_ref, ok_ref), (zv_ref, ov_ref))):
        z = z_ref[...]
        zn = _dot(shift, z).astype(BF16)
        bias = _dot(pe_ref[which], w1_ref[which])[0:1]
        pre = (_dot(z, w1_ref[which, 0:CMP_HALF, :])
               + _dot(zn, w1_ref[which, CMP_HALF:2 * CMP_HALF, :]) + bias)
        h = _gelu_tanh(pre)
        o_ref[...] = _dot(h.astype(BF16), w2_ref[which]).astype(BF16)


def _cmp(zk, zv, w1p, pe8, w2p, shift):
    b = zk.shape[0]
    z_spec = pl.BlockSpec((None, None, N_CMP, CMP_HALF), lambda bi, g: (bi, g, 0, 0))
    o_spec = pl.BlockSpec((None, None, N_CMP, LANE), lambda bi, g: (bi, g, 0, 0))
    o_shape = jax.ShapeDtypeStruct((b, NSA_KV, N_CMP, LANE), BF16)
    return pl.pallas_call(
        _cmp_kernel,
        grid=(b, NSA_KV),
        in_specs=[
            z_spec, z_spec,
            _resident((2, 2 * CMP_HALF, CMP_HID), lambda bi, g: (0, 0, 0)),
            _resident((2, 8, 2 * CMP_HALF), lambda bi, g: (0, 0, 0)),
            _resident((2, CMP_HID, LANE), lambda bi, g: (0, 0, 0)),
            _resident((N_CMP, N_CMP), lambda bi, g: (0, 0)),
        ],
        out_specs=[o_spec, o_spec],
        out_shape=[o_shape, o_shape],
        compiler_params=_params("parallel", "parallel"),
        name="nsa_cmp",
    )(zk, zv, w1p, pe8, w2p, shift)


NSA_TQ = 256
NSA_TK = 512
N_SLC = 32
WIN_SLAB = WINDOW + NSA_TQ


def _nsa_kernel(q_ref, qr_ref, kc_ref, vc_ref, ks_ref, vs_ref, kw_ref, vw_ref, gt_ref, ovl_ref,
                place_ref, o_ref):
    tq, tk, r_heads = NSA_TQ, NSA_TK, NSA_R
    i = pl.program_id(2)
    t0 = i * tq
    tpos = t0 + lax.broadcasted_iota(jnp.int32, (tq, 1), 0)
    lane = lax.broadcasted_iota(jnp.int32, (1, LANE), 1)

    def stack(ref):
        return jnp.concatenate([ref[:, r * LANE:(r + 1) * LANE] for r in range(r_heads)], axis=0)

    gt = jax.nn.sigmoid(gt_ref[...])

    def gate(c):
        return jnp.stack([gt[:, 3 * r + c:3 * r + c + 1] for r in range(r_heads)], axis=0)

    q4w = stack(qr_ref)
    w0 = pl.multiple_of(jnp.maximum(t0 - WINDOW, 0), LANE)
    kw = kw_ref[pl.ds(w0, WIN_SLAB), :]
    vw = vw_ref[pl.ds(w0, WIN_SLAB), :]
    kpos = w0 + lax.broadcasted_iota(jnp.int32, (1, WIN_SLAB), 1)
    w_bias = jnp.where((kpos <= tpos) & (kpos > tpos - WINDOW), 0.0, NEG)
    s = _dot_nt(q4w, kw).reshape(r_heads, tq, WIN_SLAB) + w_bias[None]
    p = jnp.exp2(s - jnp.max(s, axis=-1, keepdims=True))
    o_win = _dot(p.reshape(r_heads * tq, WIN_SLAB).astype(BF16), vw).reshape(r_heads, tq, LANE)
    o_part = gate(2) * (o_win / o_win[:, :, DEN_LANE:DEN_LANE + 1])

    cmask = ((lane * CMP_D + (CMP_L - 1) <= tpos) & (lane < N_CMP - 1))[None]
    s = _dot_nt(stack(q_ref), kc_ref[...]).reshape(r_heads, tq, N_CMP)
    s = jnp.where(cmask, s, NEG)
    e = jnp.where(cmask, jnp.exp2(s - jnp.max(s, axis=-1, keepdims=True)), 0.0)
    den = jnp.sum(e, axis=-1, keepdims=True)
    p = e / jnp.where(den > 0.0, den, 1.0)
    psum = jnp.sum(p, axis=0)
    o_cmp = _dot(p.reshape(r_heads * tq, N_CMP).astype(BF16), vc_ref[...])
    o_part = o_part + gate(0) * o_cmp.reshape(r_heads, tq, LANE)

    imp = lax.dot_general(ovl_ref[...], psum, (((1,), (1,)), ((), ())),
                          precision=lax.Precision.HIGHEST, preferred_element_type=F32)
    blk = lax.broadcasted_iota(jnp.int32, (N_SLC, 1), 0)
    tpos_l = t0 + lax.broadcasted_iota(jnp.int32, (1, tq), 1)
    cur = tpos_l // SEL_L
    valid = blk * SEL_L <= tpos_l
    forced = (blk == 0) | (blk == cur) | (blk == cur - 1)
    score = jnp.where(valid & forced, BIG, jnp.where(valid, imp, -BIG))
    beats = []
    for j in range(N_SLC):
        sj = score[j:j + 1, :]
        beats.append(jnp.where((sj > score) | ((sj == score) & (j < blk)), 1.0, 0.0))
    while len(beats) > 1:
        beats = [a + b for a, b in zip(beats[0::2], beats[1::2])]
    sel_t = jnp.where((beats[0] < N_SEL) & valid, 1.0, 0.0)
    sel_q = lax.dot_general(sel_t, place_ref[...], (((0,), (0,)), ((), ())),
                            preferred_element_type=F32)
    in_flags = jnp.where((lane >= SEL_LANE0) & (lane < SEL_LANE0 + N_SLC), 1.0, 0.0)
    q_bias = (sel_q - in_flags) * (-NEG)
    q4 = (q4w.astype(F32).reshape(r_heads, tq, LANE) + q_bias[None]).astype(BF16)
    q4 = q4.reshape(r_heads * tq, LANE)

    def sel_step(c, carry, causal):
        m, acc = carry
        k0 = pl.multiple_of(c * tk, tk)
        k = ks_ref[pl.ds(k0, tk), :]
        v = vs_ref[pl.ds(k0, tk), :]
        s = _dot_nt(q4, k).reshape(r_heads, tq, tk)
        if causal:
            kpos = k0 + lax.broadcasted_iota(jnp.int32, (1, tk), 1)
            s = jnp.where((kpos <= tpos)[None], s, NEG)
        m_new = jnp.maximum(m, jnp.max(s, axis=-1, keepdims=True))
        p = jnp.exp2(s - m_new)
        pv = _dot(p.reshape(r_heads * tq, tk).astype(BF16), v).reshape(r_heads, tq, LANE)
        return m_new, jnp.exp2(m - m_new) * acc + pv

    last = (t0 + tq - 1) // tk
    carry = (jnp.full((r_heads, tq, 1), NEG, F32), jnp.zeros((r_heads, tq, LANE), F32))
    carry = lax.fori_loop(0, last, functools.partial(sel_step, causal=False), carry)
    _, acc_s = sel_step(last, carry, True)
    o = o_part + gate(1) * (acc_s / acc_s[:, :, DEN_LANE:DEN_LANE + 1])
    for r in range(0, r_heads, 2):
        o_ref[:, (r // 2) * LANE:(r // 2 + 1) * LANE] = _pack_pair(o[r], o[r + 1]).astype(BF16)


def _nsa_attn(q, qr, kcmp, vcmp, ks, vs, kw, vw, gates, ovl, place):
    b, s, _ = q.shape
    tq = NSA_TQ
    gw = NSA_R * LANE
    q_spec = pl.BlockSpec((None, tq, gw), lambda bi, g, i: (bi, i, g))
    c_spec = pl.BlockSpec((None, None, N_CMP, LANE), lambda bi, g, i: (bi, g, 0, 0))
    kv_spec = pl.BlockSpec((None, None, s, LANE), lambda bi, g, i: (bi, g, 0, 0))
    return pl.pallas_call(
        _nsa_kernel,
        grid=(b, NSA_KV, s // tq),
        in_specs=[q_spec, q_spec, c_spec, c_spec, kv_spec, kv_spec, kv_spec, kv_spec,
                  pl.BlockSpec((None, tq, LANE), lambda bi, g, i: (bi, i, g)),
                  _resident((N_SLC, N_CMP), lambda bi, g, i: (0, 0)),
                  _resident((N_SLC, LANE), lambda bi, g, i: (0, 0))],
        out_specs=pl.BlockSpec((None, tq, gw // 2), lambda bi, g, i: (bi, i, g)),
        out_shape=jax.ShapeDtypeStruct((b, s, MIX_W), BF16),
        compiler_params=_params("parallel", "parallel", "arbitrary"),
        name="nsa_attn",
    )(q, qr, kcmp, vcmp, ks, vs, kw, vw, gates, ovl, place)


def _hgrn_kernel(x_ref, w_ref, lb_ref, ng_ref, o_ref, st_ref):
    s_len = x_ref.shape[0]
    c_len, sub_len = HGRN_CHUNK, HGRN_SUB
    nc = s_len // c_len
    proj = _dot(x_ref[...].astype(BF16), w_ref[...])
    q = proj[:, 0:HK]
    z = proj[:, HK:2 * HK]
    v = proj[:, 2 * HK:2 * HK + HV]
    go = proj[:, 2 * HK + HV:2 * HK + 2 * HV]
    lb = lb_ref[...]
    t = lax.broadcasted_iota(jnp.int32, (s_len, 1), 0)
    t_sub = t % sub_len

    f = lb + (1.0 - lb) * jax.nn.sigmoid(z)
    lf = jnp.log(jnp.maximum(f, F_MIN))
    k = (1.0 - lb) * jax.nn.sigmoid(-z)

    row = lax.broadcasted_iota(jnp.int32, (nc, c_len, c_len), 1)
    col = lax.broadcasted_iota(jnp.int32, (nc, c_len, c_len), 2)
    b3 = jnp.einsum('cts,csd->ctd', jnp.where(col <= row, 1.0, 0.0), lf.reshape(nc, c_len, HK),
                    precision=lax.Precision.HIGHEST, preferred_element_type=F32)
    b = b3.reshape(s_len, HK)
    k3 = k.reshape(nc, c_len, HK)
    vb3 = v.astype(BF16).reshape(nc, c_len, HV)

    a_intra = None
    h = c_len // 2
    while h >= sub_len:
        blk = b.reshape(s_len // (2 * h), 2 * h, HK)
        e = jnp.exp(-jnp.abs(blk[:, h - 1:h, :] - blk)).reshape(s_len, HK)
        upper = (t // h) % 2 == 1
        lq = jnp.where(upper, q * e, 0.0).astype(BF16).reshape(nc, c_len, HK)
        rk = jnp.where(upper, 0.0, k * e).astype(BF16).reshape(nc, c_len, HK)
        a = jnp.einsum('ctk,csk->cts', lq, rk, preferred_element_type=F32)
        if 2 * h < c_len:
            a = jnp.where(row // (2 * h) == col // (2 * h), a, 0.0)
        a_intra = a if a_intra is None else a_intra + a
        h //= 2
    o = jnp.einsum('cts,csd->ctd', a_intra.astype(BF16), vb3,
                   preferred_element_type=F32).reshape(s_len, HV)

    o = o + jnp.sum(q * k, axis=-1, keepdims=True) * v
    for d in range(1, sub_len):
        e = jnp.exp(jnp.where(t_sub >= d, b - pltpu.roll(b, d, axis=0), NEG))
        w = jnp.sum(q * pltpu.roll(k, d, axis=0) * e, axis=-1, keepdims=True)
        o = o + w * pltpu.roll(v, d, axis=0)

    b_last = b3[:, c_len - 1:c_len, :]
    kd3 = (k3 * jnp.exp(b_last - b3)).astype(BF16)
    upd = jnp.einsum('csv,csk->cvk', vb3, kd3, preferred_element_type=F32)
    dec = jnp.exp(b_last)
    state = jnp.zeros((HV, HK), F32)
    for c in range(nc):
        st_ref[c] = state.astype(BF16)
        state = state * dec[c] + upd[c]
    qe3 = (q * jnp.exp(b)).astype(BF16).reshape(nc, c_len, HK)
    o = o + jnp.einsum('ctk,cvk->ctv', qe3, st_ref[...],
                       preferred_element_type=F32).reshape(s_len, HV)

    o = o * lax.rsqrt(jnp.mean(o * o, axis=-1, keepdims=True) + RMS_EPS) * ng_ref[...]
    o_ref[...] = (o * (go * jax.nn.sigmoid(go))).astype(BF16)


def _hgrn(x3, wb, lb, ng):
    b, s, d = x3.shape
    return pl.pallas_call(
        _hgrn_kernel,
        grid=(b, HB),
        in_specs=[pl.BlockSpec((None, s, d), lambda bi, h: (bi, 0, 0)),
                  pl.BlockSpec((None, d, 4 * HK), lambda bi, h: (h, 0, 0)),
                  pl.BlockSpec((None, 1, HK), lambda bi, h: (h, 0, 0)),
                  pl.BlockSpec((1, HV), lambda bi, h: (0, 0))],
        out_specs=pl.BlockSpec((None, s, HV), lambda bi, h: (bi, 0, h)),
        out_shape=jax.ShapeDtypeStruct((b, s, HB * HV), BF16),
        scratch_shapes=[pltpu.VMEM((s // HGRN_CHUNK, HV, HK), BF16)],
        compiler_params=_params("parallel", "arbitrary"),
        name="hgrn",
    )(x3, wb, lb, ng)


MLA_W = HC * LANE
C_CQ, C_CKV, C_KR = 0, Q_RANK, Q_RANK + KV_RANK
C_COLS = C_KR + LANE
MLAP_TM = 512


def _rms(x, g):
    return x * lax.rsqrt(jnp.mean(x * x, axis=-1, keepdims=True) + RMS_EPS) * g


def _mlap_kernel(x_ref, wc_ref, qg_ref, kvg_ref, wuq_ref, wuk_ref, wuv_ref,
                 cos_ref, slo_ref, shi_ref, q_ref, k_ref, v_ref):
    xb = x_ref[...].astype(BF16)
    c = _dot(xb, wc_ref[...])
    half = ROPE_D // 2
    nq = _rms(c[:, C_CQ:C_CKV], qg_ref[...]).astype(BF16)
    nkv = _rms(c[:, C_CKV:C_KR], kvg_ref[...]).astype(BF16)
    k_pe = _rope_lanes(c[:, C_KR:C_COLS], cos_ref[...], slo_ref[...], shi_ref[...], half)
    qs = (NOPE + ROPE_D) ** -0.5 * LOG2E
    cos_q = cos_ref[...] * qs
    lo_q = slo_ref[...] * qs
    hi_q = shi_ref[...] * qs
    den_flag = jnp.where(lax.broadcasted_iota(jnp.int32, (1, LANE), 1) == DEN_LANE, 1.0, 0.0)
    for h in range(HC):
        hs = slice(h * LANE, (h + 1) * LANE)
        q_ref[:, hs] = _rope_lanes(_dot(nq, wuq_ref[:, hs]), cos_q, lo_q, hi_q, half).astype(BF16)
        k_ref[:, hs] = (_dot(nkv, wuk_ref[:, hs]) + k_pe).astype(BF16)
        v_ref[:, hs] = (_dot(nkv, wuv_ref[:, hs]) + den_flag).astype(BF16)


def _mlap(x3, wc, qg, kvg, wuq, wuk, wuv, cos_c, slo_c, shi_c):
    b, s, _ = x3.shape
    tm = MLAP_TM
    o_spec = pl.BlockSpec((None, tm, MLA_W), lambda bi, m: (bi, m, 0))
    o_shape = jax.ShapeDtypeStruct((b, s, MLA_W), BF16)
    t_spec = pl.BlockSpec((None, tm, LANE), lambda bi, m: (bi, m, 0))
    return pl.pallas_call(
        _mlap_kernel,
        grid=(b, s // tm),
        in_specs=[
            pl.BlockSpec((None, tm, D_MODEL), lambda bi, m: (bi, m, 0)),
            _resident((D_MODEL, C_COLS), lambda bi, m: (0, 0)),
            _resident((1, Q_RANK), lambda bi, m: (0, 0)),
            _resident((1, KV_RANK), lambda bi, m: (0, 0)),
            _resident((Q_RANK, MLA_W), lambda bi, m: (0, 0)),
            _resident((KV_RANK, MLA_W), lambda bi, m: (0, 0)),
            _resident((KV_RANK, MLA_W), lambda bi, m: (0, 0)),
            t_spec, t_spec, t_spec,
        ],
        out_specs=[o_spec, o_spec, o_spec],
        out_shape=[o_shape, o_shape, o_shape],
        compiler_params=_params("parallel", "parallel"),
        name="mla_proj",
    )(x3, wc, qg, kvg, wuq, wuk, wuv, cos_c, slo_c, shi_c)


MLA_TQ = 512
MLA_TK = 512


MLA_HPS = 4


def _mla_attn_kernel(q_ref, k_ref, v_ref, o_ref):
    tq, tk = MLA_TQ, MLA_TK
    i = pl.program_id(2)
    t0 = i * tq
    tpos = t0 + lax.broadcasted_iota(jnp.int32, (tq, 1), 0)
    heads = [slice(h * LANE, (h + 1) * LANE) for h in range(MLA_HPS)]
    qs = [q_ref[:, hs] for hs in heads]

    def step(c, carry, causal):
        k0 = pl.multiple_of(c * tk, tk)
        if causal:
            keep = k0 + lax.broadcasted_iota(jnp.int32, (1, tk), 1) <= tpos
        out = []
        for h, hs in enumerate(heads):
            m, acc = carry[h]
            s = _dot_nt(qs[h], k_ref[pl.ds(k0, tk), hs])
            if causal:
                s = jnp.where(keep, s, NEG)
            m_new = jnp.maximum(m, jnp.max(s, axis=-1, keepdims=True))
            p = jnp.exp2(s - m_new)
            pv = _dot(p.astype(BF16), v_ref[pl.ds(k0, tk), hs])
            out.append((m_new, jnp.exp2(m - m_new) * acc + pv))
        return tuple(out)

    last = (t0 + tq - 1) // tk
    carry = tuple((jnp.full((tq, 1), NEG, F32), jnp.zeros((tq, LANE), F32)) for _ in heads)
    carry = lax.fori_loop(0, last, functools.partial(step, causal=False), carry)
    carry = step(last, carry, True)
    outs = [acc / acc[:, DEN_LANE:DEN_LANE + 1] for _, acc in carry]
    for h in range(0, MLA_HPS, 2):
        o_ref[:, (h // 2) * LANE:(h // 2 + 1) * LANE] = _pack_pair(outs[h], outs[h + 1]).astype(BF16)


def _mla_attn(q, k, v):
    b, s, _ = q.shape
    tq = MLA_TQ
    gw = MLA_HPS * LANE
    return pl.pallas_call(
        _mla_attn_kernel,
        grid=(b, HC // MLA_HPS, s // tq),
        in_specs=[pl.BlockSpec((None, tq, gw), lambda bi, h, i: (bi, i, h)),
                  pl.BlockSpec((None, s, gw), lambda bi, h, i: (bi, 0, h)),
                  pl.BlockSpec((None, s, gw), lambda bi, h, i: (bi, 0, h))],
        out_specs=pl.BlockSpec((None, tq, gw // 2), lambda bi, h, i: (bi, i, h)),
        out_shape=jax.ShapeDtypeStruct((b, s, MIX_W), BF16),
        compiler_params=_params("parallel", "parallel", "arbitrary"),
        name="mla_attn",
    )(q, k, v)


MERGE_TM = 512


def _merge_kernel(x_ref, ya_ref, yb_ref, yc_ref, wm_ref, wa_ref, wb_ref, wc_ref, wo_ref,
                  g_ref, b_ref, o_ref, *, alpha):
    x = x_ref[...]
    xb = x.astype(BF16)
    mixed = jnp.zeros(x.shape, F32)
    for idx, (y_ref, w_ref) in enumerate(((ya_ref, wa_ref), (yb_ref, wb_ref), (yc_ref, wc_ref))):
        gate = jax.nn.sigmoid(_dot(xb, wm_ref[:, idx * D_MODEL:(idx + 1) * D_MODEL]))
        mixed = mixed + gate * _dot(y_ref[...], w_ref[...])
    y = alpha * x + _dot(mixed.astype(BF16), wo_ref[...])
    o_ref[...] = _ln(y, g_ref[...], b_ref[...])


def _merge(x2, ya, yb, yc, wm, wa, wb, wc, wo, lng, lnb, l, alpha):
    n = x2.shape[0]
    tm = min(MERGE_TM, n)

    def rows(width):
        return pl.BlockSpec((tm, width), lambda m: (m, 0))

    return pl.pallas_call(
        functools.partial(_merge_kernel, alpha=alpha),
        grid=(n // tm,),
        in_specs=[rows(D_MODEL), rows(MIX_W), rows(MIX_W), rows(MIX_W),
                  _resident((D_MODEL, 3 * D_MODEL), lambda m: (0, 0)),
                  _resident((None, None, MIX_W, D_MODEL), lambda m: (l, 0, 0, 0)),
                  _resident((None, None, MIX_W, D_MODEL), lambda m: (l, 1, 0, 0)),
                  _resident((None, None, MIX_W, D_MODEL), lambda m: (l, 2, 0, 0)),
                  _resident((None, D_MODEL, D_MODEL), lambda m: (l, 0, 0)),
                  _resident((None, None, 1, D_MODEL), lambda m: (l, 1, 0, 0)),
                  _resident((None, None, 1, D_MODEL), lambda m: (l, 1, 0, 0))],
        out_specs=rows(D_MODEL),
        out_shape=jax.ShapeDtypeStruct((n, D_MODEL), F32),
        compiler_params=_params("parallel"),
        name="merge",
    )(x2, ya, yb, yc, wm, wa, wb, wc, wo, lng, lnb)


XA_TM = 1024


def _xattn_kernel(x_ref, wq_ref, k_ref, v_ref, wo_ref, g_ref, b_ref, o_ref, *, alpha):
    x = x_ref[...]
    xb = x.astype(BF16)
    scale = XA_DH ** -0.5
    heads = []
    for h in range(XA_HEADS):
        hs = slice(h * XA_DH, (h + 1) * XA_DH)
        qh = _dot(xb, wq_ref[:, hs]).astype(BF16)
        s = _dot_nt(qh, k_ref[:, hs]) * scale
        e = jnp.exp(s - jnp.max(s, axis=-1, keepdims=True))
        p = e / jnp.sum(e, axis=-1, keepdims=True)
        heads.append(_dot(p.astype(BF16), v_ref[:, hs]).astype(BF16))
    att = jnp.concatenate(heads, axis=1)
    y = alpha * x + _dot(att, wo_ref[...])
    o_ref[...] = _ln(y, g_ref[...], b_ref[...])


def _xattn(x3, wq, k, v, wo, lng, lnb, l, alpha):
    b, s, _ = x3.shape
    m_len = k.shape[1]
    tm = XA_TM
    return pl.pallas_call(
        functools.partial(_xattn_kernel, alpha=alpha),
        grid=(b, s // tm),
        in_specs=[pl.BlockSpec((None, tm, D_MODEL), lambda bi, m: (bi, m, 0)),
                  _resident((None, D_MODEL, D_MODEL), lambda bi, m: (l, 0, 0)),
                  pl.BlockSpec((None, m_len, D_MODEL), lambda bi, m: (bi, 0, 0)),
                  pl.BlockSpec((None, m_len, D_MODEL), lambda bi, m: (bi, 0, 0)),
                  _resident((None, D_MODEL, D_MODEL), lambda bi, m: (l, 0, 0)),
                  _resident((None, None, 1, D_MODEL), lambda bi, m: (l, 2, 0, 0)),
                  _resident((None, None, 1, D_MODEL), lambda bi, m: (l, 2, 0, 0))],
        out_specs=pl.BlockSpec((None, tm, D_MODEL), lambda bi, m: (bi, m, 0)),
        out_shape=jax.ShapeDtypeStruct((b, s, D_MODEL), F32),
        compiler_params=_params("parallel", "parallel"),
        name="xattn",
    )(x3, wq, k, v, wo, lng, lnb)


def _pad_heads(w, n_heads, dh):
    k = w.shape[0]
    w = w.reshape(k, n_heads, dh)
    return jnp.pad(w, ((0, 0), (0, 0), (0, LANE - dh))).reshape(k, n_heads * LANE)


def _rope_tables(positions, rot, off):
    half = rot // 2
    inv = ROPE_THETA ** (-jnp.arange(half, dtype=F32) / half)
    ang = positions.astype(F32)[..., None] * inv
    cos, sin = jnp.cos(ang), jnp.sin(ang)
    shape = positions.shape
    zeros = lambda n: jnp.zeros(shape + (n,), F32)
    cos_t = jnp.concatenate([jnp.ones(shape + (off,), F32), cos, cos,
                             jnp.ones(shape + (LANE - off - rot,), F32)], axis=-1)
    sin_lo = jnp.concatenate([zeros(off), -sin, zeros(LANE - off - half)], axis=-1)
    sin_hi = jnp.concatenate([zeros(off + half), sin, zeros(LANE - off - rot)], axis=-1)
    return cos_t, sin_lo, sin_hi


def _overlap_matrix(s_len):
    n_cmp = (s_len - CMP_L) // CMP_D + 1
    n_slc = s_len // SEL_L
    start = np.arange(n_cmp) * CMP_D
    j = np.arange(n_slc)
    ov = np.clip(np.minimum(start[:, None] + CMP_L, (j[None, :] + 1) * SEL_L)
                 - np.maximum(start[:, None], j[None, :] * SEL_L), 0, None) / CMP_L
    out = np.zeros((N_SLC, N_CMP), np.float32)
    out[:n_slc, :n_cmp] = ov.T
    place = np.zeros((N_SLC, LANE), np.float32)
    place[np.arange(N_SLC), SEL_LANE0 + np.arange(N_SLC)] = 1.0
    return jnp.asarray(out), jnp.asarray(place)


def _layer_weights(l, w_in, nsa_cmp_pos, nsa_cmp_w1, nsa_cmp_w2, mla_w_uq, mla_w_ukv):
    wl = w_in[l]
    part = [wl[:, IN_OFF[i]:IN_OFF[i + 1]] for i in range(len(IN_SIZES))]
    (a_q, a_kc, a_vc, a_ks, a_vs, a_kw, a_vw, a_gate, b_q, b_f, b_i, b_g,
     c_q, c_kv, c_kr, merge) = part
    gate = a_gate.reshape(D_MODEL, NSA_KV, NSA_R * 3)
    gate = jnp.pad(gate, ((0, 0), (0, 0), (0, LANE - NSA_R * 3))).reshape(D_MODEL, NSA_KW)
    kv_parts = dict(kc=a_kc, vc=a_vc, ks=a_ks, vs=a_vs, kw=a_kw, vw=a_vw)
    wa = jnp.concatenate([_pad_heads(a_q, NSA_HEADS, NSA_DH)]
                         + [kv_parts[name] for name in A_KV_ORDER] + [gate], axis=1).astype(BF16)
    wb = jnp.stack([w.reshape(D_MODEL, HB, HK) for w in (b_q, b_f, b_i, b_g)], axis=2)
    wb = wb.transpose(1, 0, 2, 3).reshape(HB, D_MODEL, 4 * HK).astype(BF16)

    kr_pad = jnp.pad(c_kr, ((0, 0), (NOPE, LANE - NOPE - ROPE_D)))
    wc = jnp.concatenate([c_q, c_kv, kr_pad], axis=1).astype(BF16)
    wuq = _pad_heads(mla_w_uq[l], HC, NOPE + ROPE_D).astype(BF16)
    ukv = mla_w_ukv[l].reshape(KV_RANK, HC, NOPE + VD)
    wuk = _pad_heads(ukv[:, :, :NOPE].reshape(KV_RANK, HC * NOPE), HC, NOPE).astype(BF16)
    wuv = _pad_heads(ukv[:, :, NOPE:].reshape(KV_RANK, HC * VD), HC, VD).astype(BF16)

    w1 = nsa_cmp_w1[l].reshape(2, CMP_L, NSA_DH, CMP_HID)
    w1p = jnp.pad(w1, ((0, 0), (0, 0), (0, LANE - NSA_DH), (0, 0))).reshape(2, CMP_L * LANE, CMP_HID)
    pe = jnp.pad(nsa_cmp_pos[l], ((0, 0), (0, 0), (0, LANE - NSA_DH))).reshape(2, 1, CMP_L * LANE)
    pe8 = jnp.broadcast_to(pe, (2, 8, CMP_L * LANE))
    w2p = jnp.pad(nsa_cmp_w2[l], ((0, 0), (0, 0), (0, LANE - NSA_DH)))
    return dict(wa=wa, wb=wb, wc=wc, wuq=wuq, wuk=wuk, wuv=wuv,
                w1p=w1p.astype(BF16), pe8=pe8.astype(BF16), w2p=w2p.astype(BF16),
                wm=merge.astype(BF16))


def kernel(x, mem, positions, ln_g, ln_b, ffn_w1, ffn_w3, ffn_w2, w_in, nsa_cmp_pos,
           nsa_cmp_w1, nsa_cmp_w2, hgrn_lb_logits, hgrn_norm_g, mla_q_norm_g, mla_w_uq,
           mla_kv_norm_g, mla_w_ukv, w_branch, w_out, xa_wq, xa_wk, xa_wv, xa_wo):
    b, s, d = x.shape
    depth = ln_g.shape[0]
    n = b * s
    alpha = (2.0 * depth) ** 0.25

    lng = ln_g.reshape(depth, 4, 1, d)
    lnb = ln_b.reshape(depth, 4, 1, d)
    def col_blocks(w):
        w = w.astype(BF16).reshape(depth, 2, d, D_FF // FFN_TF, FFN_TF)
        return w.transpose(0, 1, 3, 2, 4)

    w1 = col_blocks(ffn_w1)
    w3 = col_blocks(ffn_w3)
    w2 = ffn_w2.astype(BF16)
    wo = w_out.astype(BF16)
    wbr = w_branch.astype(BF16)
    xq = xa_wq.astype(BF16)
    xk = xa_wk.astype(BF16)
    xv = xa_wv.astype(BF16)
    xo = xa_wo.astype(BF16)
    p_lb = jax.nn.softmax(hgrn_lb_logits.astype(F32), axis=0)
    lower = (jnp.cumsum(p_lb, axis=0) - p_lb[0:1]).reshape(depth, HB, 1, HK)

    rope_a = _rope_tables(positions, NSA_ROT, 0)
    rope_c = _rope_tables(positions, ROPE_D, NOPE)
    ovl, place = _overlap_matrix(s)
    shift = jnp.asarray(np.eye(N_CMP, k=1, dtype=np.float32)).astype(BF16)
    mem2 = mem.reshape(b * mem.shape[1], d)

    x2 = x.reshape(n, d)
    for l in range(depth):
        w = _layer_weights(l, w_in, nsa_cmp_pos, nsa_cmp_w1, nsa_cmp_w2, mla_w_uq, mla_w_ukv)
        x2 = _ffn_ln(x2, w1, w3, w2, lng, lnb, l, 0, 0, alpha)
        x3 = x2.reshape(b, s, d)

        q, qr, kc, vc, ks, vs, kw, vw, gates = _proja(x3, w["wa"], *rope_a)
        zk = kc.reshape(b, NSA_KV, N_CMP, CMP_HALF)
        zv = vc.reshape(b, NSA_KV, N_CMP, CMP_HALF)
        kcmp, vcmp = _cmp(zk, zv, w["w1p"], w["pe8"], w["w2p"], shift)
        ya = _nsa_attn(q, qr, kcmp, vcmp, ks, vs, kw, vw, gates, ovl, place)

        yb = _hgrn(x3, w["wb"], lower[l], hgrn_norm_g[l].reshape(1, HV))

        mq, mk, mv = _mlap(x3, w["wc"], mla_q_norm_g[l].reshape(1, Q_RANK),
                           mla_kv_norm_g[l].reshape(1, KV_RANK), w["wuq"],
                           w["wuk"], w["wuv"], *rope_c)
        yc = _mla_attn(mq, mk, mv)

        x2 = _merge(x2, ya.reshape(n, MIX_W), yb.reshape(n, MIX_W), yc.reshape(n, MIX_W),
                    w["wm"], wbr, wbr, wbr, wo, lng, lnb, l, alpha)

        xk_l = _proj(mem2, xk[l], BF16).reshape(b, -1, d)
        xv_l = _proj(mem2, xv[l], BF16).reshape(b, -1, d)
        x2 = _xattn(x2.reshape(b, s, d), xq, xk_l, xv_l, xo, lng, lnb, l, alpha).reshape(n, d)

        x2 = _ffn_ln(x2, w1, w3, w2, lng, lnb, l, 1, 3, alpha)
    return x2.reshape(b, s, d)
```

```python
import functools
import math

import numpy as np
import jax
import jax.numpy as jnp
from jax import lax
from jax.experimental import pallas as pl
from jax.experimental.pallas import tpu as pltpu

F32 = jnp.float32
BF16 = jnp.bfloat16

D_MODEL = 1024
MIX_W = D_MODEL // 2
NSA_DH = 64
NSA_HEADS = 8
NSA_KV = 2
NSA_R = 4
NSA_ROT = 16
CMP_L = 32
CMP_D = 16
CMP_HID = 256
SEL_L = 64
N_SEL = 8
WINDOW = 256
HB = 4
HK = 128
HV = 128
HGRN_CHUNK = 64
HGRN_SUB = 4
HC = 8
NOPE = 64
ROPE_D = 32
VD = 64
Q_RANK = 384
KV_RANK = 256
XA_HEADS = 4
XA_DH = D_MODEL // XA_HEADS
D_FF = 2816
ROPE_THETA = 500000.0
LN_EPS = 1e-5
RMS_EPS = 1e-6
NEG = -1e30
BIG = 1e9
F_MIN = 1e-20

LOG2E = 1.4426950408889634
LANE = 128
SEL_LANE0 = NSA_DH
DEN_LANE = 64
VMEM_LIMIT = 56 * 1024 * 1024

IN_SIZES = (512, 128, 128, 128, 128, 128, 128, 24, 512, 512, 512, 512, 384, 256, 32, 3072)
IN_OFF = tuple(int(v) for v in np.concatenate([[0], np.cumsum(IN_SIZES)]))


def _dot(a, b):
    return jnp.dot(a, b, preferred_element_type=F32)


def _dot_nt(a, b):
    return lax.dot_general(a, b, (((1,), (1,)), ((), ())), preferred_element_type=F32)


def _ln(y, g, b):
    mu = jnp.mean(y, axis=-1, keepdims=True)
    yc = y - mu
    var = jnp.mean(yc * yc, axis=-1, keepdims=True)
    return yc * lax.rsqrt(var + LN_EPS) * g + b


def _pack_pair(a, b):
    lane = lax.broadcasted_iota(jnp.int32, a.shape, a.ndim - 1)
    return jnp.where(lane < LANE // 2, a, pltpu.roll(b, LANE // 2, axis=a.ndim - 1))


def _params(*sem):
    return pltpu.CompilerParams(dimension_semantics=sem, vmem_limit_bytes=VMEM_LIMIT)


def _resident(shape, index_map):
    return pl.BlockSpec(shape, index_map, pipeline_mode=pl.Buffered(1))


FFN_TM = 1024
FFN_TF = 256


def _ffn_kernel(x_ref, w1_ref, w3_ref, w2_ref, g_ref, b_ref, o_ref, acc_ref, xb_ref, *, alpha):
    j = pl.program_id(1)

    @pl.when(j == 0)
    def _():
        acc_ref[...] = jnp.zeros_like(acc_ref)
        xb_ref[...] = x_ref[...].astype(BF16)

    xb = xb_ref[...]
    h1 = _dot(xb, w1_ref[...])
    h3 = _dot(xb, w3_ref[...])
    h = (h1 * jax.nn.sigmoid(h1)) * h3
    acc_ref[...] += _dot(h.astype(BF16), w2_ref[...])

    @pl.when(j == pl.num_programs(1) - 1)
    def _():
        y = alpha * x_ref[...] + 0.5 * acc_ref[...]
        o_ref[...] = _ln(y, g_ref[...], b_ref[...])


def _ffn_ln(x2, w1, w3, w2, lng, lnb, l, which, ln_idx, alpha):
    n = x2.shape[0]
    tm = min(FFN_TM, n)
    grid = (n // tm, D_FF // FFN_TF)
    return pl.pallas_call(
        functools.partial(_ffn_kernel, alpha=alpha),
        grid=grid,
        in_specs=[
            pl.BlockSpec((tm, D_MODEL), lambda m, j: (m, 0)),
            pl.BlockSpec((None, None, D_MODEL, FFN_TF), lambda m, j: (l, which, 0, j)),
            pl.BlockSpec((None, None, D_MODEL, FFN_TF), lambda m, j: (l, which, 0, j)),
            pl.BlockSpec((None, None, FFN_TF, D_MODEL), lambda m, j: (l, which, j, 0)),
            pl.BlockSpec((None, None, 1, D_MODEL), lambda m, j: (l, ln_idx, 0, 0)),
            pl.BlockSpec((None, None, 1, D_MODEL), lambda m, j: (l, ln_idx, 0, 0)),
        ],
        out_specs=pl.BlockSpec((tm, D_MODEL), lambda m, j: (m, 0)),
        out_shape=jax.ShapeDtypeStruct((n, D_MODEL), F32),
        scratch_shapes=[pltpu.VMEM((tm, D_MODEL), F32), pltpu.VMEM((tm, D_MODEL), BF16)],
        compiler_params=_params("parallel", "arbitrary"),
        name="ffn_ln",
    )(x2, w1, w3, w2, lng, lnb)


def _proj_kernel(x_ref, w_ref, o_ref):
    o_ref[...] = _dot(x_ref[...].astype(BF16), w_ref[...]).astype(o_ref.dtype)


def _proj(x2, w, out_dtype, tm=1024, tn=2048):
    n, k = x2.shape
    c = w.shape[1]
    tm = min(tm, n)
    tn = min(tn, c)
    return pl.pallas_call(
        _proj_kernel,
        grid=(n // tm, c // tn),
        in_specs=[pl.BlockSpec((tm, k), lambda m, j: (m, 0)),
                  pl.BlockSpec((k, tn), lambda m, j: (0, j))],
        out_specs=pl.BlockSpec((tm, tn), lambda m, j: (m, j)),
        out_shape=jax.ShapeDtypeStruct((n, c), out_dtype),
        compiler_params=_params("parallel", "arbitrary"),
        name="proj",
    )(x2, w)


NSA_QW = NSA_HEADS * LANE
NSA_KW = NSA_KV * LANE
A_Q = 0
A_KV = NSA_QW
A_KV_ORDER = ("kc", "vc", "ks", "vs", "kw", "vw")
A_GATE = A_KV + len(A_KV_ORDER) * LANE
A_COLS = A_GATE + NSA_KW
PROJA_TM = 512


def _rope_lanes(y, cos, sin_lo, sin_hi, half):
    return (y * cos + pltpu.roll(y, LANE - half, axis=1) * sin_lo
            + pltpu.roll(y, half, axis=1) * sin_hi)


def _proja_kernel(x_ref, w_ref, cos_ref, slo_ref, shi_ref,
                  q_ref, qr_ref, kc_ref, vc_ref, ks_ref, vs_ref, kw_ref, vw_ref, g_ref):
    xb = x_ref[...].astype(BF16)
    cos = cos_ref[...]
    sin_lo = slo_ref[...]
    sin_hi = shi_ref[...]
    half = NSA_ROT // 2
    tm = x_ref.shape[0]

    def mm(c0, width):
        return _dot(xb, w_ref[:, c0:c0 + width])

    qs = NSA_DH ** -0.5 * LOG2E
    y = mm(A_Q, NSA_QW)
    q_ref[...] = (y * qs).astype(BF16)
    for h in range(NSA_HEADS):
        hs = slice(h * LANE, (h + 1) * LANE)
        qr_ref[:, hs] = (_rope_lanes(y[:, hs], cos, sin_lo, sin_hi, half) * qs).astype(BF16)

    lane = lax.broadcasted_iota(jnp.int32, (tm, LANE), 1)
    first = lane < NSA_DH
    tok = pl.program_id(1) * tm + lax.broadcasted_iota(jnp.int32, (tm, LANE), 0)
    blk_flag = jnp.where(lane - SEL_LANE0 == tok // SEL_L, 1.0, 0.0)
    den_flag = jnp.where(lane == DEN_LANE, 1.0, 0.0)

    def packed(table):
        return jnp.where(first, table, pltpu.roll(table, NSA_DH, axis=1))

    def put(ref, y, flag=None):
        for g, yg in enumerate((y, pltpu.roll(y, NSA_DH, axis=1))):
            yg = jnp.where(first, yg, 0.0)
            ref[g] = (yg if flag is None else yg + flag).astype(BF16)

    kv = mm(A_KV, len(A_KV_ORDER) * LANE)
    part = {name: kv[:, i * LANE:(i + 1) * LANE] for i, name in enumerate(A_KV_ORDER)}
    cos_p, lo_p, hi_p = packed(cos), packed(sin_lo), packed(sin_hi)
    put(kc_ref, part["kc"])
    put(vc_ref, part["vc"])
    put(ks_ref, _rope_lanes(part["ks"], cos_p, lo_p, hi_p, half), blk_flag)
    put(vs_ref, part["vs"], den_flag)
    put(kw_ref, _rope_lanes(part["kw"], cos_p, lo_p, hi_p, half))
    put(vw_ref, part["vw"], den_flag)
    g_ref[...] = mm(A_GATE, NSA_KW)


def _proja(x3, wa, cos_a, slo_a, shi_a):
    b, s, _ = x3.shape
    tm = PROJA_TM
    kv_shape = jax.ShapeDtypeStruct((b, NSA_KV, s, LANE), BF16)
    kv_spec = pl.BlockSpec((None, NSA_KV, tm, LANE), lambda bi, m: (bi, 0, m, 0))
    return pl.pallas_call(
        _proja_kernel,
        grid=(b, s // tm),
        in_specs=[
            pl.BlockSpec((None, tm, D_MODEL), lambda bi, m: (bi, m, 0)),
            _resident((D_MODEL, A_COLS), lambda bi, m: (0, 0)),
            pl.BlockSpec((None, tm, LANE), lambda bi, m: (bi, m, 0)),
            pl.BlockSpec((None, tm, LANE), lambda bi, m: (bi, m, 0)),
            pl.BlockSpec((None, tm, LANE), lambda bi, m: (bi, m, 0)),
        ],
        out_specs=[
            pl.BlockSpec((None, tm, NSA_QW), lambda bi, m: (bi, m, 0)),
            pl.BlockSpec((None, tm, NSA_QW), lambda bi, m: (bi, m, 0)),
            kv_spec, kv_spec, kv_spec, kv_spec, kv_spec, kv_spec,
            pl.BlockSpec((None, tm, NSA_KW), lambda bi, m: (bi, m, 0)),
        ],
        out_shape=[
            jax.ShapeDtypeStruct((b, s, NSA_QW), BF16),
            jax.ShapeDtypeStruct((b, s, NSA_QW), BF16),
            kv_shape, kv_shape, kv_shape, kv_shape, kv_shape, kv_shape,
            jax.ShapeDtypeStruct((b, s, NSA_KW), F32),
        ],
        compiler_params=_params("parallel", "parallel"),
        name="nsa_proj",
    )(x3, wa, cos_a, slo_a, shi_a)


N_CMP = 128
CMP_HALF = CMP_D * LANE


def _gelu_tanh(x):
    return 0.5 * x * (1.0 + jnp.tanh(0.7978845608028654 * (x + 0.044715 * x * x * x)))


def _cmp_kernel(zk_ref, zv_ref, w1_ref, pe_ref, w2_ref, shift_ref, ok_ref, ov_ref):
    shift = shift_ref[...]
    for which, (z_ref, o_ref) in enumerate(((zk_ref, ok_ref), (zv_ref, ov_ref))):
        z = z_ref[...]
        zn = _dot(shift, z).astype(BF16)
        bias = _dot(pe_ref[which], w1_ref[which])[0:1]
        pre = (_dot(z, w1_ref[which, 0:CMP_HALF, :])
               + _dot(zn, w1_ref[which, CMP_HALF:2 * CMP_HALF, :]) + bias)
        h = _gelu_tanh(pre)
        o_ref[...] = _dot(h.astype(BF16), w2_ref[which]).astype(BF16)


def _cmp(zk, zv, w1p, pe8, w2p, shift):
    b = zk.shape[0]
    z_spec = pl.BlockSpec((None, None, N_CMP, CMP_HALF), lambda bi, g: (bi, g, 0, 0))
    o_spec = pl.BlockSpec((None, None, N_CMP, LANE), lambda bi, g: (bi, g, 0, 0))
    o_shape = jax.ShapeDtypeStruct((b, NSA_KV, N_CMP, LANE), BF16)
    return pl.pallas_call(
        _cmp_kernel,
        grid=(b, NSA_KV),
        in_specs=[
            z_spec, z_spec,
            _resident((2, 2 * CMP_HALF, CMP_HID), lambda bi, g: (0, 0, 0)),
            _resident((2, 8, 2 * CMP_HALF), lambda bi, g: (0, 0, 0)),
            _resident((2, CMP_HID, LANE), lambda bi, g: (0, 0, 0)),
            _resident((N_CMP, N_CMP), lambda bi, g: (0, 0)),
        ],
        out_specs=[o_spec, o_spec],
        out_shape=[o_shape, o_shape],
        compiler_params=_params("parallel", "parallel"),
        name="nsa_cmp",
    )(zk, zv, w1p, pe8, w2p, shift)


NSA_TQ = 256
NSA_TK = 512
N_SLC = 32
WIN_SLAB = WINDOW + NSA_TQ


def _nsa_kernel(q_ref, qr_ref, kc_ref, vc_ref, ks_ref, vs_ref, kw_ref, vw_ref, gt_ref, ovl_ref,
                place_ref, o_ref):
    tq, tk, r_heads = NSA_TQ, NSA_TK, NSA_R
    i = pl.program_id(2)
    t0 = i * tq
    tpos = t0 + lax.broadcasted_iota(jnp.int32, (tq, 1), 0)
    lane = lax.broadcasted_iota(jnp.int32, (1, LANE), 1)

    def stack(ref):
        return jnp.concatenate([ref[:, r * LANE:(r + 1) * LANE] for r in range(r_heads)], axis=0)

    gt = jax.nn.sigmoid(gt_ref[...])

    def gate(c):
        return jnp.stack([gt[:, 3 * r + c:3 * r + c + 1] for r in range(r_heads)], axis=0)

    q4w = stack(qr_ref)
    w0 = pl.multiple_of(jnp.maximum(t0 - WINDOW, 0), LANE)
    kw = kw_ref[pl.ds(w0, WIN_SLAB), :]
    vw = vw_ref[pl.ds(w0, WIN_SLAB), :]
    kpos = w0 + lax.broadcasted_iota(jnp.int32, (1, WIN_SLAB), 1)
    w_bias = jnp.where((kpos <= tpos) & (kpos > tpos - WINDOW), 0.0, NEG)
    s = _dot_nt(q4w, kw).reshape(r_heads, tq, WIN_SLAB) + w_bias[None]
    p = jnp.exp2(s - jnp.max(s, axis=-1, keepdims=True))
    o_win = _dot(p.reshape(r_heads * tq, WIN_SLAB).astype(BF16), vw).reshape(r_heads, tq, LANE)
    o_part = gate(2) * (o_win / o_win[:, :, DEN_LANE:DEN_LANE + 1])

    cmask = ((lane * CMP_D + (CMP_L - 1) <= tpos) & (lane < N_CMP - 1))[None]
    s = _dot_nt(stack(q_ref), kc_ref[...]).reshape(r_heads, tq, N_CMP)
    s = jnp.where(cmask, s, NEG)
    e = jnp.where(cmask, jnp.exp2(s - jnp.max(s, axis=-1, keepdims=True)), 0.0)
    den = jnp.sum(e, axis=-1, keepdims=True)
    p = e / jnp.where(den > 0.0, den, 1.0)
    psum = jnp.sum(p, axis=0)
    o_cmp = _dot(p.reshape(r_heads * tq, N_CMP).astype(BF16), vc_ref[...])
    o_part = o_part + gate(0) * o_cmp.reshape(r_heads, tq, LANE)

    imp = lax.dot_general(ovl_ref[...], psum, (((1,), (1,)), ((), ())),
                          precision=lax.Precision.HIGHEST, preferred_element_type=F32)
    blk = lax.broadcasted_iota(jnp.int32, (N_SLC, 1), 0)
    tpos_l = t0 + lax.broadcasted_iota(jnp.int32, (1, tq), 1)
    cur = tpos_l // SEL_L
    valid = blk * SEL_L <= tpos_l
    forced = (blk == 0) | (blk == cur) | (blk == cur - 1)
    score = jnp.where(valid & forced, BIG, jnp.where(valid, imp, -BIG))
    beats = []
    for j in range(N_SLC):
        sj = score[j:j + 1, :]
        beats.append(jnp.where((sj > score) | ((sj == score) & (j < blk)), 1.0, 0.0))
    while len(beats) > 1:
        beats = [a + b for a, b in zip(beats[0::2], beats[1::2])]
    sel_t = jnp.where((beats[0] < N_SEL) & valid, 1.0, 0.0)
    sel_q = lax.dot_general(sel_t, place_ref[...], (((0,), (0,)), ((), ())),
                            preferred_element_type=F32)
    in_flags = jnp.where((lane >= SEL_LANE0) & (lane < SEL_LANE0 + N_SLC), 1.0, 0.0)
    q_bias = (sel_q - in_flags) * (-NEG)
    q4 = (q4w.astype(F32).reshape(r_heads, tq, LANE) + q_bias[None]).astype(BF16)
    q4 = q4.reshape(r_heads * tq, LANE)

    def sel_step(c, carry, causal):
        m, acc = carry
        k0 = pl.multiple_of(c * tk, tk)
        k = ks_ref[pl.ds(k0, tk), :]
        v = vs_ref[pl.ds(k0, tk), :]
        s = _dot_nt(q4, k).reshape(r_heads, tq, tk)
        if causal:
            kpos = k0 + lax.broadcasted_iota(jnp.int32, (1, tk), 1)
            s = jnp.where((kpos <= tpos)[None], s, NEG)
        m_new = jnp.maximum(m, jnp.max(s, axis=-1, keepdims=True))
        p = jnp.exp2(s - m_new)
        pv = _dot(p.reshape(r_heads * tq, tk).astype(BF16), v).reshape(r_heads, tq, LANE)
        return m_new, jnp.exp2(m - m_new) * acc + pv

    last = (t0 + tq - 1) // tk
    carry = (jnp.full((r_heads, tq, 1), NEG, F32), jnp.zeros((r_heads, tq, LANE), F32))
    carry = lax.fori_loop(0, last, functools.partial(sel_step, causal=False), carry)
    _, acc_s = sel_step(last, carry, True)
    o = o_part + gate(1) * (acc_s / acc_s[:, :, DEN_LANE:DEN_LANE + 1])
    for r in range(0, r_heads, 2):
        o_ref[:, (r // 2) * LANE:(r // 2 + 1) * LANE] = _pack_pair(o[r], o[r + 1]).astype(BF16)


def _nsa_attn(q, qr, kcmp, vcmp, ks, vs, kw, vw, gates, ovl, place):
    b, s, _ = q.shape
    tq = NSA_TQ
    gw = NSA_R * LANE
    q_spec = pl.BlockSpec((None, tq, gw), lambda bi, g, i: (bi, i, g))
    c_spec = pl.BlockSpec((None, None, N_CMP, LANE), lambda bi, g, i: (bi, g, 0, 0))
    kv_spec = pl.BlockSpec((None, None, s, LANE), lambda bi, g, i: (bi, g, 0, 0))
    return pl.pallas_call(
        _nsa_kernel,
        grid=(b, NSA_KV, s // tq),
        in_specs=[q_spec, q_spec, c_spec, c_spec, kv_spec, kv_spec, kv_spec, kv_spec,
                  pl.BlockSpec((None, tq, LANE), lambda bi, g, i: (bi, i, g)),
                  _resident((N_SLC, N_CMP), lambda bi, g, i: (0, 0)),
                  _resident((N_SLC, LANE), lambda bi, g, i: (0, 0))],
        out_specs=pl.BlockSpec((None, tq, gw // 2), lambda bi, g, i: (bi, i, g)),
        out_shape=jax.ShapeDtypeStruct((b, s, MIX_W), BF16),
        compiler_params=_params("parallel", "parallel", "arbitrary"),
        name="nsa_attn",
    )(q, qr, kcmp, vcmp, ks, vs, kw, vw, gates, ovl, place)


def _hgrn_kernel(x_ref, w_ref, lb_ref, ng_ref, o_ref, st_ref):
    s_len = x_ref.shape[0]
    c_len, sub_len = HGRN_CHUNK, HGRN_SUB
    nc = s_len // c_len
    proj = _dot(x_ref[...].astype(BF16), w_ref[...])
    q = proj[:, 0:HK]
    z = proj[:, HK:2 * HK]
    v = proj[:, 2 * HK:2 * HK + HV]
    go = proj[:, 2 * HK + HV:2 * HK + 2 * HV]
    lb = lb_ref[...]
    t = lax.broadcasted_iota(jnp.int32, (s_len, 1), 0)
    t_sub = t % sub_len

    f = lb + (1.0 - lb) * jax.nn.sigmoid(z)
    lf = jnp.log(jnp.maximum(f, F_MIN))
    k = (1.0 - lb) * jax.nn.sigmoid(-z)

    row = lax.broadcasted_iota(jnp.int32, (nc, c_len, c_len), 1)
    col = lax.broadcasted_iota(jnp.int32, (nc, c_len, c_len), 2)
    b3 = jnp.einsum('cts,csd->ctd', jnp.where(col <= row, 1.0, 0.0), lf.reshape(nc, c_len, HK),
                    precision=lax.Precision.HIGHEST, preferred_element_type=F32)
    b = b3.reshape(s_len, HK)
    k3 = k.reshape(nc, c_len, HK)
    vb3 = v.astype(BF16).reshape(nc, c_len, HV)

    a_intra = None
    h = c_len // 2
    while h >= sub_len:
        blk = b.reshape(s_len // (2 * h), 2 * h, HK)
        e = jnp.exp(-jnp.abs(blk[:, h - 1:h, :] - blk)).reshape(s_len, HK)
        upper = (t // h) % 2 == 1
        lq = jnp.where(upper, q * e, 0.0).astype(BF16).reshape(nc, c_len, HK)
        rk = jnp.where(upper, 0.0, k * e).astype(BF16).reshape(nc, c_len, HK)
        a = jnp.einsum('ctk,csk->cts', lq, rk, preferred_element_type=F32)
        if 2 * h < c_len:
            a = jnp.where(row // (2 * h) == col // (2 * h), a, 0.0)
        a_intra = a if a_intra is None else a_intra + a
        h //= 2
    o = jnp.einsum('cts,csd->ctd', a_intra.astype(BF16), vb3,
                   preferred_element_type=F32).reshape(s_len, HV)

    o = o + jnp.sum(q * k, axis=-1, keepdims=True) * v
    for d in range(1, sub_len):
        e = jnp.exp(jnp.where(t_sub >= d, b - pltpu.roll(b, d, axis=0), NEG))
        w = jnp.sum(q * pltpu.roll(k, d, axis=0) * e, axis=-1, keepdims=True)
        o = o + w * pltpu.roll(v, d, axis=0)

    b_last = b3[:, c_len - 1:c_len, :]
    kd3 = (k3 * jnp.exp(b_last - b3)).astype(BF16)
    upd = jnp.einsum('csv,csk->cvk', vb3, kd3, preferred_element_type=F32)
    dec = jnp.exp(b_last)
    state = jnp.zeros((HV, HK), F32)
    for c in range(nc):
        st_ref[c] = state.astype(BF16)
        state = state * dec[c] + upd[c]
    qe3 = (q * jnp.exp(b)).astype(BF16).reshape(nc, c_len, HK)
    o = o + jnp.einsum('ctk,cvk->ctv', qe3, st_ref[...],
                       preferred_element_type=F32).reshape(s_len, HV)

    o = o * lax.rsqrt(jnp.mean(o * o, axis=-1, keepdims=True) + RMS_EPS) * ng_ref[...]
    o_ref[...] = (o * (go * jax.nn.sigmoid(go))).astype(BF16)


def _hgrn(x3, wb, lb, ng):
    b, s, d = x3.shape
    return pl.pallas_call(
        _hgrn_kernel,
        grid=(b, HB),
        in_specs=[pl.BlockSpec((None, s, d), lambda bi, h: (bi, 0, 0)),
                  pl.BlockSpec((None, d, 4 * HK), lambda bi, h: (h, 0, 0)),
                  pl.BlockSpec((None, 1, HK), lambda bi, h: (h, 0, 0)),
                  pl.BlockSpec((1, HV), lambda bi, h: (0, 0))],
        out_specs=pl.BlockSpec((None, s, HV), lambda bi, h: (bi, 0, h)),
        out_shape=jax.ShapeDtypeStruct((b, s, HB * HV), BF16),
        scratch_shapes=[pltpu.VMEM((s // HGRN_CHUNK, HV, HK), BF16)],
        compiler_params=_params("parallel", "arbitrary"),
        name="hgrn",
    )(x3, wb, lb, ng)


MLA_W = HC * LANE
C_CQ, C_CKV, C_KR = 0, Q_RANK, Q_RANK + KV_RANK
C_COLS = C_KR + LANE
MLAP_TM = 512


def _rms(x, g):
    return x * lax.rsqrt(jnp.mean(x * x, axis=-1, keepdims=True) + RMS_EPS) * g


def _mlap_kernel(x_ref, wc_ref, qg_ref, kvg_ref, wuq_ref, wuk_ref, wuv_ref,
                 cos_ref, slo_ref, shi_ref, q_ref, k_ref, v_ref):
    half = ROPE_D // 2
    qs = (NOPE + ROPE_D) ** -0.5 * LOG2E
    den_flag = jnp.where(lax.broadcasted_iota(jnp.int32, (1, LANE), 1) == DEN_LANE, 1.0, 0.0)
    cos, lo, hi = cos_ref[...], slo_ref[...], shi_ref[...]
    c = _dot(x_ref[...].astype(BF16), wc_ref[...])
    nq = _rms(c[:, C_CQ:C_CKV], qg_ref[...]).astype(BF16)
    nkv = _rms(c[:, C_CKV:C_KR], kvg_ref[...]).astype(BF16)
    k_pe = _rope_lanes(c[:, C_KR:C_COLS], cos, lo, hi, half)
    yq = _dot(nq, wuq_ref[...])
    yk = _dot(nkv, wuk_ref[...])
    yv = _dot(nkv, wuv_ref[...])
    cos_q, lo_q, hi_q = cos * qs, lo * qs, hi * qs
    for h in range(HC):
        hs = slice(h * LANE, (h + 1) * LANE)
        q_ref[:, hs] = _rope_lanes(yq[:, hs], cos_q, lo_q, hi_q, half).astype(BF16)
        k_ref[:, hs] = (yk[:, hs] + k_pe).astype(BF16)
        v_ref[:, hs] = (yv[:, hs] + den_flag).astype(BF16)


def _mlap(x3, wc, qg, kvg, wuq, wuk, wuv, cos_c, slo_c, shi_c):
    b, s, _ = x3.shape
    tm = MLAP_TM
    o_spec = pl.BlockSpec((None, tm, MLA_W), lambda bi, m: (bi, m, 0))
    o_shape = jax.ShapeDtypeStruct((b, s, MLA_W), BF16)
    t_spec = pl.BlockSpec((None, tm, LANE), lambda bi, m: (bi, m, 0))
    return pl.pallas_call(
        _mlap_kernel,
        grid=(b, s // tm),
        in_specs=[
            pl.BlockSpec((None, tm, D_MODEL), lambda bi, m: (bi, m, 0)),
            _resident((D_MODEL, C_COLS), lambda bi, m: (0, 0)),
            _resident((1, Q_RANK), lambda bi, m: (0, 0)),
            _resident((1, KV_RANK), lambda bi, m: (0, 0)),
            _resident((Q_RANK, MLA_W), lambda bi, m: (0, 0)),
            _resident((KV_RANK, MLA_W), lambda bi, m: (0, 0)),
            _resident((KV_RANK, MLA_W), lambda bi, m: (0, 0)),
            t_spec, t_spec, t_spec,
        ],
        out_specs=[o_spec, o_spec, o_spec],
        out_shape=[o_shape, o_shape, o_shape],
        compiler_params=_params("parallel", "parallel"),
        name="mla_proj",
    )(x3, wc, qg, kvg, wuq, wuk, wuv, cos_c, slo_c, shi_c)


MLA_TQ = 512
MLA_TK = 512


MLA_HPS = 4


def _mla_attn_kernel(q_ref, k_ref, v_ref, o_ref):
    tq, tk = MLA_TQ, MLA_TK
    i = pl.program_id(2)
    t0 = i * tq
    tpos = t0 + lax.broadcasted_iota(jnp.int32, (tq, 1), 0)
    heads = [slice(h * LANE, (h + 1) * LANE) for h in range(MLA_HPS)]
    qs = [q_ref[:, hs] for hs in heads]

    def step(c, carry, causal):
        k0 = pl.multiple_of(c * tk, tk)
        if causal:
            keep = k0 + lax.broadcasted_iota(jnp.int32, (1, tk), 1) <= tpos
        scores = [_dot_nt(qs[h], k_ref[pl.ds(k0, tk), hs]) for h, hs in enumerate(heads)]
        m_new, probs = [], []
        for h in range(len(heads)):
            s = jnp.where(keep, scores[h], NEG) if causal else scores[h]
            m_new.append(jnp.maximum(carry[h][0], jnp.max(s, axis=-1, keepdims=True)))
            probs.append(jnp.exp2(s - m_new[h]).astype(BF16))
        pvs = [_dot(probs[h], v_ref[pl.ds(k0, tk), hs]) for h, hs in enumerate(heads)]
        return tuple((m_new[h], jnp.exp2(carry[h][0] - m_new[h]) * carry[h][1] + pvs[h])
                     for h in range(len(heads)))

    last = (t0 + tq - 1) // tk
    carry = tuple((jnp.full((tq, 1), NEG, F32), jnp.zeros((tq, LANE), F32)) for _ in heads)
    carry = lax.fori_loop(0, last, functools.partial(step, causal=False), carry)
    carry = step(last, carry, True)
    outs = [acc / acc[:, DEN_LANE:DEN_LANE + 1] for _, acc in carry]
    for h in range(0, MLA_HPS, 2):
        o_ref[:, (h // 2) * LANE:(h // 2 + 1) * LANE] = _pack_pair(outs[h], outs[h + 1]).astype(BF16)


def _mla_attn(q, k, v):
    b, s, _ = q.shape
    tq = MLA_TQ
    gw = MLA_HPS * LANE
    return pl.pallas_call(
        _mla_attn_kernel,
        grid=(b, HC // MLA_HPS, s // tq),
        in_specs=[pl.BlockSpec((None, tq, gw), lambda bi, h, i: (bi, i, h)),
                  pl.BlockSpec((None, s, gw), lambda bi, h, i: (bi, 0, h)),
                  pl.BlockSpec((None, s, gw), lambda bi, h, i: (bi, 0, h))],
        out_specs=pl.BlockSpec((None, tq, gw // 2), lambda bi, h, i: (bi, i, h)),
        out_shape=jax.ShapeDtypeStruct((b, s, MIX_W), BF16),
        compiler_params=_params("parallel", "parallel", "arbitrary"),
        name="mla_attn",
    )(q, k, v)


MERGE_TM = 512


def _merge_kernel(x_ref, ya_ref, yb_ref, yc_ref, wm_ref, wa_ref, wb_ref, wc_ref, wo_ref,
                  g_ref, b_ref, o_ref, *, alpha):
    x = x_ref[...]
    xb = x.astype(BF16)
    mixed = jnp.zeros(x.shape, F32)
    for idx, (y_ref, w_ref) in enumerate(((ya_ref, wa_ref), (yb_ref, wb_ref), (yc_ref, wc_ref))):
        gate = jax.nn.sigmoid(_dot(xb, wm_ref[:, idx * D_MODEL:(idx + 1) * D_MODEL]))
        mixed = mixed + gate * _dot(y_ref[...], w_ref[...])
    y = alpha * x + _dot(mixed.astype(BF16), wo_ref[...])
    o_ref[...] = _ln(y, g_ref[...], b_ref[...])


def _merge(x2, ya, yb, yc, wm, wa, wb, wc, wo, lng, lnb, l, alpha):
    n = x2.shape[0]
    tm = min(MERGE_TM, n)

    def rows(width):
        return pl.BlockSpec((tm, width), lambda m: (m, 0))

    return pl.pallas_call(
        functools.partial(_merge_kernel, alpha=alpha),
        grid=(n // tm,),
        in_specs=[rows(D_MODEL), rows(MIX_W), rows(MIX_W), rows(MIX_W),
                  _resident((D_MODEL, 3 * D_MODEL), lambda m: (0, 0)),
                  _resident((None, None, MIX_W, D_MODEL), lambda m: (l, 0, 0, 0)),
                  _resident((None, None, MIX_W, D_MODEL), lambda m: (l, 1, 0, 0)),
                  _resident((None, None, MIX_W, D_MODEL), lambda m: (l, 2, 0, 0)),
                  _resident((None, D_MODEL, D_MODEL), lambda m: (l, 0, 0)),
                  _resident((None, None, 1, D_MODEL), lambda m: (l, 1, 0, 0)),
                  _resident((None, None, 1, D_MODEL), lambda m: (l, 1, 0, 0))],
        out_specs=rows(D_MODEL),
        out_shape=jax.ShapeDtypeStruct((n, D_MODEL), F32),
        compiler_params=_params("parallel"),
        name="merge",
    )(x2, ya, yb, yc, wm, wa, wb, wc, wo, lng, lnb)


XA_TM = 1024


def _xattn_kernel(x_ref, wq_ref, k_ref, v_ref, wo_ref, g_ref, b_ref, o_ref, *, alpha):
    x = x_ref[...]
    xb = x.astype(BF16)
    q = (_dot(xb, wq_ref[...]) * (XA_DH ** -0.5 * LOG2E)).astype(BF16)
    ones = jnp.ones((k_ref.shape[0], LANE), BF16)
    heads = []
    for h in range(XA_HEADS):
        hs = slice(h * XA_DH, (h + 1) * XA_DH)
        s = _dot_nt(q[:, hs], k_ref[:, hs])
        e = jnp.exp2(s - jnp.max(s, axis=-1, keepdims=True)).astype(BF16)
        den = _dot(e, ones)
        den = jnp.concatenate([den] * (XA_DH // LANE), axis=1)
        heads.append((_dot(e, v_ref[:, hs]) / den).astype(BF16))
    att = jnp.concatenate(heads, axis=1)
    y = alpha * x + _dot(att, wo_ref[...])
    o_ref[...] = _ln(y, g_ref[...], b_ref[...])


def _xattn(x3, wq, k, v, wo, lng, lnb, l, alpha):
    b, s, _ = x3.shape
    m_len = k.shape[1]
    tm = XA_TM
    return pl.pallas_call(
        functools.partial(_xattn_kernel, alpha=alpha),
        grid=(b, s // tm),
        in_specs=[pl.BlockSpec((None, tm, D_MODEL), lambda bi, m: (bi, m, 0)),
                  _resident((None, D_MODEL, D_MODEL), lambda bi, m: (l, 0, 0)),
                  pl.BlockSpec((None, m_len, D_MODEL), lambda bi, m: (bi, 0, 0)),
                  pl.BlockSpec((None, m_len, D_MODEL), lambda bi, m: (bi, 0, 0)),
                  _resident((None, D_MODEL, D_MODEL), lambda bi, m: (l, 0, 0)),
                  _resident((None, None, 1, D_MODEL), lambda bi, m: (l, 2, 0, 0)),
                  _resident((None, None, 1, D_MODEL), lambda bi, m: (l, 2, 0, 0))],
        out_specs=pl.BlockSpec((None, tm, D_MODEL), lambda bi, m: (bi, m, 0)),
        out_shape=jax.ShapeDtypeStruct((b, s, D_MODEL), F32),
        compiler_params=_params("parallel", "parallel"),
        name="xattn",
    )(x3, wq, k, v, wo, lng, lnb)


def _pad_heads(w, n_heads, dh):
    k = w.shape[0]
    w = w.reshape(k, n_heads, dh)
    return jnp.pad(w, ((0, 0), (0, 0), (0, LANE - dh))).reshape(k, n_heads * LANE)


def _rope_tables(positions, rot, off):
    half = rot // 2
    inv = ROPE_THETA ** (-jnp.arange(half, dtype=F32) / half)
    ang = positions.astype(F32)[..., None] * inv
    cos, sin = jnp.cos(ang), jnp.sin(ang)
    shape = positions.shape
    zeros = lambda n: jnp.zeros(shape + (n,), F32)
    cos_t = jnp.concatenate([jnp.ones(shape + (off,), F32), cos, cos,
                             jnp.ones(shape + (LANE - off - rot,), F32)], axis=-1)
    sin_lo = jnp.concatenate([zeros(off), -sin, zeros(LANE - off - half)], axis=-1)
    sin_hi = jnp.concatenate([zeros(off + half), sin, zeros(LANE - off - rot)], axis=-1)
    return cos_t, sin_lo, sin_hi


def _overlap_matrix(s_len):
    n_cmp = (s_len - CMP_L) // CMP_D + 1
    n_slc = s_len // SEL_L
    start = np.arange(n_cmp) * CMP_D
    j = np.arange(n_slc)
    ov = np.clip(np.minimum(start[:, None] + CMP_L, (j[None, :] + 1) * SEL_L)
                 - np.maximum(start[:, None], j[None, :] * SEL_L), 0, None) / CMP_L
    out = np.zeros((N_SLC, N_CMP), np.float32)
    out[:n_slc, :n_cmp] = ov.T
    place = np.zeros((N_SLC, LANE), np.float32)
    place[np.arange(N_SLC), SEL_LANE0 + np.arange(N_SLC)] = 1.0
    return jnp.asarray(out), jnp.asarray(place)


def _layer_weights(l, w_in, nsa_cmp_pos, nsa_cmp_w1, nsa_cmp_w2, mla_w_uq, mla_w_ukv):
    wl = w_in[l]
    part = [wl[:, IN_OFF[i]:IN_OFF[i + 1]] for i in range(len(IN_SIZES))]
    (a_q, a_kc, a_vc, a_ks, a_vs, a_kw, a_vw, a_gate, b_q, b_f, b_i, b_g,
     c_q, c_kv, c_kr, merge) = part
    gate = a_gate.reshape(D_MODEL, NSA_KV, NSA_R * 3)
    gate = jnp.pad(gate, ((0, 0), (0, 0), (0, LANE - NSA_R * 3))).reshape(D_MODEL, NSA_KW)
    kv_parts = dict(kc=a_kc, vc=a_vc, ks=a_ks, vs=a_vs, kw=a_kw, vw=a_vw)
    wa = jnp.concatenate([_pad_heads(a_q, NSA_HEADS, NSA_DH)]
                         + [kv_parts[name] for name in A_KV_ORDER] + [gate], axis=1).astype(BF16)
    wb = jnp.stack([w.reshape(D_MODEL, HB, HK) for w in (b_q, b_f, b_i, b_g)], axis=2)
    wb = wb.transpose(1, 0, 2, 3).reshape(HB, D_MODEL, 4 * HK).astype(BF16)

    kr_pad = jnp.pad(c_kr, ((0, 0), (NOPE, LANE - NOPE - ROPE_D)))
    wc = jnp.concatenate([c_q, c_kv, kr_pad], axis=1).astype(BF16)
    wuq = _pad_heads(mla_w_uq[l], HC, NOPE + ROPE_D).astype(BF16)
    ukv = mla_w_ukv[l].reshape(KV_RANK, HC, NOPE + VD)
    wuk = _pad_heads(ukv[:, :, :NOPE].reshape(KV_RANK, HC * NOPE), HC, NOPE).astype(BF16)
    wuv = _pad_heads(ukv[:, :, NOPE:].reshape(KV_RANK, HC * VD), HC, VD).astype(BF16)

    w1 = nsa_cmp_w1[l].reshape(2, CMP_L, NSA_DH, CMP_HID)
    w1p = jnp.pad(w1, ((0, 0), (0, 0), (0, LANE - NSA_DH), (0, 0))).reshape(2, CMP_L * LANE, CMP_HID)
    pe = jnp.pad(nsa_cmp_pos[l], ((0, 0), (0, 0), (0, LANE - NSA_DH))).reshape(2, 1, CMP_L * LANE)
    pe8 = jnp.broadcast_to(pe, (2, 8, CMP_L * LANE))
    w2p = jnp.pad(nsa_cmp_w2[l], ((0, 0), (0, 0), (0, LANE - NSA_DH)))
    return dict(wa=wa, wb=wb, wc=wc, wuq=wuq, wuk=wuk, wuv=wuv,
                w1p=w1p.astype(BF16), pe8=pe8.astype(BF16), w2p=w2p.astype(BF16),
                wm=merge.astype(BF16))


def kernel(x, mem, positions, ln_g, ln_b, ffn_w1, ffn_w3, ffn_w2, w_in, nsa_cmp_pos,
           nsa_cmp_w1, nsa_cmp_w2, hgrn_lb_logits, hgrn_norm_g, mla_q_norm_g, mla_w_uq,
           mla_kv_norm_g, mla_w_ukv, w_branch, w_out, xa_wq, xa_wk, xa_wv, xa_wo):
    b, s, d = x.shape
    depth = ln_g.shape[0]
    n = b * s
    alpha = (2.0 * depth) ** 0.25

    lng = ln_g.reshape(depth, 4, 1, d)
    lnb = ln_b.reshape(depth, 4, 1, d)
    w1 = ffn_w1.astype(BF16)
    w3 = ffn_w3.astype(BF16)
    w2 = ffn_w2.astype(BF16)
    wo = w_out.astype(BF16)
    wbr = w_branch.astype(BF16)
    xq = xa_wq.astype(BF16)
    xk = xa_wk.astype(BF16)
    xv = xa_wv.astype(BF16)
    xo = xa_wo.astype(BF16)
    p_lb = jax.nn.softmax(hgrn_lb_logits.astype(F32), axis=0)
    lower = (jnp.cumsum(p_lb, axis=0) - p_lb[0:1]).reshape(depth, HB, 1, HK)

    rope_a = _rope_tables(positions, NSA_ROT, 0)
    rope_c = _rope_tables(positions, ROPE_D, NOPE)
    ovl, place = _overlap_matrix(s)
    shift = jnp.asarray(np.eye(N_CMP, k=1, dtype=np.float32)).astype(BF16)
    mem2 = mem.reshape(b * mem.shape[1], d)

    x2 = x.reshape(n, d)
    for l in range(depth):
        w = _layer_weights(l, w_in, nsa_cmp_pos, nsa_cmp_w1, nsa_cmp_w2, mla_w_uq, mla_w_ukv)
        x2 = _ffn_ln(x2, w1, w3, w2, lng, lnb, l, 0, 0, alpha)
        x3 = x2.reshape(b, s, d)

        q, qr, kc, vc, ks, vs, kw, vw, gates = _proja(x3, w["wa"], *rope_a)
        zk = kc.reshape(b, NSA_KV, N_CMP, CMP_HALF)
        zv = vc.reshape(b, NSA_KV, N_CMP, CMP_HALF)
        kcmp, vcmp = _cmp(zk, zv, w["w1p"], w["pe8"], w["w2p"], shift)
        ya = _nsa_attn(q, qr, kcmp, vcmp, ks, vs, kw, vw, gates, ovl, place)

        yb = _hgrn(x3, w["wb"], lower[l], hgrn_norm_g[l].reshape(1, HV))

        mq, mk, mv = _mlap(x3, w["wc"], mla_q_norm_g[l].reshape(1, Q_RANK),
                           mla_kv_norm_g[l].reshape(1, KV_RANK), w["wuq"],
                           w["wuk"], w["wuv"], *rope_c)
        yc = _mla_attn(mq, mk, mv)

        x2 = _merge(x2, ya.reshape(n, MIX_W), yb.reshape(n, MIX_W), yc.reshape(n, MIX_W),
                    w["wm"], wbr, wbr, wbr, wo, lng, lnb, l, alpha)

        xk_l = _proj(mem2, xk[l], BF16).reshape(b, -1, d)
        xv_l = _proj(mem2, xv[l], BF16).reshape(b, -1, d)
        x2 = _xattn(x2.reshape(b, s, d), xq, xk_l, xv_l, xo, lng, lnb, l, alpha).reshape(n, d)

        x2 = _ffn_ln(x2, w1, w3, w2, lng, lnb, l, 1, 3, alpha)
    return x2.reshape(b, s, d)
```

```python
import functools
import math

import numpy as np
import jax
import jax.numpy as jnp
from jax import lax
from jax.experimental import pallas as pl
from jax.experimental.pallas import tpu as pltpu

F32 = jnp.float32
BF16 = jnp.bfloat16

D_MODEL = 1024
MIX_W = D_MODEL // 2
NSA_DH = 64
NSA_HEADS = 8
NSA_KV = 2
NSA_R = 4
NSA_ROT = 16
CMP_L = 32
CMP_D = 16
CMP_HID = 256
SEL_L = 64
N_SEL = 8
WINDOW = 256
HB = 4
HK = 128
HV = 128
HGRN_CHUNK = 64
HGRN_SUB = 4
HC = 8
NOPE = 64
ROPE_D = 32
VD = 64
Q_RANK = 384
KV_RANK = 256
XA_HEADS = 4
XA_DH = D_MODEL // XA_HEADS
D_FF = 2816
ROPE_THETA = 500000.0
LN_EPS = 1e-5
RMS_EPS = 1e-6
NEG = -1e30
BIG = 1e9
F_MIN = 1e-20

LOG2E = 1.4426950408889634
LANE = 128
SEL_LANE0 = NSA_DH
DEN_LANE = 64
VMEM_LIMIT = 56 * 1024 * 1024

IN_SIZES = (512, 128, 128, 128, 128, 128, 128, 24, 512, 512, 512, 512, 384, 256, 32, 3072)
IN_OFF = tuple(int(v) for v in np.concatenate([[0], np.cumsum(IN_SIZES)]))


def _dot(a, b):
    return jnp.dot(a, b, preferred_element_type=F32)


def _dot_nt(a, b):
    return lax.dot_general(a, b, (((1,), (1,)), ((), ())), preferred_element_type=F32)


def _ln(y, g, b):
    mu = jnp.mean(y, axis=-1, keepdims=True)
    yc = y - mu
    var = jnp.mean(yc * yc, axis=-1, keepdims=True)
    return yc * lax.rsqrt(var + LN_EPS) * g + b


def _pack_pair(a, b):
    lane = lax.broadcasted_iota(jnp.int32, a.shape, a.ndim - 1)
    return jnp.where(lane < LANE // 2, a, pltpu.roll(b, LANE // 2, axis=a.ndim - 1))


def _params(*sem):
    return pltpu.CompilerParams(dimension_semantics=sem, vmem_limit_bytes=VMEM_LIMIT)


def _resident(shape, index_map):
    return pl.BlockSpec(shape, index_map, pipeline_mode=pl.Buffered(1))


FFN_TM = 1024
FFN_TF = 256


def _ffn_kernel(x_ref, w1_ref, w3_ref, w2_ref, g_ref, b_ref, o_ref, acc_ref, xb_ref, *, alpha):
    j = pl.program_id(1)

    @pl.when(j == 0)
    def _():
        acc_ref[...] = jnp.zeros_like(acc_ref)
        xb_ref[...] = x_ref[...].astype(BF16)

    xb = xb_ref[...]
    h1 = _dot(xb, w1_ref[...])
    h3 = _dot(xb, w3_ref[...])
    h = (h1 * jax.nn.sigmoid(h1)) * h3
    acc_ref[...] += _dot(h.astype(BF16), w2_ref[...])

    @pl.when(j == pl.num_programs(1) - 1)
    def _():
        y = alpha * x_ref[...] + 0.5 * acc_ref[...]
        o_ref[...] = _ln(y, g_ref[...], b_ref[...])


def _ffn_ln(x2, w1, w3, w2, lng, lnb, l, which, ln_idx, alpha):
    n = x2.shape[0]
    tm = min(FFN_TM, n)
    grid = (n // tm, D_FF // FFN_TF)
    return pl.pallas_call(
        functools.partial(_ffn_kernel, alpha=alpha),
        grid=grid,
        in_specs=[
            pl.BlockSpec((tm, D_MODEL), lambda m, j: (m, 0)),
            pl.BlockSpec((None, None, D_MODEL, FFN_TF), lambda m, j: (l, which, 0, j)),
            pl.BlockSpec((None, None, D_MODEL, FFN_TF), lambda m, j: (l, which, 0, j)),
            pl.BlockSpec((None, None, FFN_TF, D_MODEL), lambda m, j: (l, which, j, 0)),
            pl.BlockSpec((None, None, 1, D_MODEL), lambda m, j: (l, ln_idx, 0, 0)),
            pl.BlockSpec((None, None, 1, D_MODEL), lambda m, j: (l, ln_idx, 0, 0)),
        ],
        out_specs=pl.BlockSpec((tm, D_MODEL), lambda m, j: (m, 0)),
        out_shape=jax.ShapeDtypeStruct((n, D_MODEL), F32),
        scratch_shapes=[pltpu.VMEM((tm, D_MODEL), F32), pltpu.VMEM((tm, D_MODEL), BF16)],
        compiler_params=_params("parallel", "arbitrary"),
        name="ffn_ln",
    )(x2, w1, w3, w2, lng, lnb)


def _proj_kernel(x_ref, w_ref, o_ref):
    o_ref[...] = _dot(x_ref[...].astype(BF16), w_ref[...]).astype(o_ref.dtype)


def _proj(x2, w, l, out_dtype, tm=1024, tn=2048):
    n, k = x2.shape
    c = w.shape[2]
    tm = min(tm, n)
    tn = min(tn, c)
    return pl.pallas_call(
        _proj_kernel,
        grid=(n // tm, c // tn),
        in_specs=[pl.BlockSpec((tm, k), lambda m, j: (m, 0)),
                  pl.BlockSpec((None, k, tn), lambda m, j: (l, 0, j))],
        out_specs=pl.BlockSpec((tm, tn), lambda m, j: (m, j)),
        out_shape=jax.ShapeDtypeStruct((n, c), out_dtype),
        compiler_params=_params("parallel", "arbitrary"),
        name="proj",
    )(x2, w)


NSA_QW = NSA_HEADS * LANE
NSA_KW = NSA_KV * LANE
A_Q = 0
A_KV = NSA_QW
A_KV_ORDER = ("kc", "vc", "ks", "vs", "kw", "vw")
A_GATE = A_KV + len(A_KV_ORDER) * LANE
A_COLS = A_GATE + NSA_KW
PROJA_TM = 512


def _rope_lanes(y, cos, sin_lo, sin_hi, half):
    return (y * cos + pltpu.roll(y, LANE - half, axis=1) * sin_lo
            + pltpu.roll(y, half, axis=1) * sin_hi)


def _proja_kernel(x_ref, w_ref, cos_ref, slo_ref, shi_ref,
                  q_ref, qr_ref, kc_ref, vc_ref, ks_ref, vs_ref, kw_ref, vw_ref, g_ref):
    xb = x_ref[...].astype(BF16)
    cos = cos_ref[...]
    sin_lo = slo_ref[...]
    sin_hi = shi_ref[...]
    half = NSA_ROT // 2
    tm = x_ref.shape[0]

    def mm(c0, width):
        return _dot(xb, w_ref[:, c0:c0 + width])

    qs = NSA_DH ** -0.5 * LOG2E
    y = mm(A_Q, NSA_QW)
    q_ref[...] = (y * qs).astype(BF16)
    for h in range(NSA_HEADS):
        hs = slice(h * LANE, (h + 1) * LANE)
        qr_ref[:, hs] = (_rope_lanes(y[:, hs], cos, sin_lo, sin_hi, half) * qs).astype(BF16)

    lane = lax.broadcasted_iota(jnp.int32, (tm, LANE), 1)
    first = lane < NSA_DH
    tok = pl.program_id(1) * tm + lax.broadcasted_iota(jnp.int32, (tm, LANE), 0)
    blk_flag = jnp.where(lane - SEL_LANE0 == tok // SEL_L, 1.0, 0.0)
    den_flag = jnp.where(lane == DEN_LANE, 1.0, 0.0)

    def packed(table):
        return jnp.where(first, table, pltpu.roll(table, NSA_DH, axis=1))

    def put(ref, y, flag=None):
        for g, yg in enumerate((y, pltpu.roll(y, NSA_DH, axis=1))):
            yg = jnp.where(first, yg, 0.0)
            ref[g] = (yg if flag is None else yg + flag).astype(ref.dtype)

    kv = mm(A_KV, len(A_KV_ORDER) * LANE)
    part = {name: kv[:, i * LANE:(i + 1) * LANE] for i, name in enumerate(A_KV_ORDER)}
    cos_p, lo_p, hi_p = packed(cos), packed(sin_lo), packed(sin_hi)
    put(kc_ref, part["kc"])
    put(vc_ref, part["vc"])
    put(ks_ref, _rope_lanes(part["ks"], cos_p, lo_p, hi_p, half), blk_flag)
    put(vs_ref, part["vs"], den_flag)
    put(kw_ref, _rope_lanes(part["kw"], cos_p, lo_p, hi_p, half))
    put(vw_ref, part["vw"], den_flag)
    g_ref[...] = mm(A_GATE, NSA_KW)


def _proja(x3, wa, l, cos_a, slo_a, shi_a):
    b, s, _ = x3.shape
    tm = PROJA_TM
    kv_shape = jax.ShapeDtypeStruct((b, NSA_KV, s, LANE), BF16)
    cmp_shape = jax.ShapeDtypeStruct((b, NSA_KV, s, LANE), F32)
    kv_spec = pl.BlockSpec((None, NSA_KV, tm, LANE), lambda bi, m: (bi, 0, m, 0))
    return pl.pallas_call(
        _proja_kernel,
        grid=(b, s // tm),
        in_specs=[
            pl.BlockSpec((None, tm, D_MODEL), lambda bi, m: (bi, m, 0)),
            _resident((None, D_MODEL, A_COLS), lambda bi, m: (l, 0, 0)),
            pl.BlockSpec((None, tm, LANE), lambda bi, m: (bi, m, 0)),
            pl.BlockSpec((None, tm, LANE), lambda bi, m: (bi, m, 0)),
            pl.BlockSpec((None, tm, LANE), lambda bi, m: (bi, m, 0)),
        ],
        out_specs=[
            pl.BlockSpec((None, tm, NSA_QW), lambda bi, m: (bi, m, 0)),
            pl.BlockSpec((None, tm, NSA_QW), lambda bi, m: (bi, m, 0)),
            kv_spec, kv_spec, kv_spec, kv_spec, kv_spec, kv_spec,
            pl.BlockSpec((None, tm, NSA_KW), lambda bi, m: (bi, m, 0)),
        ],
        out_shape=[
            jax.ShapeDtypeStruct((b, s, NSA_QW), BF16),
            jax.ShapeDtypeStruct((b, s, NSA_QW), BF16),
            cmp_shape, cmp_shape, kv_shape, kv_shape, kv_shape, kv_shape,
            jax.ShapeDtypeStruct((b, s, NSA_KW), F32),
        ],
        compiler_params=_params("parallel", "parallel"),
        name="nsa_proj",
    )(x3, wa, cos_a, slo_a, shi_a)


N_CMP = 128
CMP_HALF = CMP_D * LANE


def _gelu_tanh(x):
    return 0.5 * x * (1.0 + jnp.tanh(0.7978845608028654 * (x + 0.044715 * x * x * x)))


def _cmp_kernel(zk_ref, zv_ref, w1_ref, pe_ref, w2_ref, ok_ref, ov_ref):
    for which, (z_ref, o_ref) in enumerate(((zk_ref, ok_ref), (zv_ref, ov_ref))):
        first = jnp.zeros((N_CMP, CMP_HID), F32)
        second = jnp.zeros((N_CMP, CMP_HID), F32)
        for j in range(CMP_D):
            zj = z_ref[pl.ds(j, N_CMP, stride=CMP_D), :].astype(BF16)
            first = first + _dot(zj, w1_ref[which, j * LANE:(j + 1) * LANE, :])
            second = second + _dot(zj, w1_ref[which, CMP_HALF + j * LANE:CMP_HALF + (j + 1) * LANE, :])
        bias = _dot(pe_ref[which], w1_ref[which])[0:1]
        pre = first + pltpu.roll(second, N_CMP - 1, axis=0) + bias
        h = _gelu_tanh(pre)
        o_ref[...] = _dot(h.astype(BF16), w2_ref[which]).astype(BF16)


def _cmp(zk, zv, w1p, pe8, w2p, l):
    b, _, s, _ = zk.shape
    z_spec = pl.BlockSpec((None, None, s, LANE), lambda bi, g: (bi, g, 0, 0))
    o_spec = pl.BlockSpec((None, None, N_CMP, LANE), lambda bi, g: (bi, g, 0, 0))
    o_shape = jax.ShapeDtypeStruct((b, NSA_KV, N_CMP, LANE), BF16)
    return pl.pallas_call(
        _cmp_kernel,
        grid=(b, NSA_KV),
        in_specs=[
            z_spec, z_spec,
            _resident((None, 2, 2 * CMP_HALF, CMP_HID), lambda bi, g: (l, 0, 0, 0)),
            _resident((None, 2, 8, 2 * CMP_HALF), lambda bi, g: (l, 0, 0, 0)),
            _resident((None, 2, CMP_HID, LANE), lambda bi, g: (l, 0, 0, 0)),
        ],
        out_specs=[o_spec, o_spec],
        out_shape=[o_shape, o_shape],
        compiler_params=_params("parallel", "parallel"),
        name="nsa_cmp",
    )(zk, zv, w1p, pe8, w2p)


NSA_TQ = 256
NSA_TK = 512
N_SLC = 32
WIN_SLAB = WINDOW + NSA_TQ


def _nsa_kernel(q_ref, qr_ref, kc_ref, vc_ref, ks_ref, vs_ref, kw_ref, vw_ref, gt_ref, ovl_ref,
                place_ref, o_ref):
    tq, tk, r_heads = NSA_TQ, NSA_TK, NSA_R
    i = pl.program_id(2)
    t0 = i * tq
    tpos = t0 + lax.broadcasted_iota(jnp.int32, (tq, 1), 0)
    lane = lax.broadcasted_iota(jnp.int32, (1, LANE), 1)

    def stack(ref):
        return jnp.concatenate([ref[:, r * LANE:(r + 1) * LANE] for r in range(r_heads)], axis=0)

    gt = jax.nn.sigmoid(gt_ref[...])

    def gate(c):
        return jnp.stack([gt[:, 3 * r + c:3 * r + c + 1] for r in range(r_heads)], axis=0)

    q4w = stack(qr_ref)
    w0 = pl.multiple_of(jnp.maximum(t0 - WINDOW, 0), LANE)
    kw = kw_ref[pl.ds(w0, WIN_SLAB), :]
    vw = vw_ref[pl.ds(w0, WIN_SLAB), :]
    kpos = w0 + lax.broadcasted_iota(jnp.int32, (1, WIN_SLAB), 1)
    w_bias = jnp.where((kpos <= tpos) & (kpos > tpos - WINDOW), 0.0, NEG)
    s = _dot_nt(q4w, kw).reshape(r_heads, tq, WIN_SLAB) + w_bias[None]
    p = jnp.exp2(s - jnp.max(s, axis=-1, keepdims=True))
    o_win = _dot(p.reshape(r_heads * tq, WIN_SLAB).astype(BF16), vw).reshape(r_heads, tq, LANE)
    o_part = gate(2) * (o_win / o_win[:, :, DEN_LANE:DEN_LANE + 1])

    cmask = ((lane * CMP_D + (CMP_L - 1) <= tpos) & (lane < N_CMP - 1))[None]
    s = _dot_nt(stack(q_ref), kc_ref[...]).reshape(r_heads, tq, N_CMP)
    s = jnp.where(cmask, s, NEG)
    e = jnp.where(cmask, jnp.exp2(s - jnp.max(s, axis=-1, keepdims=True)), 0.0)
    den = jnp.sum(e, axis=-1, keepdims=True)
    p = e / jnp.where(den > 0.0, den, 1.0)
    psum = jnp.sum(p, axis=0)
    o_cmp = _dot(p.reshape(r_heads * tq, N_CMP).astype(BF16), vc_ref[...])
    o_part = o_part + gate(0) * o_cmp.reshape(r_heads, tq, LANE)

    imp = lax.dot_general(ovl_ref[...], psum, (((1,), (1,)), ((), ())),
                          precision=lax.Precision.HIGHEST, preferred_element_type=F32)
    blk = lax.broadcasted_iota(jnp.int32, (N_SLC, 1), 0)
    tpos_l = t0 + lax.broadcasted_iota(jnp.int32, (1, tq), 1)
    cur = tpos_l // SEL_L
    valid = blk * SEL_L <= tpos_l
    forced = (blk == 0) | (blk == cur) | (blk == cur - 1)
    score = jnp.where(valid & forced, BIG, jnp.where(valid, imp, -BIG))
    beats = []
    for j in range(N_SLC):
        sj = score[j:j + 1, :]
        beats.append(jnp.where((sj > score) | ((sj == score) & (j < blk)), 1.0, 0.0))
    while len(beats) > 1:
        beats = [a + b for a, b in zip(beats[0::2], beats[1::2])]
    sel_t = jnp.where((beats[0] < N_SEL) & valid, 1.0, 0.0)
    sel_q = lax.dot_general(sel_t, place_ref[...], (((0,), (0,)), ((), ())),
                            preferred_element_type=F32)
    in_flags = jnp.where((lane >= SEL_LANE0) & (lane < SEL_LANE0 + N_SLC), 1.0, 0.0)
    q_bias = (sel_q - in_flags) * (-NEG)
    q4 = (q4w.astype(F32).reshape(r_heads, tq, LANE) + q_bias[None]).astype(BF16)
    q4 = q4.reshape(r_heads * tq, LANE)

    def sel_step(c, carry, causal):
        m, acc = carry
        k0 = pl.multiple_of(c * tk, tk)
        k = ks_ref[pl.ds(k0, tk), :]
        v = vs_ref[pl.ds(k0, tk), :]
        s = _dot_nt(q4, k).reshape(r_heads, tq, tk)
        if causal:
            kpos = k0 + lax.broadcasted_iota(jnp.int32, (1, tk), 1)
            s = jnp.where((kpos <= tpos)[None], s, NEG)
        m_new = jnp.maximum(m, jnp.max(s, axis=-1, keepdims=True))
        p = jnp.exp2(s - m_new)
        pv = _dot(p.reshape(r_heads * tq, tk).astype(BF16), v).reshape(r_heads, tq, LANE)
        return m_new, jnp.exp2(m - m_new) * acc + pv

    last = (t0 + tq - 1) // tk
    carry = (jnp.full((r_heads, tq, 1), NEG, F32), jnp.zeros((r_heads, tq, LANE), F32))
    carry = lax.fori_loop(0, last, functools.partial(sel_step, causal=False), carry)
    _, acc_s = sel_step(last, carry, True)
    o = o_part + gate(1) * (acc_s / acc_s[:, :, DEN_LANE:DEN_LANE + 1])
    for r in range(0, r_heads, 2):
        o_ref[:, (r // 2) * LANE:(r // 2 + 1) * LANE] = _pack_pair(o[r], o[r + 1]).astype(BF16)


def _nsa_attn(q, qr, kcmp, vcmp, ks, vs, kw, vw, gates, ovl, place):
    b, s, _ = q.shape
    tq = NSA_TQ
    gw = NSA_R * LANE
    q_spec = pl.BlockSpec((None, tq, gw), lambda bi, g, i: (bi, i, g))
    c_spec = pl.BlockSpec((None, None, N_CMP, LANE), lambda bi, g, i: (bi, g, 0, 0))
    kv_spec = pl.BlockSpec((None, None, s, LANE), lambda bi, g, i: (bi, g, 0, 0))
    return pl.pallas_call(
        _nsa_kernel,
        grid=(b, NSA_KV, s // tq),
        in_specs=[q_spec, q_spec, c_spec, c_spec, kv_spec, kv_spec, kv_spec, kv_spec,
                  pl.BlockSpec((None, tq, LANE), lambda bi, g, i: (bi, i, g)),
                  _resident((N_SLC, N_CMP), lambda bi, g, i: (0, 0)),
                  _resident((N_SLC, LANE), lambda bi, g, i: (0, 0))],
        out_specs=pl.BlockSpec((None, tq, gw // 2), lambda bi, g, i: (bi, i, g)),
        out_shape=jax.ShapeDtypeStruct((b, s, MIX_W), BF16),
        compiler_params=_params("parallel", "parallel", "arbitrary"),
        name="nsa_attn",
    )(q, qr, kcmp, vcmp, ks, vs, kw, vw, gates, ovl, place)


def _hgrn_kernel(x_ref, w_ref, lb_ref, ng_ref, o_ref, st_ref):
    s_len = x_ref.shape[0]
    c_len, sub_len = HGRN_CHUNK, HGRN_SUB
    nc = s_len // c_len
    proj = _dot(x_ref[...].astype(BF16), w_ref[...])
    q = proj[:, 0:HK]
    z = proj[:, HK:2 * HK]
    v = proj[:, 2 * HK:2 * HK + HV]
    go = proj[:, 2 * HK + HV:2 * HK + 2 * HV]
    lb = lb_ref[...]
    t = lax.broadcasted_iota(jnp.int32, (s_len, 1), 0)
    t_sub = t % sub_len

    f = lb + (1.0 - lb) * jax.nn.sigmoid(z)
    lf = jnp.log(jnp.maximum(f, F_MIN))
    k = (1.0 - lb) * jax.nn.sigmoid(-z)

    row = lax.broadcasted_iota(jnp.int32, (nc, c_len, c_len), 1)
    col = lax.broadcasted_iota(jnp.int32, (nc, c_len, c_len), 2)
    b3 = jnp.einsum('cts,csd->ctd', jnp.where(col <= row, 1.0, 0.0), lf.reshape(nc, c_len, HK),
                    precision=lax.Precision.HIGHEST, preferred_element_type=F32)
    b = b3.reshape(s_len, HK)
    k3 = k.reshape(nc, c_len, HK)
    vb3 = v.astype(BF16).reshape(nc, c_len, HV)

    a_intra = None
    h = c_len // 2
    while h >= sub_len:
        blk = b.reshape(s_len // (2 * h), 2 * h, HK)
        e = jnp.exp(-jnp.abs(blk[:, h - 1:h, :] - blk)).reshape(s_len, HK)
        upper = (t // h) % 2 == 1
        lq = jnp.where(upper, q * e, 0.0).astype(BF16).reshape(nc, c_len, HK)
        rk = jnp.where(upper, 0.0, k * e).astype(BF16).reshape(nc, c_len, HK)
        a = jnp.einsum('ctk,csk->cts', lq, rk, preferred_element_type=F32)
        if 2 * h < c_len:
            a = jnp.where(row // (2 * h) == col // (2 * h), a, 0.0)
        a_intra = a if a_intra is None else a_intra + a
        h //= 2
    o = jnp.einsum('cts,csd->ctd', a_intra.astype(BF16), vb3,
                   preferred_element_type=F32).reshape(s_len, HV)

    o = o + jnp.sum(q * k, axis=-1, keepdims=True) * v
    for d in range(1, sub_len):
        e = jnp.exp(jnp.where(t_sub >= d, b - pltpu.roll(b, d, axis=0), NEG))
        w = jnp.sum(q * pltpu.roll(k, d, axis=0) * e, axis=-1, keepdims=True)
        o = o + w * pltpu.roll(v, d, axis=0)

    b_last = b3[:, c_len - 1:c_len, :]
    kd3 = (k3 * jnp.exp(b_last - b3)).astype(BF16)
    upd = jnp.einsum('csv,csk->cvk', vb3, kd3, preferred_element_type=F32)
    dec = jnp.exp(b_last)
    state = jnp.zeros((HV, HK), F32)
    for c in range(nc):
        st_ref[c] = state.astype(BF16)
        state = state * dec[c] + upd[c]
    qe3 = (q * jnp.exp(b)).astype(BF16).reshape(nc, c_len, HK)
    o = o + jnp.einsum('ctk,cvk->ctv', qe3, st_ref[...],
                       preferred_element_type=F32).reshape(s_len, HV)

    o = o * lax.rsqrt(jnp.mean(o * o, axis=-1, keepdims=True) + RMS_EPS) * ng_ref[...]
    o_ref[...] = (o * (go * jax.nn.sigmoid(go))).astype(BF16)


def _hgrn(x3, wb, lb, ng, l):
    b, s, d = x3.shape
    return pl.pallas_call(
        _hgrn_kernel,
        grid=(b, HB),
        in_specs=[pl.BlockSpec((None, s, d), lambda bi, h: (bi, 0, 0)),
                  pl.BlockSpec((None, None, d, 4 * HK), lambda bi, h: (l, h, 0, 0)),
                  pl.BlockSpec((None, None, 1, HK), lambda bi, h: (l, h, 0, 0)),
                  pl.BlockSpec((None, 1, HV), lambda bi, h: (l, 0, 0))],
        out_specs=pl.BlockSpec((None, s, HV), lambda bi, h: (bi, 0, h)),
        out_shape=jax.ShapeDtypeStruct((b, s, HB * HV), BF16),
        scratch_shapes=[pltpu.VMEM((s // HGRN_CHUNK, HV, HK), BF16)],
        compiler_params=_params("parallel", "arbitrary"),
        name="hgrn",
    )(x3, wb, lb, ng)


MLA_W = HC * LANE
C_CQ, C_CKV, C_KR = 0, Q_RANK, Q_RANK + KV_RANK
C_COLS = C_KR + LANE
MLAP_TM = 512


def _rms(x, g):
    return x * lax.rsqrt(jnp.mean(x * x, axis=-1, keepdims=True) + RMS_EPS) * g


def _mlap_kernel(x_ref, wc_ref, qg_ref, kvg_ref, wuq_ref, wuk_ref, wuv_ref,
                 cos_ref, slo_ref, shi_ref, q_ref, k_ref, v_ref):
    half = ROPE_D // 2
    qs = (NOPE + ROPE_D) ** -0.5 * LOG2E
    den_flag = jnp.where(lax.broadcasted_iota(jnp.int32, (1, LANE), 1) == DEN_LANE, 1.0, 0.0)
    cos, lo, hi = cos_ref[...], slo_ref[...], shi_ref[...]
    c = _dot(x_ref[...].astype(BF16), wc_ref[...])
    nq = _rms(c[:, C_CQ:C_CKV], qg_ref[...]).astype(BF16)
    nkv = _rms(c[:, C_CKV:C_KR], kvg_ref[...]).astype(BF16)
    k_pe = _rope_lanes(c[:, C_KR:C_COLS], cos, lo, hi, half)
    yq = _dot(nq, wuq_ref[...])
    yk = _dot(nkv, wuk_ref[...])
    yv = _dot(nkv, wuv_ref[...])
    cos_q, lo_q, hi_q = cos * qs, lo * qs, hi * qs
    for h in range(HC):
        hs = slice(h * LANE, (h + 1) * LANE)
        q_ref[:, hs] = _rope_lanes(yq[:, hs], cos_q, lo_q, hi_q, half).astype(BF16)
        k_ref[:, hs] = (yk[:, hs] + k_pe).astype(BF16)
        v_ref[:, hs] = (yv[:, hs] + den_flag).astype(BF16)


def _mlap(x3, wc, qg, kvg, wuq, wuk, wuv, l, cos_c, slo_c, shi_c):
    b, s, _ = x3.shape
    tm = MLAP_TM
    o_spec = pl.BlockSpec((None, tm, MLA_W), lambda bi, m: (bi, m, 0))
    o_shape = jax.ShapeDtypeStruct((b, s, MLA_W), BF16)
    t_spec = pl.BlockSpec((None, tm, LANE), lambda bi, m: (bi, m, 0))
    return pl.pallas_call(
        _mlap_kernel,
        grid=(b, s // tm),
        in_specs=[
            pl.BlockSpec((None, tm, D_MODEL), lambda bi, m: (bi, m, 0)),
            _resident((None, D_MODEL, C_COLS), lambda bi, m: (l, 0, 0)),
            _resident((None, 1, Q_RANK), lambda bi, m: (l, 0, 0)),
            _resident((None, 1, KV_RANK), lambda bi, m: (l, 0, 0)),
            _resident((None, Q_RANK, MLA_W), lambda bi, m: (l, 0, 0)),
            _resident((None, KV_RANK, MLA_W), lambda bi, m: (l, 0, 0)),
            _resident((None, KV_RANK, MLA_W), lambda bi, m: (l, 0, 0)),
            t_spec, t_spec, t_spec,
        ],
        out_specs=[o_spec, o_spec, o_spec],
        out_shape=[o_shape, o_shape, o_shape],
        compiler_params=_params("parallel", "parallel"),
        name="mla_proj",
    )(x3, wc, qg, kvg, wuq, wuk, wuv, cos_c, slo_c, shi_c)


MLA_TQ = 512
MLA_TK = 512


MLA_HPS = 4


def _mla_attn_kernel(q_ref, k_ref, v_ref, o_ref):
    tq, tk = MLA_TQ, MLA_TK
    i = pl.program_id(2)
    t0 = i * tq
    tpos = t0 + lax.broadcasted_iota(jnp.int32, (tq, 1), 0)
    heads = [slice(h * LANE, (h + 1) * LANE) for h in range(MLA_HPS)]
    qs = [q_ref[:, hs] for hs in heads]

    def step(c, carry, causal):
        k0 = pl.multiple_of(c * tk, tk)
        if causal:
            keep = k0 + lax.broadcasted_iota(jnp.int32, (1, tk), 1) <= tpos
        scores = [_dot_nt(qs[h], k_ref[pl.ds(k0, tk), hs]) for h, hs in enumerate(heads)]
        m_new, probs = [], []
        for h in range(len(heads)):
            s = jnp.where(keep, scores[h], NEG) if causal else scores[h]
            m_new.append(jnp.maximum(carry[h][0], jnp.max(s, axis=-1, keepdims=True)))
            probs.append(jnp.exp2(s - m_new[h]).astype(BF16))
        pvs = [_dot(probs[h], v_ref[pl.ds(k0, tk), hs]) for h, hs in enumerate(heads)]
        return tuple((m_new[h], jnp.exp2(carry[h][0] - m_new[h]) * carry[h][1] + pvs[h])
                     for h in range(len(heads)))

    last = (t0 + tq - 1) // tk
    carry = tuple((jnp.full((tq, 1), NEG, F32), jnp.zeros((tq, LANE), F32)) for _ in heads)
    carry = lax.fori_loop(0, last, functools.partial(step, causal=False), carry)
    carry = step(last, carry, True)
    outs = [acc / acc[:, DEN_LANE:DEN_LANE + 1] for _, acc in carry]
    for h in range(0, MLA_HPS, 2):
        o_ref[:, (h // 2) * LANE:(h // 2 + 1) * LANE] = _pack_pair(outs[h], outs[h + 1]).astype(BF16)


def _mla_attn(q, k, v):
    b, s, _ = q.shape
    tq = MLA_TQ
    gw = MLA_HPS * LANE
    return pl.pallas_call(
        _mla_attn_kernel,
        grid=(b, HC // MLA_HPS, s // tq),
        in_specs=[pl.BlockSpec((None, tq, gw), lambda bi, h, i: (bi, i, h)),
                  pl.BlockSpec((None, s, gw), lambda bi, h, i: (bi, 0, h)),
                  pl.BlockSpec((None, s, gw), lambda bi, h, i: (bi, 0, h))],
        out_specs=pl.BlockSpec((None, tq, gw // 2), lambda bi, h, i: (bi, i, h)),
        out_shape=jax.ShapeDtypeStruct((b, s, MIX_W), BF16),
        compiler_params=_params("parallel", "parallel", "arbitrary"),
        name="mla_attn",
    )(q, k, v)


MERGE_TM = 512


def _merge_kernel(x_ref, ya_ref, yb_ref, yc_ref, wm_ref, wa_ref, wb_ref, wc_ref, wo_ref,
                  g_ref, b_ref, o_ref, *, alpha):
    x = x_ref[...]
    xb = x.astype(BF16)
    mixed = jnp.zeros(x.shape, F32)
    for idx, (y_ref, w_ref) in enumerate(((ya_ref, wa_ref), (yb_ref, wb_ref), (yc_ref, wc_ref))):
        gate = jax.nn.sigmoid(_dot(xb, wm_ref[:, idx * D_MODEL:(idx + 1) * D_MODEL]))
        mixed = mixed + gate * _dot(y_ref[...], w_ref[...])
    y = alpha * x + _dot(mixed.astype(BF16), wo_ref[...])
    o_ref[...] = _ln(y, g_ref[...], b_ref[...])


def _merge(x2, ya, yb, yc, wm, wa, wb, wc, wo, lng, lnb, l, alpha):
    n = x2.shape[0]
    tm = min(MERGE_TM, n)

    def rows(width):
        return pl.BlockSpec((tm, width), lambda m: (m, 0))

    return pl.pallas_call(
        functools.partial(_merge_kernel, alpha=alpha),
        grid=(n // tm,),
        in_specs=[rows(D_MODEL), rows(MIX_W), rows(MIX_W), rows(MIX_W),
                  _resident((None, D_MODEL, 3 * D_MODEL), lambda m: (l, 0, 0)),
                  _resident((None, None, MIX_W, D_MODEL), lambda m: (l, 0, 0, 0)),
                  _resident((None, None, MIX_W, D_MODEL), lambda m: (l, 1, 0, 0)),
                  _resident((None, None, MIX_W, D_MODEL), lambda m: (l, 2, 0, 0)),
                  _resident((None, D_MODEL, D_MODEL), lambda m: (l, 0, 0)),
                  _resident((None, None, 1, D_MODEL), lambda m: (l, 1, 0, 0)),
                  _resident((None, None, 1, D_MODEL), lambda m: (l, 1, 0, 0))],
        out_specs=rows(D_MODEL),
        out_shape=jax.ShapeDtypeStruct((n, D_MODEL), F32),
        compiler_params=_params("parallel"),
        name="merge",
    )(x2, ya, yb, yc, wm, wa, wb, wc, wo, lng, lnb)


XA_TM = 1024


def _xattn_kernel(x_ref, wq_ref, k_ref, v_ref, wo_ref, g_ref, b_ref, o_ref, *, alpha):
    x = x_ref[...]
    xb = x.astype(BF16)
    q = (_dot(xb, wq_ref[...]) * (XA_DH ** -0.5 * LOG2E)).astype(BF16)
    ones = jnp.ones((k_ref.shape[0], LANE), BF16)
    heads = []
    for h in range(XA_HEADS):
        hs = slice(h * XA_DH, (h + 1) * XA_DH)
        s = _dot_nt(q[:, hs], k_ref[:, hs])
        e = jnp.exp2(s - jnp.max(s, axis=-1, keepdims=True)).astype(BF16)
        den = _dot(e, ones)
        den = jnp.concatenate([den] * (XA_DH // LANE), axis=1)
        heads.append((_dot(e, v_ref[:, hs]) / den).astype(BF16))
    att = jnp.concatenate(heads, axis=1)
    y = alpha * x + _dot(att, wo_ref[...])
    o_ref[...] = _ln(y, g_ref[...], b_ref[...])


def _xattn(x3, wq, k, v, wo, lng, lnb, l, alpha):
    b, s, _ = x3.shape
    m_len = k.shape[1]
    tm = XA_TM
    return pl.pallas_call(
        functools.partial(_xattn_kernel, alpha=alpha),
        grid=(b, s // tm),
        in_specs=[pl.BlockSpec((None, tm, D_MODEL), lambda bi, m: (bi, m, 0)),
                  _resident((None, D_MODEL, D_MODEL), lambda bi, m: (l, 0, 0)),
                  pl.BlockSpec((None, m_len, D_MODEL), lambda bi, m: (bi, 0, 0)),
                  pl.BlockSpec((None, m_len, D_MODEL), lambda bi, m: (bi, 0, 0)),
                  _resident((None, D_MODEL, D_MODEL), lambda bi, m: (l, 0, 0)),
                  _resident((None, None, 1, D_MODEL), lambda bi, m: (l, 2, 0, 0)),
                  _resident((None, None, 1, D_MODEL), lambda bi, m: (l, 2, 0, 0))],
        out_specs=pl.BlockSpec((None, tm, D_MODEL), lambda bi, m: (bi, m, 0)),
        out_shape=jax.ShapeDtypeStruct((b, s, D_MODEL), F32),
        compiler_params=_params("parallel", "parallel"),
        name="xattn",
    )(x3, wq, k, v, wo, lng, lnb)


def _pad_heads(w, n_heads, dh):
    depth, k, _ = w.shape
    w = w.reshape(depth, k, n_heads, dh)
    return jnp.pad(w, ((0, 0), (0, 0), (0, 0), (0, LANE - dh))).reshape(depth, k, n_heads * LANE)


def _rope_tables(positions, rot, off):
    half = rot // 2
    inv = ROPE_THETA ** (-jnp.arange(half, dtype=F32) / half)
    ang = positions.astype(F32)[..., None] * inv
    cos, sin = jnp.cos(ang), jnp.sin(ang)
    shape = positions.shape
    zeros = lambda n: jnp.zeros(shape + (n,), F32)
    cos_t = jnp.concatenate([jnp.ones(shape + (off,), F32), cos, cos,
                             jnp.ones(shape + (LANE - off - rot,), F32)], axis=-1)
    sin_lo = jnp.concatenate([zeros(off), -sin, zeros(LANE - off - half)], axis=-1)
    sin_hi = jnp.concatenate([zeros(off + half), sin, zeros(LANE - off - rot)], axis=-1)
    return cos_t, sin_lo, sin_hi


def _overlap_matrix(s_len):
    n_cmp = (s_len - CMP_L) // CMP_D + 1
    n_slc = s_len // SEL_L
    start = np.arange(n_cmp) * CMP_D
    j = np.arange(n_slc)
    ov = np.clip(np.minimum(start[:, None] + CMP_L, (j[None, :] + 1) * SEL_L)
                 - np.maximum(start[:, None], j[None, :] * SEL_L), 0, None) / CMP_L
    out = np.zeros((N_SLC, N_CMP), np.float32)
    out[:n_slc, :n_cmp] = ov.T
    place = np.zeros((N_SLC, LANE), np.float32)
    place[np.arange(N_SLC), SEL_LANE0 + np.arange(N_SLC)] = 1.0
    return jnp.asarray(out), jnp.asarray(place)


def _stacked_weights(w_in, nsa_cmp_pos, nsa_cmp_w1, nsa_cmp_w2, mla_w_uq, mla_w_ukv):
    depth = w_in.shape[0]
    part = [w_in[:, :, IN_OFF[i]:IN_OFF[i + 1]] for i in range(len(IN_SIZES))]
    (a_q, a_kc, a_vc, a_ks, a_vs, a_kw, a_vw, a_gate, b_q, b_f, b_i, b_g,
     c_q, c_kv, c_kr, merge) = part
    gate = _pad_heads(a_gate, NSA_KV, NSA_R * 3)
    kv_parts = dict(kc=a_kc, vc=a_vc, ks=a_ks, vs=a_vs, kw=a_kw, vw=a_vw)
    wa = jnp.concatenate([_pad_heads(a_q, NSA_HEADS, NSA_DH)]
                         + [kv_parts[name] for name in A_KV_ORDER] + [gate], axis=2).astype(BF16)
    wb = jnp.stack([w.reshape(depth, D_MODEL, HB, HK) for w in (b_q, b_f, b_i, b_g)], axis=3)
    wb = wb.transpose(0, 2, 1, 3, 4).reshape(depth, HB, D_MODEL, 4 * HK).astype(BF16)

    kr_pad = jnp.pad(c_kr, ((0, 0), (0, 0), (NOPE, LANE - NOPE - ROPE_D)))
    wc = jnp.concatenate([c_q, c_kv, kr_pad], axis=2).astype(BF16)
    wuq = _pad_heads(mla_w_uq, HC, NOPE + ROPE_D).astype(BF16)
    ukv = mla_w_ukv.reshape(depth, KV_RANK, HC, NOPE + VD)
    wuk = _pad_heads(ukv[..., :NOPE].reshape(depth, KV_RANK, HC * NOPE), HC, NOPE).astype(BF16)
    wuv = _pad_heads(ukv[..., NOPE:].reshape(depth, KV_RANK, HC * VD), HC, VD).astype(BF16)

    w1 = nsa_cmp_w1.reshape(depth, 2, CMP_L, NSA_DH, CMP_HID)
    w1p = jnp.pad(w1, ((0, 0), (0, 0), (0, 0), (0, LANE - NSA_DH), (0, 0)))
    w1p = w1p.reshape(depth, 2, CMP_L * LANE, CMP_HID)
    pe = jnp.pad(nsa_cmp_pos, ((0, 0), (0, 0), (0, 0), (0, LANE - NSA_DH)))
    pe8 = jnp.broadcast_to(pe.reshape(depth, 2, 1, CMP_L * LANE), (depth, 2, 8, CMP_L * LANE))
    w2p = jnp.pad(nsa_cmp_w2, ((0, 0), (0, 0), (0, 0), (0, LANE - NSA_DH)))
    return dict(wa=wa, wb=wb, wc=wc, wuq=wuq, wuk=wuk, wuv=wuv,
                w1p=w1p.astype(BF16), pe8=pe8.astype(BF16), w2p=w2p.astype(BF16),
                wm=merge.astype(BF16))


def kernel(x, mem, positions, ln_g, ln_b, ffn_w1, ffn_w3, ffn_w2, w_in, nsa_cmp_pos,
           nsa_cmp_w1, nsa_cmp_w2, hgrn_lb_logits, hgrn_norm_g, mla_q_norm_g, mla_w_uq,
           mla_kv_norm_g, mla_w_ukv, w_branch, w_out, xa_wq, xa_wk, xa_wv, xa_wo):
    b, s, d = x.shape
    depth = ln_g.shape[0]
    n = b * s
    alpha = (2.0 * depth) ** 0.25

    lng = ln_g.reshape(depth, 4, 1, d)
    lnb = ln_b.reshape(depth, 4, 1, d)
    w1 = ffn_w1.astype(BF16)
    w3 = ffn_w3.astype(BF16)
    w2 = ffn_w2.astype(BF16)
    wo = w_out.astype(BF16)
    wbr = w_branch.astype(BF16)
    xq = xa_wq.astype(BF16)
    xk = xa_wk.astype(BF16)
    xv = xa_wv.astype(BF16)
    xo = xa_wo.astype(BF16)
    p_lb = jax.nn.softmax(hgrn_lb_logits.astype(F32), axis=0)
    lower = (jnp.cumsum(p_lb, axis=0) - p_lb[0:1]).reshape(depth, HB, 1, HK)

    rope_a = _rope_tables(positions, NSA_ROT, 0)
    rope_c = _rope_tables(positions, ROPE_D, NOPE)
    ovl, place = _overlap_matrix(s)
    mem2 = mem.reshape(b * mem.shape[1], d)
    w = _stacked_weights(w_in, nsa_cmp_pos, nsa_cmp_w1, nsa_cmp_w2, mla_w_uq, mla_w_ukv)
    norm_b = hgrn_norm_g.reshape(depth, 1, HV)
    norm_q = mla_q_norm_g.reshape(depth, 1, Q_RANK)
    norm_kv = mla_kv_norm_g.reshape(depth, 1, KV_RANK)

    x2 = x.reshape(n, d)
    for l in range(depth):
        x2 = _ffn_ln(x2, w1, w3, w2, lng, lnb, l, 0, 0, alpha)
        x3 = x2.reshape(b, s, d)

        q, qr, kc, vc, ks, vs, kw, vw, gates = _proja(x3, w["wa"], l, *rope_a)
        kcmp, vcmp = _cmp(kc, vc, w["w1p"], w["pe8"], w["w2p"], l)
        ya = _nsa_attn(q, qr, kcmp, vcmp, ks, vs, kw, vw, gates, ovl, place)

        yb = _hgrn(x3, w["wb"], lower, norm_b, l)

        mq, mk, mv = _mlap(x3, w["wc"], norm_q, norm_kv, w["wuq"], w["wuk"], w["wuv"], l, *rope_c)
        yc = _mla_attn(mq, mk, mv)

        x2 = _merge(x2, ya.reshape(n, MIX_W), yb.reshape(n, MIX_W), yc.reshape(n, MIX_W),
                    w["wm"], wbr, wbr, wbr, wo, lng, lnb, l, alpha)

        xk_l = _proj(mem2, xk, l, BF16).reshape(b, -1, d)
        xv_l = _proj(mem2, xv, l, BF16).reshape(b, -1, d)
        x2 = _xattn(x2.reshape(b, s, d), xq, xk_l, xv_l, xo, lng, lnb, l, alpha).reshape(n, d)

        x2 = _ffn_ln(x2, w1, w3, w2, lng, lnb, l, 1, 3, alpha)
    return x2.reshape(b, s, d)
```

```python
import functools
import math

import numpy as np
import jax
import jax.numpy as jnp
from jax import lax
from jax.experimental import pallas as pl
from jax.experimental.pallas import tpu as pltpu

F32 = jnp.float32
BF16 = jnp.bfloat16

D_MODEL = 1024
MIX_W = D_MODEL // 2
NSA_DH = 64
NSA_HEADS = 8
NSA_KV = 2
NSA_R = 4
NSA_ROT = 16
CMP_L = 32
CMP_D = 16
CMP_HID = 256
SEL_L = 64
N_SEL = 8
WINDOW = 256
HB = 4
HK = 128
HV = 128
HGRN_CHUNK = 64
HGRN_SUB = 4
HGRN_PARTS = 4
HC = 8
NOPE = 64
ROPE_D = 32
VD = 64
Q_RANK = 384
KV_RANK = 256
XA_HEADS = 4
XA_DH = D_MODEL // XA_HEADS
D_FF = 2816
ROPE_THETA = 500000.0
LN_EPS = 1e-5
RMS_EPS = 1e-6
NEG = -1e30
BIG = 1e9
F_MIN = 1e-20

LOG2E = 1.4426950408889634
LANE = 128
SEL_LANE0 = NSA_DH
DEN_LANE = 64
VMEM_LIMIT = 56 * 1024 * 1024

IN_SIZES = (512, 128, 128, 128, 128, 128, 128, 24, 512, 512, 512, 512, 384, 256, 32, 3072)
IN_OFF = tuple(int(v) for v in np.concatenate([[0], np.cumsum(IN_SIZES)]))


def _dot(a, b):
    return jnp.dot(a, b, preferred_element_type=F32)


def _dot_nt(a, b):
    return lax.dot_general(a, b, (((1,), (1,)), ((), ())), preferred_element_type=F32)


def _ln(y, g, b):
    mu = jnp.mean(y, axis=-1, keepdims=True)
    yc = y - mu
    var = jnp.mean(yc * yc, axis=-1, keepdims=True)
    return yc * lax.rsqrt(var + LN_EPS) * g + b


def _pack_pair(a, b):
    lane = lax.broadcasted_iota(jnp.int32, a.shape, a.ndim - 1)
    return jnp.where(lane < LANE // 2, a, pltpu.roll(b, LANE // 2, axis=a.ndim - 1))


def _params(*sem):
    return pltpu.CompilerParams(dimension_semantics=sem, vmem_limit_bytes=VMEM_LIMIT)


def _resident(shape, index_map):
    return pl.BlockSpec(shape, index_map, pipeline_mode=pl.Buffered(1))


FFN_TM = 1024
FFN_TF = 256


def _ffn_kernel(x_ref, w1_ref, w3_ref, w2h_ref, g_ref, b_ref, o_ref, acc_ref, xb_ref, *, alpha):
    j = pl.program_id(1)

    @pl.when(j == 0)
    def _():
        xb_ref[...] = x_ref[...].astype(BF16)
        acc_ref[...] = alpha * x_ref[...]

    xb = xb_ref[...]
    h1 = _dot(xb, w1_ref[...])
    h3 = _dot(xb, w3_ref[...])
    h = (h1 * jax.nn.sigmoid(h1)) * h3
    acc_ref[...] += _dot(h.astype(BF16), w2h_ref[...])

    @pl.when(j == pl.num_programs(1) - 1)
    def _():
        o_ref[...] = _ln(acc_ref[...], g_ref[...], b_ref[...])


def _ffn_ln(x2, w1, w3, w2h, lng, lnb, l, which, ln_idx, alpha):
    n = x2.shape[0]
    tm = min(FFN_TM, n)
    grid = (n // tm, D_FF // FFN_TF)
    return pl.pallas_call(
        functools.partial(_ffn_kernel, alpha=alpha),
        grid=grid,
        in_specs=[
            pl.BlockSpec((tm, D_MODEL), lambda m, j: (m, 0)),
            pl.BlockSpec((None, None, D_MODEL, FFN_TF), lambda m, j: (l, which, 0, j)),
            pl.BlockSpec((None, None, D_MODEL, FFN_TF), lambda m, j: (l, which, 0, j)),
            pl.BlockSpec((None, None, FFN_TF, D_MODEL), lambda m, j: (l, which, j, 0)),
            pl.BlockSpec((None, None, 1, D_MODEL), lambda m, j: (l, ln_idx, 0, 0)),
            pl.BlockSpec((None, None, 1, D_MODEL), lambda m, j: (l, ln_idx, 0, 0)),
        ],
        out_specs=pl.BlockSpec((tm, D_MODEL), lambda m, j: (m, 0)),
        out_shape=jax.ShapeDtypeStruct((n, D_MODEL), F32),
        scratch_shapes=[pltpu.VMEM((tm, D_MODEL), F32), pltpu.VMEM((tm, D_MODEL), BF16)],
        compiler_params=_params("parallel", "arbitrary"),
        name="ffn_ln",
    )(x2, w1, w3, w2h, lng, lnb)


def _proj_kernel(x_ref, w_ref, o_ref):
    o_ref[...] = _dot(x_ref[...].astype(BF16), w_ref[...]).astype(o_ref.dtype)


def _proj(x2, w, l, out_dtype, tm=1024, tn=2048):
    n, k = x2.shape
    c = w.shape[2]
    tm = min(tm, n)
    tn = min(tn, c)
    return pl.pallas_call(
        _proj_kernel,
        grid=(n // tm, c // tn),
        in_specs=[pl.BlockSpec((tm, k), lambda m, j: (m, 0)),
                  pl.BlockSpec((None, k, tn), lambda m, j: (l, 0, j))],
        out_specs=pl.BlockSpec((tm, tn), lambda m, j: (m, j)),
        out_shape=jax.ShapeDtypeStruct((n, c), out_dtype),
        compiler_params=_params("parallel", "arbitrary"),
        name="proj",
    )(x2, w)


NSA_QW = NSA_HEADS * LANE
NSA_KW = NSA_KV * LANE
A_Q = 0
A_KV = NSA_QW
A_KV_ORDER = ("kc", "vc", "ks", "vs", "kw", "vw")
A_GATE = A_KV + len(A_KV_ORDER) * LANE
A_COLS = A_GATE + NSA_KW
PROJA_TM = 512


def _rope_lanes(y, cos, sin_lo, sin_hi, half):
    return (y * cos + pltpu.roll(y, LANE - half, axis=1) * sin_lo
            + pltpu.roll(y, half, axis=1) * sin_hi)


def _proja_kernel(x_ref, w_ref, cos_ref, slo_ref, shi_ref,
                  q_ref, qr_ref, kc_ref, vc_ref, ks_ref, vs_ref, kw_ref, vw_ref, g_ref):
    xb = x_ref[...].astype(BF16)
    cos = cos_ref[...]
    sin_lo = slo_ref[...]
    sin_hi = shi_ref[...]
    half = NSA_ROT // 2
    tm = x_ref.shape[0]

    def mm(c0, width):
        return _dot(xb, w_ref[:, c0:c0 + width])

    qs = NSA_DH ** -0.5 * LOG2E
    y = mm(A_Q, NSA_QW)
    q_ref[...] = (y * qs).astype(BF16)
    for h in range(NSA_HEADS):
        hs = slice(h * LANE, (h + 1) * LANE)
        qr_ref[:, hs] = (_rope_lanes(y[:, hs], cos, sin_lo, sin_hi, half) * qs).astype(BF16)

    lane = lax.broadcasted_iota(jnp.int32, (tm, LANE), 1)
    first = lane < NSA_DH
    tok = pl.program_id(1) * tm + lax.broadcasted_iota(jnp.int32, (tm, LANE), 0)
    blk_flag = jnp.where(lane - SEL_LANE0 == tok // SEL_L, 1.0, 0.0)
    den_flag = jnp.where(lane == DEN_LANE, 1.0, 0.0)

    def packed(table):
        return jnp.where(first, table, pltpu.roll(table, NSA_DH, axis=1))

    def put(ref, y, flag=None):
        for g, yg in enumerate((y, pltpu.roll(y, NSA_DH, axis=1))):
            yg = jnp.where(first, yg, 0.0)
            ref[g] = (yg if flag is None else yg + flag).astype(ref.dtype)

    kv = mm(A_KV, len(A_KV_ORDER) * LANE)
    part = {name: kv[:, i * LANE:(i + 1) * LANE] for i, name in enumerate(A_KV_ORDER)}
    cos_p, lo_p, hi_p = packed(cos), packed(sin_lo), packed(sin_hi)
    put(kc_ref, part["kc"])
    put(vc_ref, part["vc"])
    put(ks_ref, _rope_lanes(part["ks"], cos_p, lo_p, hi_p, half), blk_flag)
    put(vs_ref, part["vs"], den_flag)
    put(kw_ref, _rope_lanes(part["kw"], cos_p, lo_p, hi_p, half))
    put(vw_ref, part["vw"], den_flag)
    g_ref[...] = mm(A_GATE, NSA_KW)


def _proja(x3, wa, l, cos_a, slo_a, shi_a):
    b, s, _ = x3.shape
    tm = PROJA_TM
    kv_shape = jax.ShapeDtypeStruct((b, NSA_KV, s, LANE), BF16)
    cmp_shape = jax.ShapeDtypeStruct((b, NSA_KV, s, LANE), F32)
    kv_spec = pl.BlockSpec((None, NSA_KV, tm, LANE), lambda bi, m: (bi, 0, m, 0))
    return pl.pallas_call(
        _proja_kernel,
        grid=(b, s // tm),
        in_specs=[
            pl.BlockSpec((None, tm, D_MODEL), lambda bi, m: (bi, m, 0)),
            _resident((None, D_MODEL, A_COLS), lambda bi, m: (l, 0, 0)),
            pl.BlockSpec((None, tm, LANE), lambda bi, m: (bi, m, 0)),
            pl.BlockSpec((None, tm, LANE), lambda bi, m: (bi, m, 0)),
            pl.BlockSpec((None, tm, LANE), lambda bi, m: (bi, m, 0)),
        ],
        out_specs=[
            pl.BlockSpec((None, tm, NSA_QW), lambda bi, m: (bi, m, 0)),
            pl.BlockSpec((None, tm, NSA_QW), lambda bi, m: (bi, m, 0)),
            kv_spec, kv_spec, kv_spec, kv_spec, kv_spec, kv_spec,
            pl.BlockSpec((None, tm, NSA_KW), lambda bi, m: (bi, m, 0)),
        ],
        out_shape=[
            jax.ShapeDtypeStruct((b, s, NSA_QW), BF16),
            jax.ShapeDtypeStruct((b, s, NSA_QW), BF16),
            cmp_shape, cmp_shape, kv_shape, kv_shape, kv_shape, kv_shape,
            jax.ShapeDtypeStruct((b, s, NSA_KW), F32),
        ],
        compiler_params=_params("parallel", "parallel"),
        name="nsa_proj",
    )(x3, wa, cos_a, slo_a, shi_a)


N_CMP = 128
CMP_HALF = CMP_D * LANE


def _gelu_tanh(x):
    return 0.5 * x * (1.0 + jnp.tanh(0.7978845608028654 * (x + 0.044715 * x * x * x)))


def _cmp_kernel(zk_ref, zv_ref, w1_ref, pe_ref, w2_ref, ok_ref, ov_ref):
    for which, (z_ref, o_ref) in enumerate(((zk_ref, ok_ref), (zv_ref, ov_ref))):
        first = jnp.zeros((N_CMP, CMP_HID), F32)
        second = jnp.zeros((N_CMP, CMP_HID), F32)
        for j in range(CMP_D):
            zj = z_ref[pl.ds(j, N_CMP, stride=CMP_D), :].astype(BF16)
            first = first + _dot(zj, w1_ref[which, j * LANE:(j + 1) * LANE, :])
            second = second + _dot(zj, w1_ref[which, CMP_HALF + j * LANE:CMP_HALF + (j + 1) * LANE, :])
        bias = _dot(pe_ref[which], w1_ref[which])[0:1]
        pre = first + pltpu.roll(second, N_CMP - 1, axis=0) + bias
        h = _gelu_tanh(pre)
        o_ref[...] = _dot(h.astype(BF16), w2_ref[which]).astype(BF16)


def _cmp(zk, zv, w1p, pe8, w2p, l):
    b, _, s, _ = zk.shape
    z_spec = pl.BlockSpec((None, None, s, LANE), lambda bi, g: (bi, g, 0, 0))
    o_spec = pl.BlockSpec((None, None, N_CMP, LANE), lambda bi, g: (bi, g, 0, 0))
    o_shape = jax.ShapeDtypeStruct((b, NSA_KV, N_CMP, LANE), BF16)
    return pl.pallas_call(
        _cmp_kernel,
        grid=(b, NSA_KV),
        in_specs=[
            z_spec, z_spec,
            _resident((None, 2, 2 * CMP_HALF, CMP_HID), lambda bi, g: (l, 0, 0, 0)),
            _resident((None, 2, 8, 2 * CMP_HALF), lambda bi, g: (l, 0, 0, 0)),
            _resident((None, 2, CMP_HID, LANE), lambda bi, g: (l, 0, 0, 0)),
        ],
        out_specs=[o_spec, o_spec],
        out_shape=[o_shape, o_shape],
        compiler_params=_params("parallel", "parallel"),
        name="nsa_cmp",
    )(zk, zv, w1p, pe8, w2p)


NSA_TQ = 256
NSA_TK = 512
N_SLC = 32
WIN_SLAB = WINDOW + NSA_TQ


def _nsa_kernel(q_ref, qr_ref, kc_ref, vc_ref, ks_ref, vs_ref, kw_ref, vw_ref, gt_ref, ovl_ref,
                place_ref, o_ref):
    tq, tk, r_heads = NSA_TQ, NSA_TK, NSA_R
    i = pl.program_id(2)
    t0 = i * tq
    tpos = t0 + lax.broadcasted_iota(jnp.int32, (tq, 1), 0)
    lane = lax.broadcasted_iota(jnp.int32, (1, LANE), 1)

    def stack(ref):
        return jnp.concatenate([ref[:, r * LANE:(r + 1) * LANE] for r in range(r_heads)], axis=0)

    gt = jax.nn.sigmoid(gt_ref[...])

    def gate(c):
        return jnp.stack([gt[:, 3 * r + c:3 * r + c + 1] for r in range(r_heads)], axis=0)

    q4w = stack(qr_ref)
    w0 = pl.multiple_of(jnp.maximum(t0 - WINDOW, 0), LANE)
    kw = kw_ref[pl.ds(w0, WIN_SLAB), :]
    vw = vw_ref[pl.ds(w0, WIN_SLAB), :]
    kpos = w0 + lax.broadcasted_iota(jnp.int32, (1, WIN_SLAB), 1)
    w_bias = jnp.where((kpos <= tpos) & (kpos > tpos - WINDOW), 0.0, NEG)
    s = _dot_nt(q4w, kw).reshape(r_heads, tq, WIN_SLAB) + w_bias[None]
    p = jnp.exp2(s - jnp.max(s, axis=-1, keepdims=True))
    o_win = _dot(p.reshape(r_heads * tq, WIN_SLAB).astype(BF16), vw).reshape(r_heads, tq, LANE)
    o_part = gate(2) * (o_win / o_win[:, :, DEN_LANE:DEN_LANE + 1])

    cmask = ((lane * CMP_D + (CMP_L - 1) <= tpos) & (lane < N_CMP - 1))[None]
    s = _dot_nt(stack(q_ref), kc_ref[...]).reshape(r_heads, tq, N_CMP)
    s = jnp.where(cmask, s, NEG)
    e = jnp.where(cmask, jnp.exp2(s - jnp.max(s, axis=-1, keepdims=True)), 0.0)
    den = jnp.sum(e, axis=-1, keepdims=True)
    p = e / jnp.where(den > 0.0, den, 1.0)
    psum = jnp.sum(p, axis=0)
    o_cmp = _dot(p.reshape(r_heads * tq, N_CMP).astype(BF16), vc_ref[...])
    o_part = o_part + gate(0) * o_cmp.reshape(r_heads, tq, LANE)

    imp = lax.dot_general(ovl_ref[...], psum, (((1,), (1,)), ((), ())),
                          precision=lax.Precision.HIGHEST, preferred_element_type=F32)
    blk = lax.broadcasted_iota(jnp.int32, (N_SLC, 1), 0)
    tpos_l = t0 + lax.broadcasted_iota(jnp.int32, (1, tq), 1)
    cur = tpos_l // SEL_L
    valid = blk * SEL_L <= tpos_l
    forced = (blk == 0) | (blk == cur) | (blk == cur - 1)
    score = jnp.where(valid & forced, BIG, jnp.where(valid, imp, -BIG))
    beats = []
    for j in range(N_SLC):
        sj = score[j:j + 1, :]
        beats.append(jnp.where((sj > score) | ((sj == score) & (j < blk)), 1.0, 0.0))
    while len(beats) > 1:
        beats = [a + b for a, b in zip(beats[0::2], beats[1::2])]
    sel_t = jnp.where((beats[0] < N_SEL) & valid, 1.0, 0.0)
    sel_q = lax.dot_general(sel_t, place_ref[...], (((0,), (0,)), ((), ())),
                            preferred_element_type=F32)
    in_flags = jnp.where((lane >= SEL_LANE0) & (lane < SEL_LANE0 + N_SLC), 1.0, 0.0)
    q_bias = (sel_q - in_flags) * (-NEG)
    q4 = (q4w.astype(F32).reshape(r_heads, tq, LANE) + q_bias[None]).astype(BF16)
    q4 = q4.reshape(r_heads * tq, LANE)

    def sel_step(c, carry, causal):
        m, acc = carry
        k0 = pl.multiple_of(c * tk, tk)
        k = ks_ref[pl.ds(k0, tk), :]
        v = vs_ref[pl.ds(k0, tk), :]
        s = _dot_nt(q4, k).reshape(r_heads, tq, tk)
        if causal:
            kpos = k0 + lax.broadcasted_iota(jnp.int32, (1, tk), 1)
            s = jnp.where((kpos <= tpos)[None], s, NEG)
        m_new = jnp.maximum(m, jnp.max(s, axis=-1, keepdims=True))
        p = jnp.exp2(s - m_new)
        pv = _dot(p.reshape(r_heads * tq, tk).astype(BF16), v).reshape(r_heads, tq, LANE)
        return m_new, jnp.exp2(m - m_new) * acc + pv

    last = (t0 + tq - 1) // tk
    carry = (jnp.full((r_heads, tq, 1), NEG, F32), jnp.zeros((r_heads, tq, LANE), F32))
    carry = lax.fori_loop(0, last, functools.partial(sel_step, causal=False), carry)
    _, acc_s = sel_step(last, carry, True)
    o = o_part + gate(1) * (acc_s / acc_s[:, :, DEN_LANE:DEN_LANE + 1])
    for r in range(0, r_heads, 2):
        o_ref[:, (r // 2) * LANE:(r // 2 + 1) * LANE] = _pack_pair(o[r], o[r + 1]).astype(BF16)


def _nsa_attn(q, qr, kcmp, vcmp, ks, vs, kw, vw, gates, ovl, place):
    b, s, _ = q.shape
    tq = NSA_TQ
    gw = NSA_R * LANE
    q_spec = pl.BlockSpec((None, tq, gw), lambda bi, g, i: (bi, i, g))
    c_spec = pl.BlockSpec((None, None, N_CMP, LANE), lambda bi, g, i: (bi, g, 0, 0))
    kv_spec = pl.BlockSpec((None, None, s, LANE), lambda bi, g, i: (bi, g, 0, 0))
    return pl.pallas_call(
        _nsa_kernel,
        grid=(b, NSA_KV, s // tq),
        in_specs=[q_spec, q_spec, c_spec, c_spec, kv_spec, kv_spec, kv_spec, kv_spec,
                  pl.BlockSpec((None, tq, LANE), lambda bi, g, i: (bi, i, g)),
                  _resident((N_SLC, N_CMP), lambda bi, g, i: (0, 0)),
                  _resident((N_SLC, LANE), lambda bi, g, i: (0, 0))],
        out_specs=pl.BlockSpec((None, tq, gw // 2), lambda bi, g, i: (bi, i, g)),
        out_shape=jax.ShapeDtypeStruct((b, s, MIX_W), BF16),
        compiler_params=_params("parallel", "parallel", "arbitrary"),
        name="nsa_attn",
    )(q, qr, kcmp, vcmp, ks, vs, kw, vw, gates, ovl, place)


def _hgrn_kernel(x_ref, w_ref, lb_ref, ng_ref, o_ref, st_ref):
    s_len = x_ref.shape[0]
    c_len, sub_len = HGRN_CHUNK, HGRN_SUB
    p_len = s_len // HGRN_PARTS
    nc = p_len // c_len
    lb = lb_ref[...]
    t = lax.broadcasted_iota(jnp.int32, (p_len, 1), 0)
    t_sub = t % sub_len
    row = lax.broadcasted_iota(jnp.int32, (nc, c_len, c_len), 1)
    col = lax.broadcasted_iota(jnp.int32, (nc, c_len, c_len), 2)
    tri = jnp.where(col <= row, 1.0, 0.0)

    local = []
    for part in range(HGRN_PARTS):
        proj = _dot(x_ref[pl.ds(part * p_len, p_len), :].astype(BF16), w_ref[...])
        q = proj[:, 0:HK]
        z = proj[:, HK:2 * HK]
        v = proj[:, 2 * HK:2 * HK + HV]
        go = proj[:, 2 * HK + HV:2 * HK + 2 * HV]

        f = lb + (1.0 - lb) * jax.nn.sigmoid(z)
        lf = jnp.log(jnp.maximum(f, F_MIN))
        k = (1.0 - lb) * jax.nn.sigmoid(-z)

        b3 = jnp.einsum('cts,csd->ctd', tri, lf.reshape(nc, c_len, HK),
                        precision=lax.Precision.HIGHEST, preferred_element_type=F32)
        b = b3.reshape(p_len, HK)
        k3 = k.reshape(nc, c_len, HK)
        vb3 = v.astype(BF16).reshape(nc, c_len, HV)

        a_intra = None
        h = c_len // 2
        while h >= sub_len:
            blk = b.reshape(p_len // (2 * h), 2 * h, HK)
            e = jnp.exp(-jnp.abs(blk[:, h - 1:h, :] - blk)).reshape(p_len, HK)
            upper = (t // h) % 2 == 1
            lq = jnp.where(upper, q * e, 0.0).astype(BF16).reshape(nc, c_len, HK)
            rk = jnp.where(upper, 0.0, k * e).astype(BF16).reshape(nc, c_len, HK)
            a = jnp.einsum('ctk,csk->cts', lq, rk, preferred_element_type=F32)
            if 2 * h < c_len:
                a = jnp.where(row // (2 * h) == col // (2 * h), a, 0.0)
            a_intra = a if a_intra is None else a_intra + a
            h //= 2
        o = jnp.einsum('cts,csd->ctd', a_intra.astype(BF16), vb3,
                       preferred_element_type=F32).reshape(p_len, HV)

        o = o + jnp.sum(q * k, axis=-1, keepdims=True) * v
        for d in range(1, sub_len):
            e = jnp.exp(jnp.where(t_sub >= d, b - pltpu.roll(b, d, axis=0), NEG))
            w = jnp.sum(q * pltpu.roll(k, d, axis=0) * e, axis=-1, keepdims=True)
            o = o + w * pltpu.roll(v, d, axis=0)

        b_last = b3[:, c_len - 1:c_len, :]
        kd3 = (k3 * jnp.exp(b_last - b3)).astype(BF16)
        upd = jnp.einsum('csv,csk->cvk', vb3, kd3, preferred_element_type=F32)
        qe3 = (q * jnp.exp(b)).astype(BF16).reshape(nc, c_len, HK)
        local.append((o, go, qe3, upd, jnp.exp(b_last)))

    state = jnp.zeros((HV, HK), F32)
    for part, (_, _, _, upd, dec) in enumerate(local):
        for c in range(nc):
            st_ref[part * nc + c] = state.astype(BF16)
            state = state * dec[c] + upd[c]

    for part, (o, go, qe3, _, _) in enumerate(local):
        o = o + jnp.einsum('ctk,cvk->ctv', qe3, st_ref[pl.ds(part * nc, nc)],
                           preferred_element_type=F32).reshape(p_len, HV)
        o = o * lax.rsqrt(jnp.mean(o * o, axis=-1, keepdims=True) + RMS_EPS) * ng_ref[...]
        o_ref[pl.ds(part * p_len, p_len), :] = (o * (go * jax.nn.sigmoid(go))).astype(BF16)


def _hgrn(x3, wb, lb, ng, l):
    b, s, d = x3.shape
    return pl.pallas_call(
        _hgrn_kernel,
        grid=(b, HB),
        in_specs=[pl.BlockSpec((None, s, d), lambda bi, h: (bi, 0, 0)),
                  pl.BlockSpec((None, None, d, 4 * HK), lambda bi, h: (l, h, 0, 0)),
                  pl.BlockSpec((None, None, 1, HK), lambda bi, h: (l, h, 0, 0)),
                  pl.BlockSpec((None, 1, HV), lambda bi, h: (l, 0, 0))],
        out_specs=pl.BlockSpec((None, s, HV), lambda bi, h: (bi, 0, h)),
        out_shape=jax.ShapeDtypeStruct((b, s, HB * HV), BF16),
        scratch_shapes=[pltpu.VMEM((s // HGRN_CHUNK, HV, HK), BF16)],
        compiler_params=_params("parallel", "arbitrary"),
        name="hgrn",
    )(x3, wb, lb, ng)


MLA_W = HC * LANE
C_CQ, C_CKV, C_KR = 0, Q_RANK, Q_RANK + KV_RANK
C_COLS = C_KR + LANE
MLAP_TM = 512


def _rms(x, g):
    return x * lax.rsqrt(jnp.mean(x * x, axis=-1, keepdims=True) + RMS_EPS) * g


def _mlap_kernel(x_ref, wc_ref, qg_ref, kvg_ref, wuq_ref, wuk_ref, wuv_ref,
                 cos_ref, slo_ref, shi_ref, q_ref, k_ref, v_ref):
    half = ROPE_D // 2
    qs = (NOPE + ROPE_D) ** -0.5 * LOG2E
    den_flag = jnp.where(lax.broadcasted_iota(jnp.int32, (1, LANE), 1) == DEN_LANE, 1.0, 0.0)
    cos, lo, hi = cos_ref[...], slo_ref[...], shi_ref[...]
    c = _dot(x_ref[...].astype(BF16), wc_ref[...])
    nq = _rms(c[:, C_CQ:C_CKV], qg_ref[...]).astype(BF16)
    nkv = _rms(c[:, C_CKV:C_KR], kvg_ref[...]).astype(BF16)
    k_pe = _rope_lanes(c[:, C_KR:C_COLS], cos, lo, hi, half)
    yq = _dot(nq, wuq_ref[...])
    yk = _dot(nkv, wuk_ref[...])
    yv = _dot(nkv, wuv_ref[...])
    cos_q, lo_q, hi_q = cos * qs, lo * qs, hi * qs
    for h in range(HC):
        hs = slice(h * LANE, (h + 1) * LANE)
        q_ref[:, hs] = _rope_lanes(yq[:, hs], cos_q, lo_q, hi_q, half).astype(BF16)
        k_ref[:, hs] = (yk[:, hs] + k_pe).astype(BF16)
        v_ref[:, hs] = (yv[:, hs] + den_flag).astype(BF16)


def _mlap(x3, wc, qg, kvg, wuq, wuk, wuv, l, cos_c, slo_c, shi_c):
    b, s, _ = x3.shape
    tm = MLAP_TM
    o_spec = pl.BlockSpec((None, tm, MLA_W), lambda bi, m: (bi, m, 0))
    o_shape = jax.ShapeDtypeStruct((b, s, MLA_W), BF16)
    t_spec = pl.BlockSpec((None, tm, LANE), lambda bi, m: (bi, m, 0))
    return pl.pallas_call(
        _mlap_kernel,
        grid=(b, s // tm),
        in_specs=[
            pl.BlockSpec((None, tm, D_MODEL), lambda bi, m: (bi, m, 0)),
            _resident((None, D_MODEL, C_COLS), lambda bi, m: (l, 0, 0)),
            _resident((None, 1, Q_RANK), lambda bi, m: (l, 0, 0)),
            _resident((None, 1, KV_RANK), lambda bi, m: (l, 0, 0)),
            _resident((None, Q_RANK, MLA_W), lambda bi, m: (l, 0, 0)),
            _resident((None, KV_RANK, MLA_W), lambda bi, m: (l, 0, 0)),
            _resident((None, KV_RANK, MLA_W), lambda bi, m: (l, 0, 0)),
            t_spec, t_spec, t_spec,
        ],
        out_specs=[o_spec, o_spec, o_spec],
        out_shape=[o_shape, o_shape, o_shape],
        compiler_params=_params("parallel", "parallel"),
        name="mla_proj",
    )(x3, wc, qg, kvg, wuq, wuk, wuv, cos_c, slo_c, shi_c)


MLA_TQ = 512
MLA_TK = 512


MLA_HPS = 4


def _mla_attn_kernel(q_ref, k_ref, v_ref, o_ref):
    tq, tk = MLA_TQ, MLA_TK
    i = pl.program_id(2)
    t0 = i * tq
    tpos = t0 + lax.broadcasted_iota(jnp.int32, (tq, 1), 0)
    heads = [slice(h * LANE, (h + 1) * LANE) for h in range(MLA_HPS)]
    qs = [q_ref[:, hs] for hs in heads]

    def step(c, carry, causal):
        k0 = pl.multiple_of(c * tk, tk)
        if causal:
            keep = k0 + lax.broadcasted_iota(jnp.int32, (1, tk), 1) <= tpos
        scores = [_dot_nt(qs[h], k_ref[pl.ds(k0, tk), hs]) for h, hs in enumerate(heads)]
        m_new, probs = [], []
        for h in range(len(heads)):
            s = jnp.where(keep, scores[h], NEG) if causal else scores[h]
            m_new.append(jnp.maximum(carry[h][0], jnp.max(s, axis=-1, keepdims=True)))
            probs.append(jnp.exp2(s - m_new[h]).astype(BF16))
        pvs = [_dot(probs[h], v_ref[pl.ds(k0, tk), hs]) for h, hs in enumerate(heads)]
        return tuple((m_new[h], jnp.exp2(carry[h][0] - m_new[h]) * carry[h][1] + pvs[h])
                     for h in range(len(heads)))

    last = (t0 + tq - 1) // tk
    carry = tuple((jnp.full((tq, 1), NEG, F32), jnp.zeros((tq, LANE), F32)) for _ in heads)
    carry = lax.fori_loop(0, last, functools.partial(step, causal=False), carry)
    carry = step(last, carry, True)
    outs = [acc / acc[:, DEN_LANE:DEN_LANE + 1] for _, acc in carry]
    for h in range(0, MLA_HPS, 2):
        o_ref[:, (h // 2) * LANE:(h // 2 + 1) * LANE] = _pack_pair(outs[h], outs[h + 1]).astype(BF16)


def _mla_attn(q, k, v):
    b, s, _ = q.shape
    tq = MLA_TQ
    gw = MLA_HPS * LANE
    return pl.pallas_call(
        _mla_attn_kernel,
        grid=(b, HC // MLA_HPS, s // tq),
        in_specs=[pl.BlockSpec((None, tq, gw), lambda bi, h, i: (bi, i, h)),
                  pl.BlockSpec((None, s, gw), lambda bi, h, i: (bi, 0, h)),
                  pl.BlockSpec((None, s, gw), lambda bi, h, i: (bi, 0, h))],
        out_specs=pl.BlockSpec((None, tq, gw // 2), lambda bi, h, i: (bi, i, h)),
        out_shape=jax.ShapeDtypeStruct((b, s, MIX_W), BF16),
        compiler_params=_params("parallel", "parallel", "arbitrary"),
        name="mla_attn",
    )(q, k, v)


MERGE_TM = 512


def _merge_kernel(x_ref, ya_ref, yb_ref, yc_ref, wm_ref, wa_ref, wb_ref, wc_ref, wo_ref,
                  g_ref, b_ref, o_ref, *, alpha):
    x = x_ref[...]
    xb = x.astype(BF16)
    mixed = jnp.zeros(x.shape, F32)
    for idx, (y_ref, w_ref) in enumerate(((ya_ref, wa_ref), (yb_ref, wb_ref), (yc_ref, wc_ref))):
        gate = jax.nn.sigmoid(_dot(xb, wm_ref[:, idx * D_MODEL:(idx + 1) * D_MODEL]))
        mixed = mixed + gate * _dot(y_ref[...], w_ref[...])
    y = alpha * x + _dot(mixed.astype(BF16), wo_ref[...])
    o_ref[...] = _ln(y, g_ref[...], b_ref[...])


def _merge(x2, ya, yb, yc, wm, wa, wb, wc, wo, lng, lnb, l, alpha):
    n = x2.shape[0]
    tm = min(MERGE_TM, n)

    def rows(width):
        return pl.BlockSpec((tm, width), lambda m: (m, 0))

    return pl.pallas_call(
        functools.partial(_merge_kernel, alpha=alpha),
        grid=(n // tm,),
        in_specs=[rows(D_MODEL), rows(MIX_W), rows(MIX_W), rows(MIX_W),
                  _resident((None, D_MODEL, 3 * D_MODEL), lambda m: (l, 0, 0)),
                  _resident((None, None, MIX_W, D_MODEL), lambda m: (l, 0, 0, 0)),
                  _resident((None, None, MIX_W, D_MODEL), lambda m: (l, 1, 0, 0)),
                  _resident((None, None, MIX_W, D_MODEL), lambda m: (l, 2, 0, 0)),
                  _resident((None, D_MODEL, D_MODEL), lambda m: (l, 0, 0)),
                  _resident((None, None, 1, D_MODEL), lambda m: (l, 1, 0, 0)),
                  _resident((None, None, 1, D_MODEL), lambda m: (l, 1, 0, 0))],
        out_specs=rows(D_MODEL),
        out_shape=jax.ShapeDtypeStruct((n, D_MODEL), F32),
        compiler_params=_params("parallel"),
        name="merge",
    )(x2, ya, yb, yc, wm, wa, wb, wc, wo, lng, lnb)


XA_TM = 1024


def _xattn_kernel(x_ref, wq_ref, k_ref, v_ref, wo_ref, g_ref, b_ref, o_ref, *, alpha):
    x = x_ref[...]
    xb = x.astype(BF16)
    q = (_dot(xb, wq_ref[...]) * (XA_DH ** -0.5 * LOG2E)).astype(BF16)
    ones = jnp.ones((k_ref.shape[0], LANE), BF16)
    heads = []
    for h in range(XA_HEADS):
        hs = slice(h * XA_DH, (h + 1) * XA_DH)
        s = _dot_nt(q[:, hs], k_ref[:, hs])
        e = jnp.exp2(s - jnp.max(s, axis=-1, keepdims=True)).astype(BF16)
        den = _dot(e, ones)
        den = jnp.concatenate([den] * (XA_DH // LANE), axis=1)
        heads.append((_dot(e, v_ref[:, hs]) / den).astype(BF16))
    att = jnp.concatenate(heads, axis=1)
    y = alpha * x + _dot(att, wo_ref[...])
    o_ref[...] = _ln(y, g_ref[...], b_ref[...])


def _xattn(x3, wq, k, v, wo, lng, lnb, l, alpha):
    b, s, _ = x3.shape
    m_len = k.shape[1]
    tm = XA_TM
    return pl.pallas_call(
        functools.partial(_xattn_kernel, alpha=alpha),
        grid=(b, s // tm),
        in_specs=[pl.BlockSpec((None, tm, D_MODEL), lambda bi, m: (bi, m, 0)),
                  _resident((None, D_MODEL, D_MODEL), lambda bi, m: (l, 0, 0)),
                  pl.BlockSpec((None, m_len, D_MODEL), lambda bi, m: (bi, 0, 0)),
                  pl.BlockSpec((None, m_len, D_MODEL), lambda bi, m: (bi, 0, 0)),
                  _resident((None, D_MODEL, D_MODEL), lambda bi, m: (l, 0, 0)),
                  _resident((None, None, 1, D_MODEL), lambda bi, m: (l, 2, 0, 0)),
                  _resident((None, None, 1, D_MODEL), lambda bi, m: (l, 2, 0, 0))],
        out_specs=pl.BlockSpec((None, tm, D_MODEL), lambda bi, m: (bi, m, 0)),
        out_shape=jax.ShapeDtypeStruct((b, s, D_MODEL), F32),
        compiler_params=_params("parallel", "parallel"),
        name="xattn",
    )(x3, wq, k, v, wo, lng, lnb)


def _pad_heads(w, n_heads, dh):
    depth, k, _ = w.shape
    w = w.reshape(depth, k, n_heads, dh)
    return jnp.pad(w, ((0, 0), (0, 0), (0, 0), (0, LANE - dh))).reshape(depth, k, n_heads * LANE)


def _rope_tables(positions, rot, off):
    half = rot // 2
    inv = ROPE_THETA ** (-jnp.arange(half, dtype=F32) / half)
    ang = positions.astype(F32)[..., None] * inv
    cos, sin = jnp.cos(ang), jnp.sin(ang)
    shape = positions.shape
    zeros = lambda n: jnp.zeros(shape + (n,), F32)
    cos_t = jnp.concatenate([jnp.ones(shape + (off,), F32), cos, cos,
                             jnp.ones(shape + (LANE - off - rot,), F32)], axis=-1)
    sin_lo = jnp.concatenate([zeros(off), -sin, zeros(LANE - off - half)], axis=-1)
    sin_hi = jnp.concatenate([zeros(off + half), sin, zeros(LANE - off - rot)], axis=-1)
    return cos_t, sin_lo, sin_hi


def _overlap_matrix(s_len):
    n_cmp = (s_len - CMP_L) // CMP_D + 1
    n_slc = s_len // SEL_L
    start = np.arange(n_cmp) * CMP_D
    j = np.arange(n_slc)
    ov = np.clip(np.minimum(start[:, None] + CMP_L, (j[None, :] + 1) * SEL_L)
                 - np.maximum(start[:, None], j[None, :] * SEL_L), 0, None) / CMP_L
    out = np.zeros((N_SLC, N_CMP), np.float32)
    out[:n_slc, :n_cmp] = ov.T
    place = np.zeros((N_SLC, LANE), np.float32)
    place[np.arange(N_SLC), SEL_LANE0 + np.arange(N_SLC)] = 1.0
    return jnp.asarray(out), jnp.asarray(place)


def _stacked_weights(w_in, nsa_cmp_pos, nsa_cmp_w1, nsa_cmp_w2, mla_w_uq, mla_w_ukv):
    depth = w_in.shape[0]
    part = [w_in[:, :, IN_OFF[i]:IN_OFF[i + 1]] for i in range(len(IN_SIZES))]
    (a_q, a_kc, a_vc, a_ks, a_vs, a_kw, a_vw, a_gate, b_q, b_f, b_i, b_g,
     c_q, c_kv, c_kr, merge) = part
    gate = _pad_heads(a_gate, NSA_KV, NSA_R * 3)
    kv_parts = dict(kc=a_kc, vc=a_vc, ks=a_ks, vs=a_vs, kw=a_kw, vw=a_vw)
    wa = jnp.concatenate([_pad_heads(a_q, NSA_HEADS, NSA_DH)]
                         + [kv_parts[name] for name in A_KV_ORDER] + [gate], axis=2).astype(BF16)
    wb = jnp.stack([w.reshape(depth, D_MODEL, HB, HK) for w in (b_q, b_f, b_i, b_g)], axis=3)
    wb = wb.transpose(0, 2, 1, 3, 4).reshape(depth, HB, D_MODEL, 4 * HK).astype(BF16)

    kr_pad = jnp.pad(c_kr, ((0, 0), (0, 0), (NOPE, LANE - NOPE - ROPE_D)))
    wc = jnp.concatenate([c_q, c_kv, kr_pad], axis=2).astype(BF16)
    wuq = _pad_heads(mla_w_uq, HC, NOPE + ROPE_D).astype(BF16)
    ukv = mla_w_ukv.reshape(depth, KV_RANK, HC, NOPE + VD)
    wuk = _pad_heads(ukv[..., :NOPE].reshape(depth, KV_RANK, HC * NOPE), HC, NOPE).astype(BF16)
    wuv = _pad_heads(ukv[..., NOPE:].reshape(depth, KV_RANK, HC * VD), HC, VD).astype(BF16)

    w1 = nsa_cmp_w1.reshape(depth, 2, CMP_L, NSA_DH, CMP_HID)
    w1p = jnp.pad(w1, ((0, 0), (0, 0), (0, 0), (0, LANE - NSA_DH), (0, 0)))
    w1p = w1p.reshape(depth, 2, CMP_L * LANE, CMP_HID)
    pe = jnp.pad(nsa_cmp_pos, ((0, 0), (0, 0), (0, 0), (0, LANE - NSA_DH)))
    pe8 = jnp.broadcast_to(pe.reshape(depth, 2, 1, CMP_L * LANE), (depth, 2, 8, CMP_L * LANE))
    w2p = jnp.pad(nsa_cmp_w2, ((0, 0), (0, 0), (0, 0), (0, LANE - NSA_DH)))
    return dict(wa=wa, wb=wb, wc=wc, wuq=wuq, wuk=wuk, wuv=wuv,
                w1p=w1p.astype(BF16), pe8=pe8.astype(BF16), w2p=w2p.astype(BF16),
                wm=merge.astype(BF16))


def kernel(x, mem, positions, ln_g, ln_b, ffn_w1, ffn_w3, ffn_w2, w_in, nsa_cmp_pos,
           nsa_cmp_w1, nsa_cmp_w2, hgrn_lb_logits, hgrn_norm_g, mla_q_norm_g, mla_w_uq,
           mla_kv_norm_g, mla_w_ukv, w_branch, w_out, xa_wq, xa_wk, xa_wv, xa_wo):
    b, s, d = x.shape
    depth = ln_g.shape[0]
    n = b * s
    alpha = (2.0 * depth) ** 0.25

    lng = ln_g.reshape(depth, 4, 1, d)
    lnb = ln_b.reshape(depth, 4, 1, d)
    w1 = ffn_w1.astype(BF16)
    w3 = ffn_w3.astype(BF16)
    w2 = (0.5 * ffn_w2).astype(BF16)
    wo = w_out.astype(BF16)
    wbr = w_branch.astype(BF16)
    xq = xa_wq.astype(BF16)
    xk = xa_wk.astype(BF16)
    xv = xa_wv.astype(BF16)
    xo = xa_wo.astype(BF16)
    p_lb = jax.nn.softmax(hgrn_lb_logits.astype(F32), axis=0)
    lower = (jnp.cumsum(p_lb, axis=0) - p_lb[0:1]).reshape(depth, HB, 1, HK)

    rope_a = _rope_tables(positions, NSA_ROT, 0)
    rope_c = _rope_tables(positions, ROPE_D, NOPE)
    ovl, place = _overlap_matrix(s)
    mem2 = mem.reshape(b * mem.shape[1], d)
    w = _stacked_weights(w_in, nsa_cmp_pos, nsa_cmp_w1, nsa_cmp_w2, mla_w_uq, mla_w_ukv)
    norm_b = hgrn_norm_g.reshape(depth, 1, HV)
    norm_q = mla_q_norm_g.reshape(depth, 1, Q_RANK)
    norm_kv = mla_kv_norm_g.reshape(depth, 1, KV_RANK)

    x2 = x.reshape(n, d)
    for l in range(depth):
        x2 = _ffn_ln(x2, w1, w3, w2, lng, lnb, l, 0, 0, alpha)
        x3 = x2.reshape(b, s, d)

        q, qr, kc, vc, ks, vs, kw, vw, gates = _proja(x3, w["wa"], l, *rope_a)
        kcmp, vcmp = _cmp(kc, vc, w["w1p"], w["pe8"], w["w2p"], l)
        ya = _nsa_attn(q, qr, kcmp, vcmp, ks, vs, kw, vw, gates, ovl, place)

        yb = _hgrn(x3, w["wb"], lower, norm_b, l)

        mq, mk, mv = _mlap(x3, w["wc"], norm_q, norm_kv, w["wuq"], w["wuk"], w["wuv"], l, *rope_c)
        yc = _mla_attn(mq, mk, mv)

        x2 = _merge(x2, ya.reshape(n, MIX_W), yb.reshape(n, MIX_W), yc.reshape(n, MIX_W),
                    w["wm"], wbr, wbr, wbr, wo, lng, lnb, l, alpha)

        xk_l = _proj(mem2, xk, l, BF16).reshape(b, -1, d)
        xv_l = _proj(mem2, xv, l, BF16).reshape(b, -1, d)
        x2 = _xattn(x2.reshape(b, s, d), xq, xk_l, xv_l, xo, lng, lnb, l, alpha).reshape(n, d)

        x2 = _ffn_ln(x2, w1, w3, w2, lng, lnb, l, 1, 3, alpha)
    return x2.reshape(b, s, d)
```

```python
import functools
import math

import numpy as np
import jax
import jax.numpy as jnp
from jax import lax
from jax.experimental import pallas as pl
from jax.experimental.pallas import tpu as pltpu

F32 = jnp.float32
BF16 = jnp.bfloat16

D_MODEL = 1024
MIX_W = D_MODEL // 2
NSA_DH = 64
NSA_HEADS = 8
NSA_KV = 2
NSA_R = 4
NSA_ROT = 16
CMP_L = 32
CMP_D = 16
CMP_HID = 256
SEL_L = 64
N_SEL = 8
WINDOW = 256
HB = 4
HK = 128
HV = 128
HGRN_CHUNK = 64
HGRN_SUB = 4
HGRN_PARTS = 4
HC = 8
NOPE = 64
ROPE_D = 32
VD = 64
Q_RANK = 384
KV_RANK = 256
XA_HEADS = 4
XA_DH = D_MODEL // XA_HEADS
D_FF = 2816
ROPE_THETA = 500000.0
LN_EPS = 1e-5
RMS_EPS = 1e-6
NEG = -1e30
BIG = 1e9
F_MIN = 1e-20

LOG2E = 1.4426950408889634
LANE = 128
SEL_LANE0 = NSA_DH
DEN_LANE = 64
VMEM_LIMIT = 56 * 1024 * 1024

IN_SIZES = (512, 128, 128, 128, 128, 128, 128, 24, 512, 512, 512, 512, 384, 256, 32, 3072)
IN_OFF = tuple(int(v) for v in np.concatenate([[0], np.cumsum(IN_SIZES)]))


def _dot(a, b):
    return jnp.dot(a, b, preferred_element_type=F32)


def _dot_nt(a, b):
    return lax.dot_general(a, b, (((1,), (1,)), ((), ())), preferred_element_type=F32)


def _ln(y, g, b):
    mu = jnp.mean(y, axis=-1, keepdims=True)
    yc = y - mu
    var = jnp.mean(yc * yc, axis=-1, keepdims=True)
    return yc * lax.rsqrt(var + LN_EPS) * g + b


def _pack_pair(a, b):
    lane = lax.broadcasted_iota(jnp.int32, a.shape, a.ndim - 1)
    return jnp.where(lane < LANE // 2, a, pltpu.roll(b, LANE // 2, axis=a.ndim - 1))


def _params(*sem):
    return pltpu.CompilerParams(dimension_semantics=sem, vmem_limit_bytes=VMEM_LIMIT)


def _resident(shape, index_map):
    return pl.BlockSpec(shape, index_map, pipeline_mode=pl.Buffered(1))


FFN_TM = 1024
FFN_TF = 256


def _ffn_kernel(x_ref, w1_ref, w3_ref, w2h_ref, g_ref, b_ref, o_ref, acc_ref, xb_ref, *, alpha):
    j = pl.program_id(1)

    @pl.when(j == 0)
    def _():
        xb_ref[...] = x_ref[...].astype(BF16)
        acc_ref[...] = alpha * x_ref[...]

    xb = xb_ref[...]
    h1 = _dot(xb, w1_ref[...])
    h3 = _dot(xb, w3_ref[...])
    h = (h1 * jax.nn.sigmoid(h1)) * h3
    acc_ref[...] += _dot(h.astype(BF16), w2h_ref[...])

    @pl.when(j == pl.num_programs(1) - 1)
    def _():
        o_ref[...] = _ln(acc_ref[...], g_ref[...], b_ref[...])


def _ffn_ln(x2, w1, w3, w2h, lng, lnb, l, which, ln_idx, alpha):
    n = x2.shape[0]
    tm = min(FFN_TM, n)
    grid = (n // tm, D_FF // FFN_TF)
    return pl.pallas_call(
        functools.partial(_ffn_kernel, alpha=alpha),
        grid=grid,
        in_specs=[
            pl.BlockSpec((tm, D_MODEL), lambda m, j: (m, 0)),
            pl.BlockSpec((None, None, D_MODEL, FFN_TF), lambda m, j: (l, which, 0, j)),
            pl.BlockSpec((None, None, D_MODEL, FFN_TF), lambda m, j: (l, which, 0, j)),
            pl.BlockSpec((None, None, FFN_TF, D_MODEL), lambda m, j: (l, which, j, 0)),
            pl.BlockSpec((None, None, 1, D_MODEL), lambda m, j: (l, ln_idx, 0, 0)),
            pl.BlockSpec((None, None, 1, D_MODEL), lambda m, j: (l, ln_idx, 0, 0)),
        ],
        out_specs=pl.BlockSpec((tm, D_MODEL), lambda m, j: (m, 0)),
        out_shape=jax.ShapeDtypeStruct((n, D_MODEL), F32),
        scratch_shapes=[pltpu.VMEM((tm, D_MODEL), F32), pltpu.VMEM((tm, D_MODEL), BF16)],
        compiler_params=_params("parallel", "arbitrary"),
        name="ffn_ln",
    )(x2, w1, w3, w2h, lng, lnb)


def _proj_kernel(x_ref, w_ref, o_ref):
    o_ref[...] = _dot(x_ref[...].astype(BF16), w_ref[...]).astype(o_ref.dtype)


def _proj(x2, w, l, out_dtype, tm=1024, tn=2048):
    n, k = x2.shape
    c = w.shape[2]
    tm = min(tm, n)
    tn = min(tn, c)
    return pl.pallas_call(
        _proj_kernel,
        grid=(n // tm, c // tn),
        in_specs=[pl.BlockSpec((tm, k), lambda m, j: (m, 0)),
                  pl.BlockSpec((None, k, tn), lambda m, j: (l, 0, j))],
        out_specs=pl.BlockSpec((tm, tn), lambda m, j: (m, j)),
        out_shape=jax.ShapeDtypeStruct((n, c), out_dtype),
        compiler_params=_params("parallel", "arbitrary"),
        name="proj",
    )(x2, w)


NSA_QW = NSA_HEADS * LANE
NSA_KW = NSA_KV * LANE
A_Q = 0
A_KV = NSA_QW
A_KV_ORDER = ("kc", "vc", "ks", "vs", "kw", "vw")
A_GATE = A_KV + len(A_KV_ORDER) * LANE
A_COLS = A_GATE + NSA_KW
PROJA_TM = 1024


def _rope_lanes(y, cos, sin_lo, sin_hi, half):
    return (y * cos + pltpu.roll(y, LANE - half, axis=1) * sin_lo
            + pltpu.roll(y, half, axis=1) * sin_hi)


def _proja_kernel(x_ref, w_ref, cos_ref, slo_ref, shi_ref,
                  q_ref, qr_ref, kc_ref, vc_ref, ks_ref, vs_ref, kw_ref, vw_ref, g_ref):
    xb = x_ref[...].astype(BF16)
    cos = cos_ref[...]
    sin_lo = slo_ref[...]
    sin_hi = shi_ref[...]
    half = NSA_ROT // 2
    tm = x_ref.shape[0]

    def mm(c0, width):
        return _dot(xb, w_ref[:, c0:c0 + width])

    qs = NSA_DH ** -0.5 * LOG2E
    y = mm(A_Q, NSA_QW)
    q_ref[...] = (y * qs).astype(BF16)
    for h in range(NSA_HEADS):
        hs = slice(h * LANE, (h + 1) * LANE)
        qr_ref[:, hs] = (_rope_lanes(y[:, hs], cos, sin_lo, sin_hi, half) * qs).astype(BF16)

    lane = lax.broadcasted_iota(jnp.int32, (tm, LANE), 1)
    first = lane < NSA_DH
    tok = pl.program_id(1) * tm + lax.broadcasted_iota(jnp.int32, (tm, LANE), 0)
    blk_flag = jnp.where(lane - SEL_LANE0 == tok // SEL_L, 1.0, 0.0)
    den_flag = jnp.where(lane == DEN_LANE, 1.0, 0.0)

    def packed(table):
        return jnp.where(first, table, pltpu.roll(table, NSA_DH, axis=1))

    def put(ref, y, flag=None):
        for g, yg in enumerate((y, pltpu.roll(y, NSA_DH, axis=1))):
            yg = jnp.where(first, yg, 0.0)
            ref[g] = (yg if flag is None else yg + flag).astype(ref.dtype)

    kv = mm(A_KV, len(A_KV_ORDER) * LANE)
    part = {name: kv[:, i * LANE:(i + 1) * LANE] for i, name in enumerate(A_KV_ORDER)}
    cos_p, lo_p, hi_p = packed(cos), packed(sin_lo), packed(sin_hi)
    put(kc_ref, part["kc"])
    put(vc_ref, part["vc"])
    put(ks_ref, _rope_lanes(part["ks"], cos_p, lo_p, hi_p, half), blk_flag)
    put(vs_ref, part["vs"], den_flag)
    put(kw_ref, _rope_lanes(part["kw"], cos_p, lo_p, hi_p, half))
    put(vw_ref, part["vw"], den_flag)
    g_ref[...] = mm(A_GATE, NSA_KW)


def _proja(x3, wa, l, cos_a, slo_a, shi_a):
    b, s, _ = x3.shape
    tm = PROJA_TM
    kv_shape = jax.ShapeDtypeStruct((b, NSA_KV, s, LANE), BF16)
    cmp_shape = jax.ShapeDtypeStruct((b, NSA_KV, s, LANE), F32)
    kv_spec = pl.BlockSpec((None, NSA_KV, tm, LANE), lambda bi, m: (bi, 0, m, 0))
    return pl.pallas_call(
        _proja_kernel,
        grid=(b, s // tm),
        in_specs=[
            pl.BlockSpec((None, tm, D_MODEL), lambda bi, m: (bi, m, 0)),
            _resident((None, D_MODEL, A_COLS), lambda bi, m: (l, 0, 0)),
            pl.BlockSpec((None, tm, LANE), lambda bi, m: (bi, m, 0)),
            pl.BlockSpec((None, tm, LANE), lambda bi, m: (bi, m, 0)),
            pl.BlockSpec((None, tm, LANE), lambda bi, m: (bi, m, 0)),
        ],
        out_specs=[
            pl.BlockSpec((None, tm, NSA_QW), lambda bi, m: (bi, m, 0)),
            pl.BlockSpec((None, tm, NSA_QW), lambda bi, m: (bi, m, 0)),
            kv_spec, kv_spec, kv_spec, kv_spec, kv_spec, kv_spec,
            pl.BlockSpec((None, tm, NSA_KW), lambda bi, m: (bi, m, 0)),
        ],
        out_shape=[
            jax.ShapeDtypeStruct((b, s, NSA_QW), BF16),
            jax.ShapeDtypeStruct((b, s, NSA_QW), BF16),
            cmp_shape, cmp_shape, kv_shape, kv_shape, kv_shape, kv_shape,
            jax.ShapeDtypeStruct((b, s, NSA_KW), F32),
        ],
        compiler_params=_params("parallel", "parallel"),
        name="nsa_proj",
    )(x3, wa, cos_a, slo_a, shi_a)


N_CMP = 128
CMP_HALF = CMP_D * LANE


def _gelu_tanh(x):
    return 0.5 * x * (1.0 + jnp.tanh(0.7978845608028654 * (x + 0.044715 * x * x * x)))


def _cmp_kernel(zk_ref, zv_ref, w1_ref, pe_ref, w2_ref, ok_ref, ov_ref):
    for which, (z_ref, o_ref) in enumerate(((zk_ref, ok_ref), (zv_ref, ov_ref))):
        first = jnp.zeros((N_CMP, CMP_HID), F32)
        second = jnp.zeros((N_CMP, CMP_HID), F32)
        for j in range(CMP_D):
            zj = z_ref[pl.ds(j, N_CMP, stride=CMP_D), :].astype(BF16)
            first = first + _dot(zj, w1_ref[which, j * LANE:(j + 1) * LANE, :])
            second = second + _dot(zj, w1_ref[which, CMP_HALF + j * LANE:CMP_HALF + (j + 1) * LANE, :])
        bias = _dot(pe_ref[which], w1_ref[which])[0:1]
        pre = first + pltpu.roll(second, N_CMP - 1, axis=0) + bias
        h = _gelu_tanh(pre)
        o_ref[...] = _dot(h.astype(BF16), w2_ref[which]).astype(BF16)


def _cmp(zk, zv, w1p, pe8, w2p, l):
    b, _, s, _ = zk.shape
    z_spec = pl.BlockSpec((None, None, s, LANE), lambda bi, g: (bi, g, 0, 0))
    o_spec = pl.BlockSpec((None, None, N_CMP, LANE), lambda bi, g: (bi, g, 0, 0))
    o_shape = jax.ShapeDtypeStruct((b, NSA_KV, N_CMP, LANE), BF16)
    return pl.pallas_call(
        _cmp_kernel,
        grid=(b, NSA_KV),
        in_specs=[
            z_spec, z_spec,
            _resident((None, 2, 2 * CMP_HALF, CMP_HID), lambda bi, g: (l, 0, 0, 0)),
            _resident((None, 2, 8, 2 * CMP_HALF), lambda bi, g: (l, 0, 0, 0)),
            _resident((None, 2, CMP_HID, LANE), lambda bi, g: (l, 0, 0, 0)),
        ],
        out_specs=[o_spec, o_spec],
        out_shape=[o_shape, o_shape],
        compiler_params=_params("parallel", "parallel"),
        name="nsa_cmp",
    )(zk, zv, w1p, pe8, w2p)


NSA_TQ = 256
NSA_TK = 512
N_SLC = 32
WIN_SLAB = WINDOW + NSA_TQ


def _nsa_kernel(q_ref, qr_ref, kc_ref, vc_ref, ks_ref, vs_ref, kw_ref, vw_ref, gt_ref, ovl_ref,
                place_ref, o_ref):
    tq, tk, r_heads = NSA_TQ, NSA_TK, NSA_R
    i = pl.program_id(2)
    t0 = i * tq
    tpos = t0 + lax.broadcasted_iota(jnp.int32, (tq, 1), 0)
    lane = lax.broadcasted_iota(jnp.int32, (1, LANE), 1)

    def stack(ref):
        return jnp.concatenate([ref[:, r * LANE:(r + 1) * LANE] for r in range(r_heads)], axis=0)

    gt = jax.nn.sigmoid(gt_ref[...])

    def gate(c):
        return jnp.stack([gt[:, 3 * r + c:3 * r + c + 1] for r in range(r_heads)], axis=0)

    q4w = stack(qr_ref)
    w0 = pl.multiple_of(jnp.maximum(t0 - WINDOW, 0), LANE)
    kw = kw_ref[pl.ds(w0, WIN_SLAB), :]
    vw = vw_ref[pl.ds(w0, WIN_SLAB), :]
    kpos = w0 + lax.broadcasted_iota(jnp.int32, (1, WIN_SLAB), 1)
    w_bias = jnp.where((kpos <= tpos) & (kpos > tpos - WINDOW), 0.0, NEG)
    s = _dot_nt(q4w, kw).reshape(r_heads, tq, WIN_SLAB) + w_bias[None]
    p = jnp.exp2(s - jnp.max(s, axis=-1, keepdims=True))
    o_win = _dot(p.reshape(r_heads * tq, WIN_SLAB).astype(BF16), vw).reshape(r_heads, tq, LANE)
    o_part = gate(2) * (o_win / o_win[:, :, DEN_LANE:DEN_LANE + 1])

    cmask = ((lane * CMP_D + (CMP_L - 1) <= tpos) & (lane < N_CMP - 1))[None]
    s = _dot_nt(stack(q_ref), kc_ref[...]).reshape(r_heads, tq, N_CMP)
    s = jnp.where(cmask, s, NEG)
    e = jnp.where(cmask, jnp.exp2(s - jnp.max(s, axis=-1, keepdims=True)), 0.0)
    den = jnp.sum(e, axis=-1, keepdims=True)
    p = e / jnp.where(den > 0.0, den, 1.0)
    psum = jnp.sum(p, axis=0)
    o_cmp = _dot(p.reshape(r_heads * tq, N_CMP).astype(BF16), vc_ref[...])
    o_part = o_part + gate(0) * o_cmp.reshape(r_heads, tq, LANE)

    imp = lax.dot_general(ovl_ref[...], psum, (((1,), (1,)), ((), ())),
                          precision=lax.Precision.HIGHEST, preferred_element_type=F32)
    blk = lax.broadcasted_iota(jnp.int32, (N_SLC, 1), 0)
    tpos_l = t0 + lax.broadcasted_iota(jnp.int32, (1, tq), 1)
    cur = tpos_l // SEL_L
    valid = blk * SEL_L <= tpos_l
    forced = (blk == 0) | (blk == cur) | (blk == cur - 1)
    score = jnp.where(valid & forced, BIG, jnp.where(valid, imp, -BIG))
    beats = []
    for j in range(N_SLC):
        sj = score[j:j + 1, :]
        beats.append(jnp.where((sj > score) | ((sj == score) & (j < blk)), 1.0, 0.0))
    while len(beats) > 1:
        beats = [a + b for a, b in zip(beats[0::2], beats[1::2])]
    sel_t = jnp.where((beats[0] < N_SEL) & valid, 1.0, 0.0)
    sel_q = lax.dot_general(sel_t, place_ref[...], (((0,), (0,)), ((), ())),
                            preferred_element_type=F32)
    in_flags = jnp.where((lane >= SEL_LANE0) & (lane < SEL_LANE0 + N_SLC), 1.0, 0.0)
    q_bias = (sel_q - in_flags) * (-NEG)
    q4 = (q4w.astype(F32).reshape(r_heads, tq, LANE) + q_bias[None]).astype(BF16)
    q4 = q4.reshape(r_heads * tq, LANE)

    def sel_step(c, carry, causal):
        m, acc = carry
        k0 = pl.multiple_of(c * tk, tk)
        k = ks_ref[pl.ds(k0, tk), :]
        v = vs_ref[pl.ds(k0, tk), :]
        s = _dot_nt(q4, k).reshape(r_heads, tq, tk)
        if causal:
            kpos = k0 + lax.broadcasted_iota(jnp.int32, (1, tk), 1)
            s = jnp.where((kpos <= tpos)[None], s, NEG)
        m_new = jnp.maximum(m, jnp.max(s, axis=-1, keepdims=True))
        p = jnp.exp2(s - m_new)
        pv = _dot(p.reshape(r_heads * tq, tk).astype(BF16), v).reshape(r_heads, tq, LANE)
        return m_new, jnp.exp2(m - m_new) * acc + pv

    last = (t0 + tq - 1) // tk
    carry = (jnp.full((r_heads, tq, 1), NEG, F32), jnp.zeros((r_heads, tq, LANE), F32))
    carry = lax.fori_loop(0, last, functools.partial(sel_step, causal=False), carry)
    _, acc_s = sel_step(last, carry, True)
    o = o_part + gate(1) * (acc_s / acc_s[:, :, DEN_LANE:DEN_LANE + 1])
    for r in range(0, r_heads, 2):
        o_ref[:, (r // 2) * LANE:(r // 2 + 1) * LANE] = _pack_pair(o[r], o[r + 1]).astype(BF16)


def _nsa_attn(q, qr, kcmp, vcmp, ks, vs, kw, vw, gates, ovl, place):
    b, s, _ = q.shape
    tq = NSA_TQ
    gw = NSA_R * LANE
    q_spec = pl.BlockSpec((None, tq, gw), lambda bi, g, i: (bi, i, g))
    c_spec = pl.BlockSpec((None, None, N_CMP, LANE), lambda bi, g, i: (bi, g, 0, 0))
    kv_spec = pl.BlockSpec((None, None, s, LANE), lambda bi, g, i: (bi, g, 0, 0))
    return pl.pallas_call(
        _nsa_kernel,
        grid=(b, NSA_KV, s // tq),
        in_specs=[q_spec, q_spec, c_spec, c_spec, kv_spec, kv_spec, kv_spec, kv_spec,
                  pl.BlockSpec((None, tq, LANE), lambda bi, g, i: (bi, i, g)),
                  _resident((N_SLC, N_CMP), lambda bi, g, i: (0, 0)),
                  _resident((N_SLC, LANE), lambda bi, g, i: (0, 0))],
        out_specs=pl.BlockSpec((None, tq, gw // 2), lambda bi, g, i: (bi, i, g)),
        out_shape=jax.ShapeDtypeStruct((b, s, MIX_W), BF16),
        compiler_params=_params("parallel", "parallel", "arbitrary"),
        name="nsa_attn",
    )(q, qr, kcmp, vcmp, ks, vs, kw, vw, gates, ovl, place)


def _hgrn_kernel(x_ref, w_ref, lb_ref, ng_ref, o_ref, st_ref):
    s_len = x_ref.shape[0]
    c_len, sub_len = HGRN_CHUNK, HGRN_SUB
    p_len = s_len // HGRN_PARTS
    nc = p_len // c_len
    lb = lb_ref[...]
    t = lax.broadcasted_iota(jnp.int32, (p_len, 1), 0)
    t_sub = t % sub_len
    row = lax.broadcasted_iota(jnp.int32, (nc, c_len, c_len), 1)
    col = lax.broadcasted_iota(jnp.int32, (nc, c_len, c_len), 2)
    tri = jnp.where(col <= row, 1.0, 0.0)

    local = []
    for part in range(HGRN_PARTS):
        proj = _dot(x_ref[pl.ds(part * p_len, p_len), :].astype(BF16), w_ref[...])
        q = proj[:, 0:HK]
        z = proj[:, HK:2 * HK]
        v = proj[:, 2 * HK:2 * HK + HV]
        go = proj[:, 2 * HK + HV:2 * HK + 2 * HV]

        sig = jax.nn.sigmoid(z)
        f = lb + (1.0 - lb) * sig
        lf = jnp.log(jnp.maximum(f, F_MIN))
        k = (1.0 - lb) * (1.0 - sig)

        b3 = jnp.einsum('cts,csd->ctd', tri, lf.reshape(nc, c_len, HK),
                        precision=lax.Precision.HIGHEST, preferred_element_type=F32)
        b = b3.reshape(p_len, HK)
        k3 = k.reshape(nc, c_len, HK)
        vb3 = v.astype(BF16).reshape(nc, c_len, HV)

        a_intra = None
        h = c_len // 2
        while h >= sub_len:
            blk = b.reshape(p_len // (2 * h), 2 * h, HK)
            e = jnp.exp(-jnp.abs(blk[:, h - 1:h, :] - blk)).reshape(p_len, HK)
            upper = (t // h) % 2 == 1
            lq = jnp.where(upper, q * e, 0.0).astype(BF16).reshape(nc, c_len, HK)
            rk = jnp.where(upper, 0.0, k * e).astype(BF16).reshape(nc, c_len, HK)
            a = jnp.einsum('ctk,csk->cts', lq, rk, preferred_element_type=F32)
            if 2 * h < c_len:
                a = jnp.where(row // (2 * h) == col // (2 * h), a, 0.0)
            a_intra = a if a_intra is None else a_intra + a
            h //= 2
        o = jnp.einsum('cts,csd->ctd', a_intra.astype(BF16), vb3,
                       preferred_element_type=F32).reshape(p_len, HV)

        o = o + jnp.sum(q * k, axis=-1, keepdims=True) * v
        for d in range(1, sub_len):
            e = jnp.exp(jnp.where(t_sub >= d, b - pltpu.roll(b, d, axis=0), NEG))
            w = jnp.sum(q * pltpu.roll(k, d, axis=0) * e, axis=-1, keepdims=True)
            o = o + w * pltpu.roll(v, d, axis=0)

        b_last = b3[:, c_len - 1:c_len, :]
        kd3 = (k3 * jnp.exp(b_last - b3)).astype(BF16)
        upd = jnp.einsum('csv,csk->cvk', vb3, kd3, preferred_element_type=F32)
        qe3 = (q * jnp.exp(b)).astype(BF16).reshape(nc, c_len, HK)
        local.append((o, go, qe3, upd, jnp.exp(b_last)))

    state = jnp.zeros((HV, HK), F32)
    for part, (_, _, _, upd, dec) in enumerate(local):
        for c in range(nc):
            st_ref[part * nc + c] = state.astype(BF16)
            state = state * dec[c] + upd[c]

    for part, (o, go, qe3, _, _) in enumerate(local):
        o = o + jnp.einsum('ctk,cvk->ctv', qe3, st_ref[pl.ds(part * nc, nc)],
                           preferred_element_type=F32).reshape(p_len, HV)
        o = o * lax.rsqrt(jnp.mean(o * o, axis=-1, keepdims=True) + RMS_EPS) * ng_ref[...]
        o_ref[pl.ds(part * p_len, p_len), :] = (o * (go * jax.nn.sigmoid(go))).astype(BF16)


def _hgrn(x3, wb, lb, ng, l):
    b, s, d = x3.shape
    return pl.pallas_call(
        _hgrn_kernel,
        grid=(b, HB),
        in_specs=[pl.BlockSpec((None, s, d), lambda bi, h: (bi, 0, 0)),
                  pl.BlockSpec((None, None, d, 4 * HK), lambda bi, h: (l, h, 0, 0)),
                  pl.BlockSpec((None, None, 1, HK), lambda bi, h: (l, h, 0, 0)),
                  pl.BlockSpec((None, 1, HV), lambda bi, h: (l, 0, 0))],
        out_specs=pl.BlockSpec((None, s, HV), lambda bi, h: (bi, 0, h)),
        out_shape=jax.ShapeDtypeStruct((b, s, HB * HV), BF16),
        scratch_shapes=[pltpu.VMEM((s // HGRN_CHUNK, HV, HK), BF16)],
        compiler_params=_params("parallel", "arbitrary"),
        name="hgrn",
    )(x3, wb, lb, ng)


MLA_W = HC * LANE
C_CQ, C_CKV, C_KR = 0, Q_RANK, Q_RANK + KV_RANK
C_COLS = C_KR + LANE
MLAP_TM = 1024


def _rms(x, g):
    return x * lax.rsqrt(jnp.mean(x * x, axis=-1, keepdims=True) + RMS_EPS) * g


def _mlap_kernel(x_ref, wc_ref, qg_ref, kvg_ref, wuq_ref, wuk_ref, wuv_ref,
                 cos_ref, slo_ref, shi_ref, q_ref, k_ref, v_ref):
    half = ROPE_D // 2
    qs = (NOPE + ROPE_D) ** -0.5 * LOG2E
    den_flag = jnp.where(lax.broadcasted_iota(jnp.int32, (1, LANE), 1) == DEN_LANE, 1.0, 0.0)
    cos, lo, hi = cos_ref[...], slo_ref[...], shi_ref[...]
    c = _dot(x_ref[...].astype(BF16), wc_ref[...])
    nq = _rms(c[:, C_CQ:C_CKV], qg_ref[...]).astype(BF16)
    nkv = _rms(c[:, C_CKV:C_KR], kvg_ref[...]).astype(BF16)
    k_pe = _rope_lanes(c[:, C_KR:C_COLS], cos, lo, hi, half)
    yq = _dot(nq, wuq_ref[...])
    yk = _dot(nkv, wuk_ref[...])
    yv = _dot(nkv, wuv_ref[...])
    cos_q, lo_q, hi_q = cos * qs, lo * qs, hi * qs
    for h in range(HC):
        hs = slice(h * LANE, (h + 1) * LANE)
        q_ref[:, hs] = _rope_lanes(yq[:, hs], cos_q, lo_q, hi_q, half).astype(BF16)
        k_ref[:, hs] = (yk[:, hs] + k_pe).astype(BF16)
        v_ref[:, hs] = (yv[:, hs] + den_flag).astype(BF16)


def _mlap(x3, wc, qg, kvg, wuq, wuk, wuv, l, cos_c, slo_c, shi_c):
    b, s, _ = x3.shape
    tm = MLAP_TM
    o_spec = pl.BlockSpec((None, tm, MLA_W), lambda bi, m: (bi, m, 0))
    o_shape = jax.ShapeDtypeStruct((b, s, MLA_W), BF16)
    t_spec = pl.BlockSpec((None, tm, LANE), lambda bi, m: (bi, m, 0))
    return pl.pallas_call(
        _mlap_kernel,
        grid=(b, s // tm),
        in_specs=[
            pl.BlockSpec((None, tm, D_MODEL), lambda bi, m: (bi, m, 0)),
            _resident((None, D_MODEL, C_COLS), lambda bi, m: (l, 0, 0)),
            _resident((None, 1, Q_RANK), lambda bi, m: (l, 0, 0)),
            _resident((None, 1, KV_RANK), lambda bi, m: (l, 0, 0)),
            _resident((None, Q_RANK, MLA_W), lambda bi, m: (l, 0, 0)),
            _resident((None, KV_RANK, MLA_W), lambda bi, m: (l, 0, 0)),
            _resident((None, KV_RANK, MLA_W), lambda bi, m: (l, 0, 0)),
            t_spec, t_spec, t_spec,
        ],
        out_specs=[o_spec, o_spec, o_spec],
        out_shape=[o_shape, o_shape, o_shape],
        compiler_params=_params("parallel", "parallel"),
        name="mla_proj",
    )(x3, wc, qg, kvg, wuq, wuk, wuv, cos_c, slo_c, shi_c)


MLA_TQ = 512
MLA_TK = MLA_TQ


MLA_HPS = 4


def _mla_attn_kernel(q_ref, k_ref, v_ref, o_ref):
    tq, tk = MLA_TQ, MLA_TK
    i = pl.program_id(2)
    t0 = i * tq
    heads = [slice(h * LANE, (h + 1) * LANE) for h in range(MLA_HPS)]
    qs = [q_ref[:, hs] for hs in heads]

    def attend(k0, streams, causal):
        scores = [_dot_nt(q, k_ref[pl.ds(k0, nk), hs]) for q, hs, _, _, nk, _ in streams]
        m_new, probs = [], []
        for s, (q, _, m, _, nk, r0) in zip(scores, streams):
            if causal:
                kpos = k0 + lax.broadcasted_iota(jnp.int32, (1, nk), 1)
                qpos = t0 + r0 + lax.broadcasted_iota(jnp.int32, (q.shape[0], 1), 0)
                s = jnp.where(kpos <= qpos, s, NEG)
            m_new.append(jnp.maximum(m, jnp.max(s, axis=-1, keepdims=True)))
            probs.append(jnp.exp2(s - m_new[-1]).astype(BF16))
        pvs = [_dot(p, v_ref[pl.ds(k0, nk), hs]) for p, (_, hs, _, _, nk, _) in zip(probs, streams)]
        return [(mn, jnp.exp2(m - mn) * acc + pv)
                for mn, pv, (_, _, m, acc, _, _) in zip(m_new, pvs, streams)]

    def full_chunk(c, carry):
        streams = [(qs[h], hs, *carry[h], tk, 0) for h, hs in enumerate(heads)]
        return tuple(attend(pl.multiple_of(c * tk, tk), streams, False))

    carry = tuple((jnp.full((tq, 1), NEG, F32), jnp.zeros((tq, LANE), F32)) for _ in heads)
    carry = lax.fori_loop(0, i, full_chunk, carry)
    hq = tq // 2
    streams = [(qs[h][r0:r0 + hq], hs, carry[h][0][r0:r0 + hq], carry[h][1][r0:r0 + hq], r0 + hq, r0)
               for h, hs in enumerate(heads) for r0 in (0, hq)]
    done = attend(pl.multiple_of(t0, tq), streams, True)
    outs = []
    for h in range(len(heads)):
        acc = jnp.concatenate([done[2 * h][1], done[2 * h + 1][1]], axis=0)
        outs.append(acc / acc[:, DEN_LANE:DEN_LANE + 1])
    for h in range(0, MLA_HPS, 2):
        o_ref[:, (h // 2) * LANE:(h // 2 + 1) * LANE] = _pack_pair(outs[h], outs[h + 1]).astype(BF16)


def _mla_attn(q, k, v):
    b, s, _ = q.shape
    tq = MLA_TQ
    gw = MLA_HPS * LANE
    return pl.pallas_call(
        _mla_attn_kernel,
        grid=(b, HC // MLA_HPS, s // tq),
        in_specs=[pl.BlockSpec((None, tq, gw), lambda bi, h, i: (bi, i, h)),
                  pl.BlockSpec((None, s, gw), lambda bi, h, i: (bi, 0, h)),
                  pl.BlockSpec((None, s, gw), lambda bi, h, i: (bi, 0, h))],
        out_specs=pl.BlockSpec((None, tq, gw // 2), lambda bi, h, i: (bi, i, h)),
        out_shape=jax.ShapeDtypeStruct((b, s, MIX_W), BF16),
        compiler_params=_params("parallel", "parallel", "arbitrary"),
        name="mla_attn",
    )(q, k, v)


MERGE_TM = 1024


def _merge_kernel(x_ref, ya_ref, yb_ref, yc_ref, wm_ref, wa_ref, wb_ref, wc_ref, wo_ref,
                  g_ref, b_ref, o_ref, *, alpha):
    x = x_ref[...]
    xb = x.astype(BF16)
    mixed = jnp.zeros(x.shape, F32)
    for idx, (y_ref, w_ref) in enumerate(((ya_ref, wa_ref), (yb_ref, wb_ref), (yc_ref, wc_ref))):
        gate = jax.nn.sigmoid(_dot(xb, wm_ref[:, idx * D_MODEL:(idx + 1) * D_MODEL]))
        mixed = mixed + gate * _dot(y_ref[...], w_ref[...])
    y = alpha * x + _dot(mixed.astype(BF16), wo_ref[...])
    o_ref[...] = _ln(y, g_ref[...], b_ref[...])


def _merge(x2, ya, yb, yc, wm, wa, wb, wc, wo, lng, lnb, l, alpha):
    n = x2.shape[0]
    tm = min(MERGE_TM, n)

    def rows(width):
        return pl.BlockSpec((tm, width), lambda m: (m, 0))

    return pl.pallas_call(
        functools.partial(_merge_kernel, alpha=alpha),
        grid=(n // tm,),
        in_specs=[rows(D_MODEL), rows(MIX_W), rows(MIX_W), rows(MIX_W),
                  _resident((None, D_MODEL, 3 * D_MODEL), lambda m: (l, 0, 0)),
                  _resident((None, None, MIX_W, D_MODEL), lambda m: (l, 0, 0, 0)),
                  _resident((None, None, MIX_W, D_MODEL), lambda m: (l, 1, 0, 0)),
                  _resident((None, None, MIX_W, D_MODEL), lambda m: (l, 2, 0, 0)),
                  _resident((None, D_MODEL, D_MODEL), lambda m: (l, 0, 0)),
                  _resident((None, None, 1, D_MODEL), lambda m: (l, 1, 0, 0)),
                  _resident((None, None, 1, D_MODEL), lambda m: (l, 1, 0, 0))],
        out_specs=rows(D_MODEL),
        out_shape=jax.ShapeDtypeStruct((n, D_MODEL), F32),
        compiler_params=_params("parallel"),
        name="merge",
    )(x2, ya, yb, yc, wm, wa, wb, wc, wo, lng, lnb)


XA_TM = 1024


def _xattn_kernel(x_ref, wq_ref, k_ref, v_ref, wo_ref, g_ref, b_ref, o_ref, *, alpha):
    x = x_ref[...]
    xb = x.astype(BF16)
    q = (_dot(xb, wq_ref[...]) * (XA_DH ** -0.5 * LOG2E)).astype(BF16)
    ones = jnp.ones((k_ref.shape[0], LANE), BF16)
    heads = []
    for h in range(XA_HEADS):
        hs = slice(h * XA_DH, (h + 1) * XA_DH)
        s = _dot_nt(q[:, hs], k_ref[:, hs])
        e = jnp.exp2(s - jnp.max(s, axis=-1, keepdims=True)).astype(BF16)
        den = _dot(e, ones)
        den = jnp.concatenate([den] * (XA_DH // LANE), axis=1)
        heads.append((_dot(e, v_ref[:, hs]) / den).astype(BF16))
    att = jnp.concatenate(heads, axis=1)
    y = alpha * x + _dot(att, wo_ref[...])
    o_ref[...] = _ln(y, g_ref[...], b_ref[...])


def _xattn(x3, wq, k, v, wo, lng, lnb, l, alpha):
    b, s, _ = x3.shape
    m_len = k.shape[1]
    tm = XA_TM
    return pl.pallas_call(
        functools.partial(_xattn_kernel, alpha=alpha),
        grid=(b, s // tm),
        in_specs=[pl.BlockSpec((None, tm, D_MODEL), lambda bi, m: (bi, m, 0)),
                  _resident((None, D_MODEL, D_MODEL), lambda bi, m: (l, 0, 0)),
                  pl.BlockSpec((None, m_len, D_MODEL), lambda bi, m: (bi, 0, 0)),
                  pl.BlockSpec((None, m_len, D_MODEL), lambda bi, m: (bi, 0, 0)),
                  _resident((None, D_MODEL, D_MODEL), lambda bi, m: (l, 0, 0)),
                  _resident((None, None, 1, D_MODEL), lambda bi, m: (l, 2, 0, 0)),
                  _resident((None, None, 1, D_MODEL), lambda bi, m: (l, 2, 0, 0))],
        out_specs=pl.BlockSpec((None, tm, D_MODEL), lambda bi, m: (bi, m, 0)),
        out_shape=jax.ShapeDtypeStruct((b, s, D_MODEL), F32),
        compiler_params=_params("parallel", "parallel"),
        name="xattn",
    )(x3, wq, k, v, wo, lng, lnb)


def _pad_heads(w, n_heads, dh):
    depth, k, _ = w.shape
    w = w.reshape(depth, k, n_heads, dh)
    return jnp.pad(w, ((0, 0), (0, 0), (0, 0), (0, LANE - dh))).reshape(depth, k, n_heads * LANE)


def _rope_tables(positions, rot, off):
    half = rot // 2
    inv = ROPE_THETA ** (-jnp.arange(half, dtype=F32) / half)
    ang = positions.astype(F32)[..., None] * inv
    cos, sin = jnp.cos(ang), jnp.sin(ang)
    shape = positions.shape
    zeros = lambda n: jnp.zeros(shape + (n,), F32)
    cos_t = jnp.concatenate([jnp.ones(shape + (off,), F32), cos, cos,
                             jnp.ones(shape + (LANE - off - rot,), F32)], axis=-1)
    sin_lo = jnp.concatenate([zeros(off), -sin, zeros(LANE - off - half)], axis=-1)
    sin_hi = jnp.concatenate([zeros(off + half), sin, zeros(LANE - off - rot)], axis=-1)
    return cos_t, sin_lo, sin_hi


def _overlap_matrix(s_len):
    n_cmp = (s_len - CMP_L) // CMP_D + 1
    n_slc = s_len // SEL_L
    start = np.arange(n_cmp) * CMP_D
    j = np.arange(n_slc)
    ov = np.clip(np.minimum(start[:, None] + CMP_L, (j[None, :] + 1) * SEL_L)
                 - np.maximum(start[:, None], j[None, :] * SEL_L), 0, None) / CMP_L
    out = np.zeros((N_SLC, N_CMP), np.float32)
    out[:n_slc, :n_cmp] = ov.T
    place = np.zeros((N_SLC, LANE), np.float32)
    place[np.arange(N_SLC), SEL_LANE0 + np.arange(N_SLC)] = 1.0
    return jnp.asarray(out), jnp.asarray(place)


def _stacked_weights(w_in, nsa_cmp_pos, nsa_cmp_w1, nsa_cmp_w2, mla_w_uq, mla_w_ukv):
    depth = w_in.shape[0]
    part = [w_in[:, :, IN_OFF[i]:IN_OFF[i + 1]] for i in range(len(IN_SIZES))]
    (a_q, a_kc, a_vc, a_ks, a_vs, a_kw, a_vw, a_gate, b_q, b_f, b_i, b_g,
     c_q, c_kv, c_kr, merge) = part
    gate = _pad_heads(a_gate, NSA_KV, NSA_R * 3)
    kv_parts = dict(kc=a_kc, vc=a_vc, ks=a_ks, vs=a_vs, kw=a_kw, vw=a_vw)
    wa = jnp.concatenate([_pad_heads(a_q, NSA_HEADS, NSA_DH)]
                         + [kv_parts[name] for name in A_KV_ORDER] + [gate], axis=2).astype(BF16)
    wb = jnp.stack([w.reshape(depth, D_MODEL, HB, HK) for w in (b_q, b_f, b_i, b_g)], axis=3)
    wb = wb.transpose(0, 2, 1, 3, 4).reshape(depth, HB, D_MODEL, 4 * HK).astype(BF16)

    kr_pad = jnp.pad(c_kr, ((0, 0), (0, 0), (NOPE, LANE - NOPE - ROPE_D)))
    wc = jnp.concatenate([c_q, c_kv, kr_pad], axis=2).astype(BF16)
    wuq = _pad_heads(mla_w_uq, HC, NOPE + ROPE_D).astype(BF16)
    ukv = mla_w_ukv.reshape(depth, KV_RANK, HC, NOPE + VD)
    wuk = _pad_heads(ukv[..., :NOPE].reshape(depth, KV_RANK, HC * NOPE), HC, NOPE).astype(BF16)
    wuv = _pad_heads(ukv[..., NOPE:].reshape(depth, KV_RANK, HC * VD), HC, VD).astype(BF16)

    w1 = nsa_cmp_w1.reshape(depth, 2, CMP_L, NSA_DH, CMP_HID)
    w1p = jnp.pad(w1, ((0, 0), (0, 0), (0, 0), (0, LANE - NSA_DH), (0, 0)))
    w1p = w1p.reshape(depth, 2, CMP_L * LANE, CMP_HID)
    pe = jnp.pad(nsa_cmp_pos, ((0, 0), (0, 0), (0, 0), (0, LANE - NSA_DH)))
    pe8 = jnp.broadcast_to(pe.reshape(depth, 2, 1, CMP_L * LANE), (depth, 2, 8, CMP_L * LANE))
    w2p = jnp.pad(nsa_cmp_w2, ((0, 0), (0, 0), (0, 0), (0, LANE - NSA_DH)))
    return dict(wa=wa, wb=wb, wc=wc, wuq=wuq, wuk=wuk, wuv=wuv,
                w1p=w1p.astype(BF16), pe8=pe8.astype(BF16), w2p=w2p.astype(BF16),
                wm=merge.astype(BF16))


def kernel(x, mem, positions, ln_g, ln_b, ffn_w1, ffn_w3, ffn_w2, w_in, nsa_cmp_pos,
           nsa_cmp_w1, nsa_cmp_w2, hgrn_lb_logits, hgrn_norm_g, mla_q_norm_g, mla_w_uq,
           mla_kv_norm_g, mla_w_ukv, w_branch, w_out, xa_wq, xa_wk, xa_wv, xa_wo):
    b, s, d = x.shape
    depth = ln_g.shape[0]
    n = b * s
    alpha = (2.0 * depth) ** 0.25

    lng = ln_g.reshape(depth, 4, 1, d)
    lnb = ln_b.reshape(depth, 4, 1, d)
    w1 = ffn_w1.astype(BF16)
    w3 = ffn_w3.astype(BF16)
    w2 = (0.5 * ffn_w2).astype(BF16)
    wo = w_out.astype(BF16)
    wbr = w_branch.astype(BF16)
    xq = xa_wq.astype(BF16)
    xk = xa_wk.astype(BF16)
    xv = xa_wv.astype(BF16)
    xo = xa_wo.astype(BF16)
    p_lb = jax.nn.softmax(hgrn_lb_logits.astype(F32), axis=0)
    lower = (jnp.cumsum(p_lb, axis=0) - p_lb[0:1]).reshape(depth, HB, 1, HK)

    rope_a = _rope_tables(positions, NSA_ROT, 0)
    rope_c = _rope_tables(positions, ROPE_D, NOPE)
    ovl, place = _overlap_matrix(s)
    mem2 = mem.reshape(b * mem.shape[1], d)
    w = _stacked_weights(w_in, nsa_cmp_pos, nsa_cmp_w1, nsa_cmp_w2, mla_w_uq, mla_w_ukv)
    norm_b = hgrn_norm_g.reshape(depth, 1, HV)
    norm_q = mla_q_norm_g.reshape(depth, 1, Q_RANK)
    norm_kv = mla_kv_norm_g.reshape(depth, 1, KV_RANK)

    x2 = x.reshape(n, d)
    for l in range(depth):
        x2 = _ffn_ln(x2, w1, w3, w2, lng, lnb, l, 0, 0, alpha)
        x3 = x2.reshape(b, s, d)

        q, qr, kc, vc, ks, vs, kw, vw, gates = _proja(x3, w["wa"], l, *rope_a)
        kcmp, vcmp = _cmp(kc, vc, w["w1p"], w["pe8"], w["w2p"], l)
        ya = _nsa_attn(q, qr, kcmp, vcmp, ks, vs, kw, vw, gates, ovl, place)

        yb = _hgrn(x3, w["wb"], lower, norm_b, l)

        mq, mk, mv = _mlap(x3, w["wc"], norm_q, norm_kv, w["wuq"], w["wuk"], w["wuv"], l, *rope_c)
        yc = _mla_attn(mq, mk, mv)

        x2 = _merge(x2, ya.reshape(n, MIX_W), yb.reshape(n, MIX_W), yc.reshape(n, MIX_W),
                    w["wm"], wbr, wbr, wbr, wo, lng, lnb, l, alpha)

        xk_l = _proj(mem2, xk, l, BF16).reshape(b, -1, d)
        xv_l = _proj(mem2, xv, l, BF16).reshape(b, -1, d)
        x2 = _xattn(x2.reshape(b, s, d), xq, xk_l, xv_l, xo, lng, lnb, l, alpha).reshape(n, d)

        x2 = _ffn_ln(x2, w1, w3, w2, lng, lnb, l, 1, 3, alpha)
    return x2.reshape(b, s, d)
```

```python
import functools
import math

import numpy as np
import jax
import jax.numpy as jnp
from jax import lax
from jax.experimental import pallas as pl
from jax.experimental.pallas import tpu as pltpu

F32 = jnp.float32
BF16 = jnp.bfloat16

D_MODEL = 1024
MIX_W = D_MODEL // 2
NSA_DH = 64
NSA_HEADS = 8
NSA_KV = 2
NSA_R = 4
NSA_ROT = 16
CMP_L = 32
CMP_D = 16
CMP_HID = 256
SEL_L = 64
N_SEL = 8
WINDOW = 256
HB = 4
HK = 128
HV = 128
HGRN_CHUNK = 64
HGRN_SUB = 4
HGRN_PARTS = 4
HC = 8
NOPE = 64
ROPE_D = 32
VD = 64
Q_RANK = 384
KV_RANK = 256
XA_HEADS = 4
XA_DH = D_MODEL // XA_HEADS
D_FF = 2816
ROPE_THETA = 500000.0
LN_EPS = 1e-5
RMS_EPS = 1e-6
NEG = -1e30
BIG = 1e9
F_MIN = 1e-20

LOG2E = 1.4426950408889634
LANE = 128
SEL_LANE0 = NSA_DH
DEN_LANE = 64
VMEM_LIMIT = 56 * 1024 * 1024

IN_SIZES = (512, 128, 128, 128, 128, 128, 128, 24, 512, 512, 512, 512, 384, 256, 32, 3072)
IN_OFF = tuple(int(v) for v in np.concatenate([[0], np.cumsum(IN_SIZES)]))


def _dot(a, b):
    return jnp.dot(a, b, preferred_element_type=F32)


def _dot_nt(a, b):
    return lax.dot_general(a, b, (((1,), (1,)), ((), ())), preferred_element_type=F32)


def _ln(y, g, b):
    mu = jnp.mean(y, axis=-1, keepdims=True)
    yc = y - mu
    var = jnp.mean(yc * yc, axis=-1, keepdims=True)
    return yc * lax.rsqrt(var + LN_EPS) * g + b


def _pack_pair(a, b):
    lane = lax.broadcasted_iota(jnp.int32, a.shape, a.ndim - 1)
    return jnp.where(lane < LANE // 2, a, pltpu.roll(b, LANE // 2, axis=a.ndim - 1))


def _params(*sem):
    return pltpu.CompilerParams(dimension_semantics=sem, vmem_limit_bytes=VMEM_LIMIT)


def _resident(shape, index_map):
    return pl.BlockSpec(shape, index_map, pipeline_mode=pl.Buffered(1))


FFN_TM = 1024
FFN_TF = 256


def _ffn_kernel(x_ref, w1_ref, w3_ref, w2h_ref, g_ref, b_ref, o_ref, acc_ref, xb_ref, h_ref, *,
                alpha):
    s = pl.program_id(1)
    n_f = pl.num_programs(1) - 1

    def up(slot):
        xb = xb_ref[...]
        h1 = _dot(xb, w1_ref[...])
        h3 = _dot(xb, w3_ref[...])
        return slot, h1, h3

    def down(slot):
        acc_ref[...] += _dot(h_ref[slot], w2h_ref[...])

    def gate(slot, h1, h3):
        h_ref[slot] = ((h1 * jax.nn.sigmoid(h1)) * h3).astype(BF16)

    @pl.when(s == 0)
    def _():
        xb_ref[...] = x_ref[...].astype(BF16)
        acc_ref[...] = alpha * x_ref[...]
        gate(*up(0))

    for parity in range(2):
        @pl.when((s > 0) & (s < n_f) & (s % 2 == parity))
        def _():
            pending = up(parity)
            down(1 - parity)
            gate(*pending)

    @pl.when(s == n_f)
    def _():
        acc_ref[...] += _dot(h_ref[(D_FF // FFN_TF - 1) % 2], w2h_ref[...])
        o_ref[...] = _ln(acc_ref[...], g_ref[...], b_ref[...])


def _ffn_ln(x2, w1, w3, w2h, lng, lnb, l, which, ln_idx, alpha):
    n = x2.shape[0]
    tm = min(FFN_TM, n)
    n_f = D_FF // FFN_TF
    up_idx = lambda s: jnp.minimum(s, n_f - 1)
    down_idx = lambda s: jnp.maximum(s - 1, 0)
    return pl.pallas_call(
        functools.partial(_ffn_kernel, alpha=alpha),
        grid=(n // tm, n_f + 1),
        in_specs=[
            pl.BlockSpec((tm, D_MODEL), lambda m, s: (m, 0)),
            pl.BlockSpec((None, None, D_MODEL, FFN_TF), lambda m, s: (l, which, 0, up_idx(s))),
            pl.BlockSpec((None, None, D_MODEL, FFN_TF), lambda m, s: (l, which, 0, up_idx(s))),
            pl.BlockSpec((None, None, FFN_TF, D_MODEL), lambda m, s: (l, which, down_idx(s), 0)),
            pl.BlockSpec((None, None, 1, D_MODEL), lambda m, s: (l, ln_idx, 0, 0)),
            pl.BlockSpec((None, None, 1, D_MODEL), lambda m, s: (l, ln_idx, 0, 0)),
        ],
        out_specs=pl.BlockSpec((tm, D_MODEL), lambda m, s: (m, 0)),
        out_shape=jax.ShapeDtypeStruct((n, D_MODEL), F32),
        scratch_shapes=[pltpu.VMEM((tm, D_MODEL), F32), pltpu.VMEM((tm, D_MODEL), BF16),
                        pltpu.VMEM((2, tm, FFN_TF), BF16)],
        compiler_params=_params("parallel", "arbitrary"),
        name="ffn_ln",
    )(x2, w1, w3, w2h, lng, lnb)


def _proj_kernel(x_ref, w_ref, o_ref):
    o_ref[...] = _dot(x_ref[...].astype(BF16), w_ref[...]).astype(o_ref.dtype)


def _proj(x2, w, l, out_dtype, tm=1024, tn=2048):
    n, k = x2.shape
    c = w.shape[2]
    tm = min(tm, n)
    tn = min(tn, c)
    return pl.pallas_call(
        _proj_kernel,
        grid=(n // tm, c // tn),
        in_specs=[pl.BlockSpec((tm, k), lambda m, j: (m, 0)),
                  pl.BlockSpec((None, k, tn), lambda m, j: (l, 0, j))],
        out_specs=pl.BlockSpec((tm, tn), lambda m, j: (m, j)),
        out_shape=jax.ShapeDtypeStruct((n, c), out_dtype),
        compiler_params=_params("parallel", "arbitrary"),
        name="proj",
    )(x2, w)


NSA_QW = NSA_HEADS * LANE
NSA_KW = NSA_KV * LANE
A_Q = 0
A_KV = NSA_QW
A_KV_ORDER = ("kc", "vc", "ks", "vs", "kw", "vw")
A_GATE = A_KV + len(A_KV_ORDER) * LANE
A_COLS = A_GATE + NSA_KW
PROJA_TM = 1024


def _rope_lanes(y, cos, sin_lo, sin_hi, half):
    return (y * cos + pltpu.roll(y, LANE - half, axis=1) * sin_lo
            + pltpu.roll(y, half, axis=1) * sin_hi)


def _proja_kernel(x_ref, w_ref, cos_ref, slo_ref, shi_ref,
                  q_ref, qr_ref, kc_ref, vc_ref, ks_ref, vs_ref, kw_ref, vw_ref, g_ref):
    xb = x_ref[...].astype(BF16)
    cos = cos_ref[...]
    sin_lo = slo_ref[...]
    sin_hi = shi_ref[...]
    half = NSA_ROT // 2
    tm = x_ref.shape[0]

    def mm(c0, width):
        return _dot(xb, w_ref[:, c0:c0 + width])

    qs = NSA_DH ** -0.5 * LOG2E
    y = mm(A_Q, NSA_QW)
    q_ref[...] = (y * qs).astype(BF16)
    for h in range(NSA_HEADS):
        hs = slice(h * LANE, (h + 1) * LANE)
        qr_ref[:, hs] = (_rope_lanes(y[:, hs], cos, sin_lo, sin_hi, half) * qs).astype(BF16)

    lane = lax.broadcasted_iota(jnp.int32, (tm, LANE), 1)
    first = lane < NSA_DH
    tok = pl.program_id(1) * tm + lax.broadcasted_iota(jnp.int32, (tm, LANE), 0)
    blk_flag = jnp.where(lane - SEL_LANE0 == tok // SEL_L, 1.0, 0.0)
    den_flag = jnp.where(lane == DEN_LANE, 1.0, 0.0)

    def packed(table):
        return jnp.where(first, table, pltpu.roll(table, NSA_DH, axis=1))

    def put(ref, y, flag=None):
        for g, yg in enumerate((y, pltpu.roll(y, NSA_DH, axis=1))):
            yg = jnp.where(first, yg, 0.0)
            ref[g] = (yg if flag is None else yg + flag).astype(ref.dtype)

    kv = mm(A_KV, len(A_KV_ORDER) * LANE)
    part = {name: kv[:, i * LANE:(i + 1) * LANE] for i, name in enumerate(A_KV_ORDER)}
    cos_p, lo_p, hi_p = packed(cos), packed(sin_lo), packed(sin_hi)
    put(kc_ref, part["kc"])
    put(vc_ref, part["vc"])
    put(ks_ref, _rope_lanes(part["ks"], cos_p, lo_p, hi_p, half), blk_flag)
    put(vs_ref, part["vs"], den_flag)
    put(kw_ref, _rope_lanes(part["kw"], cos_p, lo_p, hi_p, half))
    put(vw_ref, part["vw"], den_flag)
    g_ref[...] = mm(A_GATE, NSA_KW)


def _proja(x3, wa, l, cos_a, slo_a, shi_a):
    b, s, _ = x3.shape
    tm = PROJA_TM
    kv_shape = jax.ShapeDtypeStruct((b, NSA_KV, s, LANE), BF16)
    cmp_shape = jax.ShapeDtypeStruct((b, NSA_KV, s, LANE), F32)
    kv_spec = pl.BlockSpec((None, NSA_KV, tm, LANE), lambda bi, m: (bi, 0, m, 0))
    return pl.pallas_call(
        _proja_kernel,
        grid=(b, s // tm),
        in_specs=[
            pl.BlockSpec((None, tm, D_MODEL), lambda bi, m: (bi, m, 0)),
            _resident((None, D_MODEL, A_COLS), lambda bi, m: (l, 0, 0)),
            pl.BlockSpec((None, tm, LANE), lambda bi, m: (bi, m, 0)),
            pl.BlockSpec((None, tm, LANE), lambda bi, m: (bi, m, 0)),
            pl.BlockSpec((None, tm, LANE), lambda bi, m: (bi, m, 0)),
        ],
        out_specs=[
            pl.BlockSpec((None, tm, NSA_QW), lambda bi, m: (bi, m, 0)),
            pl.BlockSpec((None, tm, NSA_QW), lambda bi, m: (bi, m, 0)),
            kv_spec, kv_spec, kv_spec, kv_spec, kv_spec, kv_spec,
            pl.BlockSpec((None, tm, NSA_KW), lambda bi, m: (bi, m, 0)),
        ],
        out_shape=[
            jax.ShapeDtypeStruct((b, s, NSA_QW), BF16),
            jax.ShapeDtypeStruct((b, s, NSA_QW), BF16),
            cmp_shape, cmp_shape, kv_shape, kv_shape, kv_shape, kv_shape,
            jax.ShapeDtypeStruct((b, s, NSA_KW), F32),
        ],
        compiler_params=_params("parallel", "parallel"),
        name="nsa_proj",
    )(x3, wa, cos_a, slo_a, shi_a)


N_CMP = 128
CMP_HALF = CMP_D * LANE


def _gelu_tanh(x):
    return 0.5 * x * (1.0 + jnp.tanh(0.7978845608028654 * (x + 0.044715 * x * x * x)))


def _cmp_kernel(zk_ref, zv_ref, w1_ref, pe_ref, w2_ref, ok_ref, ov_ref):
    for which, (z_ref, o_ref) in enumerate(((zk_ref, ok_ref), (zv_ref, ov_ref))):
        first = jnp.zeros((N_CMP, CMP_HID), F32)
        second = jnp.zeros((N_CMP, CMP_HID), F32)
        for j in range(CMP_D):
            zj = z_ref[pl.ds(j, N_CMP, stride=CMP_D), :].astype(BF16)
            first = first + _dot(zj, w1_ref[which, j * LANE:(j + 1) * LANE, :])
            second = second + _dot(zj, w1_ref[which, CMP_HALF + j * LANE:CMP_HALF + (j + 1) * LANE, :])
        bias = _dot(pe_ref[which], w1_ref[which])[0:1]
        pre = first + pltpu.roll(second, N_CMP - 1, axis=0) + bias
        h = _gelu_tanh(pre)
        o_ref[...] = _dot(h.astype(BF16), w2_ref[which]).astype(BF16)


def _cmp(zk, zv, w1p, pe8, w2p, l):
    b, _, s, _ = zk.shape
    z_spec = pl.BlockSpec((None, None, s, LANE), lambda bi, g: (bi, g, 0, 0))
    o_spec = pl.BlockSpec((None, None, N_CMP, LANE), lambda bi, g: (bi, g, 0, 0))
    o_shape = jax.ShapeDtypeStruct((b, NSA_KV, N_CMP, LANE), BF16)
    return pl.pallas_call(
        _cmp_kernel,
        grid=(b, NSA_KV),
        in_specs=[
            z_spec, z_spec,
            _resident((None, 2, 2 * CMP_HALF, CMP_HID), lambda bi, g: (l, 0, 0, 0)),
            _resident((None, 2, 8, 2 * CMP_HALF), lambda bi, g: (l, 0, 0, 0)),
            _resident((None, 2, CMP_HID, LANE), lambda bi, g: (l, 0, 0, 0)),
        ],
        out_specs=[o_spec, o_spec],
        out_shape=[o_shape, o_shape],
        compiler_params=_params("parallel", "parallel"),
        name="nsa_cmp",
    )(zk, zv, w1p, pe8, w2p)


NSA_TQ = 256
NSA_TK = 512
N_SLC = 32
WIN_SLAB = WINDOW + NSA_TQ


def _nsa_kernel(q_ref, qr_ref, kc_ref, vc_ref, ks_ref, vs_ref, kw_ref, vw_ref, gt_ref, ovl_ref,
                place_ref, o_ref):
    tq, tk, r_heads = NSA_TQ, NSA_TK, NSA_R
    i = pl.program_id(2)
    t0 = i * tq
    tpos = t0 + lax.broadcasted_iota(jnp.int32, (tq, 1), 0)
    lane = lax.broadcasted_iota(jnp.int32, (1, LANE), 1)

    def stack(ref):
        return jnp.concatenate([ref[:, r * LANE:(r + 1) * LANE] for r in range(r_heads)], axis=0)

    gt = jax.nn.sigmoid(gt_ref[...])

    def gate(c):
        return jnp.stack([gt[:, 3 * r + c:3 * r + c + 1] for r in range(r_heads)], axis=0)

    q4w = stack(qr_ref)
    w0 = pl.multiple_of(jnp.maximum(t0 - WINDOW, 0), LANE)
    kw = kw_ref[pl.ds(w0, WIN_SLAB), :]
    vw = vw_ref[pl.ds(w0, WIN_SLAB), :]
    kpos = w0 + lax.broadcasted_iota(jnp.int32, (1, WIN_SLAB), 1)
    w_bias = jnp.where((kpos <= tpos) & (kpos > tpos - WINDOW), 0.0, NEG)
    s = _dot_nt(q4w, kw).reshape(r_heads, tq, WIN_SLAB) + w_bias[None]
    p = jnp.exp2(s - jnp.max(s, axis=-1, keepdims=True))
    o_win = _dot(p.reshape(r_heads * tq, WIN_SLAB).astype(BF16), vw).reshape(r_heads, tq, LANE)
    o_part = gate(2) * (o_win / o_win[:, :, DEN_LANE:DEN_LANE + 1])

    cmask = ((lane * CMP_D + (CMP_L - 1) <= tpos) & (lane < N_CMP - 1))[None]
    s = _dot_nt(stack(q_ref), kc_ref[...]).reshape(r_heads, tq, N_CMP)
    s = jnp.where(cmask, s, NEG)
    e = jnp.where(cmask, jnp.exp2(s - jnp.max(s, axis=-1, keepdims=True)), 0.0)
    den = jnp.sum(e, axis=-1, keepdims=True)
    p = e / jnp.where(den > 0.0, den, 1.0)
    psum = jnp.sum(p, axis=0)
    o_cmp = _dot(p.reshape(r_heads * tq, N_CMP).astype(BF16), vc_ref[...])
    o_part = o_part + gate(0) * o_cmp.reshape(r_heads, tq, LANE)

    imp = lax.dot_general(ovl_ref[...], psum, (((1,), (1,)), ((), ())),
                          precision=lax.Precision.HIGHEST, preferred_element_type=F32)
    blk = lax.broadcasted_iota(jnp.int32, (N_SLC, 1), 0)
    tpos_l = t0 + lax.broadcasted_iota(jnp.int32, (1, tq), 1)
    cur = tpos_l // SEL_L
    valid = blk * SEL_L <= tpos_l
    forced = (blk == 0) | (blk == cur) | (blk == cur - 1)
    score = jnp.where(valid & forced, BIG, jnp.where(valid, imp, -BIG))
    beats = []
    for j in range(N_SLC):
        sj = score[j:j + 1, :]
        beats.append(jnp.where((sj > score) | ((sj == score) & (j < blk)), 1.0, 0.0))
    while len(beats) > 1:
        beats = [a + b for a, b in zip(beats[0::2], beats[1::2])]
    sel_t = jnp.where((beats[0] < N_SEL) & valid, 1.0, 0.0)
    sel_q = lax.dot_general(sel_t, place_ref[...], (((0,), (0,)), ((), ())),
                            preferred_element_type=F32)
    in_flags = jnp.where((lane >= SEL_LANE0) & (lane < SEL_LANE0 + N_SLC), 1.0, 0.0)
    q_bias = (sel_q - in_flags) * (-NEG)
    q4 = (q4w.astype(F32).reshape(r_heads, tq, LANE) + q_bias[None]).astype(BF16)
    q4 = q4.reshape(r_heads * tq, LANE)

    def sel_step(c, carry, causal):
        m, acc = carry
        k0 = pl.multiple_of(c * tk, tk)
        k = ks_ref[pl.ds(k0, tk), :]
        v = vs_ref[pl.ds(k0, tk), :]
        s = _dot_nt(q4, k).reshape(r_heads, tq, tk)
        if causal:
            kpos = k0 + lax.broadcasted_iota(jnp.int32, (1, tk), 1)
            s = jnp.where((kpos <= tpos)[None], s, NEG)
        m_new = jnp.maximum(m, jnp.max(s, axis=-1, keepdims=True))
        p = jnp.exp2(s - m_new)
        pv = _dot(p.reshape(r_heads * tq, tk).astype(BF16), v).reshape(r_heads, tq, LANE)
        return m_new, jnp.exp2(m - m_new) * acc + pv

    last = (t0 + tq - 1) // tk
    carry = (jnp.full((r_heads, tq, 1), NEG, F32), jnp.zeros((r_heads, tq, LANE), F32))
    carry = lax.fori_loop(0, last, functools.partial(sel_step, causal=False), carry)
    _, acc_s = sel_step(last, carry, True)
    o = o_part + gate(1) * (acc_s / acc_s[:, :, DEN_LANE:DEN_LANE + 1])
    for r in range(0, r_heads, 2):
        o_ref[:, (r // 2) * LANE:(r // 2 + 1) * LANE] = _pack_pair(o[r], o[r + 1]).astype(BF16)


def _nsa_attn(q, qr, kcmp, vcmp, ks, vs, kw, vw, gates, ovl, place):
    b, s, _ = q.shape
    tq = NSA_TQ
    gw = NSA_R * LANE
    q_spec = pl.BlockSpec((None, tq, gw), lambda bi, g, i: (bi, i, g))
    c_spec = pl.BlockSpec((None, None, N_CMP, LANE), lambda bi, g, i: (bi, g, 0, 0))
    kv_spec = pl.BlockSpec((None, None, s, LANE), lambda bi, g, i: (bi, g, 0, 0))
    return pl.pallas_call(
        _nsa_kernel,
        grid=(b, NSA_KV, s // tq),
        in_specs=[q_spec, q_spec, c_spec, c_spec, kv_spec, kv_spec, kv_spec, kv_spec,
                  pl.BlockSpec((None, tq, LANE), lambda bi, g, i: (bi, i, g)),
                  _resident((N_SLC, N_CMP), lambda bi, g, i: (0, 0)),
                  _resident((N_SLC, LANE), lambda bi, g, i: (0, 0))],
        out_specs=pl.BlockSpec((None, tq, gw // 2), lambda bi, g, i: (bi, i, g)),
        out_shape=jax.ShapeDtypeStruct((b, s, MIX_W), BF16),
        compiler_params=_params("parallel", "parallel", "arbitrary"),
        name="nsa_attn",
    )(q, qr, kcmp, vcmp, ks, vs, kw, vw, gates, ovl, place)


def _hgrn_kernel(x_ref, w_ref, lb_ref, ng_ref, o_ref, st_ref):
    s_len = x_ref.shape[0]
    c_len, sub_len = HGRN_CHUNK, HGRN_SUB
    p_len = s_len // HGRN_PARTS
    nc = p_len // c_len
    lb = lb_ref[...]
    t = lax.broadcasted_iota(jnp.int32, (p_len, 1), 0)
    t_sub = t % sub_len
    row = lax.broadcasted_iota(jnp.int32, (nc, c_len, c_len), 1)
    col = lax.broadcasted_iota(jnp.int32, (nc, c_len, c_len), 2)
    tri = jnp.where(col <= row, 1.0, 0.0)

    local = []
    for part in range(HGRN_PARTS):
        proj = _dot(x_ref[pl.ds(part * p_len, p_len), :].astype(BF16), w_ref[...])
        q = proj[:, 0:HK]
        z = proj[:, HK:2 * HK]
        v = proj[:, 2 * HK:2 * HK + HV]
        go = proj[:, 2 * HK + HV:2 * HK + 2 * HV]

        sig = jax.nn.sigmoid(z)
        f = lb + (1.0 - lb) * sig
        lf = jnp.log(jnp.maximum(f, F_MIN))
        k = (1.0 - lb) * (1.0 - sig)

        b3 = jnp.einsum('cts,csd->ctd', tri, lf.reshape(nc, c_len, HK),
                        precision=lax.Precision.HIGHEST, preferred_element_type=F32)
        b = b3.reshape(p_len, HK)
        k3 = k.reshape(nc, c_len, HK)
        vb3 = v.astype(BF16).reshape(nc, c_len, HV)

        a_intra = None
        h = c_len // 2
        while h >= sub_len:
            blk = b.reshape(p_len // (2 * h), 2 * h, HK)
            e = jnp.exp(-jnp.abs(blk[:, h - 1:h, :] - blk)).reshape(p_len, HK)
            upper = (t // h) % 2 == 1
            lq = jnp.where(upper, q * e, 0.0).astype(BF16).reshape(nc, c_len, HK)
            rk = jnp.where(upper, 0.0, k * e).astype(BF16).reshape(nc, c_len, HK)
            a = jnp.einsum('ctk,csk->cts', lq, rk, preferred_element_type=F32)
            if 2 * h < c_len:
                a = jnp.where(row // (2 * h) == col // (2 * h), a, 0.0)
            a_intra = a if a_intra is None else a_intra + a
            h //= 2
        o = jnp.einsum('cts,csd->ctd', a_intra.astype(BF16), vb3,
                       preferred_element_type=F32).reshape(p_len, HV)

        o = o + jnp.sum(q * k, axis=-1, keepdims=True) * v
        for d in range(1, sub_len):
            e = jnp.exp(jnp.where(t_sub >= d, b - pltpu.roll(b, d, axis=0), NEG))
            w = jnp.sum(q * pltpu.roll(k, d, axis=0) * e, axis=-1, keepdims=True)
            o = o + w * pltpu.roll(v, d, axis=0)

        b_last = b3[:, c_len - 1:c_len, :]
        kd3 = (k3 * jnp.exp(b_last - b3)).astype(BF16)
        upd = jnp.einsum('csv,csk->cvk', vb3, kd3, preferred_element_type=F32)
        qe3 = (q * jnp.exp(b)).astype(BF16).reshape(nc, c_len, HK)
        local.append((o, go, qe3, upd, jnp.exp(b_last)))

    state = jnp.zeros((HV, HK), F32)
    for part, (_, _, _, upd, dec) in enumerate(local):
        for c in range(nc):
            st_ref[part * nc + c] = state.astype(BF16)
            state = state * dec[c] + upd[c]

    for part, (o, go, qe3, _, _) in enumerate(local):
        o = o + jnp.einsum('ctk,cvk->ctv', qe3, st_ref[pl.ds(part * nc, nc)],
                           preferred_element_type=F32).reshape(p_len, HV)
        o = o * lax.rsqrt(jnp.mean(o * o, axis=-1, keepdims=True) + RMS_EPS) * ng_ref[...]
        o_ref[pl.ds(part * p_len, p_len), :] = (o * (go * jax.nn.sigmoid(go))).astype(BF16)


def _hgrn(x3, wb, lb, ng, l):
    b, s, d = x3.shape
    return pl.pallas_call(
        _hgrn_kernel,
        grid=(b, HB),
        in_specs=[pl.BlockSpec((None, s, d), lambda bi, h: (bi, 0, 0)),
                  pl.BlockSpec((None, None, d, 4 * HK), lambda bi, h: (l, h, 0, 0)),
                  pl.BlockSpec((None, None, 1, HK), lambda bi, h: (l, h, 0, 0)),
                  pl.BlockSpec((None, 1, HV), lambda bi, h: (l, 0, 0))],
        out_specs=pl.BlockSpec((None, s, HV), lambda bi, h: (bi, 0, h)),
        out_shape=jax.ShapeDtypeStruct((b, s, HB * HV), BF16),
        scratch_shapes=[pltpu.VMEM((s // HGRN_CHUNK, HV, HK), BF16)],
        compiler_params=_params("parallel", "arbitrary"),
        name="hgrn",
    )(x3, wb, lb, ng)


MLA_W = HC * LANE
C_CQ, C_CKV, C_KR = 0, Q_RANK, Q_RANK + KV_RANK
C_COLS = C_KR + LANE
MLAP_TM = 1024


def _rms(x, g):
    return x * lax.rsqrt(jnp.mean(x * x, axis=-1, keepdims=True) + RMS_EPS) * g


def _mlap_kernel(x_ref, wc_ref, qg_ref, kvg_ref, wuq_ref, wuk_ref, wuv_ref,
                 cos_ref, slo_ref, shi_ref, q_ref, k_ref, v_ref):
    half = ROPE_D // 2
    qs = (NOPE + ROPE_D) ** -0.5 * LOG2E
    den_flag = jnp.where(lax.broadcasted_iota(jnp.int32, (1, LANE), 1) == DEN_LANE, 1.0, 0.0)
    cos, lo, hi = cos_ref[...], slo_ref[...], shi_ref[...]
    c = _dot(x_ref[...].astype(BF16), wc_ref[...])
    nq = _rms(c[:, C_CQ:C_CKV], qg_ref[...]).astype(BF16)
    nkv = _rms(c[:, C_CKV:C_KR], kvg_ref[...]).astype(BF16)
    k_pe = _rope_lanes(c[:, C_KR:C_COLS], cos, lo, hi, half)
    yq = _dot(nq, wuq_ref[...])
    yk = _dot(nkv, wuk_ref[...])
    yv = _dot(nkv, wuv_ref[...])
    cos_q, lo_q, hi_q = cos * qs, lo * qs, hi * qs
    for h in range(HC):
        hs = slice(h * LANE, (h + 1) * LANE)
        q_ref[:, hs] = _rope_lanes(yq[:, hs], cos_q, lo_q, hi_q, half).astype(BF16)
        k_ref[:, hs] = (yk[:, hs] + k_pe).astype(BF16)
        v_ref[:, hs] = (yv[:, hs] + den_flag).astype(BF16)


def _mlap(x3, wc, qg, kvg, wuq, wuk, wuv, l, cos_c, slo_c, shi_c):
    b, s, _ = x3.shape
    tm = MLAP_TM
    o_spec = pl.BlockSpec((None, tm, MLA_W), lambda bi, m: (bi, m, 0))
    o_shape = jax.ShapeDtypeStruct((b, s, MLA_W), BF16)
    t_spec = pl.BlockSpec((None, tm, LANE), lambda bi, m: (bi, m, 0))
    return pl.pallas_call(
        _mlap_kernel,
        grid=(b, s // tm),
        in_specs=[
            pl.BlockSpec((None, tm, D_MODEL), lambda bi, m: (bi, m, 0)),
            _resident((None, D_MODEL, C_COLS), lambda bi, m: (l, 0, 0)),
            _resident((None, 1, Q_RANK), lambda bi, m: (l, 0, 0)),
            _resident((None, 1, KV_RANK), lambda bi, m: (l, 0, 0)),
            _resident((None, Q_RANK, MLA_W), lambda bi, m: (l, 0, 0)),
            _resident((None, KV_RANK, MLA_W), lambda bi, m: (l, 0, 0)),
            _resident((None, KV_RANK, MLA_W), lambda bi, m: (l, 0, 0)),
            t_spec, t_spec, t_spec,
        ],
        out_specs=[o_spec, o_spec, o_spec],
        out_shape=[o_shape, o_shape, o_shape],
        compiler_params=_params("parallel", "parallel"),
        name="mla_proj",
    )(x3, wc, qg, kvg, wuq, wuk, wuv, cos_c, slo_c, shi_c)


MLA_TQ = 512
MLA_TK = MLA_TQ


MLA_HPS = 4


def _mla_attn_kernel(q_ref, k_ref, v_ref, o_ref):
    tq, tk = MLA_TQ, MLA_TK
    i = pl.program_id(2)
    t0 = i * tq
    heads = [slice(h * LANE, (h + 1) * LANE) for h in range(MLA_HPS)]
    qs = [q_ref[:, hs] for hs in heads]

    def attend(k0, streams, causal):
        scores = [_dot_nt(q, k_ref[pl.ds(k0, nk), hs]) for q, hs, _, _, nk, _ in streams]
        m_new, probs = [], []
        for s, (q, _, m, _, nk, r0) in zip(scores, streams):
            if causal:
                kpos = k0 + lax.broadcasted_iota(jnp.int32, (1, nk), 1)
                qpos = t0 + r0 + lax.broadcasted_iota(jnp.int32, (q.shape[0], 1), 0)
                s = jnp.where(kpos <= qpos, s, NEG)
            m_new.append(jnp.maximum(m, jnp.max(s, axis=-1, keepdims=True)))
            probs.append(jnp.exp2(s - m_new[-1]).astype(BF16))
        pvs = [_dot(p, v_ref[pl.ds(k0, nk), hs]) for p, (_, hs, _, _, nk, _) in zip(probs, streams)]
        return [(mn, jnp.exp2(m - mn) * acc + pv)
                for mn, pv, (_, _, m, acc, _, _) in zip(m_new, pvs, streams)]

    def full_chunk(c, carry):
        streams = [(qs[h], hs, *carry[h], tk, 0) for h, hs in enumerate(heads)]
        return tuple(attend(pl.multiple_of(c * tk, tk), streams, False))

    carry = tuple((jnp.full((tq, 1), NEG, F32), jnp.zeros((tq, LANE), F32)) for _ in heads)
    carry = lax.fori_loop(0, i, full_chunk, carry)
    hq = tq // 2
    streams = [(qs[h][r0:r0 + hq], hs, carry[h][0][r0:r0 + hq], carry[h][1][r0:r0 + hq], r0 + hq, r0)
               for h, hs in enumerate(heads) for r0 in (0, hq)]
    done = attend(pl.multiple_of(t0, tq), streams, True)
    outs = []
    for h in range(len(heads)):
        acc = jnp.concatenate([done[2 * h][1], done[2 * h + 1][1]], axis=0)
        outs.append(acc / acc[:, DEN_LANE:DEN_LANE + 1])
    for h in range(0, MLA_HPS, 2):
        o_ref[:, (h // 2) * LANE:(h // 2 + 1) * LANE] = _pack_pair(outs[h], outs[h + 1]).astype(BF16)


def _mla_attn(q, k, v):
    b, s, _ = q.shape
    tq = MLA_TQ
    gw = MLA_HPS * LANE
    return pl.pallas_call(
        _mla_attn_kernel,
        grid=(b, HC // MLA_HPS, s // tq),
        in_specs=[pl.BlockSpec((None, tq, gw), lambda bi, h, i: (bi, i, h)),
                  pl.BlockSpec((None, s, gw), lambda bi, h, i: (bi, 0, h)),
                  pl.BlockSpec((None, s, gw), lambda bi, h, i: (bi, 0, h))],
        out_specs=pl.BlockSpec((None, tq, gw // 2), lambda bi, h, i: (bi, i, h)),
        out_shape=jax.ShapeDtypeStruct((b, s, MIX_W), BF16),
        compiler_params=_params("parallel", "parallel", "arbitrary"),
        name="mla_attn",
    )(q, k, v)


MERGE_TM = 1024


def _merge_kernel(x_ref, ya_ref, yb_ref, yc_ref, wm_ref, wa_ref, wb_ref, wc_ref, wo_ref,
                  g_ref, b_ref, o_ref, *, alpha):
    x = x_ref[...]
    xb = x.astype(BF16)
    mixed = jnp.zeros(x.shape, F32)
    for idx, (y_ref, w_ref) in enumerate(((ya_ref, wa_ref), (yb_ref, wb_ref), (yc_ref, wc_ref))):
        gate = jax.nn.sigmoid(_dot(xb, wm_ref[:, idx * D_MODEL:(idx + 1) * D_MODEL]))
        mixed = mixed + gate * _dot(y_ref[...], w_ref[...])
    y = alpha * x + _dot(mixed.astype(BF16), wo_ref[...])
    o_ref[...] = _ln(y, g_ref[...], b_ref[...])


def _merge(x2, ya, yb, yc, wm, wa, wb, wc, wo, lng, lnb, l, alpha):
    n = x2.shape[0]
    tm = min(MERGE_TM, n)

    def rows(width):
        return pl.BlockSpec((tm, width), lambda m: (m, 0))

    return pl.pallas_call(
        functools.partial(_merge_kernel, alpha=alpha),
        grid=(n // tm,),
        in_specs=[rows(D_MODEL), rows(MIX_W), rows(MIX_W), rows(MIX_W),
                  _resident((None, D_MODEL, 3 * D_MODEL), lambda m: (l, 0, 0)),
                  _resident((None, None, MIX_W, D_MODEL), lambda m: (l, 0, 0, 0)),
                  _resident((None, None, MIX_W, D_MODEL), lambda m: (l, 1, 0, 0)),
                  _resident((None, None, MIX_W, D_MODEL), lambda m: (l, 2, 0, 0)),
                  _resident((None, D_MODEL, D_MODEL), lambda m: (l, 0, 0)),
                  _resident((None, None, 1, D_MODEL), lambda m: (l, 1, 0, 0)),
                  _resident((None, None, 1, D_MODEL), lambda m: (l, 1, 0, 0))],
        out_specs=rows(D_MODEL),
        out_shape=jax.ShapeDtypeStruct((n, D_MODEL), F32),
        compiler_params=_params("parallel"),
        name="merge",
    )(x2, ya, yb, yc, wm, wa, wb, wc, wo, lng, lnb)


XA_TM = 1024


def _xattn_kernel(x_ref, wq_ref, k_ref, v_ref, wo_ref, g_ref, b_ref, o_ref, *, alpha):
    x = x_ref[...]
    xb = x.astype(BF16)
    q = (_dot(xb, wq_ref[...]) * (XA_DH ** -0.5 * LOG2E)).astype(BF16)
    ones = jnp.ones((k_ref.shape[0], LANE), BF16)
    heads = []
    for h in range(XA_HEADS):
        hs = slice(h * XA_DH, (h + 1) * XA_DH)
        s = _dot_nt(q[:, hs], k_ref[:, hs])
        e = jnp.exp2(s - jnp.max(s, axis=-1, keepdims=True)).astype(BF16)
        den = _dot(e, ones)
        den = jnp.concatenate([den] * (XA_DH // LANE), axis=1)
        heads.append((_dot(e, v_ref[:, hs]) / den).astype(BF16))
    att = jnp.concatenate(heads, axis=1)
    y = alpha * x + _dot(att, wo_ref[...])
    o_ref[...] = _ln(y, g_ref[...], b_ref[...])


def _xattn(x3, wq, k, v, wo, lng, lnb, l, alpha):
    b, s, _ = x3.shape
    m_len = k.shape[1]
    tm = XA_TM
    return pl.pallas_call(
        functools.partial(_xattn_kernel, alpha=alpha),
        grid=(b, s // tm),
        in_specs=[pl.BlockSpec((None, tm, D_MODEL), lambda bi, m: (bi, m, 0)),
                  _resident((None, D_MODEL, D_MODEL), lambda bi, m: (l, 0, 0)),
                  pl.BlockSpec((None, m_len, D_MODEL), lambda bi, m: (bi, 0, 0)),
                  pl.BlockSpec((None, m_len, D_MODEL), lambda bi, m: (bi, 0, 0)),
                  _resident((None, D_MODEL, D_MODEL), lambda bi, m: (l, 0, 0)),
                  _resident((None, None, 1, D_MODEL), lambda bi, m: (l, 2, 0, 0)),
                  _resident((None, None, 1, D_MODEL), lambda bi, m: (l, 2, 0, 0))],
        out_specs=pl.BlockSpec((None, tm, D_MODEL), lambda bi, m: (bi, m, 0)),
        out_shape=jax.ShapeDtypeStruct((b, s, D_MODEL), F32),
        compiler_params=_params("parallel", "parallel"),
        name="xattn",
    )(x3, wq, k, v, wo, lng, lnb)


def _pad_heads(w, n_heads, dh):
    depth, k, _ = w.shape
    w = w.reshape(depth, k, n_heads, dh)
    return jnp.pad(w, ((0, 0), (0, 0), (0, 0), (0, LANE - dh))).reshape(depth, k, n_heads * LANE)


def _rope_tables(positions, rot, off):
    half = rot // 2
    inv = ROPE_THETA ** (-jnp.arange(half, dtype=F32) / half)
    ang = positions.astype(F32)[..., None] * inv
    cos, sin = jnp.cos(ang), jnp.sin(ang)
    shape = positions.shape
    zeros = lambda n: jnp.zeros(shape + (n,), F32)
    cos_t = jnp.concatenate([jnp.ones(shape + (off,), F32), cos, cos,
                             jnp.ones(shape + (LANE - off - rot,), F32)], axis=-1)
    sin_lo = jnp.concatenate([zeros(off), -sin, zeros(LANE - off - half)], axis=-1)
    sin_hi = jnp.concatenate([zeros(off + half), sin, zeros(LANE - off - rot)], axis=-1)
    return cos_t, sin_lo, sin_hi


def _overlap_matrix(s_len):
    n_cmp = (s_len - CMP_L) // CMP_D + 1
    n_slc = s_len // SEL_L
    start = np.arange(n_cmp) * CMP_D
    j = np.arange(n_slc)
    ov = np.clip(np.minimum(start[:, None] + CMP_L, (j[None, :] + 1) * SEL_L)
                 - np.maximum(start[:, None], j[None, :] * SEL_L), 0, None) / CMP_L
    out = np.zeros((N_SLC, N_CMP), np.float32)
    out[:n_slc, :n_cmp] = ov.T
    place = np.zeros((N_SLC, LANE), np.float32)
    place[np.arange(N_SLC), SEL_LANE0 + np.arange(N_SLC)] = 1.0
    return jnp.asarray(out), jnp.asarray(place)


def _stacked_weights(w_in, nsa_cmp_pos, nsa_cmp_w1, nsa_cmp_w2, mla_w_uq, mla_w_ukv):
    depth = w_in.shape[0]
    part = [w_in[:, :, IN_OFF[i]:IN_OFF[i + 1]] for i in range(len(IN_SIZES))]
    (a_q, a_kc, a_vc, a_ks, a_vs, a_kw, a_vw, a_gate, b_q, b_f, b_i, b_g,
     c_q, c_kv, c_kr, merge) = part
    gate = _pad_heads(a_gate, NSA_KV, NSA_R * 3)
    kv_parts = dict(kc=a_kc, vc=a_vc, ks=a_ks, vs=a_vs, kw=a_kw, vw=a_vw)
    wa = jnp.concatenate([_pad_heads(a_q, NSA_HEADS, NSA_DH)]
                         + [kv_parts[name] for name in A_KV_ORDER] + [gate], axis=2).astype(BF16)
    wb = jnp.stack([w.reshape(depth, D_MODEL, HB, HK) for w in (b_q, b_f, b_i, b_g)], axis=3)
    wb = wb.transpose(0, 2, 1, 3, 4).reshape(depth, HB, D_MODEL, 4 * HK).astype(BF16)

    kr_pad = jnp.pad(c_kr, ((0, 0), (0, 0), (NOPE, LANE - NOPE - ROPE_D)))
    wc = jnp.concatenate([c_q, c_kv, kr_pad], axis=2).astype(BF16)
    wuq = _pad_heads(mla_w_uq, HC, NOPE + ROPE_D).astype(BF16)
    ukv = mla_w_ukv.reshape(depth, KV_RANK, HC, NOPE + VD)
    wuk = _pad_heads(ukv[..., :NOPE].reshape(depth, KV_RANK, HC * NOPE), HC, NOPE).astype(BF16)
    wuv = _pad_heads(ukv[..., NOPE:].reshape(depth, KV_RANK, HC * VD), HC, VD).astype(BF16)

    w1 = nsa_cmp_w1.reshape(depth, 2, CMP_L, NSA_DH, CMP_HID)
    w1p = jnp.pad(w1, ((0, 0), (0, 0), (0, 0), (0, LANE - NSA_DH), (0, 0)))
    w1p = w1p.reshape(depth, 2, CMP_L * LANE, CMP_HID)
    pe = jnp.pad(nsa_cmp_pos, ((0, 0), (0, 0), (0, 0), (0, LANE - NSA_DH)))
    pe8 = jnp.broadcast_to(pe.reshape(depth, 2, 1, CMP_L * LANE), (depth, 2, 8, CMP_L * LANE))
    w2p = jnp.pad(nsa_cmp_w2, ((0, 0), (0, 0), (0, 0), (0, LANE - NSA_DH)))
    return dict(wa=wa, wb=wb, wc=wc, wuq=wuq, wuk=wuk, wuv=wuv,
                w1p=w1p.astype(BF16), pe8=pe8.astype(BF16), w2p=w2p.astype(BF16),
                wm=merge.astype(BF16))


def kernel(x, mem, positions, ln_g, ln_b, ffn_w1, ffn_w3, ffn_w2, w_in, nsa_cmp_pos,
           nsa_cmp_w1, nsa_cmp_w2, hgrn_lb_logits, hgrn_norm_g, mla_q_norm_g, mla_w_uq,
           mla_kv_norm_g, mla_w_ukv, w_branch, w_out, xa_wq, xa_wk, xa_wv, xa_wo):
    b, s, d = x.shape
    depth = ln_g.shape[0]
    n = b * s
    alpha = (2.0 * depth) ** 0.25

    lng = ln_g.reshape(depth, 4, 1, d)
    lnb = ln_b.reshape(depth, 4, 1, d)
    w1 = ffn_w1.astype(BF16)
    w3 = ffn_w3.astype(BF16)
    w2 = (0.5 * ffn_w2).astype(BF16)
    wo = w_out.astype(BF16)
    wbr = w_branch.astype(BF16)
    xq = xa_wq.astype(BF16)
    xk = xa_wk.astype(BF16)
    xv = xa_wv.astype(BF16)
    xo = xa_wo.astype(BF16)
    p_lb = jax.nn.softmax(hgrn_lb_logits.astype(F32), axis=0)
    lower = (jnp.cumsum(p_lb, axis=0) - p_lb[0:1]).reshape(depth, HB, 1, HK)

    rope_a = _rope_tables(positions, NSA_ROT, 0)
    rope_c = _rope_tables(positions, ROPE_D, NOPE)
    ovl, place = _overlap_matrix(s)
    mem2 = mem.reshape(b * mem.shape[1], d)
    w = _stacked_weights(w_in, nsa_cmp_pos, nsa_cmp_w1, nsa_cmp_w2, mla_w_uq, mla_w_ukv)
    norm_b = hgrn_norm_g.reshape(depth, 1, HV)
    norm_q = mla_q_norm_g.reshape(depth, 1, Q_RANK)
    norm_kv = mla_kv_norm_g.reshape(depth, 1, KV_RANK)

    x2 = x.reshape(n, d)
    for l in range(depth):
        x2 = _ffn_ln(x2, w1, w3, w2, lng, lnb, l, 0, 0, alpha)
        x3 = x2.reshape(b, s, d)

        q, qr, kc, vc, ks, vs, kw, vw, gates = _proja(x3, w["wa"], l, *rope_a)
        kcmp, vcmp = _cmp(kc, vc, w["w1p"], w["pe8"], w["w2p"], l)
        ya = _nsa_attn(q, qr, kcmp, vcmp, ks, vs, kw, vw, gates, ovl, place)

        yb = _hgrn(x3, w["wb"], lower, norm_b, l)

        mq, mk, mv = _mlap(x3, w["wc"], norm_q, norm_kv, w["wuq"], w["wuk"], w["wuv"], l, *rope_c)
        yc = _mla_attn(mq, mk, mv)

        x2 = _merge(x2, ya.reshape(n, MIX_W), yb.reshape(n, MIX_W), yc.reshape(n, MIX_W),
                    w["wm"], wbr, wbr, wbr, wo, lng, lnb, l, alpha)

        xk_l = _proj(mem2, xk, l, BF16).reshape(b, -1, d)
        xv_l = _proj(mem2, xv, l, BF16).reshape(b, -1, d)
        x2 = _xattn(x2.reshape(b, s, d), xq, xk_l, xv_l, xo, lng, lnb, l, alpha).reshape(n, d)

        x2 = _ffn_ln(x2, w1, w3, w2, lng, lnb, l, 1, 3, alpha)
    return x2.reshape(b, s, d)
```

```python
import functools
import math

import numpy as np
import jax
import jax.numpy as jnp
from jax import lax
from jax.experimental import pallas as pl
from jax.experimental.pallas import tpu as pltpu

F32 = jnp.float32
BF16 = jnp.bfloat16

D_MODEL = 1024
MIX_W = D_MODEL // 2
NSA_DH = 64
NSA_HEADS = 8
NSA_KV = 2
NSA_R = 4
NSA_ROT = 16
CMP_L = 32
CMP_D = 16
CMP_HID = 256
SEL_L = 64
N_SEL = 8
WINDOW = 256
HB = 4
HK = 128
HV = 128
HGRN_CHUNK = 64
HGRN_SUB = 4
HGRN_PARTS = 4
HC = 8
NOPE = 64
ROPE_D = 32
VD = 64
Q_RANK = 384
KV_RANK = 256
XA_HEADS = 4
XA_DH = D_MODEL // XA_HEADS
D_FF = 2816
ROPE_THETA = 500000.0
LN_EPS = 1e-5
RMS_EPS = 1e-6
NEG = -1e30
BIG = 1e9
F_MIN = 1e-20

LOG2E = 1.4426950408889634
LANE = 128
SEL_LANE0 = NSA_DH
DEN_LANE = 64
VMEM_LIMIT = 56 * 1024 * 1024

IN_SIZES = (512, 128, 128, 128, 128, 128, 128, 24, 512, 512, 512, 512, 384, 256, 32, 3072)
IN_OFF = tuple(int(v) for v in np.concatenate([[0], np.cumsum(IN_SIZES)]))


def _dot(a, b):
    return jnp.dot(a, b, preferred_element_type=F32)


def _dot_nt(a, b):
    return lax.dot_general(a, b, (((1,), (1,)), ((), ())), preferred_element_type=F32)


def _ln(y, g, b):
    mu = jnp.mean(y, axis=-1, keepdims=True)
    yc = y - mu
    var = jnp.mean(yc * yc, axis=-1, keepdims=True)
    return yc * lax.rsqrt(var + LN_EPS) * g + b


def _pack_pair(a, b):
    lane = lax.broadcasted_iota(jnp.int32, a.shape, a.ndim - 1)
    return jnp.where(lane < LANE // 2, a, pltpu.roll(b, LANE // 2, axis=a.ndim - 1))


def _params(*sem):
    return pltpu.CompilerParams(dimension_semantics=sem, vmem_limit_bytes=VMEM_LIMIT)


def _resident(shape, index_map):
    return pl.BlockSpec(shape, index_map, pipeline_mode=pl.Buffered(1))


FFN_TM = 2048
FFN_TF = 256


def _ffn_kernel(x_ref, w1_ref, w3_ref, w2h_ref, g_ref, b_ref, o_ref, acc_ref, xb_ref, *, alpha):
    j = pl.program_id(1)

    @pl.when(j == 0)
    def _():
        xb_ref[...] = x_ref[...].astype(BF16)
        acc_ref[...] = alpha * x_ref[...]

    xb = xb_ref[...]
    h1 = _dot(xb, w1_ref[...])
    h3 = _dot(xb, w3_ref[...])
    h = (h1 * jax.nn.sigmoid(h1)) * h3
    acc_ref[...] += _dot(h.astype(BF16), w2h_ref[...])

    @pl.when(j == pl.num_programs(1) - 1)
    def _():
        o_ref[...] = _ln(acc_ref[...], g_ref[...], b_ref[...])


def _ffn_ln(x2, w1, w3, w2h, lng, lnb, l, which, ln_idx, alpha):
    n = x2.shape[0]
    tm = min(FFN_TM, n)
    grid = (n // tm, D_FF // FFN_TF)
    return pl.pallas_call(
        functools.partial(_ffn_kernel, alpha=alpha),
        grid=grid,
        in_specs=[
            pl.BlockSpec((tm, D_MODEL), lambda m, j: (m, 0)),
            pl.BlockSpec((None, None, D_MODEL, FFN_TF), lambda m, j: (l, which, 0, j)),
            pl.BlockSpec((None, None, D_MODEL, FFN_TF), lambda m, j: (l, which, 0, j)),
            pl.BlockSpec((None, None, FFN_TF, D_MODEL), lambda m, j: (l, which, j, 0)),
            pl.BlockSpec((None, None, 1, D_MODEL), lambda m, j: (l, ln_idx, 0, 0)),
            pl.BlockSpec((None, None, 1, D_MODEL), lambda m, j: (l, ln_idx, 0, 0)),
        ],
        out_specs=pl.BlockSpec((tm, D_MODEL), lambda m, j: (m, 0)),
        out_shape=jax.ShapeDtypeStruct((n, D_MODEL), F32),
        scratch_shapes=[pltpu.VMEM((tm, D_MODEL), F32), pltpu.VMEM((tm, D_MODEL), BF16)],
        compiler_params=_params("parallel", "arbitrary"),
        name="ffn_ln",
    )(x2, w1, w3, w2h, lng, lnb)


def _proj_kernel(x_ref, w_ref, o_ref):
    o_ref[...] = _dot(x_ref[...].astype(BF16), w_ref[...]).astype(o_ref.dtype)


def _proj(x2, w, l, out_dtype, tm=1024, tn=2048):
    n, k = x2.shape
    c = w.shape[2]
    tm = min(tm, n)
    tn = min(tn, c)
    return pl.pallas_call(
        _proj_kernel,
        grid=(n // tm, c // tn),
        in_specs=[pl.BlockSpec((tm, k), lambda m, j: (m, 0)),
                  pl.BlockSpec((None, k, tn), lambda m, j: (l, 0, j))],
        out_specs=pl.BlockSpec((tm, tn), lambda m, j: (m, j)),
        out_shape=jax.ShapeDtypeStruct((n, c), out_dtype),
        compiler_params=_params("parallel", "arbitrary"),
        name="proj",
    )(x2, w)


NSA_QW = NSA_HEADS * LANE
NSA_KW = NSA_KV * LANE
A_Q = 0
A_KV = NSA_QW
A_KV_ORDER = ("kc", "vc", "ks", "vs", "kw", "vw")
A_GATE = A_KV + len(A_KV_ORDER) * LANE
A_COLS = A_GATE + NSA_KW
PROJA_TM = 1024


def _rope_lanes(y, cos, sin_lo, sin_hi, half):
    return (y * cos + pltpu.roll(y, LANE - half, axis=1) * sin_lo
            + pltpu.roll(y, half, axis=1) * sin_hi)


def _proja_kernel(x_ref, w_ref, cos_ref, slo_ref, shi_ref,
                  q_ref, qr_ref, kc_ref, vc_ref, ks_ref, vs_ref, kw_ref, vw_ref, g_ref):
    xb = x_ref[...].astype(BF16)
    cos = cos_ref[...]
    sin_lo = slo_ref[...]
    sin_hi = shi_ref[...]
    half = NSA_ROT // 2
    tm = x_ref.shape[0]

    def mm(c0, width):
        return _dot(xb, w_ref[:, c0:c0 + width])

    qs = NSA_DH ** -0.5 * LOG2E
    y = mm(A_Q, NSA_QW)
    q_ref[...] = (y * qs).astype(BF16)
    for h in range(NSA_HEADS):
        hs = slice(h * LANE, (h + 1) * LANE)
        qr_ref[:, hs] = (_rope_lanes(y[:, hs], cos, sin_lo, sin_hi, half) * qs).astype(BF16)

    lane = lax.broadcasted_iota(jnp.int32, (tm, LANE), 1)
    first = lane < NSA_DH
    tok = pl.program_id(1) * tm + lax.broadcasted_iota(jnp.int32, (tm, LANE), 0)
    blk_flag = jnp.where(lane - SEL_LANE0 == tok // SEL_L, 1.0, 0.0)
    den_flag = jnp.where(lane == DEN_LANE, 1.0, 0.0)

    def packed(table):
        return jnp.where(first, table, pltpu.roll(table, NSA_DH, axis=1))

    def put(ref, y, flag=None):
        for g, yg in enumerate((y, pltpu.roll(y, NSA_DH, axis=1))):
            yg = jnp.where(first, yg, 0.0)
            ref[g] = (yg if flag is None else yg + flag).astype(ref.dtype)

    kv = mm(A_KV, len(A_KV_ORDER) * LANE)
    part = {name: kv[:, i * LANE:(i + 1) * LANE] for i, name in enumerate(A_KV_ORDER)}
    cos_p, lo_p, hi_p = packed(cos), packed(sin_lo), packed(sin_hi)
    put(kc_ref, part["kc"])
    put(vc_ref, part["vc"])
    put(ks_ref, _rope_lanes(part["ks"], cos_p, lo_p, hi_p, half), blk_flag)
    put(vs_ref, part["vs"], den_flag)
    put(kw_ref, _rope_lanes(part["kw"], cos_p, lo_p, hi_p, half))
    put(vw_ref, part["vw"], den_flag)
    g_ref[...] = mm(A_GATE, NSA_KW)


def _proja(x3, wa, l, cos_a, slo_a, shi_a):
    b, s, _ = x3.shape
    tm = PROJA_TM
    kv_shape = jax.ShapeDtypeStruct((b, NSA_KV, s, LANE), BF16)
    cmp_shape = jax.ShapeDtypeStruct((b, NSA_KV, s, LANE), F32)
    kv_spec = pl.BlockSpec((None, NSA_KV, tm, LANE), lambda bi, m: (bi, 0, m, 0))
    return pl.pallas_call(
        _proja_kernel,
        grid=(b, s // tm),
        in_specs=[
            pl.BlockSpec((None, tm, D_MODEL), lambda bi, m: (bi, m, 0)),
            _resident((None, D_MODEL, A_COLS), lambda bi, m: (l, 0, 0)),
            pl.BlockSpec((None, tm, LANE), lambda bi, m: (bi, m, 0)),
            pl.BlockSpec((None, tm, LANE), lambda bi, m: (bi, m, 0)),
            pl.BlockSpec((None, tm, LANE), lambda bi, m: (bi, m, 0)),
        ],
        out_specs=[
            pl.BlockSpec((None, tm, NSA_QW), lambda bi, m: (bi, m, 0)),
            pl.BlockSpec((None, tm, NSA_QW), lambda bi, m: (bi, m, 0)),
            kv_spec, kv_spec, kv_spec, kv_spec, kv_spec, kv_spec,
            pl.BlockSpec((None, tm, NSA_KW), lambda bi, m: (bi, m, 0)),
        ],
        out_shape=[
            jax.ShapeDtypeStruct((b, s, NSA_QW), BF16),
            jax.ShapeDtypeStruct((b, s, NSA_QW), BF16),
            cmp_shape, cmp_shape, kv_shape, kv_shape, kv_shape, kv_shape,
            jax.ShapeDtypeStruct((b, s, NSA_KW), F32),
        ],
        compiler_params=_params("parallel", "parallel"),
        name="nsa_proj",
    )(x3, wa, cos_a, slo_a, shi_a)


N_CMP = 128
CMP_HALF = CMP_D * LANE


def _gelu_tanh(x):
    return 0.5 * x * (1.0 + jnp.tanh(0.7978845608028654 * (x + 0.044715 * x * x * x)))


def _cmp_kernel(zk_ref, zv_ref, w1_ref, pe_ref, w2_ref, ok_ref, ov_ref):
    for which, (z_ref, o_ref) in enumerate(((zk_ref, ok_ref), (zv_ref, ov_ref))):
        first = jnp.zeros((N_CMP, CMP_HID), F32)
        second = jnp.zeros((N_CMP, CMP_HID), F32)
        for j in range(CMP_D):
            zj = z_ref[pl.ds(j, N_CMP, stride=CMP_D), :].astype(BF16)
            first = first + _dot(zj, w1_ref[which, j * LANE:(j + 1) * LANE, :])
            second = second + _dot(zj, w1_ref[which, CMP_HALF + j * LANE:CMP_HALF + (j + 1) * LANE, :])
        bias = _dot(pe_ref[which], w1_ref[which])[0:1]
        pre = first + pltpu.roll(second, N_CMP - 1, axis=0) + bias
        h = _gelu_tanh(pre)
        o_ref[...] = _dot(h.astype(BF16), w2_ref[which]).astype(BF16)


def _cmp(zk, zv, w1p, pe8, w2p, l):
    b, _, s, _ = zk.shape
    z_spec = pl.BlockSpec((None, None, s, LANE), lambda bi, g: (bi, g, 0, 0))
    o_spec = pl.BlockSpec((None, None, N_CMP, LANE), lambda bi, g: (bi, g, 0, 0))
    o_shape = jax.ShapeDtypeStruct((b, NSA_KV, N_CMP, LANE), BF16)
    return pl.pallas_call(
        _cmp_kernel,
        grid=(b, NSA_KV),
        in_specs=[
            z_spec, z_spec,
            _resident((None, 2, 2 * CMP_HALF, CMP_HID), lambda bi, g: (l, 0, 0, 0)),
            _resident((None, 2, 8, 2 * CMP_HALF), lambda bi, g: (l, 0, 0, 0)),
            _resident((None, 2, CMP_HID, LANE), lambda bi, g: (l, 0, 0, 0)),
        ],
        out_specs=[o_spec, o_spec],
        out_shape=[o_shape, o_shape],
        compiler_params=_params("parallel", "parallel"),
        name="nsa_cmp",
    )(zk, zv, w1p, pe8, w2p)


NSA_TQ = 256
NSA_TK = 512
N_SLC = 32
WIN_SLAB = WINDOW + NSA_TQ


def _nsa_kernel(q_ref, qr_ref, kc_ref, vc_ref, ks_ref, vs_ref, kw_ref, vw_ref, gt_ref, ovl_ref,
                place_ref, o_ref):
    tq, tk, r_heads = NSA_TQ, NSA_TK, NSA_R
    i = pl.program_id(2)
    t0 = i * tq
    tpos = t0 + lax.broadcasted_iota(jnp.int32, (tq, 1), 0)
    lane = lax.broadcasted_iota(jnp.int32, (1, LANE), 1)

    def stack(ref):
        return jnp.concatenate([ref[:, r * LANE:(r + 1) * LANE] for r in range(r_heads)], axis=0)

    gt = jax.nn.sigmoid(gt_ref[...])

    def gate(c):
        return jnp.stack([gt[:, 3 * r + c:3 * r + c + 1] for r in range(r_heads)], axis=0)

    q4w = stack(qr_ref)
    w0 = pl.multiple_of(jnp.maximum(t0 - WINDOW, 0), LANE)
    kw = kw_ref[pl.ds(w0, WIN_SLAB), :]
    vw = vw_ref[pl.ds(w0, WIN_SLAB), :]
    kpos = w0 + lax.broadcasted_iota(jnp.int32, (1, WIN_SLAB), 1)
    w_bias = jnp.where((kpos <= tpos) & (kpos > tpos - WINDOW), 0.0, NEG)
    s = _dot_nt(q4w, kw).reshape(r_heads, tq, WIN_SLAB) + w_bias[None]
    p = jnp.exp2(s - jnp.max(s, axis=-1, keepdims=True))
    o_win = _dot(p.reshape(r_heads * tq, WIN_SLAB).astype(BF16), vw).reshape(r_heads, tq, LANE)
    o_part = gate(2) * (o_win / o_win[:, :, DEN_LANE:DEN_LANE + 1])

    cmask = ((lane * CMP_D + (CMP_L - 1) <= tpos) & (lane < N_CMP - 1))[None]
    s = _dot_nt(stack(q_ref), kc_ref[...]).reshape(r_heads, tq, N_CMP)
    s = jnp.where(cmask, s, NEG)
    e = jnp.where(cmask, jnp.exp2(s - jnp.max(s, axis=-1, keepdims=True)), 0.0)
    den = jnp.sum(e, axis=-1, keepdims=True)
    p = e / jnp.where(den > 0.0, den, 1.0)
    psum = jnp.sum(p, axis=0)
    o_cmp = _dot(p.reshape(r_heads * tq, N_CMP).astype(BF16), vc_ref[...])
    o_part = o_part + gate(0) * o_cmp.reshape(r_heads, tq, LANE)

    imp = lax.dot_general(ovl_ref[...], psum, (((1,), (1,)), ((), ())),
                          precision=lax.Precision.HIGHEST, preferred_element_type=F32)
    blk = lax.broadcasted_iota(jnp.int32, (N_SLC, 1), 0)
    tpos_l = t0 + lax.broadcasted_iota(jnp.int32, (1, tq), 1)
    cur = tpos_l // SEL_L
    valid = blk * SEL_L <= tpos_l
    forced = (blk == 0) | (blk == cur) | (blk == cur - 1)
    score = jnp.where(valid & forced, BIG, jnp.where(valid, imp, -BIG))
    beats = []
    for j in range(N_SLC):
        sj = score[j:j + 1, :]
        beats.append(jnp.where((sj > score) | ((sj == score) & (j < blk)), 1.0, 0.0))
    while len(beats) > 1:
        beats = [a + b for a, b in zip(beats[0::2], beats[1::2])]
    sel_t = jnp.where((beats[0] < N_SEL) & valid, 1.0, 0.0)
    sel_q = lax.dot_general(sel_t, place_ref[...], (((0,), (0,)), ((), ())),
                            preferred_element_type=F32)
    in_flags = jnp.where((lane >= SEL_LANE0) & (lane < SEL_LANE0 + N_SLC), 1.0, 0.0)
    q_bias = (sel_q - in_flags) * (-NEG)
    q4 = (q4w.astype(F32).reshape(r_heads, tq, LANE) + q_bias[None]).astype(BF16)
    q4 = q4.reshape(r_heads * tq, LANE)

    def sel_step(c, carry, causal):
        m, acc = carry
        k0 = pl.multiple_of(c * tk, tk)
        k = ks_ref[pl.ds(k0, tk), :]
        v = vs_ref[pl.ds(k0, tk), :]
        s = _dot_nt(q4, k).reshape(r_heads, tq, tk)
        if causal:
            kpos = k0 + lax.broadcasted_iota(jnp.int32, (1, tk), 1)
            s = jnp.where((kpos <= tpos)[None], s, NEG)
        m_new = jnp.maximum(m, jnp.max(s, axis=-1, keepdims=True))
        p = jnp.exp2(s - m_new)
        pv = _dot(p.reshape(r_heads * tq, tk).astype(BF16), v).reshape(r_heads, tq, LANE)
        return m_new, jnp.exp2(m - m_new) * acc + pv

    last = (t0 + tq - 1) // tk
    carry = (jnp.full((r_heads, tq, 1), NEG, F32), jnp.zeros((r_heads, tq, LANE), F32))
    carry = lax.fori_loop(0, last, functools.partial(sel_step, causal=False), carry)
    _, acc_s = sel_step(last, carry, True)
    o = o_part + gate(1) * (acc_s / acc_s[:, :, DEN_LANE:DEN_LANE + 1])
    for r in range(0, r_heads, 2):
        o_ref[:, (r // 2) * LANE:(r // 2 + 1) * LANE] = _pack_pair(o[r], o[r + 1]).astype(BF16)


def _nsa_attn(q, qr, kcmp, vcmp, ks, vs, kw, vw, gates, ovl, place):
    b, s, _ = q.shape
    tq = NSA_TQ
    gw = NSA_R * LANE
    q_spec = pl.BlockSpec((None, tq, gw), lambda bi, g, i: (bi, i, g))
    c_spec = pl.BlockSpec((None, None, N_CMP, LANE), lambda bi, g, i: (bi, g, 0, 0))
    kv_spec = pl.BlockSpec((None, None, s, LANE), lambda bi, g, i: (bi, g, 0, 0))
    return pl.pallas_call(
        _nsa_kernel,
        grid=(b, NSA_KV, s // tq),
        in_specs=[q_spec, q_spec, c_spec, c_spec, kv_spec, kv_spec, kv_spec, kv_spec,
                  pl.BlockSpec((None, tq, LANE), lambda bi, g, i: (bi, i, g)),
                  _resident((N_SLC, N_CMP), lambda bi, g, i: (0, 0)),
                  _resident((N_SLC, LANE), lambda bi, g, i: (0, 0))],
        out_specs=pl.BlockSpec((None, tq, gw // 2), lambda bi, g, i: (bi, i, g)),
        out_shape=jax.ShapeDtypeStruct((b, s, MIX_W), BF16),
        compiler_params=_params("parallel", "parallel", "arbitrary"),
        name="nsa_attn",
    )(q, qr, kcmp, vcmp, ks, vs, kw, vw, gates, ovl, place)


def _hgrn_kernel(x_ref, w_ref, lb_ref, ng_ref, o_ref, st_ref):
    s_len = x_ref.shape[0]
    c_len, sub_len = HGRN_CHUNK, HGRN_SUB
    p_len = s_len // HGRN_PARTS
    nc = p_len // c_len
    lb = lb_ref[...]
    t = lax.broadcasted_iota(jnp.int32, (p_len, 1), 0)
    t_sub = t % sub_len
    row = lax.broadcasted_iota(jnp.int32, (nc, c_len, c_len), 1)
    col = lax.broadcasted_iota(jnp.int32, (nc, c_len, c_len), 2)
    tri = jnp.where(col <= row, 1.0, 0.0)

    local = []
    for part in range(HGRN_PARTS):
        proj = _dot(x_ref[pl.ds(part * p_len, p_len), :].astype(BF16), w_ref[...])
        q = proj[:, 0:HK]
        z = proj[:, HK:2 * HK]
        v = proj[:, 2 * HK:2 * HK + HV]
        go = proj[:, 2 * HK + HV:2 * HK + 2 * HV]

        sig = jax.nn.sigmoid(z)
        f = lb + (1.0 - lb) * sig
        lf = jnp.log(jnp.maximum(f, F_MIN))
        k = (1.0 - lb) * (1.0 - sig)

        b3 = jnp.einsum('cts,csd->ctd', tri, lf.reshape(nc, c_len, HK),
                        precision=lax.Precision.HIGHEST, preferred_element_type=F32)
        b = b3.reshape(p_len, HK)
        k3 = k.reshape(nc, c_len, HK)
        vb3 = v.astype(BF16).reshape(nc, c_len, HV)

        a_intra = None
        h = c_len // 2
        while h >= sub_len:
            blk = b.reshape(p_len // (2 * h), 2 * h, HK)
            e = jnp.exp(-jnp.abs(blk[:, h - 1:h, :] - blk)).reshape(p_len, HK)
            upper = (t // h) % 2 == 1
            lq = jnp.where(upper, q * e, 0.0).astype(BF16).reshape(nc, c_len, HK)
            rk = jnp.where(upper, 0.0, k * e).astype(BF16).reshape(nc, c_len, HK)
            a = jnp.einsum('ctk,csk->cts', lq, rk, preferred_element_type=F32)
            if 2 * h < c_len:
                a = jnp.where(row // (2 * h) == col // (2 * h), a, 0.0)
            a_intra = a if a_intra is None else a_intra + a
            h //= 2
        o = jnp.einsum('cts,csd->ctd', a_intra.astype(BF16), vb3,
                       preferred_element_type=F32).reshape(p_len, HV)

        o = o + jnp.sum(q * k, axis=-1, keepdims=True) * v
        for d in range(1, sub_len):
            e = jnp.exp(jnp.where(t_sub >= d, b - pltpu.roll(b, d, axis=0), NEG))
            w = jnp.sum(q * pltpu.roll(k, d, axis=0) * e, axis=-1, keepdims=True)
            o = o + w * pltpu.roll(v, d, axis=0)

        b_last = b3[:, c_len - 1:c_len, :]
        kd3 = (k3 * jnp.exp(b_last - b3)).astype(BF16)
        upd = jnp.einsum('csv,csk->cvk', vb3, kd3, preferred_element_type=F32)
        qe3 = (q * jnp.exp(b)).astype(BF16).reshape(nc, c_len, HK)
        local.append((o, go, qe3, upd, jnp.exp(b_last)))

    state = jnp.zeros((HV, HK), F32)
    for part, (_, _, _, upd, dec) in enumerate(local):
        for c in range(nc):
            st_ref[part * nc + c] = state.astype(BF16)
            state = state * dec[c] + upd[c]

    for part, (o, go, qe3, _, _) in enumerate(local):
        o = o + jnp.einsum('ctk,cvk->ctv', qe3, st_ref[pl.ds(part * nc, nc)],
                           preferred_element_type=F32).reshape(p_len, HV)
        o = o * lax.rsqrt(jnp.mean(o * o, axis=-1, keepdims=True) + RMS_EPS) * ng_ref[...]
        o_ref[pl.ds(part * p_len, p_len), :] = (o * (go * jax.nn.sigmoid(go))).astype(BF16)


def _hgrn(x3, wb, lb, ng, l):
    b, s, d = x3.shape
    return pl.pallas_call(
        _hgrn_kernel,
        grid=(b, HB),
        in_specs=[pl.BlockSpec((None, s, d), lambda bi, h: (bi, 0, 0)),
                  pl.BlockSpec((None, None, d, 4 * HK), lambda bi, h: (l, h, 0, 0)),
                  pl.BlockSpec((None, None, 1, HK), lambda bi, h: (l, h, 0, 0)),
                  pl.BlockSpec((None, 1, HV), lambda bi, h: (l, 0, 0))],
        out_specs=pl.BlockSpec((None, s, HV), lambda bi, h: (bi, 0, h)),
        out_shape=jax.ShapeDtypeStruct((b, s, HB * HV), BF16),
        scratch_shapes=[pltpu.VMEM((s // HGRN_CHUNK, HV, HK), BF16)],
        compiler_params=_params("parallel", "arbitrary"),
        name="hgrn",
    )(x3, wb, lb, ng)


MLA_W = HC * LANE
C_CQ, C_CKV, C_KR = 0, Q_RANK, Q_RANK + KV_RANK
C_COLS = C_KR + LANE
MLAP_TM = 1024


def _rms(x, g):
    return x * lax.rsqrt(jnp.mean(x * x, axis=-1, keepdims=True) + RMS_EPS) * g


def _mlap_kernel(x_ref, wc_ref, qg_ref, kvg_ref, wuq_ref, wuk_ref, wuv_ref,
                 cos_ref, slo_ref, shi_ref, q_ref, k_ref, v_ref):
    half = ROPE_D // 2
    qs = (NOPE + ROPE_D) ** -0.5 * LOG2E
    den_flag = jnp.where(lax.broadcasted_iota(jnp.int32, (1, LANE), 1) == DEN_LANE, 1.0, 0.0)
    cos, lo, hi = cos_ref[...], slo_ref[...], shi_ref[...]
    c = _dot(x_ref[...].astype(BF16), wc_ref[...])
    nq = _rms(c[:, C_CQ:C_CKV], qg_ref[...]).astype(BF16)
    nkv = _rms(c[:, C_CKV:C_KR], kvg_ref[...]).astype(BF16)
    k_pe = _rope_lanes(c[:, C_KR:C_COLS], cos, lo, hi, half)
    yq = _dot(nq, wuq_ref[...])
    yk = _dot(nkv, wuk_ref[...])
    yv = _dot(nkv, wuv_ref[...])
    cos_q, lo_q, hi_q = cos * qs, lo * qs, hi * qs
    for h in range(HC):
        hs = slice(h * LANE, (h + 1) * LANE)
        q_ref[:, hs] = _rope_lanes(yq[:, hs], cos_q, lo_q, hi_q, half).astype(BF16)
        k_ref[:, hs] = (yk[:, hs] + k_pe).astype(BF16)
        v_ref[:, hs] = (yv[:, hs] + den_flag).astype(BF16)


def _mlap(x3, wc, qg, kvg, wuq, wuk, wuv, l, cos_c, slo_c, shi_c):
    b, s, _ = x3.shape
    tm = MLAP_TM
    o_spec = pl.BlockSpec((None, tm, MLA_W), lambda bi, m: (bi, m, 0))
    o_shape = jax.ShapeDtypeStruct((b, s, MLA_W), BF16)
    t_spec = pl.BlockSpec((None, tm, LANE), lambda bi, m: (bi, m, 0))
    return pl.pallas_call(
        _mlap_kernel,
        grid=(b, s // tm),
        in_specs=[
            pl.BlockSpec((None, tm, D_MODEL), lambda bi, m: (bi, m, 0)),
            _resident((None, D_MODEL, C_COLS), lambda bi, m: (l, 0, 0)),
            _resident((None, 1, Q_RANK), lambda bi, m: (l, 0, 0)),
            _resident((None, 1, KV_RANK), lambda bi, m: (l, 0, 0)),
            _resident((None, Q_RANK, MLA_W), lambda bi, m: (l, 0, 0)),
            _resident((None, KV_RANK, MLA_W), lambda bi, m: (l, 0, 0)),
            _resident((None, KV_RANK, MLA_W), lambda bi, m: (l, 0, 0)),
            t_spec, t_spec, t_spec,
        ],
        out_specs=[o_spec, o_spec, o_spec],
        out_shape=[o_shape, o_shape, o_shape],
        compiler_params=_params("parallel", "parallel"),
        name="mla_proj",
    )(x3, wc, qg, kvg, wuq, wuk, wuv, cos_c, slo_c, shi_c)


MLA_TQ = 512
MLA_TK = MLA_TQ


MLA_HPS = 4


def _mla_attn_kernel(q_ref, k_ref, v_ref, o_ref):
    tq, tk = MLA_TQ, MLA_TK
    i = pl.program_id(2)
    t0 = i * tq
    heads = [slice(h * LANE, (h + 1) * LANE) for h in range(MLA_HPS)]
    qs = [q_ref[:, hs] for hs in heads]

    def attend(k0, streams, causal):
        scores = [_dot_nt(q, k_ref[pl.ds(k0, nk), hs]) for q, hs, _, _, nk, _ in streams]
        m_new, probs = [], []
        for s, (q, _, m, _, nk, r0) in zip(scores, streams):
            if causal:
                kpos = k0 + lax.broadcasted_iota(jnp.int32, (1, nk), 1)
                qpos = t0 + r0 + lax.broadcasted_iota(jnp.int32, (q.shape[0], 1), 0)
                s = jnp.where(kpos <= qpos, s, NEG)
            m_new.append(jnp.maximum(m, jnp.max(s, axis=-1, keepdims=True)))
            probs.append(jnp.exp2(s - m_new[-1]).astype(BF16))
        pvs = [_dot(p, v_ref[pl.ds(k0, nk), hs]) for p, (_, hs, _, _, nk, _) in zip(probs, streams)]
        return [(mn, jnp.exp2(m - mn) * acc + pv)
                for mn, pv, (_, _, m, acc, _, _) in zip(m_new, pvs, streams)]

    def full_chunk(c, carry):
        streams = [(qs[h], hs, *carry[h], tk, 0) for h, hs in enumerate(heads)]
        return tuple(attend(pl.multiple_of(c * tk, tk), streams, False))

    carry = tuple((jnp.full((tq, 1), NEG, F32), jnp.zeros((tq, LANE), F32)) for _ in heads)
    carry = lax.fori_loop(0, i, full_chunk, carry)
    hq = tq // 2
    streams = [(qs[h][r0:r0 + hq], hs, carry[h][0][r0:r0 + hq], carry[h][1][r0:r0 + hq], r0 + hq, r0)
               for h, hs in enumerate(heads) for r0 in (0, hq)]
    done = attend(pl.multiple_of(t0, tq), streams, True)
    outs = []
    for h in range(len(heads)):
        acc = jnp.concatenate([done[2 * h][1], done[2 * h + 1][1]], axis=0)
        outs.append(acc / acc[:, DEN_LANE:DEN_LANE + 1])
    for h in range(0, MLA_HPS, 2):
        o_ref[:, (h // 2) * LANE:(h // 2 + 1) * LANE] = _pack_pair(outs[h], outs[h + 1]).astype(BF16)


def _mla_attn(q, k, v):
    b, s, _ = q.shape
    tq = MLA_TQ
    gw = MLA_HPS * LANE
    return pl.pallas_call(
        _mla_attn_kernel,
        grid=(b, HC // MLA_HPS, s // tq),
        in_specs=[pl.BlockSpec((None, tq, gw), lambda bi, h, i: (bi, i, h)),
                  pl.BlockSpec((None, s, gw), lambda bi, h, i: (bi, 0, h)),
                  pl.BlockSpec((None, s, gw), lambda bi, h, i: (bi, 0, h))],
        out_specs=pl.BlockSpec((None, tq, gw // 2), lambda bi, h, i: (bi, i, h)),
        out_shape=jax.ShapeDtypeStruct((b, s, MIX_W), BF16),
        compiler_params=_params("parallel", "parallel", "arbitrary"),
        name="mla_attn",
    )(q, k, v)


MERGE_TM = 1024


def _merge_kernel(x_ref, ya_ref, yb_ref, yc_ref, wm_ref, wa_ref, wb_ref, wc_ref, wo_ref,
                  g_ref, b_ref, o_ref, *, alpha):
    x = x_ref[...]
    xb = x.astype(BF16)
    mixed = jnp.zeros(x.shape, F32)
    for idx, (y_ref, w_ref) in enumerate(((ya_ref, wa_ref), (yb_ref, wb_ref), (yc_ref, wc_ref))):
        gate = jax.nn.sigmoid(_dot(xb, wm_ref[:, idx * D_MODEL:(idx + 1) * D_MODEL]))
        mixed = mixed + gate * _dot(y_ref[...], w_ref[...])
    y = alpha * x + _dot(mixed.astype(BF16), wo_ref[...])
    o_ref[...] = _ln(y, g_ref[...], b_ref[...])


def _merge(x2, ya, yb, yc, wm, wa, wb, wc, wo, lng, lnb, l, alpha):
    n = x2.shape[0]
    tm = min(MERGE_TM, n)

    def rows(width):
        return pl.BlockSpec((tm, width), lambda m: (m, 0))

    return pl.pallas_call(
        functools.partial(_merge_kernel, alpha=alpha),
        grid=(n // tm,),
        in_specs=[rows(D_MODEL), rows(MIX_W), rows(MIX_W), rows(MIX_W),
                  _resident((None, D_MODEL, 3 * D_MODEL), lambda m: (l, 0, 0)),
                  _resident((None, None, MIX_W, D_MODEL), lambda m: (l, 0, 0, 0)),
                  _resident((None, None, MIX_W, D_MODEL), lambda m: (l, 1, 0, 0)),
                  _resident((None, None, MIX_W, D_MODEL), lambda m: (l, 2, 0, 0)),
                  _resident((None, D_MODEL, D_MODEL), lambda m: (l, 0, 0)),
                  _resident((None, None, 1, D_MODEL), lambda m: (l, 1, 0, 0)),
                  _resident((None, None, 1, D_MODEL), lambda m: (l, 1, 0, 0))],
        out_specs=rows(D_MODEL),
        out_shape=jax.ShapeDtypeStruct((n, D_MODEL), F32),
        compiler_params=_params("parallel"),
        name="merge",
    )(x2, ya, yb, yc, wm, wa, wb, wc, wo, lng, lnb)


XA_TM = 1024


def _xattn_kernel(x_ref, wq_ref, k_ref, v_ref, wo_ref, g_ref, b_ref, o_ref, *, alpha):
    x = x_ref[...]
    xb = x.astype(BF16)
    q = (_dot(xb, wq_ref[...]) * (XA_DH ** -0.5 * LOG2E)).astype(BF16)
    ones = jnp.ones((k_ref.shape[0], LANE), BF16)
    heads = []
    for h in range(XA_HEADS):
        hs = slice(h * XA_DH, (h + 1) * XA_DH)
        s = _dot_nt(q[:, hs], k_ref[:, hs])
        e = jnp.exp2(s - jnp.max(s, axis=-1, keepdims=True)).astype(BF16)
        den = _dot(e, ones)
        den = jnp.concatenate([den] * (XA_DH // LANE), axis=1)
        heads.append((_dot(e, v_ref[:, hs]) / den).astype(BF16))
    att = jnp.concatenate(heads, axis=1)
    y = alpha * x + _dot(att, wo_ref[...])
    o_ref[...] = _ln(y, g_ref[...], b_ref[...])


def _xattn(x3, wq, k, v, wo, lng, lnb, l, alpha):
    b, s, _ = x3.shape
    m_len = k.shape[1]
    tm = XA_TM
    return pl.pallas_call(
        functools.partial(_xattn_kernel, alpha=alpha),
        grid=(b, s // tm),
        in_specs=[pl.BlockSpec((None, tm, D_MODEL), lambda bi, m: (bi, m, 0)),
                  _resident((None, D_MODEL, D_MODEL), lambda bi, m: (l, 0, 0)),
                  pl.BlockSpec((None, m_len, D_MODEL), lambda bi, m: (bi, 0, 0)),
                  pl.BlockSpec((None, m_len, D_MODEL), lambda bi, m: (bi, 0, 0)),
                  _resident((None, D_MODEL, D_MODEL), lambda bi, m: (l, 0, 0)),
                  _resident((None, None, 1, D_MODEL), lambda bi, m: (l, 2, 0, 0)),
                  _resident((None, None, 1, D_MODEL), lambda bi, m: (l, 2, 0, 0))],
        out_specs=pl.BlockSpec((None, tm, D_MODEL), lambda bi, m: (bi, m, 0)),
        out_shape=jax.ShapeDtypeStruct((b, s, D_MODEL), F32),
        compiler_params=_params("parallel", "parallel"),
        name="xattn",
    )(x3, wq, k, v, wo, lng, lnb)


def _pad_heads(w, n_heads, dh):
    depth, k, _ = w.shape
    w = w.reshape(depth, k, n_heads, dh)
    return jnp.pad(w, ((0, 0), (0, 0), (0, 0), (0, LANE - dh))).reshape(depth, k, n_heads * LANE)


def _rope_tables(positions, rot, off):
    half = rot // 2
    inv = ROPE_THETA ** (-jnp.arange(half, dtype=F32) / half)
    ang = positions.astype(F32)[..., None] * inv
    cos, sin = jnp.cos(ang), jnp.sin(ang)
    shape = positions.shape
    zeros = lambda n: jnp.zeros(shape + (n,), F32)
    cos_t = jnp.concatenate([jnp.ones(shape + (off,), F32), cos, cos,
                             jnp.ones(shape + (LANE - off - rot,), F32)], axis=-1)
    sin_lo = jnp.concatenate([zeros(off), -sin, zeros(LANE - off - half)], axis=-1)
    sin_hi = jnp.concatenate([zeros(off + half), sin, zeros(LANE - off - rot)], axis=-1)
    return cos_t, sin_lo, sin_hi


def _overlap_matrix(s_len):
    n_cmp = (s_len - CMP_L) // CMP_D + 1
    n_slc = s_len // SEL_L
    start = np.arange(n_cmp) * CMP_D
    j = np.arange(n_slc)
    ov = np.clip(np.minimum(start[:, None] + CMP_L, (j[None, :] + 1) * SEL_L)
                 - np.maximum(start[:, None], j[None, :] * SEL_L), 0, None) / CMP_L
    out = np.zeros((N_SLC, N_CMP), np.float32)
    out[:n_slc, :n_cmp] = ov.T
    place = np.zeros((N_SLC, LANE), np.float32)
    place[np.arange(N_SLC), SEL_LANE0 + np.arange(N_SLC)] = 1.0
    return jnp.asarray(out), jnp.asarray(place)


def _stacked_weights(w_in, nsa_cmp_pos, nsa_cmp_w1, nsa_cmp_w2, mla_w_uq, mla_w_ukv):
    depth = w_in.shape[0]
    part = [w_in[:, :, IN_OFF[i]:IN_OFF[i + 1]] for i in range(len(IN_SIZES))]
    (a_q, a_kc, a_vc, a_ks, a_vs, a_kw, a_vw, a_gate, b_q, b_f, b_i, b_g,
     c_q, c_kv, c_kr, merge) = part
    gate = _pad_heads(a_gate, NSA_KV, NSA_R * 3)
    kv_parts = dict(kc=a_kc, vc=a_vc, ks=a_ks, vs=a_vs, kw=a_kw, vw=a_vw)
    wa = jnp.concatenate([_pad_heads(a_q, NSA_HEADS, NSA_DH)]
                         + [kv_parts[name] for name in A_KV_ORDER] + [gate], axis=2).astype(BF16)
    wb = jnp.stack([w.reshape(depth, D_MODEL, HB, HK) for w in (b_q, b_f, b_i, b_g)], axis=3)
    wb = wb.transpose(0, 2, 1, 3, 4).reshape(depth, HB, D_MODEL, 4 * HK).astype(BF16)

    kr_pad = jnp.pad(c_kr, ((0, 0), (0, 0), (NOPE, LANE - NOPE - ROPE_D)))
    wc = jnp.concatenate([c_q, c_kv, kr_pad], axis=2).astype(BF16)
    wuq = _pad_heads(mla_w_uq, HC, NOPE + ROPE_D).astype(BF16)
    ukv = mla_w_ukv.reshape(depth, KV_RANK, HC, NOPE + VD)
    wuk = _pad_heads(ukv[..., :NOPE].reshape(depth, KV_RANK, HC * NOPE), HC, NOPE).astype(BF16)
    wuv = _pad_heads(ukv[..., NOPE:].reshape(depth, KV_RANK, HC * VD), HC, VD).astype(BF16)

    w1 = nsa_cmp_w1.reshape(depth, 2, CMP_L, NSA_DH, CMP_HID)
    w1p = jnp.pad(w1, ((0, 0), (0, 0), (0, 0), (0, LANE - NSA_DH), (0, 0)))
    w1p = w1p.reshape(depth, 2, CMP_L * LANE, CMP_HID)
    pe = jnp.pad(nsa_cmp_pos, ((0, 0), (0, 0), (0, 0), (0, LANE - NSA_DH)))
    pe8 = jnp.broadcast_to(pe.reshape(depth, 2, 1, CMP_L * LANE), (depth, 2, 8, CMP_L * LANE))
    w2p = jnp.pad(nsa_cmp_w2, ((0, 0), (0, 0), (0, 0), (0, LANE - NSA_DH)))
    return dict(wa=wa, wb=wb, wc=wc, wuq=wuq, wuk=wuk, wuv=wuv,
                w1p=w1p.astype(BF16), pe8=pe8.astype(BF16), w2p=w2p.astype(BF16),
                wm=merge.astype(BF16))


def kernel(x, mem, positions, ln_g, ln_b, ffn_w1, ffn_w3, ffn_w2, w_in, nsa_cmp_pos,
           nsa_cmp_w1, nsa_cmp_w2, hgrn_lb_logits, hgrn_norm_g, mla_q_norm_g, mla_w_uq,
           mla_kv_norm_g, mla_w_ukv, w_branch, w_out, xa_wq, xa_wk, xa_wv, xa_wo):
    b, s, d = x.shape
    depth = ln_g.shape[0]
    n = b * s
    alpha = (2.0 * depth) ** 0.25

    lng = ln_g.reshape(depth, 4, 1, d)
    lnb = ln_b.reshape(depth, 4, 1, d)
    w1 = ffn_w1.astype(BF16)
    w3 = ffn_w3.astype(BF16)
    w2 = (0.5 * ffn_w2).astype(BF16)
    wo = w_out.astype(BF16)
    wbr = w_branch.astype(BF16)
    xq = xa_wq.astype(BF16)
    xk = xa_wk.astype(BF16)
    xv = xa_wv.astype(BF16)
    xo = xa_wo.astype(BF16)
    p_lb = jax.nn.softmax(hgrn_lb_logits.astype(F32), axis=0)
    lower = (jnp.cumsum(p_lb, axis=0) - p_lb[0:1]).reshape(depth, HB, 1, HK)

    rope_a = _rope_tables(positions, NSA_ROT, 0)
    rope_c = _rope_tables(positions, ROPE_D, NOPE)
    ovl, place = _overlap_matrix(s)
    mem2 = mem.reshape(b * mem.shape[1], d)
    w = _stacked_weights(w_in, nsa_cmp_pos, nsa_cmp_w1, nsa_cmp_w2, mla_w_uq, mla_w_ukv)
    norm_b = hgrn_norm_g.reshape(depth, 1, HV)
    norm_q = mla_q_norm_g.reshape(depth, 1, Q_RANK)
    norm_kv = mla_kv_norm_g.reshape(depth, 1, KV_RANK)

    x2 = x.reshape(n, d)
    for l in range(depth):
        x2 = _ffn_ln(x2, w1, w3, w2, lng, lnb, l, 0, 0, alpha)
        x3 = x2.reshape(b, s, d)

        q, qr, kc, vc, ks, vs, kw, vw, gates = _proja(x3, w["wa"], l, *rope_a)
        kcmp, vcmp = _cmp(kc, vc, w["w1p"], w["pe8"], w["w2p"], l)
        ya = _nsa_attn(q, qr, kcmp, vcmp, ks, vs, kw, vw, gates, ovl, place)

        yb = _hgrn(x3, w["wb"], lower, norm_b, l)

        mq, mk, mv = _mlap(x3, w["wc"], norm_q, norm_kv, w["wuq"], w["wuk"], w["wuv"], l, *rope_c)
        yc = _mla_attn(mq, mk, mv)

        x2 = _merge(x2, ya.reshape(n, MIX_W), yb.reshape(n, MIX_W), yc.reshape(n, MIX_W),
                    w["wm"], wbr, wbr, wbr, wo, lng, lnb, l, alpha)

        xk_l = _proj(mem2, xk, l, BF16).reshape(b, -1, d)
        xv_l = _proj(mem2, xv, l, BF16).reshape(b, -1, d)
        x2 = _xattn(x2.reshape(b, s, d), xq, xk_l, xv_l, xo, lng, lnb, l, alpha).reshape(n, d)

        x2 = _ffn_ln(x2, w1, w3, w2, lng, lnb, l, 1, 3, alpha)
    return x2.reshape(b, s, d)
```

```python
import functools

import numpy as np
import jax
import jax.numpy as jnp
from jax import lax
from jax.experimental import pallas as pl
from jax.experimental.pallas import tpu as pltpu

F32 = jnp.float32
BF16 = jnp.bfloat16

D_MODEL = 1024
MIX_W = D_MODEL // 2
NSA_DH = 64
NSA_HEADS = 8
NSA_KV = 2
NSA_R = 4
NSA_ROT = 16
CMP_L = 32
CMP_D = 16
CMP_HID = 256
SEL_L = 64
N_SEL = 8
WINDOW = 256
HB = 4
HK = 128
HV = 128
HGRN_CHUNK = 64
HGRN_SUB = 4
HGRN_PARTS = 4
HC = 8
NOPE = 64
ROPE_D = 32
VD = 64
Q_RANK = 384
KV_RANK = 256
XA_HEADS = 4
XA_DH = D_MODEL // XA_HEADS
D_FF = 2816
ROPE_THETA = 500000.0
LN_EPS = 1e-5
RMS_EPS = 1e-6
NEG = -1e30
BIG = 1e9
F_MIN = 1e-20

LOG2E = 1.4426950408889634
SQRT_2_OVER_PI = 0.7978845608028654
LANE = 128
SEL_LANE0 = NSA_DH
DEN_LANE = 64
VMEM_LIMIT = 56 * 1024 * 1024

IN_SIZES = (512, 128, 128, 128, 128, 128, 128, 24, 512, 512, 512, 512, 384, 256, 32, 3072)
IN_OFF = tuple(int(v) for v in np.concatenate([[0], np.cumsum(IN_SIZES)]))


def _dot(a, b):
    return jnp.dot(a, b, preferred_element_type=F32)


def _dot_nt(a, b):
    return lax.dot_general(a, b, (((1,), (1,)), ((), ())), preferred_element_type=F32)


def _ln(y, g, b):
    mu = jnp.mean(y, axis=-1, keepdims=True)
    yc = y - mu
    var = jnp.mean(yc * yc, axis=-1, keepdims=True)
    return yc * lax.rsqrt(var + LN_EPS) * g + b


def _pack_pair(a, b):
    lane = lax.broadcasted_iota(jnp.int32, a.shape, a.ndim - 1)
    return jnp.where(lane < LANE // 2, a, pltpu.roll(b, LANE // 2, axis=a.ndim - 1))


def _params(*sem):
    return pltpu.CompilerParams(dimension_semantics=sem, vmem_limit_bytes=VMEM_LIMIT)


def _resident(shape, index_map):
    return pl.BlockSpec(shape, index_map, pipeline_mode=pl.Buffered(1))


FFN_TM = 2048
FFN_TF = 256


def _ffn_kernel(x_ref, w1_ref, w3_ref, w2h_ref, g_ref, b_ref, o_ref, acc_ref, xb_ref, *, alpha):
    j = pl.program_id(1)

    @pl.when(j == 0)
    def _():
        xb_ref[...] = x_ref[...].astype(BF16)
        acc_ref[...] = alpha * x_ref[...]

    xb = xb_ref[...]
    h1 = _dot(xb, w1_ref[...])
    h3 = _dot(xb, w3_ref[...])
    h = (h1 * jax.nn.sigmoid(h1)) * h3
    acc_ref[...] += _dot(h.astype(BF16), w2h_ref[...])

    @pl.when(j == pl.num_programs(1) - 1)
    def _():
        o_ref[...] = _ln(acc_ref[...], g_ref[...], b_ref[...])


def _ffn_ln(x2, w1, w3, w2h, lng, lnb, l, which, ln_idx, alpha):
    n = x2.shape[0]
    tm = min(FFN_TM, n)
    grid = (n // tm, D_FF // FFN_TF)
    return pl.pallas_call(
        functools.partial(_ffn_kernel, alpha=alpha),
        grid=grid,
        in_specs=[
            pl.BlockSpec((tm, D_MODEL), lambda m, j: (m, 0)),
            pl.BlockSpec((None, None, D_MODEL, FFN_TF), lambda m, j: (l, which, 0, j)),
            pl.BlockSpec((None, None, D_MODEL, FFN_TF), lambda m, j: (l, which, 0, j)),
            pl.BlockSpec((None, None, FFN_TF, D_MODEL), lambda m, j: (l, which, j, 0)),
            pl.BlockSpec((None, None, 1, D_MODEL), lambda m, j: (l, ln_idx, 0, 0)),
            pl.BlockSpec((None, None, 1, D_MODEL), lambda m, j: (l, ln_idx, 0, 0)),
        ],
        out_specs=pl.BlockSpec((tm, D_MODEL), lambda m, j: (m, 0)),
        out_shape=jax.ShapeDtypeStruct((n, D_MODEL), F32),
        scratch_shapes=[pltpu.VMEM((tm, D_MODEL), F32), pltpu.VMEM((tm, D_MODEL), BF16)],
        compiler_params=_params("parallel", "arbitrary"),
        name="ffn_ln",
    )(x2, w1, w3, w2h, lng, lnb)


def _proj_kernel(x_ref, w_ref, o_ref):
    o_ref[...] = _dot(x_ref[...].astype(BF16), w_ref[...]).astype(o_ref.dtype)


def _proj(x2, w, l, out_dtype, tm=1024, tn=2048):
    n, k = x2.shape
    c = w.shape[2]
    tm = min(tm, n)
    tn = min(tn, c)
    return pl.pallas_call(
        _proj_kernel,
        grid=(n // tm, c // tn),
        in_specs=[pl.BlockSpec((tm, k), lambda m, j: (m, 0)),
                  pl.BlockSpec((None, k, tn), lambda m, j: (l, 0, j))],
        out_specs=pl.BlockSpec((tm, tn), lambda m, j: (m, j)),
        out_shape=jax.ShapeDtypeStruct((n, c), out_dtype),
        compiler_params=_params("parallel", "arbitrary"),
        name="proj",
    )(x2, w)


NSA_QW = NSA_HEADS * LANE
NSA_KW = NSA_KV * LANE
A_Q = 0
A_KV = NSA_QW
A_KV_ORDER = ("kc", "vc", "ks", "vs", "kw", "vw")
A_GATE = A_KV + len(A_KV_ORDER) * LANE
A_COLS = A_GATE + NSA_KW
PROJA_TM = 1024


def _rope_lanes(y, cos, sin_lo, sin_hi, half):
    return (y * cos + pltpu.roll(y, LANE - half, axis=1) * sin_lo
            + pltpu.roll(y, half, axis=1) * sin_hi)


def _proja_kernel(x_ref, w_ref, cos_ref, slo_ref, shi_ref,
                  q_ref, qr_ref, kc_ref, vc_ref, ks_ref, vs_ref, kw_ref, vw_ref, g_ref):
    xb = x_ref[...].astype(BF16)
    cos = cos_ref[...]
    sin_lo = slo_ref[...]
    sin_hi = shi_ref[...]
    half = NSA_ROT // 2
    tm = x_ref.shape[0]

    def mm(c0, width):
        return _dot(xb, w_ref[:, c0:c0 + width])

    qs = NSA_DH ** -0.5 * LOG2E
    y = mm(A_Q, NSA_QW)
    q_ref[...] = (y * qs).astype(BF16)
    for h in range(NSA_HEADS):
        hs = slice(h * LANE, (h + 1) * LANE)
        qr_ref[:, hs] = (_rope_lanes(y[:, hs], cos, sin_lo, sin_hi, half) * qs).astype(BF16)

    lane = lax.broadcasted_iota(jnp.int32, (tm, LANE), 1)
    first = lane < NSA_DH
    tok = pl.program_id(1) * tm + lax.broadcasted_iota(jnp.int32, (tm, LANE), 0)
    blk_flag = jnp.where(lane - SEL_LANE0 == tok // SEL_L, 1.0, 0.0)
    den_flag = jnp.where(lane == DEN_LANE, 1.0, 0.0)

    def packed(table):
        return jnp.where(first, table, pltpu.roll(table, NSA_DH, axis=1))

    def put(ref, y, flag=None):
        for g, yg in enumerate((y, pltpu.roll(y, NSA_DH, axis=1))):
            yg = jnp.where(first, yg, 0.0)
            ref[g] = (yg if flag is None else yg + flag).astype(ref.dtype)

    kv = mm(A_KV, len(A_KV_ORDER) * LANE)
    part = {name: kv[:, i * LANE:(i + 1) * LANE] for i, name in enumerate(A_KV_ORDER)}
    cos_p, lo_p, hi_p = packed(cos), packed(sin_lo), packed(sin_hi)
    put(kc_ref, part["kc"])
    put(vc_ref, part["vc"])
    put(ks_ref, _rope_lanes(part["ks"], cos_p, lo_p, hi_p, half), blk_flag)
    put(vs_ref, part["vs"], den_flag)
    put(kw_ref, _rope_lanes(part["kw"], cos_p, lo_p, hi_p, half))
    put(vw_ref, part["vw"], den_flag)
    g_ref[...] = mm(A_GATE, NSA_KW)


def _proja(x3, wa, l, cos_a, slo_a, shi_a):
    b, s, _ = x3.shape
    tm = PROJA_TM
    kv_shape = jax.ShapeDtypeStruct((b, NSA_KV, s, LANE), BF16)
    cmp_shape = jax.ShapeDtypeStruct((b, NSA_KV, s, LANE), F32)
    kv_spec = pl.BlockSpec((None, NSA_KV, tm, LANE), lambda bi, m: (bi, 0, m, 0))
    return pl.pallas_call(
        _proja_kernel,
        grid=(b, s // tm),
        in_specs=[
            pl.BlockSpec((None, tm, D_MODEL), lambda bi, m: (bi, m, 0)),
            _resident((None, D_MODEL, A_COLS), lambda bi, m: (l, 0, 0)),
            pl.BlockSpec((None, tm, LANE), lambda bi, m: (bi, m, 0)),
            pl.BlockSpec((None, tm, LANE), lambda bi, m: (bi, m, 0)),
            pl.BlockSpec((None, tm, LANE), lambda bi, m: (bi, m, 0)),
        ],
        out_specs=[
            pl.BlockSpec((None, tm, NSA_QW), lambda bi, m: (bi, m, 0)),
            pl.BlockSpec((None, tm, NSA_QW), lambda bi, m: (bi, m, 0)),
            kv_spec, kv_spec, kv_spec, kv_spec, kv_spec, kv_spec,
            pl.BlockSpec((None, tm, NSA_KW), lambda bi, m: (bi, m, 0)),
        ],
        out_shape=[
            jax.ShapeDtypeStruct((b, s, NSA_QW), BF16),
            jax.ShapeDtypeStruct((b, s, NSA_QW), BF16),
            cmp_shape, cmp_shape, kv_shape, kv_shape, kv_shape, kv_shape,
            jax.ShapeDtypeStruct((b, s, NSA_KW), F32),
        ],
        compiler_params=_params("parallel", "parallel"),
        name="nsa_proj",
    )(x3, wa, cos_a, slo_a, shi_a)


N_CMP = 128
CMP_HALF = CMP_D * LANE


def _gelu_tanh(x):
    return 0.5 * x * (1.0 + jnp.tanh(SQRT_2_OVER_PI * (x + 0.044715 * x * x * x)))


def _cmp_kernel(zk_ref, zv_ref, w1_ref, pe_ref, w2_ref, ok_ref, ov_ref):
    for which, (z_ref, o_ref) in enumerate(((zk_ref, ok_ref), (zv_ref, ov_ref))):
        first = jnp.zeros((N_CMP, CMP_HID), F32)
        second = jnp.zeros((N_CMP, CMP_HID), F32)
        for j in range(CMP_D):
            zj = z_ref[pl.ds(j, N_CMP, stride=CMP_D), :].astype(BF16)
            first = first + _dot(zj, w1_ref[which, j * LANE:(j + 1) * LANE, :])
            second = second + _dot(zj, w1_ref[which, CMP_HALF + j * LANE:CMP_HALF + (j + 1) * LANE, :])
        bias = _dot(pe_ref[which], w1_ref[which])[0:1]
        pre = first + pltpu.roll(second, N_CMP - 1, axis=0) + bias
        h = _gelu_tanh(pre)
        o_ref[...] = _dot(h.astype(BF16), w2_ref[which]).astype(BF16)


def _cmp(zk, zv, w1p, pe8, w2p, l):
    b, _, s, _ = zk.shape
    z_spec = pl.BlockSpec((None, None, s, LANE), lambda bi, g: (bi, g, 0, 0))
    o_spec = pl.BlockSpec((None, None, N_CMP, LANE), lambda bi, g: (bi, g, 0, 0))
    o_shape = jax.ShapeDtypeStruct((b, NSA_KV, N_CMP, LANE), BF16)
    return pl.pallas_call(
        _cmp_kernel,
        grid=(b, NSA_KV),
        in_specs=[
            z_spec, z_spec,
            _resident((None, 2, 2 * CMP_HALF, CMP_HID), lambda bi, g: (l, 0, 0, 0)),
            _resident((None, 2, 8, 2 * CMP_HALF), lambda bi, g: (l, 0, 0, 0)),
            _resident((None, 2, CMP_HID, LANE), lambda bi, g: (l, 0, 0, 0)),
        ],
        out_specs=[o_spec, o_spec],
        out_shape=[o_shape, o_shape],
        compiler_params=_params("parallel", "parallel"),
        name="nsa_cmp",
    )(zk, zv, w1p, pe8, w2p)


NSA_TQ = 256
NSA_TK = 512
N_SLC = 32
WIN_SLAB = WINDOW + NSA_TQ


def _nsa_kernel(q_ref, qr_ref, kc_ref, vc_ref, ks_ref, vs_ref, kw_ref, vw_ref, gt_ref, ovl_ref,
                place_ref, o_ref):
    tq, tk, r_heads = NSA_TQ, NSA_TK, NSA_R
    i = pl.program_id(2)
    t0 = i * tq
    tpos = t0 + lax.broadcasted_iota(jnp.int32, (tq, 1), 0)
    lane = lax.broadcasted_iota(jnp.int32, (1, LANE), 1)

    def stack(ref):
        return jnp.concatenate([ref[:, r * LANE:(r + 1) * LANE] for r in range(r_heads)], axis=0)

    gt = jax.nn.sigmoid(gt_ref[...])

    def gate(c):
        return jnp.stack([gt[:, 3 * r + c:3 * r + c + 1] for r in range(r_heads)], axis=0)

    q4w = stack(qr_ref)
    w0 = pl.multiple_of(jnp.maximum(t0 - WINDOW, 0), LANE)
    kw = kw_ref[pl.ds(w0, WIN_SLAB), :]
    vw = vw_ref[pl.ds(w0, WIN_SLAB), :]
    kpos = w0 + lax.broadcasted_iota(jnp.int32, (1, WIN_SLAB), 1)
    w_bias = jnp.where((kpos <= tpos) & (kpos > tpos - WINDOW), 0.0, NEG)
    s = _dot_nt(q4w, kw).reshape(r_heads, tq, WIN_SLAB) + w_bias[None]
    p = jnp.exp2(s - jnp.max(s, axis=-1, keepdims=True))
    o_win = _dot(p.reshape(r_heads * tq, WIN_SLAB).astype(BF16), vw).reshape(r_heads, tq, LANE)
    o_part = gate(2) * (o_win / o_win[:, :, DEN_LANE:DEN_LANE + 1])

    cmask = ((lane * CMP_D + (CMP_L - 1) <= tpos) & (lane < N_CMP - 1))[None]
    s = _dot_nt(stack(q_ref), kc_ref[...]).reshape(r_heads, tq, N_CMP)
    s = jnp.where(cmask, s, NEG)
    e = jnp.where(cmask, jnp.exp2(s - jnp.max(s, axis=-1, keepdims=True)), 0.0)
    den = jnp.sum(e, axis=-1, keepdims=True)
    p = e / jnp.where(den > 0.0, den, 1.0)
    psum = jnp.sum(p, axis=0)
    o_cmp = _dot(p.reshape(r_heads * tq, N_CMP).astype(BF16), vc_ref[...])
    o_part = o_part + gate(0) * o_cmp.reshape(r_heads, tq, LANE)

    imp = lax.dot_general(ovl_ref[...], psum, (((1,), (1,)), ((), ())),
                          precision=lax.Precision.HIGHEST, preferred_element_type=F32)
    blk = lax.broadcasted_iota(jnp.int32, (N_SLC, 1), 0)
    tpos_l = t0 + lax.broadcasted_iota(jnp.int32, (1, tq), 1)
    cur = tpos_l // SEL_L
    valid = blk * SEL_L <= tpos_l
    forced = (blk == 0) | (blk == cur) | (blk == cur - 1)
    score = jnp.where(valid & forced, BIG, jnp.where(valid, imp, -BIG))
    beats = []
    for j in range(N_SLC):
        sj = score[j:j + 1, :]
        beats.append(jnp.where((sj > score) | ((sj == score) & (j < blk)), 1.0, 0.0))
    while len(beats) > 1:
        beats = [a + b for a, b in zip(beats[0::2], beats[1::2])]
    sel_t = jnp.where((beats[0] < N_SEL) & valid, 1.0, 0.0)
    sel_q = lax.dot_general(sel_t, place_ref[...], (((0,), (0,)), ((), ())),
                            preferred_element_type=F32)
    in_flags = jnp.where((lane >= SEL_LANE0) & (lane < SEL_LANE0 + N_SLC), 1.0, 0.0)
    q_bias = (sel_q - in_flags) * (-NEG)
    q4 = (q4w.astype(F32).reshape(r_heads, tq, LANE) + q_bias[None]).astype(BF16)
    q4 = q4.reshape(r_heads * tq, LANE)

    def sel_step(c, carry, causal):
        m, acc = carry
        k0 = pl.multiple_of(c * tk, tk)
        k = ks_ref[pl.ds(k0, tk), :]
        v = vs_ref[pl.ds(k0, tk), :]
        s = _dot_nt(q4, k).reshape(r_heads, tq, tk)
        if causal:
            kpos = k0 + lax.broadcasted_iota(jnp.int32, (1, tk), 1)
            s = jnp.where((kpos <= tpos)[None], s, NEG)
        m_new = jnp.maximum(m, jnp.max(s, axis=-1, keepdims=True))
        p = jnp.exp2(s - m_new)
        pv = _dot(p.reshape(r_heads * tq, tk).astype(BF16), v).reshape(r_heads, tq, LANE)
        return m_new, jnp.exp2(m - m_new) * acc + pv

    last = (t0 + tq - 1) // tk
    carry = (jnp.full((r_heads, tq, 1), NEG, F32), jnp.zeros((r_heads, tq, LANE), F32))
    carry = lax.fori_loop(0, last, functools.partial(sel_step, causal=False), carry)
    _, acc_s = sel_step(last, carry, True)
    o = o_part + gate(1) * (acc_s / acc_s[:, :, DEN_LANE:DEN_LANE + 1])
    for r in range(0, r_heads, 2):
        o_ref[:, (r // 2) * LANE:(r // 2 + 1) * LANE] = _pack_pair(o[r], o[r + 1]).astype(BF16)


def _nsa_attn(q, qr, kcmp, vcmp, ks, vs, kw, vw, gates, ovl, place):
    b, s, _ = q.shape
    tq = NSA_TQ
    gw = NSA_R * LANE
    q_spec = pl.BlockSpec((None, tq, gw), lambda bi, g, i: (bi, i, g))
    c_spec = pl.BlockSpec((None, None, N_CMP, LANE), lambda bi, g, i: (bi, g, 0, 0))
    kv_spec = pl.BlockSpec((None, None, s, LANE), lambda bi, g, i: (bi, g, 0, 0))
    return pl.pallas_call(
        _nsa_kernel,
        grid=(b, NSA_KV, s // tq),
        in_specs=[q_spec, q_spec, c_spec, c_spec, kv_spec, kv_spec, kv_spec, kv_spec,
                  pl.BlockSpec((None, tq, LANE), lambda bi, g, i: (bi, i, g)),
                  _resident((N_SLC, N_CMP), lambda bi, g, i: (0, 0)),
                  _resident((N_SLC, LANE), lambda bi, g, i: (0, 0))],
        out_specs=pl.BlockSpec((None, tq, gw // 2), lambda bi, g, i: (bi, i, g)),
        out_shape=jax.ShapeDtypeStruct((b, s, MIX_W), BF16),
        compiler_params=_params("parallel", "parallel", "arbitrary"),
        name="nsa_attn",
    )(q, qr, kcmp, vcmp, ks, vs, kw, vw, gates, ovl, place)


def _hgrn_kernel(x_ref, w_ref, lb_ref, ng_ref, o_ref, st_ref):
    s_len = x_ref.shape[0]
    c_len, sub_len = HGRN_CHUNK, HGRN_SUB
    p_len = s_len // HGRN_PARTS
    nc = p_len // c_len
    lb = lb_ref[...]
    t = lax.broadcasted_iota(jnp.int32, (p_len, 1), 0)
    t_sub = t % sub_len
    row = lax.broadcasted_iota(jnp.int32, (nc, c_len, c_len), 1)
    col = lax.broadcasted_iota(jnp.int32, (nc, c_len, c_len), 2)
    tri = jnp.where(col <= row, 1.0, 0.0)

    local = []
    for part in range(HGRN_PARTS):
        proj = _dot(x_ref[pl.ds(part * p_len, p_len), :].astype(BF16), w_ref[...])
        q = proj[:, 0:HK]
        z = proj[:, HK:2 * HK]
        v = proj[:, 2 * HK:2 * HK + HV]
        go = proj[:, 2 * HK + HV:2 * HK + 2 * HV]

        sig = jax.nn.sigmoid(z)
        f = lb + (1.0 - lb) * sig
        lf = jnp.log(jnp.maximum(f, F_MIN))
        k = (1.0 - lb) * (1.0 - sig)

        b3 = jnp.einsum('cts,csd->ctd', tri, lf.reshape(nc, c_len, HK),
                        precision=lax.Precision.HIGHEST, preferred_element_type=F32)
        b = b3.reshape(p_len, HK)
        k3 = k.reshape(nc, c_len, HK)
        vb3 = v.astype(BF16).reshape(nc, c_len, HV)

        a_intra = None
        h = c_len // 2
        while h >= sub_len:
            blk = b.reshape(p_len // (2 * h), 2 * h, HK)
            e = jnp.exp(-jnp.abs(blk[:, h - 1:h, :] - blk)).reshape(p_len, HK)
            upper = (t // h) % 2 == 1
            lq = jnp.where(upper, q * e, 0.0).astype(BF16).reshape(nc, c_len, HK)
            rk = jnp.where(upper, 0.0, k * e).astype(BF16).reshape(nc, c_len, HK)
            a = jnp.einsum('ctk,csk->cts', lq, rk, preferred_element_type=F32)
            if 2 * h < c_len:
                a = jnp.where(row // (2 * h) == col // (2 * h), a, 0.0)
            a_intra = a if a_intra is None else a_intra + a
            h //= 2
        o = jnp.einsum('cts,csd->ctd', a_intra.astype(BF16), vb3,
                       preferred_element_type=F32).reshape(p_len, HV)

        o = o + jnp.sum(q * k, axis=-1, keepdims=True) * v
        for d in range(1, sub_len):
            e = jnp.exp(jnp.where(t_sub >= d, b - pltpu.roll(b, d, axis=0), NEG))
            w = jnp.sum(q * pltpu.roll(k, d, axis=0) * e, axis=-1, keepdims=True)
            o = o + w * pltpu.roll(v, d, axis=0)

        b_last = b3[:, c_len - 1:c_len, :]
        kd3 = (k3 * jnp.exp(b_last - b3)).astype(BF16)
        upd = jnp.einsum('csv,csk->cvk', vb3, kd3, preferred_element_type=F32)
        qe3 = (q * jnp.exp(b)).astype(BF16).reshape(nc, c_len, HK)
        local.append((o, go, qe3, upd, jnp.exp(b_last)))

    state = jnp.zeros((HV, HK), F32)
    for part, (_, _, _, upd, dec) in enumerate(local):
        for c in range(nc):
            st_ref[part * nc + c] = state.astype(BF16)
            state = state * dec[c] + upd[c]

    for part, (o, go, qe3, _, _) in enumerate(local):
        o = o + jnp.einsum('ctk,cvk->ctv', qe3, st_ref[pl.ds(part * nc, nc)],
                           preferred_element_type=F32).reshape(p_len, HV)
        o = o * lax.rsqrt(jnp.mean(o * o, axis=-1, keepdims=True) + RMS_EPS) * ng_ref[...]
        o_ref[pl.ds(part * p_len, p_len), :] = (o * (go * jax.nn.sigmoid(go))).astype(BF16)


def _hgrn(x3, wb, lb, ng, l):
    b, s, d = x3.shape
    return pl.pallas_call(
        _hgrn_kernel,
        grid=(b, HB),
        in_specs=[pl.BlockSpec((None, s, d), lambda bi, h: (bi, 0, 0)),
                  pl.BlockSpec((None, None, d, 4 * HK), lambda bi, h: (l, h, 0, 0)),
                  pl.BlockSpec((None, None, 1, HK), lambda bi, h: (l, h, 0, 0)),
                  pl.BlockSpec((None, 1, HV), lambda bi, h: (l, 0, 0))],
        out_specs=pl.BlockSpec((None, s, HV), lambda bi, h: (bi, 0, h)),
        out_shape=jax.ShapeDtypeStruct((b, s, HB * HV), BF16),
        scratch_shapes=[pltpu.VMEM((s // HGRN_CHUNK, HV, HK), BF16)],
        compiler_params=_params("parallel", "arbitrary"),
        name="hgrn",
    )(x3, wb, lb, ng)


MLA_W = HC * LANE
C_CQ, C_CKV, C_KR = 0, Q_RANK, Q_RANK + KV_RANK
C_COLS = C_KR + LANE
MLAP_TM = 1024


def _rms(x, g):
    return x * lax.rsqrt(jnp.mean(x * x, axis=-1, keepdims=True) + RMS_EPS) * g


def _mlap_kernel(x_ref, wc_ref, qg_ref, kvg_ref, wuq_ref, wuk_ref, wuv_ref,
                 cos_ref, slo_ref, shi_ref, q_ref, k_ref, v_ref):
    half = ROPE_D // 2
    qs = (NOPE + ROPE_D) ** -0.5 * LOG2E
    den_flag = jnp.where(lax.broadcasted_iota(jnp.int32, (1, LANE), 1) == DEN_LANE, 1.0, 0.0)
    cos, lo, hi = cos_ref[...], slo_ref[...], shi_ref[...]
    c = _dot(x_ref[...].astype(BF16), wc_ref[...])
    nq = _rms(c[:, C_CQ:C_CKV], qg_ref[...]).astype(BF16)
    nkv = _rms(c[:, C_CKV:C_KR], kvg_ref[...]).astype(BF16)
    k_pe = _rope_lanes(c[:, C_KR:C_COLS], cos, lo, hi, half)
    yq = _dot(nq, wuq_ref[...])
    yk = _dot(nkv, wuk_ref[...])
    yv = _dot(nkv, wuv_ref[...])
    cos_q, lo_q, hi_q = cos * qs, lo * qs, hi * qs
    for h in range(HC):
        hs = slice(h * LANE, (h + 1) * LANE)
        q_ref[:, hs] = _rope_lanes(yq[:, hs], cos_q, lo_q, hi_q, half).astype(BF16)
        k_ref[:, hs] = (yk[:, hs] + k_pe).astype(BF16)
        v_ref[:, hs] = (yv[:, hs] + den_flag).astype(BF16)


def _mlap(x3, wc, qg, kvg, wuq, wuk, wuv, l, cos_c, slo_c, shi_c):
    b, s, _ = x3.shape
    tm = MLAP_TM
    o_spec = pl.BlockSpec((None, tm, MLA_W), lambda bi, m: (bi, m, 0))
    o_shape = jax.ShapeDtypeStruct((b, s, MLA_W), BF16)
    t_spec = pl.BlockSpec((None, tm, LANE), lambda bi, m: (bi, m, 0))
    return pl.pallas_call(
        _mlap_kernel,
        grid=(b, s // tm),
        in_specs=[
            pl.BlockSpec((None, tm, D_MODEL), lambda bi, m: (bi, m, 0)),
            _resident((None, D_MODEL, C_COLS), lambda bi, m: (l, 0, 0)),
            _resident((None, 1, Q_RANK), lambda bi, m: (l, 0, 0)),
            _resident((None, 1, KV_RANK), lambda bi, m: (l, 0, 0)),
            _resident((None, Q_RANK, MLA_W), lambda bi, m: (l, 0, 0)),
            _resident((None, KV_RANK, MLA_W), lambda bi, m: (l, 0, 0)),
            _resident((None, KV_RANK, MLA_W), lambda bi, m: (l, 0, 0)),
            t_spec, t_spec, t_spec,
        ],
        out_specs=[o_spec, o_spec, o_spec],
        out_shape=[o_shape, o_shape, o_shape],
        compiler_params=_params("parallel", "parallel"),
        name="mla_proj",
    )(x3, wc, qg, kvg, wuq, wuk, wuv, cos_c, slo_c, shi_c)


MLA_TQ = 512
MLA_TK = MLA_TQ


MLA_HPS = 4


def _mla_attn_kernel(q_ref, k_ref, v_ref, o_ref):
    tq, tk = MLA_TQ, MLA_TK
    i = pl.program_id(2)
    t0 = i * tq
    heads = [slice(h * LANE, (h + 1) * LANE) for h in range(MLA_HPS)]
    qs = [q_ref[:, hs] for hs in heads]

    def attend(k0, streams, causal):
        scores = [_dot_nt(q, k_ref[pl.ds(k0, nk), hs]) for q, hs, _, _, nk, _ in streams]
        m_new, probs = [], []
        for s, (q, _, m, _, nk, r0) in zip(scores, streams):
            if causal:
                kpos = k0 + lax.broadcasted_iota(jnp.int32, (1, nk), 1)
                qpos = t0 + r0 + lax.broadcasted_iota(jnp.int32, (q.shape[0], 1), 0)
                s = jnp.where(kpos <= qpos, s, NEG)
            m_new.append(jnp.maximum(m, jnp.max(s, axis=-1, keepdims=True)))
            probs.append(jnp.exp2(s - m_new[-1]).astype(BF16))
        pvs = [_dot(p, v_ref[pl.ds(k0, nk), hs]) for p, (_, hs, _, _, nk, _) in zip(probs, streams)]
        return [(mn, jnp.exp2(m - mn) * acc + pv)
                for mn, pv, (_, _, m, acc, _, _) in zip(m_new, pvs, streams)]

    def full_chunk(c, carry):
        streams = [(qs[h], hs, *carry[h], tk, 0) for h, hs in enumerate(heads)]
        return tuple(attend(pl.multiple_of(c * tk, tk), streams, False))

    carry = tuple((jnp.full((tq, 1), NEG, F32), jnp.zeros((tq, LANE), F32)) for _ in heads)
    carry = lax.fori_loop(0, i, full_chunk, carry)
    hq = tq // 2
    streams = [(qs[h][r0:r0 + hq], hs, carry[h][0][r0:r0 + hq], carry[h][1][r0:r0 + hq], r0 + hq, r0)
               for h, hs in enumerate(heads) for r0 in (0, hq)]
    done = attend(pl.multiple_of(t0, tq), streams, True)
    outs = []
    for h in range(len(heads)):
        acc = jnp.concatenate([done[2 * h][1], done[2 * h + 1][1]], axis=0)
        outs.append(acc / acc[:, DEN_LANE:DEN_LANE + 1])
    for h in range(0, MLA_HPS, 2):
        o_ref[:, (h // 2) * LANE:(h // 2 + 1) * LANE] = _pack_pair(outs[h], outs[h + 1]).astype(BF16)


def _mla_attn(q, k, v):
    b, s, _ = q.shape
    tq = MLA_TQ
    gw = MLA_HPS * LANE
    return pl.pallas_call(
        _mla_attn_kernel,
        grid=(b, HC // MLA_HPS, s // tq),
        in_specs=[pl.BlockSpec((None, tq, gw), lambda bi, h, i: (bi, i, h)),
                  pl.BlockSpec((None, s, gw), lambda bi, h, i: (bi, 0, h)),
                  pl.BlockSpec((None, s, gw), lambda bi, h, i: (bi, 0, h))],
        out_specs=pl.BlockSpec((None, tq, gw // 2), lambda bi, h, i: (bi, i, h)),
        out_shape=jax.ShapeDtypeStruct((b, s, MIX_W), BF16),
        compiler_params=_params("parallel", "parallel", "arbitrary"),
        name="mla_attn",
    )(q, k, v)


MERGE_TM = 1024


def _merge_kernel(x_ref, ya_ref, yb_ref, yc_ref, wm_ref, wa_ref, wb_ref, wc_ref, wo_ref,
                  g_ref, b_ref, o_ref, *, alpha):
    x = x_ref[...]
    xb = x.astype(BF16)
    mixed = jnp.zeros(x.shape, F32)
    for idx, (y_ref, w_ref) in enumerate(((ya_ref, wa_ref), (yb_ref, wb_ref), (yc_ref, wc_ref))):
        gate = jax.nn.sigmoid(_dot(xb, wm_ref[:, idx * D_MODEL:(idx + 1) * D_MODEL]))
        mixed = mixed + gate * _dot(y_ref[...], w_ref[...])
    y = alpha * x + _dot(mixed.astype(BF16), wo_ref[...])
    o_ref[...] = _ln(y, g_ref[...], b_ref[...])


def _merge(x2, ya, yb, yc, wm, wa, wb, wc, wo, lng, lnb, l, alpha):
    n = x2.shape[0]
    tm = min(MERGE_TM, n)

    def rows(width):
        return pl.BlockSpec((tm, width), lambda m: (m, 0))

    return pl.pallas_call(
        functools.partial(_merge_kernel, alpha=alpha),
        grid=(n // tm,),
        in_specs=[rows(D_MODEL), rows(MIX_W), rows(MIX_W), rows(MIX_W),
                  _resident((None, D_MODEL, 3 * D_MODEL), lambda m: (l, 0, 0)),
                  _resident((None, None, MIX_W, D_MODEL), lambda m: (l, 0, 0, 0)),
                  _resident((None, None, MIX_W, D_MODEL), lambda m: (l, 1, 0, 0)),
                  _resident((None, None, MIX_W, D_MODEL), lambda m: (l, 2, 0, 0)),
                  _resident((None, D_MODEL, D_MODEL), lambda m: (l, 0, 0)),
                  _resident((None, None, 1, D_MODEL), lambda m: (l, 1, 0, 0)),
                  _resident((None, None, 1, D_MODEL), lambda m: (l, 1, 0, 0))],
        out_specs=rows(D_MODEL),
        out_shape=jax.ShapeDtypeStruct((n, D_MODEL), F32),
        compiler_params=_params("parallel"),
        name="merge",
    )(x2, ya, yb, yc, wm, wa, wb, wc, wo, lng, lnb)


XA_TM = 1024


def _xattn_kernel(x_ref, wq_ref, k_ref, v_ref, wo_ref, g_ref, b_ref, o_ref, *, alpha):
    x = x_ref[...]
    xb = x.astype(BF16)
    q = (_dot(xb, wq_ref[...]) * (XA_DH ** -0.5 * LOG2E)).astype(BF16)
    ones = jnp.ones((k_ref.shape[0], LANE), BF16)
    heads = []
    for h in range(XA_HEADS):
        hs = slice(h * XA_DH, (h + 1) * XA_DH)
        s = _dot_nt(q[:, hs], k_ref[:, hs])
        e = jnp.exp2(s - jnp.max(s, axis=-1, keepdims=True)).astype(BF16)
        den = _dot(e, ones)
        den = jnp.concatenate([den] * (XA_DH // LANE), axis=1)
        heads.append((_dot(e, v_ref[:, hs]) / den).astype(BF16))
    att = jnp.concatenate(heads, axis=1)
    y = alpha * x + _dot(att, wo_ref[...])
    o_ref[...] = _ln(y, g_ref[...], b_ref[...])


def _xattn(x3, wq, kv, wo, lng, lnb, l, alpha):
    b, s, _ = x3.shape
    m_len = kv.shape[1]
    tm = XA_TM
    return pl.pallas_call(
        functools.partial(_xattn_kernel, alpha=alpha),
        grid=(b, s // tm),
        in_specs=[pl.BlockSpec((None, tm, D_MODEL), lambda bi, m: (bi, m, 0)),
                  _resident((None, D_MODEL, D_MODEL), lambda bi, m: (l, 0, 0)),
                  pl.BlockSpec((None, m_len, D_MODEL), lambda bi, m: (bi, 0, 0)),
                  pl.BlockSpec((None, m_len, D_MODEL), lambda bi, m: (bi, 0, 1)),
                  _resident((None, D_MODEL, D_MODEL), lambda bi, m: (l, 0, 0)),
                  _resident((None, None, 1, D_MODEL), lambda bi, m: (l, 2, 0, 0)),
                  _resident((None, None, 1, D_MODEL), lambda bi, m: (l, 2, 0, 0))],
        out_specs=pl.BlockSpec((None, tm, D_MODEL), lambda bi, m: (bi, m, 0)),
        out_shape=jax.ShapeDtypeStruct((b, s, D_MODEL), F32),
        compiler_params=_params("parallel", "parallel"),
        name="xattn",
    )(x3, wq, kv, kv, wo, lng, lnb)


def _pad_heads(w, n_heads, dh):
    depth, k, _ = w.shape
    w = w.reshape(depth, k, n_heads, dh)
    return jnp.pad(w, ((0, 0), (0, 0), (0, 0), (0, LANE - dh))).reshape(depth, k, n_heads * LANE)


def _rope_tables(positions, rot, off):
    half = rot // 2
    inv = ROPE_THETA ** (-jnp.arange(half, dtype=F32) / half)
    ang = positions.astype(F32)[..., None] * inv
    cos, sin = jnp.cos(ang), jnp.sin(ang)
    shape = positions.shape
    zeros = lambda n: jnp.zeros(shape + (n,), F32)
    cos_t = jnp.concatenate([jnp.ones(shape + (off,), F32), cos, cos,
                             jnp.ones(shape + (LANE - off - rot,), F32)], axis=-1)
    sin_lo = jnp.concatenate([zeros(off), -sin, zeros(LANE - off - half)], axis=-1)
    sin_hi = jnp.concatenate([zeros(off + half), sin, zeros(LANE - off - rot)], axis=-1)
    return cos_t, sin_lo, sin_hi


def _overlap_matrix(s_len):
    n_cmp = (s_len - CMP_L) // CMP_D + 1
    n_slc = s_len // SEL_L
    start = np.arange(n_cmp) * CMP_D
    j = np.arange(n_slc)
    ov = np.clip(np.minimum(start[:, None] + CMP_L, (j[None, :] + 1) * SEL_L)
                 - np.maximum(start[:, None], j[None, :] * SEL_L), 0, None) / CMP_L
    out = np.zeros((N_SLC, N_CMP), np.float32)
    out[:n_slc, :n_cmp] = ov.T
    place = np.zeros((N_SLC, LANE), np.float32)
    place[np.arange(N_SLC), SEL_LANE0 + np.arange(N_SLC)] = 1.0
    return jnp.asarray(out), jnp.asarray(place)


def _stacked_weights(w_in, nsa_cmp_pos, nsa_cmp_w1, nsa_cmp_w2, mla_w_uq, mla_w_ukv):
    depth = w_in.shape[0]
    part = [w_in[:, :, IN_OFF[i]:IN_OFF[i + 1]] for i in range(len(IN_SIZES))]
    (a_q, a_kc, a_vc, a_ks, a_vs, a_kw, a_vw, a_gate, b_q, b_f, b_i, b_g,
     c_q, c_kv, c_kr, merge) = part
    gate = _pad_heads(a_gate, NSA_KV, NSA_R * 3)
    kv_parts = dict(kc=a_kc, vc=a_vc, ks=a_ks, vs=a_vs, kw=a_kw, vw=a_vw)
    wa = jnp.concatenate([_pad_heads(a_q, NSA_HEADS, NSA_DH)]
                         + [kv_parts[name] for name in A_KV_ORDER] + [gate], axis=2).astype(BF16)
    wb = jnp.stack([w.reshape(depth, D_MODEL, HB, HK) for w in (b_q, b_f, b_i, b_g)], axis=3)
    wb = wb.transpose(0, 2, 1, 3, 4).reshape(depth, HB, D_MODEL, 4 * HK).astype(BF16)

    kr_pad = jnp.pad(c_kr, ((0, 0), (0, 0), (NOPE, LANE - NOPE - ROPE_D)))
    wc = jnp.concatenate([c_q, c_kv, kr_pad], axis=2).astype(BF16)
    wuq = _pad_heads(mla_w_uq, HC, NOPE + ROPE_D).astype(BF16)
    ukv = mla_w_ukv.reshape(depth, KV_RANK, HC, NOPE + VD)
    wuk = _pad_heads(ukv[..., :NOPE].reshape(depth, KV_RANK, HC * NOPE), HC, NOPE).astype(BF16)
    wuv = _pad_heads(ukv[..., NOPE:].reshape(depth, KV_RANK, HC * VD), HC, VD).astype(BF16)

    w1 = nsa_cmp_w1.reshape(depth, 2, CMP_L, NSA_DH, CMP_HID)
    w1p = jnp.pad(w1, ((0, 0), (0, 0), (0, 0), (0, LANE - NSA_DH), (0, 0)))
    w1p = w1p.reshape(depth, 2, CMP_L * LANE, CMP_HID)
    pe = jnp.pad(nsa_cmp_pos, ((0, 0), (0, 0), (0, 0), (0, LANE - NSA_DH)))
    pe8 = jnp.broadcast_to(pe.reshape(depth, 2, 1, CMP_L * LANE), (depth, 2, 8, CMP_L * LANE))
    w2p = jnp.pad(nsa_cmp_w2, ((0, 0), (0, 0), (0, 0), (0, LANE - NSA_DH)))
    return dict(wa=wa, wb=wb, wc=wc, wuq=wuq, wuk=wuk, wuv=wuv,
                w1p=w1p.astype(BF16), pe8=pe8.astype(BF16), w2p=w2p.astype(BF16),
                wm=merge.astype(BF16))


def kernel(x, mem, positions, ln_g, ln_b, ffn_w1, ffn_w3, ffn_w2, w_in, nsa_cmp_pos,
           nsa_cmp_w1, nsa_cmp_w2, hgrn_lb_logits, hgrn_norm_g, mla_q_norm_g, mla_w_uq,
           mla_kv_norm_g, mla_w_ukv, w_branch, w_out, xa_wq, xa_wk, xa_wv, xa_wo):
    b, s, d = x.shape
    depth = ln_g.shape[0]
    assert d == D_MODEL and s // SEL_L == N_SLC and (s - CMP_L) // CMP_D + 2 == N_CMP
    assert s % max(NSA_TQ, NSA_TK, MLA_TQ, PROJA_TM, MLAP_TM, XA_TM) == 0 and NSA_TK % NSA_TQ == 0
    n = b * s
    alpha = (2.0 * depth) ** 0.25

    lng = ln_g.reshape(depth, 4, 1, d)
    lnb = ln_b.reshape(depth, 4, 1, d)
    w1 = ffn_w1.astype(BF16)
    w3 = ffn_w3.astype(BF16)
    w2 = (0.5 * ffn_w2).astype(BF16)
    wo = w_out.astype(BF16)
    wbr = w_branch.astype(BF16)
    xq = xa_wq.astype(BF16)
    xkv = jnp.concatenate([xa_wk, xa_wv], axis=2).astype(BF16)
    xo = xa_wo.astype(BF16)
    p_lb = jax.nn.softmax(hgrn_lb_logits.astype(F32), axis=0)
    lower = (jnp.cumsum(p_lb, axis=0) - p_lb[0:1]).reshape(depth, HB, 1, HK)

    rope_a = _rope_tables(positions, NSA_ROT, 0)
    rope_c = _rope_tables(positions, ROPE_D, NOPE)
    ovl, place = _overlap_matrix(s)
    mem2 = mem.reshape(b * mem.shape[1], d)
    w = _stacked_weights(w_in, nsa_cmp_pos, nsa_cmp_w1, nsa_cmp_w2, mla_w_uq, mla_w_ukv)
    norm_b = hgrn_norm_g.reshape(depth, 1, HV)
    norm_q = mla_q_norm_g.reshape(depth, 1, Q_RANK)
    norm_kv = mla_kv_norm_g.reshape(depth, 1, KV_RANK)

    x2 = x.reshape(n, d)
    for l in range(depth):
        x2 = _ffn_ln(x2, w1, w3, w2, lng, lnb, l, 0, 0, alpha)
        x3 = x2.reshape(b, s, d)

        q, qr, kc, vc, ks, vs, kw, vw, gates = _proja(x3, w["wa"], l, *rope_a)
        kcmp, vcmp = _cmp(kc, vc, w["w1p"], w["pe8"], w["w2p"], l)
        ya = _nsa_attn(q, qr, kcmp, vcmp, ks, vs, kw, vw, gates, ovl, place)

        yb = _hgrn(x3, w["wb"], lower, norm_b, l)

        mq, mk, mv = _mlap(x3, w["wc"], norm_q, norm_kv, w["wuq"], w["wuk"], w["wuv"], l, *rope_c)
        yc = _mla_attn(mq, mk, mv)

        x2 = _merge(x2, ya.reshape(n, MIX_W), yb.reshape(n, MIX_W), yc.reshape(n, MIX_W),
                    w["wm"], wbr, wbr, wbr, wo, lng, lnb, l, alpha)

        kv_l = _proj(mem2, xkv, l, BF16).reshape(b, -1, 2 * d)
        x2 = _xattn(x2.reshape(b, s, d), xq, kv_l, xo, lng, lnb, l, alpha).reshape(n, d)

        x2 = _ffn_ln(x2, w1, w3, w2, lng, lnb, l, 1, 3, alpha)
    return x2.reshape(b, s, d)
```

```python
import functools

import numpy as np
import jax
import jax.numpy as jnp
from jax import lax
from jax.experimental import pallas as pl
from jax.experimental.pallas import tpu as pltpu

F32 = jnp.float32
BF16 = jnp.bfloat16

D_MODEL = 1024
MIX_W = D_MODEL // 2
NSA_DH = 64
NSA_HEADS = 8
NSA_KV = 2
NSA_R = 4
NSA_ROT = 16
CMP_L = 32
CMP_D = 16
CMP_HID = 256
SEL_L = 64
N_SEL = 8
WINDOW = 256
HB = 4
HK = 128
HV = 128
HGRN_CHUNK = 64
HGRN_SUB = 4
HGRN_PARTS = 4
HC = 8
NOPE = 64
ROPE_D = 32
VD = 64
Q_RANK = 384
KV_RANK = 256
XA_HEADS = 4
XA_DH = D_MODEL // XA_HEADS
D_FF = 2816
ROPE_THETA = 500000.0
LN_EPS = 1e-5
RMS_EPS = 1e-6
NEG = -1e30
BIG = 1e9
F_MIN = 1e-20

LOG2E = 1.4426950408889634
SQRT_2_OVER_PI = 0.7978845608028654
LANE = 128
SUBLANES = 8
SEL_LANE0 = NSA_DH
DEN_LANE = 64
VMEM_LIMIT = 56 * 1024 * 1024

IN_SIZES = (512, 128, 128, 128, 128, 128, 128, 24, 512, 512, 512, 512, 384, 256, 32, 3072)
IN_OFF = tuple(int(v) for v in np.concatenate([[0], np.cumsum(IN_SIZES)]))


def _dot(a, b):
    return jnp.dot(a, b, preferred_element_type=F32)


def _dot_nt(a, b):
    return lax.dot_general(a, b, (((1,), (1,)), ((), ())), preferred_element_type=F32)


def _ln(y, g, b):
    mu = jnp.mean(y, axis=-1, keepdims=True)
    yc = y - mu
    var = jnp.mean(yc * yc, axis=-1, keepdims=True)
    return yc * lax.rsqrt(var + LN_EPS) * g + b


def _pack_pair(a, b):
    lane = lax.broadcasted_iota(jnp.int32, a.shape, a.ndim - 1)
    return jnp.where(lane < LANE // 2, a, pltpu.roll(b, LANE // 2, axis=a.ndim - 1))


def _params(*sem):
    return pltpu.CompilerParams(dimension_semantics=sem, vmem_limit_bytes=VMEM_LIMIT)


def _resident(shape, index_map):
    return pl.BlockSpec(shape, index_map, pipeline_mode=pl.Buffered(1))


FFN_TM = 2048
FFN_TF = 256


def _ffn_kernel(x_ref, w1_ref, w3_ref, w2h_ref, g_ref, b_ref, o_ref, acc_ref, xb_ref, *, alpha):
    j = pl.program_id(1)

    @pl.when(j == 0)
    def _():
        xb_ref[...] = x_ref[...].astype(BF16)
        acc_ref[...] = alpha * x_ref[...]

    xb = xb_ref[...]
    h1 = _dot(xb, w1_ref[...])
    h3 = _dot(xb, w3_ref[...])
    h = (h1 * jax.nn.sigmoid(h1)) * h3
    acc_ref[...] += _dot(h.astype(BF16), w2h_ref[...])

    @pl.when(j == pl.num_programs(1) - 1)
    def _():
        o_ref[...] = _ln(acc_ref[...], g_ref[...], b_ref[...])


def _ffn_ln(x2, w1, w3, w2h, lng, lnb, l, which, ln_idx, alpha):
    n = x2.shape[0]
    tm = min(FFN_TM, n)
    grid = (n // tm, D_FF // FFN_TF)
    return pl.pallas_call(
        functools.partial(_ffn_kernel, alpha=alpha),
        grid=grid,
        in_specs=[
            pl.BlockSpec((tm, D_MODEL), lambda m, j: (m, 0)),
            pl.BlockSpec((None, None, D_MODEL, FFN_TF), lambda m, j: (l, which, 0, j)),
            pl.BlockSpec((None, None, D_MODEL, FFN_TF), lambda m, j: (l, which, 0, j)),
            pl.BlockSpec((None, None, FFN_TF, D_MODEL), lambda m, j: (l, which, j, 0)),
            pl.BlockSpec((None, None, 1, D_MODEL), lambda m, j: (l, ln_idx, 0, 0)),
            pl.BlockSpec((None, None, 1, D_MODEL), lambda m, j: (l, ln_idx, 0, 0)),
        ],
        out_specs=pl.BlockSpec((tm, D_MODEL), lambda m, j: (m, 0)),
        out_shape=jax.ShapeDtypeStruct((n, D_MODEL), F32),
        scratch_shapes=[pltpu.VMEM((tm, D_MODEL), F32), pltpu.VMEM((tm, D_MODEL), BF16)],
        compiler_params=_params("parallel", "arbitrary"),
        name="ffn_ln",
    )(x2, w1, w3, w2h, lng, lnb)


def _proj_kernel(x_ref, w_ref, o_ref):
    o_ref[...] = _dot(x_ref[...].astype(BF16), w_ref[...]).astype(o_ref.dtype)


def _proj(x2, w, l, out_dtype, tm=1024, tn=2048):
    n, k = x2.shape
    c = w.shape[2]
    tm = min(tm, n)
    tn = min(tn, c)
    return pl.pallas_call(
        _proj_kernel,
        grid=(n // tm, c // tn),
        in_specs=[pl.BlockSpec((tm, k), lambda m, j: (m, 0)),
                  pl.BlockSpec((None, k, tn), lambda m, j: (l, 0, j))],
        out_specs=pl.BlockSpec((tm, tn), lambda m, j: (m, j)),
        out_shape=jax.ShapeDtypeStruct((n, c), out_dtype),
        compiler_params=_params("parallel", "arbitrary"),
        name="proj",
    )(x2, w)


NSA_QW = NSA_HEADS * LANE
NSA_KW = NSA_KV * LANE
A_Q = 0
A_KV = NSA_QW
A_KV_ORDER = ("kc", "vc", "ks", "vs", "kw", "vw")
A_GATE = A_KV + len(A_KV_ORDER) * LANE
A_COLS = A_GATE + NSA_KW
PROJA_TM = 1024


def _rope_lanes(y, cos, sin_lo, sin_hi, half):
    return (y * cos + pltpu.roll(y, LANE - half, axis=1) * sin_lo
            + pltpu.roll(y, half, axis=1) * sin_hi)


def _proja_kernel(x_ref, w_ref, cos_ref, slo_ref, shi_ref,
                  q_ref, qr_ref, kc_ref, vc_ref, ks_ref, vs_ref, kw_ref, vw_ref, g_ref):
    xb = x_ref[...].astype(BF16)
    cos = cos_ref[...]
    sin_lo = slo_ref[...]
    sin_hi = shi_ref[...]
    half = NSA_ROT // 2
    tm = x_ref.shape[0]

    def mm(c0, width):
        return _dot(xb, w_ref[:, c0:c0 + width])

    qs = NSA_DH ** -0.5 * LOG2E
    y = mm(A_Q, NSA_QW)
    q_ref[...] = (y * qs).astype(BF16)
    for h in range(NSA_HEADS):
        hs = slice(h * LANE, (h + 1) * LANE)
        qr_ref[:, hs] = (_rope_lanes(y[:, hs], cos, sin_lo, sin_hi, half) * qs).astype(BF16)

    lane = lax.broadcasted_iota(jnp.int32, (tm, LANE), 1)
    first = lane < NSA_DH
    tok = pl.program_id(1) * tm + lax.broadcasted_iota(jnp.int32, (tm, LANE), 0)
    blk_flag = jnp.where(lane - SEL_LANE0 == tok // SEL_L, 1.0, 0.0)
    den_flag = jnp.where(lane == DEN_LANE, 1.0, 0.0)

    def packed(table):
        return jnp.where(first, table, pltpu.roll(table, NSA_DH, axis=1))

    def put(ref, y, flag=None):
        for g, yg in enumerate((y, pltpu.roll(y, NSA_DH, axis=1))):
            yg = jnp.where(first, yg, 0.0)
            ref[g] = (yg if flag is None else yg + flag).astype(ref.dtype)

    kv = mm(A_KV, len(A_KV_ORDER) * LANE)
    part = {name: kv[:, i * LANE:(i + 1) * LANE] for i, name in enumerate(A_KV_ORDER)}
    cos_p, lo_p, hi_p = packed(cos), packed(sin_lo), packed(sin_hi)
    put(kc_ref, part["kc"])
    put(vc_ref, part["vc"])
    put(ks_ref, _rope_lanes(part["ks"], cos_p, lo_p, hi_p, half), blk_flag)
    put(vs_ref, part["vs"], den_flag)
    put(kw_ref, _rope_lanes(part["kw"], cos_p, lo_p, hi_p, half))
    put(vw_ref, part["vw"], den_flag)
    g_ref[...] = mm(A_GATE, NSA_KW)


def _proja(x3, wa, l, cos_a, slo_a, shi_a):
    b, s, _ = x3.shape
    tm = PROJA_TM
    kv_shape = jax.ShapeDtypeStruct((b, NSA_KV, s, LANE), BF16)
    cmp_shape = jax.ShapeDtypeStruct((b, NSA_KV, s, LANE), F32)
    kv_spec = pl.BlockSpec((None, NSA_KV, tm, LANE), lambda bi, m: (bi, 0, m, 0))
    return pl.pallas_call(
        _proja_kernel,
        grid=(b, s // tm),
        in_specs=[
            pl.BlockSpec((None, tm, D_MODEL), lambda bi, m: (bi, m, 0)),
            _resident((None, D_MODEL, A_COLS), lambda bi, m: (l, 0, 0)),
            pl.BlockSpec((None, tm, LANE), lambda bi, m: (bi, m, 0)),
            pl.BlockSpec((None, tm, LANE), lambda bi, m: (bi, m, 0)),
            pl.BlockSpec((None, tm, LANE), lambda bi, m: (bi, m, 0)),
        ],
        out_specs=[
            pl.BlockSpec((None, tm, NSA_QW), lambda bi, m: (bi, m, 0)),
            pl.BlockSpec((None, tm, NSA_QW), lambda bi, m: (bi, m, 0)),
            kv_spec, kv_spec, kv_spec, kv_spec, kv_spec, kv_spec,
            pl.BlockSpec((None, tm, NSA_KW), lambda bi, m: (bi, m, 0)),
        ],
        out_shape=[
            jax.ShapeDtypeStruct((b, s, NSA_QW), BF16),
            jax.ShapeDtypeStruct((b, s, NSA_QW), BF16),
            cmp_shape, cmp_shape, kv_shape, kv_shape, kv_shape, kv_shape,
            jax.ShapeDtypeStruct((b, s, NSA_KW), F32),
        ],
        compiler_params=_params("parallel", "parallel"),
        name="nsa_proj",
    )(x3, wa, cos_a, slo_a, shi_a)


N_CMP = 128
CMP_HALF = CMP_D * LANE


def _gelu_tanh(x):
    return 0.5 * x * (1.0 + jnp.tanh(SQRT_2_OVER_PI * (x + 0.044715 * x * x * x)))


def _cmp_kernel(zk_ref, zv_ref, w1_ref, pe_ref, w2_ref, ok_ref, ov_ref):
    for which, (z_ref, o_ref) in enumerate(((zk_ref, ok_ref), (zv_ref, ov_ref))):
        first = jnp.zeros((N_CMP, CMP_HID), F32)
        second = jnp.zeros((N_CMP, CMP_HID), F32)
        for j in range(CMP_D):
            zj = z_ref[pl.ds(j, N_CMP, stride=CMP_D), :].astype(BF16)
            first = first + _dot(zj, w1_ref[which, j * LANE:(j + 1) * LANE, :])
            second = second + _dot(zj, w1_ref[which, CMP_HALF + j * LANE:CMP_HALF + (j + 1) * LANE, :])
        bias = _dot(pe_ref[which], w1_ref[which])[0:1]
        pre = first + pltpu.roll(second, N_CMP - 1, axis=0) + bias
        h = _gelu_tanh(pre)
        o_ref[...] = _dot(h.astype(BF16), w2_ref[which]).astype(BF16)


def _cmp(zk, zv, w1p, pe8, w2p, l):
    b, _, s, _ = zk.shape
    z_spec = pl.BlockSpec((None, None, s, LANE), lambda bi, g: (bi, g, 0, 0))
    o_spec = pl.BlockSpec((None, None, N_CMP, LANE), lambda bi, g: (bi, g, 0, 0))
    o_shape = jax.ShapeDtypeStruct((b, NSA_KV, N_CMP, LANE), BF16)
    return pl.pallas_call(
        _cmp_kernel,
        grid=(b, NSA_KV),
        in_specs=[
            z_spec, z_spec,
            _resident((None, 2, 2 * CMP_HALF, CMP_HID), lambda bi, g: (l, 0, 0, 0)),
            _resident((None, 2, 8, 2 * CMP_HALF), lambda bi, g: (l, 0, 0, 0)),
            _resident((None, 2, CMP_HID, LANE), lambda bi, g: (l, 0, 0, 0)),
        ],
        out_specs=[o_spec, o_spec],
        out_shape=[o_shape, o_shape],
        compiler_params=_params("parallel", "parallel"),
        name="nsa_cmp",
    )(zk, zv, w1p, pe8, w2p)


NSA_TQ = 256
NSA_TK = 512
N_SLC = 32
WIN_SLAB = WINDOW + NSA_TQ


def _nsa_kernel(q_ref, qr_ref, kc_ref, vc_ref, ks_ref, vs_ref, kw_ref, vw_ref, gt_ref, ovl_ref,
                place_ref, o_ref):
    tq, tk, r_heads = NSA_TQ, NSA_TK, NSA_R
    i = pl.program_id(2)
    t0 = i * tq
    tpos = t0 + lax.broadcasted_iota(jnp.int32, (tq, 1), 0)
    lane = lax.broadcasted_iota(jnp.int32, (1, LANE), 1)

    def stack(ref):
        return jnp.concatenate([ref[:, r * LANE:(r + 1) * LANE] for r in range(r_heads)], axis=0)

    gt = jax.nn.sigmoid(gt_ref[...])

    def gate(c):
        return jnp.stack([gt[:, 3 * r + c:3 * r + c + 1] for r in range(r_heads)], axis=0)

    q4w = stack(qr_ref)
    w0 = pl.multiple_of(jnp.maximum(t0 - WINDOW, 0), LANE)
    kw = kw_ref[pl.ds(w0, WIN_SLAB), :]
    vw = vw_ref[pl.ds(w0, WIN_SLAB), :]
    kpos = w0 + lax.broadcasted_iota(jnp.int32, (1, WIN_SLAB), 1)
    w_bias = jnp.where((kpos <= tpos) & (kpos > tpos - WINDOW), 0.0, NEG)
    s = _dot_nt(q4w, kw).reshape(r_heads, tq, WIN_SLAB) + w_bias[None]
    p = jnp.exp2(s - jnp.max(s, axis=-1, keepdims=True))
    o_win = _dot(p.reshape(r_heads * tq, WIN_SLAB).astype(BF16), vw).reshape(r_heads, tq, LANE)
    o_part = gate(2) * (o_win / o_win[:, :, DEN_LANE:DEN_LANE + 1])

    cmask = ((lane * CMP_D + (CMP_L - 1) <= tpos) & (lane < N_CMP - 1))[None]
    s = _dot_nt(stack(q_ref), kc_ref[...]).reshape(r_heads, tq, N_CMP)
    s = jnp.where(cmask, s, NEG)
    e = jnp.where(cmask, jnp.exp2(s - jnp.max(s, axis=-1, keepdims=True)), 0.0)
    den = jnp.sum(e, axis=-1, keepdims=True)
    p = e / jnp.where(den > 0.0, den, 1.0)
    psum = jnp.sum(p, axis=0)
    o_cmp = _dot(p.reshape(r_heads * tq, N_CMP).astype(BF16), vc_ref[...])
    o_part = o_part + gate(0) * o_cmp.reshape(r_heads, tq, LANE)

    imp = lax.dot_general(ovl_ref[...], psum, (((1,), (1,)), ((), ())),
                          precision=lax.Precision.HIGHEST, preferred_element_type=F32)
    blk = lax.broadcasted_iota(jnp.int32, (N_SLC, 1), 0)
    tpos_l = t0 + lax.broadcasted_iota(jnp.int32, (1, tq), 1)
    cur = tpos_l // SEL_L
    valid = blk * SEL_L <= tpos_l
    forced = (blk == 0) | (blk == cur) | (blk == cur - 1)
    score = jnp.where(valid & forced, BIG, jnp.where(valid, imp, -BIG))
    beats = []
    for j in range(N_SLC):
        sj = score[j:j + 1, :]
        beats.append(jnp.where((sj > score) | ((sj == score) & (j < blk)), 1.0, 0.0))
    while len(beats) > 1:
        beats = [a + b for a, b in zip(beats[0::2], beats[1::2])]
    sel_t = jnp.where((beats[0] < N_SEL) & valid, 1.0, 0.0)
    sel_q = lax.dot_general(sel_t, place_ref[...], (((0,), (0,)), ((), ())),
                            preferred_element_type=F32)
    in_flags = jnp.where((lane >= SEL_LANE0) & (lane < SEL_LANE0 + N_SLC), 1.0, 0.0)
    q_bias = (sel_q - in_flags) * (-NEG)
    q4 = (q4w.astype(F32).reshape(r_heads, tq, LANE) + q_bias[None]).astype(BF16)
    q4 = q4.reshape(r_heads * tq, LANE)

    def sel_step(c, carry, causal):
        m, acc = carry
        k0 = pl.multiple_of(c * tk, tk)
        k = ks_ref[pl.ds(k0, tk), :]
        v = vs_ref[pl.ds(k0, tk), :]
        s = _dot_nt(q4, k).reshape(r_heads, tq, tk)
        if causal:
            kpos = k0 + lax.broadcasted_iota(jnp.int32, (1, tk), 1)
            s = jnp.where((kpos <= tpos)[None], s, NEG)
        m_new = jnp.maximum(m, jnp.max(s, axis=-1, keepdims=True))
        p = jnp.exp2(s - m_new)
        pv = _dot(p.reshape(r_heads * tq, tk).astype(BF16), v).reshape(r_heads, tq, LANE)
        return m_new, jnp.exp2(m - m_new) * acc + pv

    last = (t0 + tq - 1) // tk
    carry = (jnp.full((r_heads, tq, 1), NEG, F32), jnp.zeros((r_heads, tq, LANE), F32))
    carry = lax.fori_loop(0, last, functools.partial(sel_step, causal=False), carry)
    _, acc_s = sel_step(last, carry, True)
    o = o_part + gate(1) * (acc_s / acc_s[:, :, DEN_LANE:DEN_LANE + 1])
    for r in range(0, r_heads, 2):
        o_ref[:, (r // 2) * LANE:(r // 2 + 1) * LANE] = _pack_pair(o[r], o[r + 1]).astype(BF16)


def _nsa_attn(q, qr, kcmp, vcmp, ks, vs, kw, vw, gates, ovl, place):
    b, s, _ = q.shape
    tq = NSA_TQ
    gw = NSA_R * LANE
    q_spec = pl.BlockSpec((None, tq, gw), lambda bi, g, i: (bi, i, g))
    c_spec = pl.BlockSpec((None, None, N_CMP, LANE), lambda bi, g, i: (bi, g, 0, 0))
    kv_spec = pl.BlockSpec((None, None, s, LANE), lambda bi, g, i: (bi, g, 0, 0))
    return pl.pallas_call(
        _nsa_kernel,
        grid=(b, NSA_KV, s // tq),
        in_specs=[q_spec, q_spec, c_spec, c_spec, kv_spec, kv_spec, kv_spec, kv_spec,
                  pl.BlockSpec((None, tq, LANE), lambda bi, g, i: (bi, i, g)),
                  _resident((N_SLC, N_CMP), lambda bi, g, i: (0, 0)),
                  _resident((N_SLC, LANE), lambda bi, g, i: (0, 0))],
        out_specs=pl.BlockSpec((None, tq, gw // 2), lambda bi, g, i: (bi, i, g)),
        out_shape=jax.ShapeDtypeStruct((b, s, MIX_W), BF16),
        compiler_params=_params("parallel", "parallel", "arbitrary"),
        name="nsa_attn",
    )(q, qr, kcmp, vcmp, ks, vs, kw, vw, gates, ovl, place)


def _hgrn_kernel(x_ref, w_ref, lb_ref, ng_ref, o_ref, st_ref):
    s_len = x_ref.shape[0]
    c_len, sub_len = HGRN_CHUNK, HGRN_SUB
    p_len = s_len // HGRN_PARTS
    nc = p_len // c_len
    lb = lb_ref[...]
    t = lax.broadcasted_iota(jnp.int32, (p_len, 1), 0)
    t_sub = t % sub_len
    row = lax.broadcasted_iota(jnp.int32, (nc, c_len, c_len), 1)
    col = lax.broadcasted_iota(jnp.int32, (nc, c_len, c_len), 2)
    tri = jnp.where(col <= row, 1.0, 0.0).astype(BF16)

    local = []
    for part in range(HGRN_PARTS):
        proj = _dot(x_ref[pl.ds(part * p_len, p_len), :].astype(BF16), w_ref[...])
        q = proj[:, 0:HK]
        z = proj[:, HK:2 * HK]
        v = proj[:, 2 * HK:2 * HK + HV]
        go = proj[:, 2 * HK + HV:2 * HK + 2 * HV]

        sig = jax.nn.sigmoid(z)
        f = lb + (1.0 - lb) * sig
        lf = jnp.log(jnp.maximum(f, F_MIN))
        k = (1.0 - lb) * (1.0 - sig)

        lf3 = lf.reshape(nc, c_len, HK)
        hi = lf3.astype(BF16)
        rest = lf3 - hi.astype(F32)
        mid = rest.astype(BF16)
        low = (rest - mid.astype(F32)).astype(BF16)
        sums = jnp.einsum('cts,csd->ctd', tri, jnp.concatenate([hi, mid, low], axis=-1),
                          preferred_element_type=F32)
        b3 = sums[..., 0:HK] + sums[..., HK:2 * HK] + sums[..., 2 * HK:3 * HK]
        b = b3.reshape(p_len, HK)
        k3 = k.reshape(nc, c_len, HK)
        vb3 = v.astype(BF16).reshape(nc, c_len, HV)

        a_intra = None
        h = c_len // 2
        while h >= sub_len:
            blk = b.reshape(p_len // (2 * h), 2 * h, HK)
            e = jnp.exp(-jnp.abs(blk[:, h - 1:h, :] - blk)).reshape(p_len, HK)
            upper = (t // h) % 2 == 1
            lq = jnp.where(upper, q * e, 0.0).astype(BF16).reshape(nc, c_len, HK)
            rk = jnp.where(upper, 0.0, k * e).astype(BF16).reshape(nc, c_len, HK)
            a = jnp.einsum('ctk,csk->cts', lq, rk, preferred_element_type=F32)
            if 2 * h < c_len:
                a = jnp.where(row // (2 * h) == col // (2 * h), a, 0.0)
            a_intra = a if a_intra is None else a_intra + a
            h //= 2
        o = jnp.einsum('cts,csd->ctd', a_intra.astype(BF16), vb3,
                       preferred_element_type=F32).reshape(p_len, HV)

        o = o + jnp.sum(q * k, axis=-1, keepdims=True) * v

        def back(x, d):
            x3 = x.reshape(p_len // SUBLANES, SUBLANES, x.shape[-1])
            return pltpu.roll(x3, d, axis=1).reshape(x.shape)

        for d in range(1, sub_len):
            e = jnp.exp(jnp.where(t_sub >= d, b - back(b, d), NEG))
            w = jnp.sum(q * back(k, d) * e, axis=-1, keepdims=True)
            o = o + w * back(v, d)

        b_last = b3[:, c_len - 1:c_len, :]
        kd3 = (k3 * jnp.exp(b_last - b3)).astype(BF16)
        upd = jnp.einsum('csv,csk->cvk', vb3, kd3, preferred_element_type=F32)
        qe3 = (q * jnp.exp(b)).astype(BF16).reshape(nc, c_len, HK)
        local.append((o, go, qe3, upd, jnp.exp(b_last)))

    state = jnp.zeros((HV, HK), F32)
    for part, (_, _, _, upd, dec) in enumerate(local):
        for c in range(nc):
            st_ref[part * nc + c] = state.astype(BF16)
            state = state * dec[c] + upd[c]

    for part, (o, go, qe3, _, _) in enumerate(local):
        o = o + jnp.einsum('ctk,cvk->ctv', qe3, st_ref[pl.ds(part * nc, nc)],
                           preferred_element_type=F32).reshape(p_len, HV)
        o = o * lax.rsqrt(jnp.mean(o * o, axis=-1, keepdims=True) + RMS_EPS) * ng_ref[...]
        o_ref[pl.ds(part * p_len, p_len), :] = (o * (go * jax.nn.sigmoid(go))).astype(BF16)


def _hgrn(x3, wb, lb, ng, l):
    b, s, d = x3.shape
    return pl.pallas_call(
        _hgrn_kernel,
        grid=(b, HB),
        in_specs=[pl.BlockSpec((None, s, d), lambda bi, h: (bi, 0, 0)),
                  pl.BlockSpec((None, None, d, 4 * HK), lambda bi, h: (l, h, 0, 0)),
                  pl.BlockSpec((None, None, 1, HK), lambda bi, h: (l, h, 0, 0)),
                  pl.BlockSpec((None, 1, HV), lambda bi, h: (l, 0, 0))],
        out_specs=pl.BlockSpec((None, s, HV), lambda bi, h: (bi, 0, h)),
        out_shape=jax.ShapeDtypeStruct((b, s, HB * HV), BF16),
        scratch_shapes=[pltpu.VMEM((s // HGRN_CHUNK, HV, HK), BF16)],
        compiler_params=_params("parallel", "arbitrary"),
        name="hgrn",
    )(x3, wb, lb, ng)


MLA_W = HC * LANE
C_CQ, C_CKV, C_KR = 0, Q_RANK, Q_RANK + KV_RANK
C_COLS = C_KR + LANE
MLAP_TM = 1024


def _rms(x, g):
    return x * lax.rsqrt(jnp.mean(x * x, axis=-1, keepdims=True) + RMS_EPS) * g


def _mlap_kernel(x_ref, wc_ref, qg_ref, kvg_ref, wuq_ref, wuk_ref, wuv_ref,
                 cos_ref, slo_ref, shi_ref, q_ref, k_ref, v_ref):
    half = ROPE_D // 2
    qs = (NOPE + ROPE_D) ** -0.5 * LOG2E
    den_flag = jnp.where(lax.broadcasted_iota(jnp.int32, (1, LANE), 1) == DEN_LANE, 1.0, 0.0)
    cos, lo, hi = cos_ref[...], slo_ref[...], shi_ref[...]
    c = _dot(x_ref[...].astype(BF16), wc_ref[...])
    nq = _rms(c[:, C_CQ:C_CKV], qg_ref[...]).astype(BF16)
    nkv = _rms(c[:, C_CKV:C_KR], kvg_ref[...]).astype(BF16)
    k_pe = _rope_lanes(c[:, C_KR:C_COLS], cos, lo, hi, half)
    yq = _dot(nq, wuq_ref[...])
    yk = _dot(nkv, wuk_ref[...])
    yv = _dot(nkv, wuv_ref[...])
    cos_q, lo_q, hi_q = cos * qs, lo * qs, hi * qs
    for h in range(HC):
        hs = slice(h * LANE, (h + 1) * LANE)
        q_ref[:, hs] = _rope_lanes(yq[:, hs], cos_q, lo_q, hi_q, half).astype(BF16)
        k_ref[:, hs] = (yk[:, hs] + k_pe).astype(BF16)
        v_ref[:, hs] = (yv[:, hs] + den_flag).astype(BF16)


def _mlap(x3, wc, qg, kvg, wuq, wuk, wuv, l, cos_c, slo_c, shi_c):
    b, s, _ = x3.shape
    tm = MLAP_TM
    o_spec = pl.BlockSpec((None, tm, MLA_W), lambda bi, m: (bi, m, 0))
    o_shape = jax.ShapeDtypeStruct((b, s, MLA_W), BF16)
    t_spec = pl.BlockSpec((None, tm, LANE), lambda bi, m: (bi, m, 0))
    return pl.pallas_call(
        _mlap_kernel,
        grid=(b, s // tm),
        in_specs=[
            pl.BlockSpec((None, tm, D_MODEL), lambda bi, m: (bi, m, 0)),
            _resident((None, D_MODEL, C_COLS), lambda bi, m: (l, 0, 0)),
            _resident((None, 1, Q_RANK), lambda bi, m: (l, 0, 0)),
            _resident((None, 1, KV_RANK), lambda bi, m: (l, 0, 0)),
            _resident((None, Q_RANK, MLA_W), lambda bi, m: (l, 0, 0)),
            _resident((None, KV_RANK, MLA_W), lambda bi, m: (l, 0, 0)),
            _resident((None, KV_RANK, MLA_W), lambda bi, m: (l, 0, 0)),
            t_spec, t_spec, t_spec,
        ],
        out_specs=[o_spec, o_spec, o_spec],
        out_shape=[o_shape, o_shape, o_shape],
        compiler_params=_params("parallel", "parallel"),
        name="mla_proj",
    )(x3, wc, qg, kvg, wuq, wuk, wuv, cos_c, slo_c, shi_c)


MLA_TQ = 512
MLA_TK = MLA_TQ


MLA_HPS = 4


def _mla_attn_kernel(q_ref, k_ref, v_ref, o_ref):
    tq, tk = MLA_TQ, MLA_TK
    i = pl.program_id(2)
    t0 = i * tq
    heads = [slice(h * LANE, (h + 1) * LANE) for h in range(MLA_HPS)]
    qs = [q_ref[:, hs] for hs in heads]

    def attend(k0, streams, causal):
        scores = [_dot_nt(q, k_ref[pl.ds(k0, nk), hs]) for q, hs, _, _, nk, _ in streams]
        m_new, probs = [], []
        for s, (q, _, m, _, nk, r0) in zip(scores, streams):
            if causal:
                kpos = k0 + lax.broadcasted_iota(jnp.int32, (1, nk), 1)
                qpos = t0 + r0 + lax.broadcasted_iota(jnp.int32, (q.shape[0], 1), 0)
                s = jnp.where(kpos <= qpos, s, NEG)
            m_new.append(jnp.maximum(m, jnp.max(s, axis=-1, keepdims=True)))
            probs.append(jnp.exp2(s - m_new[-1]).astype(BF16))
        pvs = [_dot(p, v_ref[pl.ds(k0, nk), hs]) for p, (_, hs, _, _, nk, _) in zip(probs, streams)]
        return [(mn, jnp.exp2(m - mn) * acc + pv)
                for mn, pv, (_, _, m, acc, _, _) in zip(m_new, pvs, streams)]

    def full_chunk(c, carry):
        streams = [(qs[h], hs, *carry[h], tk, 0) for h, hs in enumerate(heads)]
        return tuple(attend(pl.multiple_of(c * tk, tk), streams, False))

    carry = tuple((jnp.full((tq, 1), NEG, F32), jnp.zeros((tq, LANE), F32)) for _ in heads)
    carry = lax.fori_loop(0, i, full_chunk, carry)
    hq = tq // 2
    streams = [(qs[h][r0:r0 + hq], hs, carry[h][0][r0:r0 + hq], carry[h][1][r0:r0 + hq], r0 + hq, r0)
               for h, hs in enumerate(heads) for r0 in (0, hq)]
    done = attend(pl.multiple_of(t0, tq), streams, True)
    outs = []
    for h in range(len(heads)):
        acc = jnp.concatenate([done[2 * h][1], done[2 * h + 1][1]], axis=0)
        outs.append(acc / acc[:, DEN_LANE:DEN_LANE + 1])
    for h in range(0, MLA_HPS, 2):
        o_ref[:, (h // 2) * LANE:(h // 2 + 1) * LANE] = _pack_pair(outs[h], outs[h + 1]).astype(BF16)


def _mla_attn(q, k, v):
    b, s, _ = q.shape
    tq = MLA_TQ
    gw = MLA_HPS * LANE
    return pl.pallas_call(
        _mla_attn_kernel,
        grid=(b, HC // MLA_HPS, s // tq),
        in_specs=[pl.BlockSpec((None, tq, gw), lambda bi, h, i: (bi, i, h)),
                  pl.BlockSpec((None, s, gw), lambda bi, h, i: (bi, 0, h)),
                  pl.BlockSpec((None, s, gw), lambda bi, h, i: (bi, 0, h))],
        out_specs=pl.BlockSpec((None, tq, gw // 2), lambda bi, h, i: (bi, i, h)),
        out_shape=jax.ShapeDtypeStruct((b, s, MIX_W), BF16),
        compiler_params=_params("parallel", "parallel", "arbitrary"),
        name="mla_attn",
    )(q, k, v)


MERGE_TM = 1024


def _merge_kernel(x_ref, ya_ref, yb_ref, yc_ref, wm_ref, wa_ref, wb_ref, wc_ref, wo_ref,
                  g_ref, b_ref, o_ref, *, alpha):
    x = x_ref[...]
    xb = x.astype(BF16)
    mixed = jnp.zeros(x.shape, F32)
    for idx, (y_ref, w_ref) in enumerate(((ya_ref, wa_ref), (yb_ref, wb_ref), (yc_ref, wc_ref))):
        gate = jax.nn.sigmoid(_dot(xb, wm_ref[:, idx * D_MODEL:(idx + 1) * D_MODEL]))
        mixed = mixed + gate * _dot(y_ref[...], w_ref[...])
    y = alpha * x + _dot(mixed.astype(BF16), wo_ref[...])
    o_ref[...] = _ln(y, g_ref[...], b_ref[...])


def _merge(x2, ya, yb, yc, wm, wa, wb, wc, wo, lng, lnb, l, alpha):
    n = x2.shape[0]
    tm = min(MERGE_TM, n)

    def rows(width):
        return pl.BlockSpec((tm, width), lambda m: (m, 0))

    return pl.pallas_call(
        functools.partial(_merge_kernel, alpha=alpha),
        grid=(n // tm,),
        in_specs=[rows(D_MODEL), rows(MIX_W), rows(MIX_W), rows(MIX_W),
                  _resident((None, D_MODEL, 3 * D_MODEL), lambda m: (l, 0, 0)),
                  _resident((None, None, MIX_W, D_MODEL), lambda m: (l, 0, 0, 0)),
                  _resident((None, None, MIX_W, D_MODEL), lambda m: (l, 1, 0, 0)),
                  _resident((None, None, MIX_W, D_MODEL), lambda m: (l, 2, 0, 0)),
                  _resident((None, D_MODEL, D_MODEL), lambda m: (l, 0, 0)),
                  _resident((None, None, 1, D_MODEL), lambda m: (l, 1, 0, 0)),
                  _resident((None, None, 1, D_MODEL), lambda m: (l, 1, 0, 0))],
        out_specs=rows(D_MODEL),
        out_shape=jax.ShapeDtypeStruct((n, D_MODEL), F32),
        compiler_params=_params("parallel"),
        name="merge",
    )(x2, ya, yb, yc, wm, wa, wb, wc, wo, lng, lnb)


XA_TM = 1024


def _xattn_kernel(x_ref, wq_ref, k_ref, v_ref, wo_ref, g_ref, b_ref, o_ref, *, alpha):
    x = x_ref[...]
    xb = x.astype(BF16)
    q = (_dot(xb, wq_ref[...]) * (XA_DH ** -0.5 * LOG2E)).astype(BF16)
    ones = jnp.ones((k_ref.shape[0], LANE), BF16)
    heads = []
    for h in range(XA_HEADS):
        hs = slice(h * XA_DH, (h + 1) * XA_DH)
        s = _dot_nt(q[:, hs], k_ref[:, hs])
        e = jnp.exp2(s - jnp.max(s, axis=-1, keepdims=True)).astype(BF16)
        den = _dot(e, ones)
        den = jnp.concatenate([den] * (XA_DH // LANE), axis=1)
        heads.append((_dot(e, v_ref[:, hs]) / den).astype(BF16))
    att = jnp.concatenate(heads, axis=1)
    y = alpha * x + _dot(att, wo_ref[...])
    o_ref[...] = _ln(y, g_ref[...], b_ref[...])


def _xattn(x3, wq, kv, wo, lng, lnb, l, alpha):
    b, s, _ = x3.shape
    m_len = kv.shape[1]
    tm = XA_TM
    return pl.pallas_call(
        functools.partial(_xattn_kernel, alpha=alpha),
        grid=(b, s // tm),
        in_specs=[pl.BlockSpec((None, tm, D_MODEL), lambda bi, m: (bi, m, 0)),
                  _resident((None, D_MODEL, D_MODEL), lambda bi, m: (l, 0, 0)),
                  pl.BlockSpec((None, m_len, D_MODEL), lambda bi, m: (bi, 0, 0)),
                  pl.BlockSpec((None, m_len, D_MODEL), lambda bi, m: (bi, 0, 1)),
                  _resident((None, D_MODEL, D_MODEL), lambda bi, m: (l, 0, 0)),
                  _resident((None, None, 1, D_MODEL), lambda bi, m: (l, 2, 0, 0)),
                  _resident((None, None, 1, D_MODEL), lambda bi, m: (l, 2, 0, 0))],
        out_specs=pl.BlockSpec((None, tm, D_MODEL), lambda bi, m: (bi, m, 0)),
        out_shape=jax.ShapeDtypeStruct((b, s, D_MODEL), F32),
        compiler_params=_params("parallel", "parallel"),
        name="xattn",
    )(x3, wq, kv, kv, wo, lng, lnb)


def _pad_heads(w, n_heads, dh):
    depth, k, _ = w.shape
    w = w.reshape(depth, k, n_heads, dh)
    return jnp.pad(w, ((0, 0), (0, 0), (0, 0), (0, LANE - dh))).reshape(depth, k, n_heads * LANE)


def _rope_tables(positions, rot, off):
    half = rot // 2
    inv = ROPE_THETA ** (-jnp.arange(half, dtype=F32) / half)
    ang = positions.astype(F32)[..., None] * inv
    cos, sin = jnp.cos(ang), jnp.sin(ang)
    shape = positions.shape
    zeros = lambda n: jnp.zeros(shape + (n,), F32)
    cos_t = jnp.concatenate([jnp.ones(shape + (off,), F32), cos, cos,
                             jnp.ones(shape + (LANE - off - rot,), F32)], axis=-1)
    sin_lo = jnp.concatenate([zeros(off), -sin, zeros(LANE - off - half)], axis=-1)
    sin_hi = jnp.concatenate([zeros(off + half), sin, zeros(LANE - off - rot)], axis=-1)
    return cos_t, sin_lo, sin_hi


def _overlap_matrix(s_len):
    n_cmp = (s_len - CMP_L) // CMP_D + 1
    n_slc = s_len // SEL_L
    start = np.arange(n_cmp) * CMP_D
    j = np.arange(n_slc)
    ov = np.clip(np.minimum(start[:, None] + CMP_L, (j[None, :] + 1) * SEL_L)
                 - np.maximum(start[:, None], j[None, :] * SEL_L), 0, None) / CMP_L
    out = np.zeros((N_SLC, N_CMP), np.float32)
    out[:n_slc, :n_cmp] = ov.T
    place = np.zeros((N_SLC, LANE), np.float32)
    place[np.arange(N_SLC), SEL_LANE0 + np.arange(N_SLC)] = 1.0
    return jnp.asarray(out), jnp.asarray(place)


def _stacked_weights(w_in, nsa_cmp_pos, nsa_cmp_w1, nsa_cmp_w2, mla_w_uq, mla_w_ukv):
    depth = w_in.shape[0]
    part = [w_in[:, :, IN_OFF[i]:IN_OFF[i + 1]] for i in range(len(IN_SIZES))]
    (a_q, a_kc, a_vc, a_ks, a_vs, a_kw, a_vw, a_gate, b_q, b_f, b_i, b_g,
     c_q, c_kv, c_kr, merge) = part
    gate = _pad_heads(a_gate, NSA_KV, NSA_R * 3)
    kv_parts = dict(kc=a_kc, vc=a_vc, ks=a_ks, vs=a_vs, kw=a_kw, vw=a_vw)
    wa = jnp.concatenate([_pad_heads(a_q, NSA_HEADS, NSA_DH)]
                         + [kv_parts[name] for name in A_KV_ORDER] + [gate], axis=2).astype(BF16)
    wb = jnp.stack([w.reshape(depth, D_MODEL, HB, HK) for w in (b_q, b_f, b_i, b_g)], axis=3)
    wb = wb.transpose(0, 2, 1, 3, 4).reshape(depth, HB, D_MODEL, 4 * HK).astype(BF16)

    kr_pad = jnp.pad(c_kr, ((0, 0), (0, 0), (NOPE, LANE - NOPE - ROPE_D)))
    wc = jnp.concatenate([c_q, c_kv, kr_pad], axis=2).astype(BF16)
    wuq = _pad_heads(mla_w_uq, HC, NOPE + ROPE_D).astype(BF16)
    ukv = mla_w_ukv.reshape(depth, KV_RANK, HC, NOPE + VD)
    wuk = _pad_heads(ukv[..., :NOPE].reshape(depth, KV_RANK, HC * NOPE), HC, NOPE).astype(BF16)
    wuv = _pad_heads(ukv[..., NOPE:].reshape(depth, KV_RANK, HC * VD), HC, VD).astype(BF16)

    w1 = nsa_cmp_w1.reshape(depth, 2, CMP_L, NSA_DH, CMP_HID)
    w1p = jnp.pad(w1, ((0, 0), (0, 0), (0, 0), (0, LANE - NSA_DH), (0, 0)))
    w1p = w1p.reshape(depth, 2, CMP_L * LANE, CMP_HID)
    pe = jnp.pad(nsa_cmp_pos, ((0, 0), (0, 0), (0, 0), (0, LANE - NSA_DH)))
    pe8 = jnp.broadcast_to(pe.reshape(depth, 2, 1, CMP_L * LANE), (depth, 2, 8, CMP_L * LANE))
    w2p = jnp.pad(nsa_cmp_w2, ((0, 0), (0, 0), (0, 0), (0, LANE - NSA_DH)))
    return dict(wa=wa, wb=wb, wc=wc, wuq=wuq, wuk=wuk, wuv=wuv,
                w1p=w1p.astype(BF16), pe8=pe8.astype(BF16), w2p=w2p.astype(BF16),
                wm=merge.astype(BF16))


def kernel(x, mem, positions, ln_g, ln_b, ffn_w1, ffn_w3, ffn_w2, w_in, nsa_cmp_pos,
           nsa_cmp_w1, nsa_cmp_w2, hgrn_lb_logits, hgrn_norm_g, mla_q_norm_g, mla_w_uq,
           mla_kv_norm_g, mla_w_ukv, w_branch, w_out, xa_wq, xa_wk, xa_wv, xa_wo):
    b, s, d = x.shape
    depth = ln_g.shape[0]
    assert d == D_MODEL and s // SEL_L == N_SLC and (s - CMP_L) // CMP_D + 2 == N_CMP
    assert s % max(NSA_TQ, NSA_TK, MLA_TQ, PROJA_TM, MLAP_TM, XA_TM) == 0 and NSA_TK % NSA_TQ == 0
    n = b * s
    alpha = (2.0 * depth) ** 0.25

    lng = ln_g.reshape(depth, 4, 1, d)
    lnb = ln_b.reshape(depth, 4, 1, d)
    w1 = ffn_w1.astype(BF16)
    w3 = ffn_w3.astype(BF16)
    w2 = (0.5 * ffn_w2).astype(BF16)
    wo = w_out.astype(BF16)
    wbr = w_branch.astype(BF16)
    xq = xa_wq.astype(BF16)
    xkv = jnp.concatenate([xa_wk, xa_wv], axis=2).astype(BF16)
    xo = xa_wo.astype(BF16)
    p_lb = jax.nn.softmax(hgrn_lb_logits.astype(F32), axis=0)
    lower = (jnp.cumsum(p_lb, axis=0) - p_lb[0:1]).reshape(depth, HB, 1, HK)

    rope_a = _rope_tables(positions, NSA_ROT, 0)
    rope_c = _rope_tables(positions, ROPE_D, NOPE)
    ovl, place = _overlap_matrix(s)
    mem2 = mem.reshape(b * mem.shape[1], d)
    w = _stacked_weights(w_in, nsa_cmp_pos, nsa_cmp_w1, nsa_cmp_w2, mla_w_uq, mla_w_ukv)
    norm_b = hgrn_norm_g.reshape(depth, 1, HV)
    norm_q = mla_q_norm_g.reshape(depth, 1, Q_RANK)
    norm_kv = mla_kv_norm_g.reshape(depth, 1, KV_RANK)

    x2 = x.reshape(n, d)
    for l in range(depth):
        x2 = _ffn_ln(x2, w1, w3, w2, lng, lnb, l, 0, 0, alpha)
        x3 = x2.reshape(b, s, d)

        q, qr, kc, vc, ks, vs, kw, vw, gates = _proja(x3, w["wa"], l, *rope_a)
        kcmp, vcmp = _cmp(kc, vc, w["w1p"], w["pe8"], w["w2p"], l)
        ya = _nsa_attn(q, qr, kcmp, vcmp, ks, vs, kw, vw, gates, ovl, place)

        yb = _hgrn(x3, w["wb"], lower, norm_b, l)

        mq, mk, mv = _mlap(x3, w["wc"], norm_q, norm_kv, w["wuq"], w["wuk"], w["wuv"], l, *rope_c)
        yc = _mla_attn(mq, mk, mv)

        x2 = _merge(x2, ya.reshape(n, MIX_W), yb.reshape(n, MIX_W), yc.reshape(n, MIX_W),
                    w["wm"], wbr, wbr, wbr, wo, lng, lnb, l, alpha)

        kv_l = _proj(mem2, xkv, l, BF16).reshape(b, -1, 2 * d)
        x2 = _xattn(x2.reshape(b, s, d), xq, kv_l, xo, lng, lnb, l, alpha).reshape(n, d)

        x2 = _ffn_ln(x2, w1, w3, w2, lng, lnb, l, 1, 3, alpha)
    return x2.reshape(b, s, d)
```

```python
import functools

import numpy as np
import jax
import jax.numpy as jnp
from jax import lax
from jax.experimental import pallas as pl
from jax.experimental.pallas import tpu as pltpu

F32 = jnp.float32
BF16 = jnp.bfloat16

D_MODEL = 1024
MIX_W = D_MODEL // 2
NSA_DH = 64
NSA_HEADS = 8
NSA_KV = 2
NSA_R = 4
NSA_ROT = 16
CMP_L = 32
CMP_D = 16
CMP_HID = 256
SEL_L = 64
N_SEL = 8
WINDOW = 256
HB = 4
HK = 128
HV = 128
HGRN_CHUNK = 64
HGRN_SUB = 4
HGRN_PARTS = 4
HC = 8
NOPE = 64
ROPE_D = 32
VD = 64
Q_RANK = 384
KV_RANK = 256
XA_HEADS = 4
XA_DH = D_MODEL // XA_HEADS
D_FF = 2816
ROPE_THETA = 500000.0
LN_EPS = 1e-5
RMS_EPS = 1e-6
NEG = -1e30
BIG = 1e9
F_MIN = 1e-20

LOG2E = 1.4426950408889634
SQRT_2_OVER_PI = 0.7978845608028654
LANE = 128
SUBLANES = 8
SEL_LANE0 = NSA_DH
DEN_LANE = 64
VMEM_LIMIT = 56 * 1024 * 1024

IN_SIZES = (512, 128, 128, 128, 128, 128, 128, 24, 512, 512, 512, 512, 384, 256, 32, 3072)
IN_OFF = tuple(int(v) for v in np.concatenate([[0], np.cumsum(IN_SIZES)]))


def _dot(a, b):
    return jnp.dot(a, b, preferred_element_type=F32)


def _dot_nt(a, b):
    return lax.dot_general(a, b, (((1,), (1,)), ((), ())), preferred_element_type=F32)


def _ln(y, g, b):
    mu = jnp.mean(y, axis=-1, keepdims=True)
    yc = y - mu
    var = jnp.mean(yc * yc, axis=-1, keepdims=True)
    return yc * lax.rsqrt(var + LN_EPS) * g + b


def _pack_pair(a, b):
    lane = lax.broadcasted_iota(jnp.int32, a.shape, a.ndim - 1)
    return jnp.where(lane < LANE // 2, a, pltpu.roll(b, LANE // 2, axis=a.ndim - 1))


def _params(*sem):
    return pltpu.CompilerParams(dimension_semantics=sem, vmem_limit_bytes=VMEM_LIMIT)


def _resident(shape, index_map):
    return pl.BlockSpec(shape, index_map, pipeline_mode=pl.Buffered(1))


FFN_TM = 2048
FFN_TF = 256


def _ffn_kernel(x_ref, w1_ref, w3_ref, w2h_ref, g_ref, b_ref, o_ref, acc_ref, xb_ref, *, alpha):
    j = pl.program_id(1)

    @pl.when(j == 0)
    def _():
        xb_ref[...] = x_ref[...].astype(BF16)
        acc_ref[...] = alpha * x_ref[...]

    xb = xb_ref[...]
    h1 = _dot(xb, w1_ref[...])
    h3 = _dot(xb, w3_ref[...])
    h = (h1 * jax.nn.sigmoid(h1)) * h3
    acc_ref[...] += _dot(h.astype(BF16), w2h_ref[...])

    @pl.when(j == pl.num_programs(1) - 1)
    def _():
        o_ref[...] = _ln(acc_ref[...], g_ref[...], b_ref[...])


def _ffn_ln(x2, w1, w3, w2h, lng, lnb, l, which, ln_idx, alpha):
    n = x2.shape[0]
    tm = min(FFN_TM, n)
    grid = (n // tm, D_FF // FFN_TF)
    return pl.pallas_call(
        functools.partial(_ffn_kernel, alpha=alpha),
        grid=grid,
        in_specs=[
            pl.BlockSpec((tm, D_MODEL), lambda m, j: (m, 0)),
            pl.BlockSpec((None, None, D_MODEL, FFN_TF), lambda m, j: (l, which, 0, j)),
            pl.BlockSpec((None, None, D_MODEL, FFN_TF), lambda m, j: (l, which, 0, j)),
            pl.BlockSpec((None, None, FFN_TF, D_MODEL), lambda m, j: (l, which, j, 0)),
            pl.BlockSpec((None, None, 1, D_MODEL), lambda m, j: (l, ln_idx, 0, 0)),
            pl.BlockSpec((None, None, 1, D_MODEL), lambda m, j: (l, ln_idx, 0, 0)),
        ],
        out_specs=pl.BlockSpec((tm, D_MODEL), lambda m, j: (m, 0)),
        out_shape=jax.ShapeDtypeStruct((n, D_MODEL), F32),
        scratch_shapes=[pltpu.VMEM((tm, D_MODEL), F32), pltpu.VMEM((tm, D_MODEL), BF16)],
        compiler_params=_params("parallel", "arbitrary"),
        name="ffn_ln",
    )(x2, w1, w3, w2h, lng, lnb)


def _proj_kernel(x_ref, w_ref, o_ref):
    o_ref[...] = _dot(x_ref[...].astype(BF16), w_ref[...]).astype(o_ref.dtype)


def _proj(x2, w, l, out_dtype, tm=1024, tn=2048):
    n, k = x2.shape
    c = w.shape[2]
    tm = min(tm, n)
    tn = min(tn, c)
    return pl.pallas_call(
        _proj_kernel,
        grid=(n // tm, c // tn),
        in_specs=[pl.BlockSpec((tm, k), lambda m, j: (m, 0)),
                  pl.BlockSpec((None, k, tn), lambda m, j: (l, 0, j))],
        out_specs=pl.BlockSpec((tm, tn), lambda m, j: (m, j)),
        out_shape=jax.ShapeDtypeStruct((n, c), out_dtype),
        compiler_params=_params("parallel", "arbitrary"),
        name="proj",
    )(x2, w)


NSA_QW = NSA_HEADS * LANE
NSA_KW = NSA_KV * LANE
A_Q = 0
A_KV = NSA_QW
A_KV_ORDER = ("kc", "vc", "ks", "vs", "kw", "vw")
A_GATE = A_KV + len(A_KV_ORDER) * LANE
A_COLS = A_GATE + NSA_KW
PROJA_TM = 1024


def _rope_lanes(y, cos, sin_lo, sin_hi, half):
    return (y * cos + pltpu.roll(y, LANE - half, axis=1) * sin_lo
            + pltpu.roll(y, half, axis=1) * sin_hi)


def _proja_kernel(x_ref, w_ref, cos_ref, slo_ref, shi_ref,
                  q_ref, qr_ref, kc_ref, vc_ref, ks_ref, vs_ref, kw_ref, vw_ref, g_ref):
    xb = x_ref[...].astype(BF16)
    cos = cos_ref[...]
    sin_lo = slo_ref[...]
    sin_hi = shi_ref[...]
    half = NSA_ROT // 2
    tm = x_ref.shape[0]

    def mm(c0, width):
        return _dot(xb, w_ref[:, c0:c0 + width])

    qs = NSA_DH ** -0.5 * LOG2E
    y = mm(A_Q, NSA_QW)
    q_ref[...] = (y * qs).astype(BF16)
    for h in range(NSA_HEADS):
        hs = slice(h * LANE, (h + 1) * LANE)
        qr_ref[:, hs] = (_rope_lanes(y[:, hs], cos, sin_lo, sin_hi, half) * qs).astype(BF16)

    lane = lax.broadcasted_iota(jnp.int32, (tm, LANE), 1)
    first = lane < NSA_DH
    tok = pl.program_id(1) * tm + lax.broadcasted_iota(jnp.int32, (tm, LANE), 0)
    blk_flag = jnp.where(lane - SEL_LANE0 == tok // SEL_L, 1.0, 0.0)
    den_flag = jnp.where(lane == DEN_LANE, 1.0, 0.0)

    def packed(table):
        return jnp.where(first, table, pltpu.roll(table, NSA_DH, axis=1))

    def put(ref, y, flag=None):
        for g, yg in enumerate((y, pltpu.roll(y, NSA_DH, axis=1))):
            yg = jnp.where(first, yg, 0.0)
            ref[g] = (yg if flag is None else yg + flag).astype(ref.dtype)

    kv = mm(A_KV, len(A_KV_ORDER) * LANE)
    part = {name: kv[:, i * LANE:(i + 1) * LANE] for i, name in enumerate(A_KV_ORDER)}
    cos_p, lo_p, hi_p = packed(cos), packed(sin_lo), packed(sin_hi)
    put(kc_ref, part["kc"])
    put(vc_ref, part["vc"])
    put(ks_ref, _rope_lanes(part["ks"], cos_p, lo_p, hi_p, half), blk_flag)
    put(vs_ref, part["vs"], den_flag)
    put(kw_ref, _rope_lanes(part["kw"], cos_p, lo_p, hi_p, half))
    put(vw_ref, part["vw"], den_flag)
    g_ref[...] = mm(A_GATE, NSA_KW)


def _proja(x3, wa, l, cos_a, slo_a, shi_a):
    b, s, _ = x3.shape
    tm = PROJA_TM
    kv_shape = jax.ShapeDtypeStruct((b, NSA_KV, s, LANE), BF16)
    cmp_shape = jax.ShapeDtypeStruct((b, NSA_KV, s, LANE), F32)
    kv_spec = pl.BlockSpec((None, NSA_KV, tm, LANE), lambda bi, m: (bi, 0, m, 0))
    return pl.pallas_call(
        _proja_kernel,
        grid=(b, s // tm),
        in_specs=[
            pl.BlockSpec((None, tm, D_MODEL), lambda bi, m: (bi, m, 0)),
            _resident((None, D_MODEL, A_COLS), lambda bi, m: (l, 0, 0)),
            pl.BlockSpec((None, tm, LANE), lambda bi, m: (bi, m, 0)),
            pl.BlockSpec((None, tm, LANE), lambda bi, m: (bi, m, 0)),
            pl.BlockSpec((None, tm, LANE), lambda bi, m: (bi, m, 0)),
        ],
        out_specs=[
            pl.BlockSpec((None, tm, NSA_QW), lambda bi, m: (bi, m, 0)),
            pl.BlockSpec((None, tm, NSA_QW), lambda bi, m: (bi, m, 0)),
            kv_spec, kv_spec, kv_spec, kv_spec, kv_spec, kv_spec,
            pl.BlockSpec((None, tm, NSA_KW), lambda bi, m: (bi, m, 0)),
        ],
        out_shape=[
            jax.ShapeDtypeStruct((b, s, NSA_QW), BF16),
            jax.ShapeDtypeStruct((b, s, NSA_QW), BF16),
            cmp_shape, cmp_shape, kv_shape, kv_shape, kv_shape, kv_shape,
            jax.ShapeDtypeStruct((b, s, NSA_KW), F32),
        ],
        compiler_params=_params("parallel", "parallel"),
        name="nsa_proj",
    )(x3, wa, cos_a, slo_a, shi_a)


N_CMP = 128
CMP_HALF = CMP_D * LANE


def _gelu_tanh(x):
    return 0.5 * x * (1.0 + jnp.tanh(SQRT_2_OVER_PI * (x + 0.044715 * x * x * x)))


def _cmp_kernel(zk_ref, zv_ref, w1_ref, pe_ref, w2_ref, ok_ref, ov_ref):
    for which, (z_ref, o_ref) in enumerate(((zk_ref, ok_ref), (zv_ref, ov_ref))):
        first = jnp.zeros((N_CMP, CMP_HID), F32)
        second = jnp.zeros((N_CMP, CMP_HID), F32)
        for j in range(CMP_D):
            zj = z_ref[pl.ds(j, N_CMP, stride=CMP_D), :].astype(BF16)
            first = first + _dot(zj, w1_ref[which, j * LANE:(j + 1) * LANE, :])
            second = second + _dot(zj, w1_ref[which, CMP_HALF + j * LANE:CMP_HALF + (j + 1) * LANE, :])
        bias = _dot(pe_ref[which], w1_ref[which])[0:1]
        pre = first + pltpu.roll(second, N_CMP - 1, axis=0) + bias
        h = _gelu_tanh(pre)
        o_ref[...] = _dot(h.astype(BF16), w2_ref[which]).astype(BF16)


def _cmp(zk, zv, w1p, pe8, w2p, l):
    b, _, s, _ = zk.shape
    z_spec = pl.BlockSpec((None, None, s, LANE), lambda bi, g: (bi, g, 0, 0))
    o_spec = pl.BlockSpec((None, None, N_CMP, LANE), lambda bi, g: (bi, g, 0, 0))
    o_shape = jax.ShapeDtypeStruct((b, NSA_KV, N_CMP, LANE), BF16)
    return pl.pallas_call(
        _cmp_kernel,
        grid=(b, NSA_KV),
        in_specs=[
            z_spec, z_spec,
            _resident((None, 2, 2 * CMP_HALF, CMP_HID), lambda bi, g: (l, 0, 0, 0)),
            _resident((None, 2, 8, 2 * CMP_HALF), lambda bi, g: (l, 0, 0, 0)),
            _resident((None, 2, CMP_HID, LANE), lambda bi, g: (l, 0, 0, 0)),
        ],
        out_specs=[o_spec, o_spec],
        out_shape=[o_shape, o_shape],
        compiler_params=_params("parallel", "parallel"),
        name="nsa_cmp",
    )(zk, zv, w1p, pe8, w2p)


NSA_TQ = 256
NSA_TK = 512
N_SLC = 32
WIN_SLAB = WINDOW + NSA_TQ


def _nsa_kernel(q_ref, qr_ref, kc_ref, vc_ref, ks_ref, vs_ref, kw_ref, vw_ref, gt_ref, ovl_ref,
                place_ref, o_ref):
    tq, tk, r_heads = NSA_TQ, NSA_TK, NSA_R
    i = pl.program_id(2)
    t0 = i * tq
    tpos = t0 + lax.broadcasted_iota(jnp.int32, (tq, 1), 0)
    lane = lax.broadcasted_iota(jnp.int32, (1, LANE), 1)

    def stack(ref):
        return jnp.concatenate([ref[:, r * LANE:(r + 1) * LANE] for r in range(r_heads)], axis=0)

    gt = jax.nn.sigmoid(gt_ref[...])

    def gate(c):
        return jnp.stack([gt[:, 3 * r + c:3 * r + c + 1] for r in range(r_heads)], axis=0)

    q4w = stack(qr_ref)
    w0 = pl.multiple_of(jnp.maximum(t0 - WINDOW, 0), LANE)
    kw = kw_ref[pl.ds(w0, WIN_SLAB), :]
    vw = vw_ref[pl.ds(w0, WIN_SLAB), :]
    kpos = w0 + lax.broadcasted_iota(jnp.int32, (1, WIN_SLAB), 1)
    w_bias = jnp.where((kpos <= tpos) & (kpos > tpos - WINDOW), 0.0, NEG)
    s = _dot_nt(q4w, kw).reshape(r_heads, tq, WIN_SLAB) + w_bias[None]
    p = jnp.exp2(s - jnp.max(s, axis=-1, keepdims=True))
    o_win = _dot(p.reshape(r_heads * tq, WIN_SLAB).astype(BF16), vw).reshape(r_heads, tq, LANE)
    o_part = gate(2) * (o_win / o_win[:, :, DEN_LANE:DEN_LANE + 1])

    cmask = ((lane * CMP_D + (CMP_L - 1) <= tpos) & (lane < N_CMP - 1))[None]
    s = _dot_nt(stack(q_ref), kc_ref[...]).reshape(r_heads, tq, N_CMP)
    s = jnp.where(cmask, s, NEG)
    e = jnp.where(cmask, jnp.exp2(s - jnp.max(s, axis=-1, keepdims=True)), 0.0)
    den = jnp.sum(e, axis=-1, keepdims=True)
    p = e / jnp.where(den > 0.0, den, 1.0)
    psum = jnp.sum(p, axis=0)
    o_cmp = _dot(p.reshape(r_heads * tq, N_CMP).astype(BF16), vc_ref[...])
    o_part = o_part + gate(0) * o_cmp.reshape(r_heads, tq, LANE)

    imp = lax.dot_general(ovl_ref[...], psum, (((1,), (1,)), ((), ())),
                          precision=lax.Precision.HIGHEST, preferred_element_type=F32)
    blk = lax.broadcasted_iota(jnp.int32, (N_SLC, 1), 0)
    tpos_l = t0 + lax.broadcasted_iota(jnp.int32, (1, tq), 1)
    cur = tpos_l // SEL_L
    valid = blk * SEL_L <= tpos_l
    forced = (blk == 0) | (blk == cur) | (blk == cur - 1)
    score = jnp.where(valid & forced, BIG, jnp.where(valid, imp, -BIG))
    beats = []
    for j in range(N_SLC):
        sj = score[j:j + 1, :]
        beats.append(jnp.where((sj > score) | ((sj == score) & (j < blk)), 1.0, 0.0))
    while len(beats) > 1:
        beats = [a + b for a, b in zip(beats[0::2], beats[1::2])]
    sel_t = jnp.where((beats[0] < N_SEL) & valid, 1.0, 0.0)
    sel_q = lax.dot_general(sel_t, place_ref[...], (((0,), (0,)), ((), ())),
                            preferred_element_type=F32)
    in_flags = jnp.where((lane >= SEL_LANE0) & (lane < SEL_LANE0 + N_SLC), 1.0, 0.0)
    q_bias = (sel_q - in_flags) * (-NEG)
    q4 = (q4w.astype(F32).reshape(r_heads, tq, LANE) + q_bias[None]).astype(BF16)
    q4 = q4.reshape(r_heads * tq, LANE)

    def sel_step(c, carry, causal):
        m, acc = carry
        k0 = pl.multiple_of(c * tk, tk)
        k = ks_ref[pl.ds(k0, tk), :]
        v = vs_ref[pl.ds(k0, tk), :]
        s = _dot_nt(q4, k).reshape(r_heads, tq, tk)
        if causal:
            kpos = k0 + lax.broadcasted_iota(jnp.int32, (1, tk), 1)
            s = jnp.where((kpos <= tpos)[None], s, NEG)
        m_new = jnp.maximum(m, jnp.max(s, axis=-1, keepdims=True))
        p = jnp.exp2(s - m_new)
        pv = _dot(p.reshape(r_heads * tq, tk).astype(BF16), v).reshape(r_heads, tq, LANE)
        return m_new, jnp.exp2(m - m_new) * acc + pv

    last = (t0 + tq - 1) // tk
    carry = (jnp.full((r_heads, tq, 1), NEG, F32), jnp.zeros((r_heads, tq, LANE), F32))
    carry = lax.fori_loop(0, last, functools.partial(sel_step, causal=False), carry)
    _, acc_s = sel_step(last, carry, True)
    o = o_part + gate(1) * (acc_s / acc_s[:, :, DEN_LANE:DEN_LANE + 1])
    for r in range(0, r_heads, 2):
        o_ref[:, (r // 2) * LANE:(r // 2 + 1) * LANE] = _pack_pair(o[r], o[r + 1]).astype(BF16)


def _nsa_attn(q, qr, kcmp, vcmp, ks, vs, kw, vw, gates, ovl, place):
    b, s, _ = q.shape
    tq = NSA_TQ
    gw = NSA_R * LANE
    q_spec = pl.BlockSpec((None, tq, gw), lambda bi, g, i: (bi, i, g))
    c_spec = pl.BlockSpec((None, None, N_CMP, LANE), lambda bi, g, i: (bi, g, 0, 0))
    kv_spec = pl.BlockSpec((None, None, s, LANE), lambda bi, g, i: (bi, g, 0, 0))
    return pl.pallas_call(
        _nsa_kernel,
        grid=(b, NSA_KV, s // tq),
        in_specs=[q_spec, q_spec, c_spec, c_spec, kv_spec, kv_spec, kv_spec, kv_spec,
                  pl.BlockSpec((None, tq, LANE), lambda bi, g, i: (bi, i, g)),
                  _resident((N_SLC, N_CMP), lambda bi, g, i: (0, 0)),
                  _resident((N_SLC, LANE), lambda bi, g, i: (0, 0))],
        out_specs=pl.BlockSpec((None, tq, gw // 2), lambda bi, g, i: (bi, i, g)),
        out_shape=jax.ShapeDtypeStruct((b, s, MIX_W), BF16),
        compiler_params=_params("parallel", "parallel", "arbitrary"),
        name="nsa_attn",
    )(q, qr, kcmp, vcmp, ks, vs, kw, vw, gates, ovl, place)


def _hgrn_kernel(x_ref, wq_ref, wf_ref, wi_ref, wg_ref, lb_ref, ng_ref, o_ref, st_ref):
    w_head = jnp.concatenate([wq_ref[...], wf_ref[...], wi_ref[...], wg_ref[...]], axis=1)
    s_len = x_ref.shape[0]
    c_len, sub_len = HGRN_CHUNK, HGRN_SUB
    p_len = s_len // HGRN_PARTS
    nc = p_len // c_len
    lb = lb_ref[...]
    t = lax.broadcasted_iota(jnp.int32, (p_len, 1), 0)
    t_sub = t % sub_len
    row = lax.broadcasted_iota(jnp.int32, (nc, c_len, c_len), 1)
    col = lax.broadcasted_iota(jnp.int32, (nc, c_len, c_len), 2)
    tri = jnp.where(col <= row, 1.0, 0.0).astype(BF16)

    local = []
    for part in range(HGRN_PARTS):
        proj = _dot(x_ref[pl.ds(part * p_len, p_len), :].astype(BF16), w_head)
        q = proj[:, 0:HK]
        z = proj[:, HK:2 * HK]
        v = proj[:, 2 * HK:2 * HK + HV]
        go = proj[:, 2 * HK + HV:2 * HK + 2 * HV]

        sig = jax.nn.sigmoid(z)
        f = lb + (1.0 - lb) * sig
        lf = jnp.log(jnp.maximum(f, F_MIN))
        k = (1.0 - lb) * (1.0 - sig)

        lf3 = lf.reshape(nc, c_len, HK)
        hi = lf3.astype(BF16)
        rest = lf3 - hi.astype(F32)
        mid = rest.astype(BF16)
        low = (rest - mid.astype(F32)).astype(BF16)
        sums = jnp.einsum('cts,csd->ctd', tri, jnp.concatenate([hi, mid, low], axis=-1),
                          preferred_element_type=F32)
        b3 = sums[..., 0:HK] + sums[..., HK:2 * HK] + sums[..., 2 * HK:3 * HK]
        b = b3.reshape(p_len, HK)
        k3 = k.reshape(nc, c_len, HK)
        vb3 = v.astype(BF16).reshape(nc, c_len, HV)

        a_intra = None
        h = c_len // 2
        while h >= sub_len:
            blk = b.reshape(p_len // (2 * h), 2 * h, HK)
            e = jnp.exp(-jnp.abs(blk[:, h - 1:h, :] - blk)).reshape(p_len, HK)
            upper = (t // h) % 2 == 1
            lq = jnp.where(upper, q * e, 0.0).astype(BF16).reshape(nc, c_len, HK)
            rk = jnp.where(upper, 0.0, k * e).astype(BF16).reshape(nc, c_len, HK)
            a = jnp.einsum('ctk,csk->cts', lq, rk, preferred_element_type=F32)
            if 2 * h < c_len:
                a = jnp.where(row // (2 * h) == col // (2 * h), a, 0.0)
            a_intra = a if a_intra is None else a_intra + a
            h //= 2
        o = jnp.einsum('cts,csd->ctd', a_intra.astype(BF16), vb3,
                       preferred_element_type=F32).reshape(p_len, HV)

        o = o + jnp.sum(q * k, axis=-1, keepdims=True) * v

        def back(x, d):
            x3 = x.reshape(p_len // SUBLANES, SUBLANES, x.shape[-1])
            return pltpu.roll(x3, d, axis=1).reshape(x.shape)

        for d in range(1, sub_len):
            e = jnp.exp(jnp.where(t_sub >= d, b - back(b, d), NEG))
            w = jnp.sum(q * back(k, d) * e, axis=-1, keepdims=True)
            o = o + w * back(v, d)

        b_last = b3[:, c_len - 1:c_len, :]
        kd3 = (k3 * jnp.exp(b_last - b3)).astype(BF16)
        upd = jnp.einsum('csv,csk->cvk', vb3, kd3, preferred_element_type=F32)
        qe3 = (q * jnp.exp(b)).astype(BF16).reshape(nc, c_len, HK)
        local.append((o, go, qe3, upd, jnp.exp(b_last)))

    state = jnp.zeros((HV, HK), F32)
    for part, (_, _, _, upd, dec) in enumerate(local):
        for c in range(nc):
            st_ref[part * nc + c] = state.astype(BF16)
            state = state * dec[c] + upd[c]

    for part, (o, go, qe3, _, _) in enumerate(local):
        o = o + jnp.einsum('ctk,cvk->ctv', qe3, st_ref[pl.ds(part * nc, nc)],
                           preferred_element_type=F32).reshape(p_len, HV)
        o = o * lax.rsqrt(jnp.mean(o * o, axis=-1, keepdims=True) + RMS_EPS) * ng_ref[...]
        o_ref[pl.ds(part * p_len, p_len), :] = (o * (go * jax.nn.sigmoid(go))).astype(BF16)


def _hgrn(x3, wb, lb, ng, l):
    b, s, d = x3.shape

    def cols(p):
        return pl.BlockSpec((None, d, HK), lambda bi, h: (l, 0, p * HB + h))

    return pl.pallas_call(
        _hgrn_kernel,
        grid=(b, HB),
        in_specs=[pl.BlockSpec((None, s, d), lambda bi, h: (bi, 0, 0)),
                  cols(0), cols(1), cols(2), cols(3),
                  pl.BlockSpec((None, None, 1, HK), lambda bi, h: (l, h, 0, 0)),
                  pl.BlockSpec((None, 1, HV), lambda bi, h: (l, 0, 0))],
        out_specs=pl.BlockSpec((None, s, HV), lambda bi, h: (bi, 0, h)),
        out_shape=jax.ShapeDtypeStruct((b, s, HB * HV), BF16),
        scratch_shapes=[pltpu.VMEM((s // HGRN_CHUNK, HV, HK), BF16)],
        compiler_params=_params("parallel", "arbitrary"),
        name="hgrn",
    )(x3, wb, wb, wb, wb, lb, ng)


MLA_W = HC * LANE
C_CQ, C_CKV, C_KR = 0, Q_RANK, Q_RANK + KV_RANK
C_COLS = C_KR + LANE
MLAP_TM = 1024


def _rms(x, g):
    return x * lax.rsqrt(jnp.mean(x * x, axis=-1, keepdims=True) + RMS_EPS) * g


def _mlap_kernel(x_ref, wc_ref, qg_ref, kvg_ref, wuq_ref, wuk_ref, wuv_ref,
                 cos_ref, slo_ref, shi_ref, q_ref, k_ref, v_ref):
    half = ROPE_D // 2
    qs = (NOPE + ROPE_D) ** -0.5 * LOG2E
    den_flag = jnp.where(lax.broadcasted_iota(jnp.int32, (1, LANE), 1) == DEN_LANE, 1.0, 0.0)
    cos, lo, hi = cos_ref[...], slo_ref[...], shi_ref[...]
    c = _dot(x_ref[...].astype(BF16), wc_ref[...])
    nq = _rms(c[:, C_CQ:C_CKV], qg_ref[...]).astype(BF16)
    nkv = _rms(c[:, C_CKV:C_KR], kvg_ref[...]).astype(BF16)
    k_pe = _rope_lanes(c[:, C_KR:C_COLS], cos, lo, hi, half)
    yq = _dot(nq, wuq_ref[...])
    yk = _dot(nkv, wuk_ref[...])
    yv = _dot(nkv, wuv_ref[...])
    cos_q, lo_q, hi_q = cos * qs, lo * qs, hi * qs
    for h in range(HC):
        hs = slice(h * LANE, (h + 1) * LANE)
        q_ref[:, hs] = _rope_lanes(yq[:, hs], cos_q, lo_q, hi_q, half).astype(BF16)
        k_ref[:, hs] = (yk[:, hs] + k_pe).astype(BF16)
        v_ref[:, hs] = (yv[:, hs] + den_flag).astype(BF16)


def _mlap(x3, wc, qg, kvg, wuq, wuk, wuv, l, cos_c, slo_c, shi_c):
    b, s, _ = x3.shape
    tm = MLAP_TM
    o_spec = pl.BlockSpec((None, tm, MLA_W), lambda bi, m: (bi, m, 0))
    o_shape = jax.ShapeDtypeStruct((b, s, MLA_W), BF16)
    t_spec = pl.BlockSpec((None, tm, LANE), lambda bi, m: (bi, m, 0))
    return pl.pallas_call(
        _mlap_kernel,
        grid=(b, s // tm),
        in_specs=[
            pl.BlockSpec((None, tm, D_MODEL), lambda bi, m: (bi, m, 0)),
            _resident((None, D_MODEL, C_COLS), lambda bi, m: (l, 0, 0)),
            _resident((None, 1, Q_RANK), lambda bi, m: (l, 0, 0)),
            _resident((None, 1, KV_RANK), lambda bi, m: (l, 0, 0)),
            _resident((None, Q_RANK, MLA_W), lambda bi, m: (l, 0, 0)),
            _resident((None, KV_RANK, MLA_W), lambda bi, m: (l, 0, 0)),
            _resident((None, KV_RANK, MLA_W), lambda bi, m: (l, 0, 0)),
            t_spec, t_spec, t_spec,
        ],
        out_specs=[o_spec, o_spec, o_spec],
        out_shape=[o_shape, o_shape, o_shape],
        compiler_params=_params("parallel", "parallel"),
        name="mla_proj",
    )(x3, wc, qg, kvg, wuq, wuk, wuv, cos_c, slo_c, shi_c)


MLA_TQ = 512
MLA_TK = MLA_TQ


MLA_HPS = 4


def _mla_attn_kernel(q_ref, k_ref, v_ref, o_ref):
    tq, tk = MLA_TQ, MLA_TK
    i = pl.program_id(2)
    t0 = i * tq
    heads = [slice(h * LANE, (h + 1) * LANE) for h in range(MLA_HPS)]
    qs = [q_ref[:, hs] for hs in heads]

    def attend(k0, streams, causal):
        scores = [_dot_nt(q, k_ref[pl.ds(k0, nk), hs]) for q, hs, _, _, nk, _ in streams]
        m_new, probs = [], []
        for s, (q, _, m, _, nk, r0) in zip(scores, streams):
            if causal:
                kpos = k0 + lax.broadcasted_iota(jnp.int32, (1, nk), 1)
                qpos = t0 + r0 + lax.broadcasted_iota(jnp.int32, (q.shape[0], 1), 0)
                s = jnp.where(kpos <= qpos, s, NEG)
            m_new.append(jnp.maximum(m, jnp.max(s, axis=-1, keepdims=True)))
            probs.append(jnp.exp2(s - m_new[-1]).astype(BF16))
        pvs = [_dot(p, v_ref[pl.ds(k0, nk), hs]) for p, (_, hs, _, _, nk, _) in zip(probs, streams)]
        return [(mn, jnp.exp2(m - mn) * acc + pv)
                for mn, pv, (_, _, m, acc, _, _) in zip(m_new, pvs, streams)]

    def full_chunk(c, carry):
        streams = [(qs[h], hs, *carry[h], tk, 0) for h, hs in enumerate(heads)]
        return tuple(attend(pl.multiple_of(c * tk, tk), streams, False))

    carry = tuple((jnp.full((tq, 1), NEG, F32), jnp.zeros((tq, LANE), F32)) for _ in heads)
    carry = lax.fori_loop(0, i, full_chunk, carry)
    hq = tq // 2
    streams = [(qs[h][r0:r0 + hq], hs, carry[h][0][r0:r0 + hq], carry[h][1][r0:r0 + hq], r0 + hq, r0)
               for h, hs in enumerate(heads) for r0 in (0, hq)]
    done = attend(pl.multiple_of(t0, tq), streams, True)
    outs = []
    for h in range(len(heads)):
        acc = jnp.concatenate([done[2 * h][1], done[2 * h + 1][1]], axis=0)
        outs.append(acc / acc[:, DEN_LANE:DEN_LANE + 1])
    for h in range(0, MLA_HPS, 2):
        o_ref[:, (h // 2) * LANE:(h // 2 + 1) * LANE] = _pack_pair(outs[h], outs[h + 1]).astype(BF16)


def _mla_attn(q, k, v):
    b, s, _ = q.shape
    tq = MLA_TQ
    gw = MLA_HPS * LANE
    return pl.pallas_call(
        _mla_attn_kernel,
        grid=(b, HC // MLA_HPS, s // tq),
        in_specs=[pl.BlockSpec((None, tq, gw), lambda bi, h, i: (bi, i, h)),
                  pl.BlockSpec((None, s, gw), lambda bi, h, i: (bi, 0, h)),
                  pl.BlockSpec((None, s, gw), lambda bi, h, i: (bi, 0, h))],
        out_specs=pl.BlockSpec((None, tq, gw // 2), lambda bi, h, i: (bi, i, h)),
        out_shape=jax.ShapeDtypeStruct((b, s, MIX_W), BF16),
        compiler_params=_params("parallel", "parallel", "arbitrary"),
        name="mla_attn",
    )(q, k, v)


MERGE_TM = 1024


def _merge_kernel(x_ref, ya_ref, yb_ref, yc_ref, wm_ref, wa_ref, wb_ref, wc_ref, wo_ref,
                  g_ref, b_ref, o_ref, *, alpha):
    x = x_ref[...]
    xb = x.astype(BF16)
    mixed = jnp.zeros(x.shape, F32)
    for idx, (y_ref, w_ref) in enumerate(((ya_ref, wa_ref), (yb_ref, wb_ref), (yc_ref, wc_ref))):
        gate = jax.nn.sigmoid(_dot(xb, wm_ref[:, idx * D_MODEL:(idx + 1) * D_MODEL]))
        mixed = mixed + gate * _dot(y_ref[...], w_ref[...])
    y = alpha * x + _dot(mixed.astype(BF16), wo_ref[...])
    o_ref[...] = _ln(y, g_ref[...], b_ref[...])


def _merge(x2, ya, yb, yc, wm, wa, wb, wc, wo, lng, lnb, l, alpha):
    n = x2.shape[0]
    tm = min(MERGE_TM, n)

    def rows(width):
        return pl.BlockSpec((tm, width), lambda m: (m, 0))

    return pl.pallas_call(
        functools.partial(_merge_kernel, alpha=alpha),
        grid=(n // tm,),
        in_specs=[rows(D_MODEL), rows(MIX_W), rows(MIX_W), rows(MIX_W),
                  _resident((None, D_MODEL, 3 * D_MODEL), lambda m: (l, 0, 0)),
                  _resident((None, None, MIX_W, D_MODEL), lambda m: (l, 0, 0, 0)),
                  _resident((None, None, MIX_W, D_MODEL), lambda m: (l, 1, 0, 0)),
                  _resident((None, None, MIX_W, D_MODEL), lambda m: (l, 2, 0, 0)),
                  _resident((None, D_MODEL, D_MODEL), lambda m: (l, 0, 0)),
                  _resident((None, None, 1, D_MODEL), lambda m: (l, 1, 0, 0)),
                  _resident((None, None, 1, D_MODEL), lambda m: (l, 1, 0, 0))],
        out_specs=rows(D_MODEL),
        out_shape=jax.ShapeDtypeStruct((n, D_MODEL), F32),
        compiler_params=_params("parallel"),
        name="merge",
    )(x2, ya, yb, yc, wm, wa, wb, wc, wo, lng, lnb)


XA_TM = 1024


def _xattn_kernel(x_ref, wq_ref, k_ref, v_ref, wo_ref, g_ref, b_ref, o_ref, *, alpha):
    x = x_ref[...]
    xb = x.astype(BF16)
    q = (_dot(xb, wq_ref[...]) * (XA_DH ** -0.5 * LOG2E)).astype(BF16)
    ones = jnp.ones((k_ref.shape[0], LANE), BF16)
    heads = []
    for h in range(XA_HEADS):
        hs = slice(h * XA_DH, (h + 1) * XA_DH)
        s = _dot_nt(q[:, hs], k_ref[:, hs])
        e = jnp.exp2(s - jnp.max(s, axis=-1, keepdims=True)).astype(BF16)
        den = _dot(e, ones)
        den = jnp.concatenate([den] * (XA_DH // LANE), axis=1)
        heads.append((_dot(e, v_ref[:, hs]) / den).astype(BF16))
    att = jnp.concatenate(heads, axis=1)
    y = alpha * x + _dot(att, wo_ref[...])
    o_ref[...] = _ln(y, g_ref[...], b_ref[...])


def _xattn(x3, wq, kv, wo, lng, lnb, l, alpha):
    b, s, _ = x3.shape
    m_len = kv.shape[1]
    tm = XA_TM
    return pl.pallas_call(
        functools.partial(_xattn_kernel, alpha=alpha),
        grid=(b, s // tm),
        in_specs=[pl.BlockSpec((None, tm, D_MODEL), lambda bi, m: (bi, m, 0)),
                  _resident((None, D_MODEL, D_MODEL), lambda bi, m: (l, 0, 0)),
                  pl.BlockSpec((None, m_len, D_MODEL), lambda bi, m: (bi, 0, 0)),
                  pl.BlockSpec((None, m_len, D_MODEL), lambda bi, m: (bi, 0, 1)),
                  _resident((None, D_MODEL, D_MODEL), lambda bi, m: (l, 0, 0)),
                  _resident((None, None, 1, D_MODEL), lambda bi, m: (l, 2, 0, 0)),
                  _resident((None, None, 1, D_MODEL), lambda bi, m: (l, 2, 0, 0))],
        out_specs=pl.BlockSpec((None, tm, D_MODEL), lambda bi, m: (bi, m, 0)),
        out_shape=jax.ShapeDtypeStruct((b, s, D_MODEL), F32),
        compiler_params=_params("parallel", "parallel"),
        name="xattn",
    )(x3, wq, kv, kv, wo, lng, lnb)


def _pad_heads(w, n_heads, dh):
    depth, k, _ = w.shape
    w = w.reshape(depth, k, n_heads, dh)
    return jnp.pad(w, ((0, 0), (0, 0), (0, 0), (0, LANE - dh))).reshape(depth, k, n_heads * LANE)


def _rope_tables(positions, rot, off):
    half = rot // 2
    inv = ROPE_THETA ** (-jnp.arange(half, dtype=F32) / half)
    ang = positions.astype(F32)[..., None] * inv
    cos, sin = jnp.cos(ang), jnp.sin(ang)
    shape = positions.shape
    zeros = lambda n: jnp.zeros(shape + (n,), F32)
    cos_t = jnp.concatenate([jnp.ones(shape + (off,), F32), cos, cos,
                             jnp.ones(shape + (LANE - off - rot,), F32)], axis=-1)
    sin_lo = jnp.concatenate([zeros(off), -sin, zeros(LANE - off - half)], axis=-1)
    sin_hi = jnp.concatenate([zeros(off + half), sin, zeros(LANE - off - rot)], axis=-1)
    return cos_t, sin_lo, sin_hi


def _overlap_matrix(s_len):
    n_cmp = (s_len - CMP_L) // CMP_D + 1
    n_slc = s_len // SEL_L
    start = np.arange(n_cmp) * CMP_D
    j = np.arange(n_slc)
    ov = np.clip(np.minimum(start[:, None] + CMP_L, (j[None, :] + 1) * SEL_L)
                 - np.maximum(start[:, None], j[None, :] * SEL_L), 0, None) / CMP_L
    out = np.zeros((N_SLC, N_CMP), np.float32)
    out[:n_slc, :n_cmp] = ov.T
    place = np.zeros((N_SLC, LANE), np.float32)
    place[np.arange(N_SLC), SEL_LANE0 + np.arange(N_SLC)] = 1.0
    return jnp.asarray(out), jnp.asarray(place)


def _stacked_weights(w_in, nsa_cmp_pos, nsa_cmp_w1, nsa_cmp_w2, mla_w_uq, mla_w_ukv):
    depth = w_in.shape[0]
    part = [w_in[:, :, IN_OFF[i]:IN_OFF[i + 1]] for i in range(len(IN_SIZES))]
    (a_q, a_kc, a_vc, a_ks, a_vs, a_kw, a_vw, a_gate, b_q, b_f, b_i, b_g,
     c_q, c_kv, c_kr, merge) = part
    gate = _pad_heads(a_gate, NSA_KV, NSA_R * 3)
    kv_parts = dict(kc=a_kc, vc=a_vc, ks=a_ks, vs=a_vs, kw=a_kw, vw=a_vw)
    wa = jnp.concatenate([_pad_heads(a_q, NSA_HEADS, NSA_DH)]
                         + [kv_parts[name] for name in A_KV_ORDER] + [gate], axis=2).astype(BF16)
    wb = jnp.concatenate([b_q, b_f, b_i, b_g], axis=2).astype(BF16)

    kr_pad = jnp.pad(c_kr, ((0, 0), (0, 0), (NOPE, LANE - NOPE - ROPE_D)))
    wc = jnp.concatenate([c_q, c_kv, kr_pad], axis=2).astype(BF16)
    wuq = _pad_heads(mla_w_uq, HC, NOPE + ROPE_D).astype(BF16)
    ukv = mla_w_ukv.reshape(depth, KV_RANK, HC, NOPE + VD)
    wuk = _pad_heads(ukv[..., :NOPE].reshape(depth, KV_RANK, HC * NOPE), HC, NOPE).astype(BF16)
    wuv = _pad_heads(ukv[..., NOPE:].reshape(depth, KV_RANK, HC * VD), HC, VD).astype(BF16)

    w1 = nsa_cmp_w1.reshape(depth, 2, CMP_L, NSA_DH, CMP_HID)
    w1p = jnp.pad(w1, ((0, 0), (0, 0), (0, 0), (0, LANE - NSA_DH), (0, 0)))
    w1p = w1p.reshape(depth, 2, CMP_L * LANE, CMP_HID)
    pe = jnp.pad(nsa_cmp_pos, ((0, 0), (0, 0), (0, 0), (0, LANE - NSA_DH)))
    pe8 = jnp.broadcast_to(pe.reshape(depth, 2, 1, CMP_L * LANE), (depth, 2, 8, CMP_L * LANE))
    w2p = jnp.pad(nsa_cmp_w2, ((0, 0), (0, 0), (0, 0), (0, LANE - NSA_DH)))
    return dict(wa=wa, wb=wb, wc=wc, wuq=wuq, wuk=wuk, wuv=wuv,
                w1p=w1p.astype(BF16), pe8=pe8.astype(BF16), w2p=w2p.astype(BF16),
                wm=merge.astype(BF16))


def kernel(x, mem, positions, ln_g, ln_b, ffn_w1, ffn_w3, ffn_w2, w_in, nsa_cmp_pos,
           nsa_cmp_w1, nsa_cmp_w2, hgrn_lb_logits, hgrn_norm_g, mla_q_norm_g, mla_w_uq,
           mla_kv_norm_g, mla_w_ukv, w_branch, w_out, xa_wq, xa_wk, xa_wv, xa_wo):
    b, s, d = x.shape
    depth = ln_g.shape[0]
    assert d == D_MODEL and s // SEL_L == N_SLC and (s - CMP_L) // CMP_D + 2 == N_CMP
    assert s % max(NSA_TQ, NSA_TK, MLA_TQ, PROJA_TM, MLAP_TM, XA_TM) == 0 and NSA_TK % NSA_TQ == 0
    n = b * s
    alpha = (2.0 * depth) ** 0.25

    lng = ln_g.reshape(depth, 4, 1, d)
    lnb = ln_b.reshape(depth, 4, 1, d)
    w1 = ffn_w1.astype(BF16)
    w3 = ffn_w3.astype(BF16)
    w2 = (0.5 * ffn_w2).astype(BF16)
    wo = w_out.astype(BF16)
    wbr = w_branch.astype(BF16)
    xq = xa_wq.astype(BF16)
    xkv = jnp.concatenate([xa_wk, xa_wv], axis=2).astype(BF16)
    xo = xa_wo.astype(BF16)
    p_lb = jax.nn.softmax(hgrn_lb_logits.astype(F32), axis=0)
    lower = (jnp.cumsum(p_lb, axis=0) - p_lb[0:1]).reshape(depth, HB, 1, HK)

    rope_a = _rope_tables(positions, NSA_ROT, 0)
    rope_c = _rope_tables(positions, ROPE_D, NOPE)
    ovl, place = _overlap_matrix(s)
    mem2 = mem.reshape(b * mem.shape[1], d)
    w = _stacked_weights(w_in, nsa_cmp_pos, nsa_cmp_w1, nsa_cmp_w2, mla_w_uq, mla_w_ukv)
    norm_b = hgrn_norm_g.reshape(depth, 1, HV)
    norm_q = mla_q_norm_g.reshape(depth, 1, Q_RANK)
    norm_kv = mla_kv_norm_g.reshape(depth, 1, KV_RANK)

    x2 = x.reshape(n, d)
    for l in range(depth):
        x2 = _ffn_ln(x2, w1, w3, w2, lng, lnb, l, 0, 0, alpha)
        x3 = x2.reshape(b, s, d)

        q, qr, kc, vc, ks, vs, kw, vw, gates = _proja(x3, w["wa"], l, *rope_a)
        kcmp, vcmp = _cmp(kc, vc, w["w1p"], w["pe8"], w["w2p"], l)
        ya = _nsa_attn(q, qr, kcmp, vcmp, ks, vs, kw, vw, gates, ovl, place)

        yb = _hgrn(x3, w["wb"], lower, norm_b, l)

        mq, mk, mv = _mlap(x3, w["wc"], norm_q, norm_kv, w["wuq"], w["wuk"], w["wuv"], l, *rope_c)
        yc = _mla_attn(mq, mk, mv)

        x2 = _merge(x2, ya.reshape(n, MIX_W), yb.reshape(n, MIX_W), yc.reshape(n, MIX_W),
                    w["wm"], wbr, wbr, wbr, wo, lng, lnb, l, alpha)

        kv_l = _proj(mem2, xkv, l, BF16).reshape(b, -1, 2 * d)
        x2 = _xattn(x2.reshape(b, s, d), xq, kv_l, xo, lng, lnb, l, alpha).reshape(n, d)

        x2 = _ffn_ln(x2, w1, w3, w2, lng, lnb, l, 1, 3, alpha)
    return x2.reshape(b, s, d)
```

```python
import functools

import numpy as np
import jax
import jax.numpy as jnp
from jax import lax
from jax.experimental import pallas as pl
from jax.experimental.pallas import tpu as pltpu

F32 = jnp.float32
BF16 = jnp.bfloat16

D_MODEL = 1024
MIX_W = D_MODEL // 2
NSA_DH = 64
NSA_HEADS = 8
NSA_KV = 2
NSA_R = 4
NSA_ROT = 16
CMP_L = 32
CMP_D = 16
CMP_HID = 256
SEL_L = 64
N_SEL = 8
WINDOW = 256
HB = 4
HK = 128
HV = 128
HGRN_CHUNK = 64
HGRN_SUB = 4
HGRN_PARTS = 4
HC = 8
NOPE = 64
ROPE_D = 32
VD = 64
Q_RANK = 384
KV_RANK = 256
XA_HEADS = 4
XA_DH = D_MODEL // XA_HEADS
D_FF = 2816
ROPE_THETA = 500000.0
LN_EPS = 1e-5
RMS_EPS = 1e-6
NEG = -1e30
BIG = 1e9
F_MIN = 1e-20

LOG2E = 1.4426950408889634
SQRT_2_OVER_PI = 0.7978845608028654
LANE = 128
SUBLANES = 8
SEL_LANE0 = NSA_DH
DEN_LANE = 64
VMEM_LIMIT = 56 * 1024 * 1024

IN_SIZES = (512, 128, 128, 128, 128, 128, 128, 24, 512, 512, 512, 512, 384, 256, 32, 3072)
IN_OFF = tuple(int(v) for v in np.concatenate([[0], np.cumsum(IN_SIZES)]))


def _dot(a, b):
    return jnp.dot(a, b, preferred_element_type=F32)


def _dot_nt(a, b):
    return lax.dot_general(a, b, (((1,), (1,)), ((), ())), preferred_element_type=F32)


def _ln(y, g, b):
    mu = jnp.mean(y, axis=-1, keepdims=True)
    yc = y - mu
    var = jnp.mean(yc * yc, axis=-1, keepdims=True)
    return yc * lax.rsqrt(var + LN_EPS) * g + b


def _pack_pair(a, b):
    lane = lax.broadcasted_iota(jnp.int32, a.shape, a.ndim - 1)
    return jnp.where(lane < LANE // 2, a, pltpu.roll(b, LANE // 2, axis=a.ndim - 1))


def _params(*sem):
    return pltpu.CompilerParams(dimension_semantics=sem, vmem_limit_bytes=VMEM_LIMIT)


def _resident(shape, index_map):
    return pl.BlockSpec(shape, index_map, pipeline_mode=pl.Buffered(1))


FFN_TM = 2048
FFN_TF = 256


def _ffn_kernel(x_ref, w1_ref, w3_ref, w2h_ref, g_ref, b_ref, o_ref, acc_ref, xb_ref, *, alpha):
    j = pl.program_id(1)

    @pl.when(j == 0)
    def _():
        xb_ref[...] = x_ref[...].astype(BF16)
        acc_ref[...] = alpha * x_ref[...]

    xb = xb_ref[...]
    h1 = _dot(xb, w1_ref[...])
    h3 = _dot(xb, w3_ref[...])
    h = (h1 * jax.nn.sigmoid(h1)) * h3
    acc_ref[...] += _dot(h.astype(BF16), w2h_ref[...])

    @pl.when(j == pl.num_programs(1) - 1)
    def _():
        o_ref[...] = _ln(acc_ref[...], g_ref[...], b_ref[...])


def _ffn_ln(x2, w1, w3, w2h, lng, lnb, l, which, ln_idx, alpha):
    n = x2.shape[0]
    tm = min(FFN_TM, n)
    grid = (n // tm, D_FF // FFN_TF)
    return pl.pallas_call(
        functools.partial(_ffn_kernel, alpha=alpha),
        grid=grid,
        in_specs=[
            pl.BlockSpec((tm, D_MODEL), lambda m, j: (m, 0)),
            pl.BlockSpec((None, None, D_MODEL, FFN_TF), lambda m, j: (l, which, 0, j)),
            pl.BlockSpec((None, None, D_MODEL, FFN_TF), lambda m, j: (l, which, 0, j)),
            pl.BlockSpec((None, None, FFN_TF, D_MODEL), lambda m, j: (l, which, j, 0)),
            pl.BlockSpec((None, None, 1, D_MODEL), lambda m, j: (l, ln_idx, 0, 0)),
            pl.BlockSpec((None, None, 1, D_MODEL), lambda m, j: (l, ln_idx, 0, 0)),
        ],
        out_specs=pl.BlockSpec((tm, D_MODEL), lambda m, j: (m, 0)),
        out_shape=jax.ShapeDtypeStruct((n, D_MODEL), F32),
        scratch_shapes=[pltpu.VMEM((tm, D_MODEL), F32), pltpu.VMEM((tm, D_MODEL), BF16)],
        compiler_params=_params("parallel", "arbitrary"),
        name="ffn_ln",
    )(x2, w1, w3, w2h, lng, lnb)


def _proj_kernel(x_ref, w_ref, o_ref):
    o_ref[...] = _dot(x_ref[...].astype(BF16), w_ref[...]).astype(o_ref.dtype)


def _proj(x2, w, l, out_dtype, tm=1024, tn=2048):
    n, k = x2.shape
    c = w.shape[2]
    tm = min(tm, n)
    tn = min(tn, c)
    return pl.pallas_call(
        _proj_kernel,
        grid=(n // tm, c // tn),
        in_specs=[pl.BlockSpec((tm, k), lambda m, j: (m, 0)),
                  pl.BlockSpec((None, k, tn), lambda m, j: (l, 0, j))],
        out_specs=pl.BlockSpec((tm, tn), lambda m, j: (m, j)),
        out_shape=jax.ShapeDtypeStruct((n, c), out_dtype),
        compiler_params=_params("parallel", "arbitrary"),
        name="proj",
    )(x2, w)


NSA_QW = NSA_HEADS * LANE
NSA_KW = NSA_KV * LANE
A_Q = 0
A_KV = NSA_QW
A_KV_ORDER = ("kc", "vc", "ks", "vs", "kw", "vw")
A_GATE = A_KV + len(A_KV_ORDER) * LANE
A_COLS = A_GATE + NSA_KW
PROJA_TM = 1024


def _rope_lanes(y, cos, sin_lo, sin_hi, half):
    return (y * cos + pltpu.roll(y, LANE - half, axis=1) * sin_lo
            + pltpu.roll(y, half, axis=1) * sin_hi)


def _proja_kernel(x_ref, w_ref, cos_ref, slo_ref, shi_ref,
                  q_ref, qr_ref, kc_ref, vc_ref, ks_ref, vs_ref, kw_ref, vw_ref, g_ref):
    xb = x_ref[...].astype(BF16)
    cos = cos_ref[...]
    sin_lo = slo_ref[...]
    sin_hi = shi_ref[...]
    half = NSA_ROT // 2
    tm = x_ref.shape[0]

    def mm(c0, width):
        return _dot(xb, w_ref[:, c0:c0 + width])

    qs = NSA_DH ** -0.5 * LOG2E
    y = mm(A_Q, NSA_QW)
    q_ref[...] = (y * qs).astype(BF16)
    for h in range(NSA_HEADS):
        hs = slice(h * LANE, (h + 1) * LANE)
        qr_ref[:, hs] = (_rope_lanes(y[:, hs], cos, sin_lo, sin_hi, half) * qs).astype(BF16)

    lane = lax.broadcasted_iota(jnp.int32, (tm, LANE), 1)
    first = lane < NSA_DH
    tok = pl.program_id(1) * tm + lax.broadcasted_iota(jnp.int32, (tm, LANE), 0)
    blk_flag = jnp.where(lane - SEL_LANE0 == tok // SEL_L, 1.0, 0.0)
    den_flag = jnp.where(lane == DEN_LANE, 1.0, 0.0)

    def packed(table):
        return jnp.where(first, table, pltpu.roll(table, NSA_DH, axis=1))

    def put(ref, y, flag=None):
        for g, yg in enumerate((y, pltpu.roll(y, NSA_DH, axis=1))):
            yg = jnp.where(first, yg, 0.0)
            ref[g] = (yg if flag is None else yg + flag).astype(ref.dtype)

    kv = mm(A_KV, len(A_KV_ORDER) * LANE)
    part = {name: kv[:, i * LANE:(i + 1) * LANE] for i, name in enumerate(A_KV_ORDER)}
    cos_p, lo_p, hi_p = packed(cos), packed(sin_lo), packed(sin_hi)
    put(kc_ref, part["kc"])
    put(vc_ref, part["vc"])
    put(ks_ref, _rope_lanes(part["ks"], cos_p, lo_p, hi_p, half), blk_flag)
    put(vs_ref, part["vs"], den_flag)
    put(kw_ref, _rope_lanes(part["kw"], cos_p, lo_p, hi_p, half))
    put(vw_ref, part["vw"], den_flag)
    g_ref[...] = mm(A_GATE, NSA_KW)


def _proja(x3, wa, l, cos_a, slo_a, shi_a):
    b, s, _ = x3.shape
    tm = PROJA_TM
    kv_shape = jax.ShapeDtypeStruct((b, NSA_KV, s, LANE), BF16)
    cmp_shape = jax.ShapeDtypeStruct((b, NSA_KV, s, LANE), F32)
    kv_spec = pl.BlockSpec((None, NSA_KV, tm, LANE), lambda bi, m: (bi, 0, m, 0))
    return pl.pallas_call(
        _proja_kernel,
        grid=(b, s // tm),
        in_specs=[
            pl.BlockSpec((None, tm, D_MODEL), lambda bi, m: (bi, m, 0)),
            _resident((None, D_MODEL, A_COLS), lambda bi, m: (l, 0, 0)),
            pl.BlockSpec((None, tm, LANE), lambda bi, m: (bi, m, 0)),
            pl.BlockSpec((None, tm, LANE), lambda bi, m: (bi, m, 0)),
            pl.BlockSpec((None, tm, LANE), lambda bi, m: (bi, m, 0)),
        ],
        out_specs=[
            pl.BlockSpec((None, tm, NSA_QW), lambda bi, m: (bi, m, 0)),
            pl.BlockSpec((None, tm, NSA_QW), lambda bi, m: (bi, m, 0)),
            kv_spec, kv_spec, kv_spec, kv_spec, kv_spec, kv_spec,
            pl.BlockSpec((None, tm, NSA_KW), lambda bi, m: (bi, m, 0)),
        ],
        out_shape=[
            jax.ShapeDtypeStruct((b, s, NSA_QW), BF16),
            jax.ShapeDtypeStruct((b, s, NSA_QW), BF16),
            cmp_shape, cmp_shape, kv_shape, kv_shape, kv_shape, kv_shape,
            jax.ShapeDtypeStruct((b, s, NSA_KW), F32),
        ],
        compiler_params=_params("parallel", "parallel"),
        name="nsa_proj",
    )(x3, wa, cos_a, slo_a, shi_a)


N_CMP = 128
CMP_HALF = CMP_D * LANE


def _gelu_tanh(x):
    return 0.5 * x * (1.0 + jnp.tanh(SQRT_2_OVER_PI * (x + 0.044715 * x * x * x)))


def _cmp_kernel(zk_ref, zv_ref, w1_ref, pe_ref, w2_ref, ok_ref, ov_ref):
    for which, (z_ref, o_ref) in enumerate(((zk_ref, ok_ref), (zv_ref, ov_ref))):
        first = jnp.zeros((N_CMP, CMP_HID), F32)
        second = jnp.zeros((N_CMP, CMP_HID), F32)
        for j in range(CMP_D):
            zj = z_ref[pl.ds(j, N_CMP, stride=CMP_D), :].astype(BF16)
            first = first + _dot(zj, w1_ref[which, j * LANE:(j + 1) * LANE, :])
            second = second + _dot(zj, w1_ref[which, CMP_HALF + j * LANE:CMP_HALF + (j + 1) * LANE, :])
        bias = _dot(pe_ref[which], w1_ref[which])[0:1]
        pre = first + pltpu.roll(second, N_CMP - 1, axis=0) + bias
        h = _gelu_tanh(pre)
        o_ref[...] = _dot(h.astype(BF16), w2_ref[which]).astype(BF16)


def _cmp(zk, zv, w1p, pe8, w2p, l):
    b, _, s, _ = zk.shape
    z_spec = pl.BlockSpec((None, None, s, LANE), lambda bi, g: (bi, g, 0, 0))
    o_spec = pl.BlockSpec((None, None, N_CMP, LANE), lambda bi, g: (bi, g, 0, 0))
    o_shape = jax.ShapeDtypeStruct((b, NSA_KV, N_CMP, LANE), BF16)
    return pl.pallas_call(
        _cmp_kernel,
        grid=(b, NSA_KV),
        in_specs=[
            z_spec, z_spec,
            _resident((None, 2, 2 * CMP_HALF, CMP_HID), lambda bi, g: (l, 0, 0, 0)),
            _resident((None, 2, 8, 2 * CMP_HALF), lambda bi, g: (l, 0, 0, 0)),
            _resident((None, 2, CMP_HID, LANE), lambda bi, g: (l, 0, 0, 0)),
        ],
        out_specs=[o_spec, o_spec],
        out_shape=[o_shape, o_shape],
        compiler_params=_params("parallel", "parallel"),
        name="nsa_cmp",
    )(zk, zv, w1p, pe8, w2p)


NSA_TQ = 256
NSA_TK = 512
N_SLC = 32
WIN_SLAB = WINDOW + NSA_TQ


def _nsa_kernel(q_ref, qr_ref, kc_ref, vc_ref, ks_ref, vs_ref, kw_ref, vw_ref, gt_ref, ovl_ref,
                place_ref, o_ref):
    tq, tk, r_heads = NSA_TQ, NSA_TK, NSA_R
    i = pl.program_id(2)
    t0 = i * tq
    tpos = t0 + lax.broadcasted_iota(jnp.int32, (tq, 1), 0)
    lane = lax.broadcasted_iota(jnp.int32, (1, LANE), 1)

    def stack(ref):
        return jnp.concatenate([ref[:, r * LANE:(r + 1) * LANE] for r in range(r_heads)], axis=0)

    gt = jax.nn.sigmoid(gt_ref[...])

    def gate(c):
        return jnp.stack([gt[:, 3 * r + c:3 * r + c + 1] for r in range(r_heads)], axis=0)

    q4w = stack(qr_ref)
    w0 = pl.multiple_of(jnp.maximum(t0 - WINDOW, 0), LANE)
    kw = kw_ref[pl.ds(w0, WIN_SLAB), :]
    vw = vw_ref[pl.ds(w0, WIN_SLAB), :]
    kpos = w0 + lax.broadcasted_iota(jnp.int32, (1, WIN_SLAB), 1)
    w_bias = jnp.where((kpos <= tpos) & (kpos > tpos - WINDOW), 0.0, NEG)
    s = _dot_nt(q4w, kw).reshape(r_heads, tq, WIN_SLAB) + w_bias[None]
    p = jnp.exp2(s - jnp.max(s, axis=-1, keepdims=True))
    o_win = _dot(p.reshape(r_heads * tq, WIN_SLAB).astype(BF16), vw).reshape(r_heads, tq, LANE)
    o_part = gate(2) * (o_win / o_win[:, :, DEN_LANE:DEN_LANE + 1])

    cmask = ((lane * CMP_D + (CMP_L - 1) <= tpos) & (lane < N_CMP - 1))[None]
    s = _dot_nt(stack(q_ref), kc_ref[...]).reshape(r_heads, tq, N_CMP)
    s = jnp.where(cmask, s, NEG)
    e = jnp.where(cmask, jnp.exp2(s - jnp.max(s, axis=-1, keepdims=True)), 0.0)
    den = jnp.sum(e, axis=-1, keepdims=True)
    p = e / jnp.where(den > 0.0, den, 1.0)
    psum = jnp.sum(p, axis=0)
    o_cmp = _dot(p.reshape(r_heads * tq, N_CMP).astype(BF16), vc_ref[...])
    o_part = o_part + gate(0) * o_cmp.reshape(r_heads, tq, LANE)

    imp = lax.dot_general(ovl_ref[...], psum, (((1,), (1,)), ((), ())),
                          precision=lax.Precision.HIGHEST, preferred_element_type=F32)
    blk = lax.broadcasted_iota(jnp.int32, (N_SLC, 1), 0)
    tpos_l = t0 + lax.broadcasted_iota(jnp.int32, (1, tq), 1)
    cur = tpos_l // SEL_L
    valid = blk * SEL_L <= tpos_l
    forced = (blk == 0) | (blk == cur) | (blk == cur - 1)
    score = jnp.where(valid & forced, BIG, jnp.where(valid, imp, -BIG))
    beats = []
    for j in range(N_SLC):
        sj = score[j:j + 1, :]
        beats.append(jnp.where((sj > score) | ((sj == score) & (j < blk)), 1.0, 0.0))
    while len(beats) > 1:
        beats = [a + b for a, b in zip(beats[0::2], beats[1::2])]
    sel_t = jnp.where((beats[0] < N_SEL) & valid, 1.0, 0.0)
    sel_q = lax.dot_general(sel_t, place_ref[...], (((0,), (0,)), ((), ())),
                            preferred_element_type=F32)
    in_flags = jnp.where((lane >= SEL_LANE0) & (lane < SEL_LANE0 + N_SLC), 1.0, 0.0)
    q_bias = (sel_q - in_flags) * (-NEG)
    q4 = (q4w.astype(F32).reshape(r_heads, tq, LANE) + q_bias[None]).astype(BF16)
    q4 = q4.reshape(r_heads * tq, LANE)

    def sel_step(c, carry, causal):
        m, acc = carry
        k0 = pl.multiple_of(c * tk, tk)
        k = ks_ref[pl.ds(k0, tk), :]
        v = vs_ref[pl.ds(k0, tk), :]
        s = _dot_nt(q4, k).reshape(r_heads, tq, tk)
        if causal:
            kpos = k0 + lax.broadcasted_iota(jnp.int32, (1, tk), 1)
            s = jnp.where((kpos <= tpos)[None], s, NEG)
        m_new = jnp.maximum(m, jnp.max(s, axis=-1, keepdims=True))
        p = jnp.exp2(s - m_new)
        pv = _dot(p.reshape(r_heads * tq, tk).astype(BF16), v).reshape(r_heads, tq, LANE)
        return m_new, jnp.exp2(m - m_new) * acc + pv

    last = (t0 + tq - 1) // tk
    carry = (jnp.full((r_heads, tq, 1), NEG, F32), jnp.zeros((r_heads, tq, LANE), F32))
    carry = lax.fori_loop(0, last, functools.partial(sel_step, causal=False), carry)
    _, acc_s = sel_step(last, carry, True)
    o = o_part + gate(1) * (acc_s / acc_s[:, :, DEN_LANE:DEN_LANE + 1])
    for r in range(0, r_heads, 2):
        o_ref[:, (r // 2) * LANE:(r // 2 + 1) * LANE] = _pack_pair(o[r], o[r + 1]).astype(BF16)


def _nsa_attn(q, qr, kcmp, vcmp, ks, vs, kw, vw, gates, ovl, place):
    b, s, _ = q.shape
    tq = NSA_TQ
    gw = NSA_R * LANE
    q_spec = pl.BlockSpec((None, tq, gw), lambda bi, g, i: (bi, i, g))
    c_spec = pl.BlockSpec((None, None, N_CMP, LANE), lambda bi, g, i: (bi, g, 0, 0))
    kv_spec = pl.BlockSpec((None, None, s, LANE), lambda bi, g, i: (bi, g, 0, 0))
    return pl.pallas_call(
        _nsa_kernel,
        grid=(b, NSA_KV, s // tq),
        in_specs=[q_spec, q_spec, c_spec, c_spec, kv_spec, kv_spec, kv_spec, kv_spec,
                  pl.BlockSpec((None, tq, LANE), lambda bi, g, i: (bi, i, g)),
                  _resident((N_SLC, N_CMP), lambda bi, g, i: (0, 0)),
                  _resident((N_SLC, LANE), lambda bi, g, i: (0, 0))],
        out_specs=pl.BlockSpec((None, tq, gw // 2), lambda bi, g, i: (bi, i, g)),
        out_shape=jax.ShapeDtypeStruct((b, s, MIX_W), BF16),
        compiler_params=_params("parallel", "parallel", "arbitrary"),
        name="nsa_attn",
    )(q, qr, kcmp, vcmp, ks, vs, kw, vw, gates, ovl, place)


def _hgrn_kernel(x_ref, wq_ref, wf_ref, wi_ref, wg_ref, lb_ref, ng_ref, o_ref, st_ref):
    w_head = jnp.concatenate([wq_ref[...], wf_ref[...], wi_ref[...], wg_ref[...]], axis=1)
    s_len = x_ref.shape[0]
    c_len, sub_len = HGRN_CHUNK, HGRN_SUB
    p_len = s_len // HGRN_PARTS
    nc = p_len // c_len
    lb = lb_ref[...]
    t = lax.broadcasted_iota(jnp.int32, (p_len, 1), 0)
    t_sub = t % sub_len
    row = lax.broadcasted_iota(jnp.int32, (nc, c_len, c_len), 1)
    col = lax.broadcasted_iota(jnp.int32, (nc, c_len, c_len), 2)
    tri = jnp.where(col <= row, 1.0, 0.0).astype(BF16)

    local = []
    for part in range(HGRN_PARTS):
        proj = _dot(x_ref[pl.ds(part * p_len, p_len), :].astype(BF16), w_head)
        q = proj[:, 0:HK]
        z = proj[:, HK:2 * HK]
        v = proj[:, 2 * HK:2 * HK + HV]
        go = proj[:, 2 * HK + HV:2 * HK + 2 * HV]

        sig = jax.nn.sigmoid(z)
        f = lb + (1.0 - lb) * sig
        lf = jnp.log(jnp.maximum(f, F_MIN))
        k = (1.0 - lb) * (1.0 - sig)

        lf3 = lf.reshape(nc, c_len, HK)
        hi = lf3.astype(BF16)
        rest = lf3 - hi.astype(F32)
        mid = rest.astype(BF16)
        low = (rest - mid.astype(F32)).astype(BF16)
        sums = jnp.einsum('cts,csd->ctd', tri, jnp.concatenate([hi, mid, low], axis=-1),
                          preferred_element_type=F32)
        b3 = sums[..., 0:HK] + sums[..., HK:2 * HK] + sums[..., 2 * HK:3 * HK]
        b = b3.reshape(p_len, HK)
        k3 = k.reshape(nc, c_len, HK)
        vb3 = v.astype(BF16).reshape(nc, c_len, HV)

        a_intra = None
        h = c_len // 2
        while h >= sub_len:
            blk = b.reshape(p_len // (2 * h), 2 * h, HK)
            e = jnp.exp(-jnp.abs(blk[:, h - 1:h, :] - blk)).reshape(p_len, HK)
            upper = (t // h) % 2 == 1
            lq = jnp.where(upper, q * e, 0.0).astype(BF16).reshape(nc, c_len, HK)
            rk = jnp.where(upper, 0.0, k * e).astype(BF16).reshape(nc, c_len, HK)
            a = jnp.einsum('ctk,csk->cts', lq, rk, preferred_element_type=F32)
            if 2 * h < c_len:
                a = jnp.where(row // (2 * h) == col // (2 * h), a, 0.0)
            a_intra = a if a_intra is None else a_intra + a
            h //= 2
        o = jnp.einsum('cts,csd->ctd', a_intra.astype(BF16), vb3,
                       preferred_element_type=F32).reshape(p_len, HV)

        o = o + jnp.sum(q * k, axis=-1, keepdims=True) * v

        def back(x, d):
            x3 = x.reshape(p_len // SUBLANES, SUBLANES, x.shape[-1])
            return pltpu.roll(x3, d, axis=1).reshape(x.shape)

        for d in range(1, sub_len):
            e = jnp.exp(jnp.where(t_sub >= d, b - back(b, d), NEG))
            w = jnp.sum(q * back(k, d) * e, axis=-1, keepdims=True)
            o = o + w * back(v, d)

        b_last = b3[:, c_len - 1:c_len, :]
        kd3 = (k3 * jnp.exp(b_last - b3)).astype(BF16)
        upd = jnp.einsum('csv,csk->cvk', vb3, kd3, preferred_element_type=F32)
        qe3 = (q * jnp.exp(b)).astype(BF16).reshape(nc, c_len, HK)
        local.append((o, go, qe3, upd, jnp.exp(b_last)))

    state = jnp.zeros((HV, HK), F32)
    for part, (_, _, _, upd, dec) in enumerate(local):
        for c in range(nc):
            st_ref[part * nc + c] = state.astype(BF16)
            state = state * dec[c] + upd[c]

    for part, (o, go, qe3, _, _) in enumerate(local):
        o = o + jnp.einsum('ctk,cvk->ctv', qe3, st_ref[pl.ds(part * nc, nc)],
                           preferred_element_type=F32).reshape(p_len, HV)
        o = o * lax.rsqrt(jnp.mean(o * o, axis=-1, keepdims=True) + RMS_EPS) * ng_ref[...]
        o_ref[pl.ds(part * p_len, p_len), :] = (o * (go * jax.nn.sigmoid(go))).astype(BF16)


def _hgrn(x3, wb, lb, ng, l):
    b, s, d = x3.shape

    def cols(p):
        return pl.BlockSpec((None, d, HK), lambda bi, h: (l, 0, p * HB + h))

    return pl.pallas_call(
        _hgrn_kernel,
        grid=(b, HB),
        in_specs=[pl.BlockSpec((None, s, d), lambda bi, h: (bi, 0, 0)),
                  cols(0), cols(1), cols(2), cols(3),
                  pl.BlockSpec((None, None, 1, HK), lambda bi, h: (l, h, 0, 0)),
                  pl.BlockSpec((None, 1, HV), lambda bi, h: (l, 0, 0))],
        out_specs=pl.BlockSpec((None, s, HV), lambda bi, h: (bi, 0, h)),
        out_shape=jax.ShapeDtypeStruct((b, s, HB * HV), BF16),
        scratch_shapes=[pltpu.VMEM((s // HGRN_CHUNK, HV, HK), BF16)],
        compiler_params=_params("parallel", "arbitrary"),
        name="hgrn",
    )(x3, wb, wb, wb, wb, lb, ng)


MLA_W = HC * LANE
C_CQ, C_CKV, C_KR = 0, Q_RANK, Q_RANK + KV_RANK
C_COLS = C_KR + LANE
MLAP_TM = 1024


def _rms(x, g):
    return x * lax.rsqrt(jnp.mean(x * x, axis=-1, keepdims=True) + RMS_EPS) * g


def _mlap_kernel(x_ref, wc_ref, qg_ref, kvg_ref, wuq_ref, wuk_ref, wuv_ref,
                 cos_ref, slo_ref, shi_ref, q_ref, k_ref, v_ref):
    half = ROPE_D // 2
    qs = (NOPE + ROPE_D) ** -0.5 * LOG2E
    den_flag = jnp.where(lax.broadcasted_iota(jnp.int32, (1, LANE), 1) == DEN_LANE, 1.0, 0.0)
    cos, lo, hi = cos_ref[...], slo_ref[...], shi_ref[...]
    c = _dot(x_ref[...].astype(BF16), wc_ref[...])
    nq = _rms(c[:, C_CQ:C_CKV], qg_ref[...]).astype(BF16)
    nkv = _rms(c[:, C_CKV:C_KR], kvg_ref[...]).astype(BF16)
    k_pe = _rope_lanes(c[:, C_KR:C_COLS], cos, lo, hi, half)
    yq = _dot(nq, wuq_ref[...])
    yk = _dot(nkv, wuk_ref[...])
    yv = _dot(nkv, wuv_ref[...])
    cos_q, lo_q, hi_q = cos * qs, lo * qs, hi * qs
    for h in range(HC):
        hs = slice(h * LANE, (h + 1) * LANE)
        q_ref[:, hs] = _rope_lanes(yq[:, hs], cos_q, lo_q, hi_q, half).astype(BF16)
        k_ref[:, hs] = (yk[:, hs] + k_pe).astype(BF16)
        v_ref[:, hs] = (yv[:, hs] + den_flag).astype(BF16)


def _mlap(x3, wc, qg, kvg, wuq, wuk, wuv, l, cos_c, slo_c, shi_c):
    b, s, _ = x3.shape
    tm = MLAP_TM
    o_spec = pl.BlockSpec((None, tm, MLA_W), lambda bi, m: (bi, m, 0))
    o_shape = jax.ShapeDtypeStruct((b, s, MLA_W), BF16)
    t_spec = pl.BlockSpec((None, tm, LANE), lambda bi, m: (bi, m, 0))
    return pl.pallas_call(
        _mlap_kernel,
        grid=(b, s // tm),
        in_specs=[
            pl.BlockSpec((None, tm, D_MODEL), lambda bi, m: (bi, m, 0)),
            _resident((None, D_MODEL, C_COLS), lambda bi, m: (l, 0, 0)),
            _resident((None, 1, Q_RANK), lambda bi, m: (l, 0, 0)),
            _resident((None, 1, KV_RANK), lambda bi, m: (l, 0, 0)),
            _resident((None, Q_RANK, MLA_W), lambda bi, m: (l, 0, 0)),
            _resident((None, KV_RANK, MLA_W), lambda bi, m: (l, 0, 0)),
            _resident((None, KV_RANK, MLA_W), lambda bi, m: (l, 0, 0)),
            t_spec, t_spec, t_spec,
        ],
        out_specs=[o_spec, o_spec, o_spec],
        out_shape=[o_shape, o_shape, o_shape],
        compiler_params=_params("parallel", "parallel"),
        name="mla_proj",
    )(x3, wc, qg, kvg, wuq, wuk, wuv, cos_c, slo_c, shi_c)


MLA_TQ = 512
MLA_TK = MLA_TQ


MLA_HPS = 4


def _mla_attn_kernel(q_ref, k_ref, v_ref, o_ref):
    tq, tk = MLA_TQ, MLA_TK
    i = pl.program_id(2)
    t0 = i * tq
    heads = [slice(h * LANE, (h + 1) * LANE) for h in range(MLA_HPS)]
    qs = [q_ref[:, hs] for hs in heads]

    def attend(k0, streams, causal):
        scores = [_dot_nt(q, k_ref[pl.ds(k0, nk), hs]) for q, hs, _, _, nk, _ in streams]
        m_new, probs = [], []
        for s, (q, _, m, _, nk, r0) in zip(scores, streams):
            if causal:
                kpos = k0 + lax.broadcasted_iota(jnp.int32, (1, nk), 1)
                qpos = t0 + r0 + lax.broadcasted_iota(jnp.int32, (q.shape[0], 1), 0)
                s = jnp.where(kpos <= qpos, s, NEG)
            m_new.append(jnp.maximum(m, jnp.max(s, axis=-1, keepdims=True)))
            probs.append(jnp.exp2(s - m_new[-1]).astype(BF16))
        pvs = [_dot(p, v_ref[pl.ds(k0, nk), hs]) for p, (_, hs, _, _, nk, _) in zip(probs, streams)]
        return [(mn, jnp.exp2(m - mn) * acc + pv)
                for mn, pv, (_, _, m, acc, _, _) in zip(m_new, pvs, streams)]

    def full_chunk(c, carry):
        streams = [(qs[h], hs, *carry[h], tk, 0) for h, hs in enumerate(heads)]
        return tuple(attend(pl.multiple_of(c * tk, tk), streams, False))

    carry = tuple((jnp.full((tq, 1), NEG, F32), jnp.zeros((tq, LANE), F32)) for _ in heads)
    carry = lax.fori_loop(0, i, full_chunk, carry)
    hq = tq // 2
    streams = [(qs[h][r0:r0 + hq], hs, carry[h][0][r0:r0 + hq], carry[h][1][r0:r0 + hq], r0 + hq, r0)
               for h, hs in enumerate(heads) for r0 in (0, hq)]
    done = attend(pl.multiple_of(t0, tq), streams, True)
    outs = []
    for h in range(len(heads)):
        acc = jnp.concatenate([done[2 * h][1], done[2 * h + 1][1]], axis=0)
        outs.append(acc / acc[:, DEN_LANE:DEN_LANE + 1])
    for h in range(0, MLA_HPS, 2):
        o_ref[:, (h // 2) * LANE:(h // 2 + 1) * LANE] = _pack_pair(outs[h], outs[h + 1]).astype(BF16)


def _mla_attn(q, k, v):
    b, s, _ = q.shape
    tq = MLA_TQ
    gw = MLA_HPS * LANE
    return pl.pallas_call(
        _mla_attn_kernel,
        grid=(b, HC // MLA_HPS, s // tq),
        in_specs=[pl.BlockSpec((None, tq, gw), lambda bi, h, i: (bi, i, h)),
                  pl.BlockSpec((None, s, gw), lambda bi, h, i: (bi, 0, h)),
                  pl.BlockSpec((None, s, gw), lambda bi, h, i: (bi, 0, h))],
        out_specs=pl.BlockSpec((None, tq, gw // 2), lambda bi, h, i: (bi, i, h)),
        out_shape=jax.ShapeDtypeStruct((b, s, MIX_W), BF16),
        compiler_params=_params("parallel", "parallel", "arbitrary"),
        name="mla_attn",
    )(q, k, v)


MERGE_TM = 1024


def _merge_kernel(x_ref, ya_ref, yb_ref, yc_ref, wm_ref, wa_ref, wb_ref, wc_ref, wo_ref,
                  g_ref, b_ref, o_ref, *, alpha):
    x = x_ref[...]
    xb = x.astype(BF16)
    mixed = jnp.zeros(x.shape, F32)
    for idx, (y_ref, w_ref) in enumerate(((ya_ref, wa_ref), (yb_ref, wb_ref), (yc_ref, wc_ref))):
        gate = jax.nn.sigmoid(_dot(xb, wm_ref[:, idx * D_MODEL:(idx + 1) * D_MODEL]))
        mixed = mixed + gate * _dot(y_ref[...], w_ref[...])
    y = alpha * x + _dot(mixed.astype(BF16), wo_ref[...])
    o_ref[...] = _ln(y, g_ref[...], b_ref[...])


def _merge(x2, ya, yb, yc, wm, wa, wb, wc, wo, lng, lnb, l, alpha):
    n = x2.shape[0]
    tm = min(MERGE_TM, n)

    def rows(width):
        return pl.BlockSpec((tm, width), lambda m: (m, 0))

    return pl.pallas_call(
        functools.partial(_merge_kernel, alpha=alpha),
        grid=(n // tm,),
        in_specs=[rows(D_MODEL), rows(MIX_W), rows(MIX_W), rows(MIX_W),
                  _resident((None, D_MODEL, 3 * D_MODEL), lambda m: (l, 0, 0)),
                  _resident((None, None, MIX_W, D_MODEL), lambda m: (l, 0, 0, 0)),
                  _resident((None, None, MIX_W, D_MODEL), lambda m: (l, 1, 0, 0)),
                  _resident((None, None, MIX_W, D_MODEL), lambda m: (l, 2, 0, 0)),
                  _resident((None, D_MODEL, D_MODEL), lambda m: (l, 0, 0)),
                  _resident((None, None, 1, D_MODEL), lambda m: (l, 1, 0, 0)),
                  _resident((None, None, 1, D_MODEL), lambda m: (l, 1, 0, 0))],
        out_specs=rows(D_MODEL),
        out_shape=jax.ShapeDtypeStruct((n, D_MODEL), F32),
        compiler_params=_params("parallel"),
        name="merge",
    )(x2, ya, yb, yc, wm, wa, wb, wc, wo, lng, lnb)


XA_TM = 1024


def _xattn_kernel(x_ref, wq_ref, k_ref, v_ref, wo_ref, g_ref, b_ref, o_ref, *, alpha):
    x = x_ref[...]
    xb = x.astype(BF16)
    q = (_dot(xb, wq_ref[...]) * (XA_DH ** -0.5 * LOG2E)).astype(BF16)
    ones = jnp.ones((k_ref.shape[0], LANE), BF16)
    heads = []
    for h in range(XA_HEADS):
        hs = slice(h * XA_DH, (h + 1) * XA_DH)
        s = _dot_nt(q[:, hs], k_ref[:, hs])
        e = jnp.exp2(s - jnp.max(s, axis=-1, keepdims=True)).astype(BF16)
        den = _dot(e, ones)
        den = jnp.concatenate([den] * (XA_DH // LANE), axis=1)
        heads.append((_dot(e, v_ref[:, hs]) / den).astype(BF16))
    att = jnp.concatenate(heads, axis=1)
    y = alpha * x + _dot(att, wo_ref[...])
    o_ref[...] = _ln(y, g_ref[...], b_ref[...])


def _xattn(x3, wq, kv, wo, lng, lnb, l, alpha):
    b, s, _ = x3.shape
    m_len = kv.shape[1]
    tm = XA_TM
    return pl.pallas_call(
        functools.partial(_xattn_kernel, alpha=alpha),
        grid=(b, s // tm),
        in_specs=[pl.BlockSpec((None, tm, D_MODEL), lambda bi, m: (bi, m, 0)),
                  _resident((None, D_MODEL, D_MODEL), lambda bi, m: (l, 0, 0)),
                  pl.BlockSpec((None, m_len, D_MODEL), lambda bi, m: (bi, 0, 0)),
                  pl.BlockSpec((None, m_len, D_MODEL), lambda bi, m: (bi, 0, 1)),
                  _resident((None, D_MODEL, D_MODEL), lambda bi, m: (l, 0, 0)),
                  _resident((None, None, 1, D_MODEL), lambda bi, m: (l, 2, 0, 0)),
                  _resident((None, None, 1, D_MODEL), lambda bi, m: (l, 2, 0, 0))],
        out_specs=pl.BlockSpec((None, tm, D_MODEL), lambda bi, m: (bi, m, 0)),
        out_shape=jax.ShapeDtypeStruct((b, s, D_MODEL), F32),
        compiler_params=_params("parallel", "parallel"),
        name="xattn",
    )(x3, wq, kv, kv, wo, lng, lnb)


def _pad_heads(w, n_heads, dh):
    depth, k, _ = w.shape
    w = w.reshape(depth, k, n_heads, dh)
    return jnp.pad(w, ((0, 0), (0, 0), (0, 0), (0, LANE - dh))).reshape(depth, k, n_heads * LANE)


def _rope_tables(positions, rot, off):
    half = rot // 2
    inv = ROPE_THETA ** (-jnp.arange(half, dtype=F32) / half)
    ang = positions.astype(F32)[..., None] * inv
    cos, sin = jnp.cos(ang), jnp.sin(ang)
    shape = positions.shape
    zeros = lambda n: jnp.zeros(shape + (n,), F32)
    cos_t = jnp.concatenate([jnp.ones(shape + (off,), F32), cos, cos,
                             jnp.ones(shape + (LANE - off - rot,), F32)], axis=-1)
    sin_lo = jnp.concatenate([zeros(off), -sin, zeros(LANE - off - half)], axis=-1)
    sin_hi = jnp.concatenate([zeros(off + half), sin, zeros(LANE - off - rot)], axis=-1)
    return cos_t, sin_lo, sin_hi


def _overlap_matrix(s_len):
    n_cmp = (s_len - CMP_L) // CMP_D + 1
    n_slc = s_len // SEL_L
    start = np.arange(n_cmp) * CMP_D
    j = np.arange(n_slc)
    ov = np.clip(np.minimum(start[:, None] + CMP_L, (j[None, :] + 1) * SEL_L)
                 - np.maximum(start[:, None], j[None, :] * SEL_L), 0, None) / CMP_L
    out = np.zeros((N_SLC, N_CMP), np.float32)
    out[:n_slc, :n_cmp] = ov.T
    place = np.zeros((N_SLC, LANE), np.float32)
    place[np.arange(N_SLC), SEL_LANE0 + np.arange(N_SLC)] = 1.0
    return jnp.asarray(out), jnp.asarray(place)


def _stacked_weights(w_in, nsa_cmp_pos, nsa_cmp_w1, nsa_cmp_w2, mla_w_uq, mla_w_ukv):
    depth = w_in.shape[0]
    w_in = w_in.astype(BF16)
    part = [w_in[:, :, IN_OFF[i]:IN_OFF[i + 1]] for i in range(len(IN_SIZES))]
    (a_q, a_kc, a_vc, a_ks, a_vs, a_kw, a_vw, a_gate, b_q, b_f, b_i, b_g,
     c_q, c_kv, c_kr, merge) = part
    gate = _pad_heads(a_gate, NSA_KV, NSA_R * 3)
    kv_parts = dict(kc=a_kc, vc=a_vc, ks=a_ks, vs=a_vs, kw=a_kw, vw=a_vw)
    wa = jnp.concatenate([_pad_heads(a_q, NSA_HEADS, NSA_DH)]
                         + [kv_parts[name] for name in A_KV_ORDER] + [gate], axis=2).astype(BF16)
    wb = jnp.concatenate([b_q, b_f, b_i, b_g], axis=2).astype(BF16)

    kr_pad = jnp.pad(c_kr, ((0, 0), (0, 0), (NOPE, LANE - NOPE - ROPE_D)))
    wc = jnp.concatenate([c_q, c_kv, kr_pad], axis=2).astype(BF16)
    wuq = _pad_heads(mla_w_uq, HC, NOPE + ROPE_D).astype(BF16)
    ukv = mla_w_ukv.reshape(depth, KV_RANK, HC, NOPE + VD)
    wuk = _pad_heads(ukv[..., :NOPE].reshape(depth, KV_RANK, HC * NOPE), HC, NOPE).astype(BF16)
    wuv = _pad_heads(ukv[..., NOPE:].reshape(depth, KV_RANK, HC * VD), HC, VD).astype(BF16)

    w1 = nsa_cmp_w1.reshape(depth, 2, CMP_L, NSA_DH, CMP_HID)
    w1p = jnp.pad(w1, ((0, 0), (0, 0), (0, 0), (0, LANE - NSA_DH), (0, 0)))
    w1p = w1p.reshape(depth, 2, CMP_L * LANE, CMP_HID)
    pe = jnp.pad(nsa_cmp_pos, ((0, 0), (0, 0), (0, 0), (0, LANE - NSA_DH)))
    pe8 = jnp.broadcast_to(pe.reshape(depth, 2, 1, CMP_L * LANE), (depth, 2, 8, CMP_L * LANE))
    w2p = jnp.pad(nsa_cmp_w2, ((0, 0), (0, 0), (0, 0), (0, LANE - NSA_DH)))
    return dict(wa=wa, wb=wb, wc=wc, wuq=wuq, wuk=wuk, wuv=wuv,
                w1p=w1p.astype(BF16), pe8=pe8.astype(BF16), w2p=w2p.astype(BF16),
                wm=merge.astype(BF16))


def kernel(x, mem, positions, ln_g, ln_b, ffn_w1, ffn_w3, ffn_w2, w_in, nsa_cmp_pos,
           nsa_cmp_w1, nsa_cmp_w2, hgrn_lb_logits, hgrn_norm_g, mla_q_norm_g, mla_w_uq,
           mla_kv_norm_g, mla_w_ukv, w_branch, w_out, xa_wq, xa_wk, xa_wv, xa_wo):
    b, s, d = x.shape
    depth = ln_g.shape[0]
    assert d == D_MODEL and s // SEL_L == N_SLC and (s - CMP_L) // CMP_D + 2 == N_CMP
    assert s % max(NSA_TQ, NSA_TK, MLA_TQ, PROJA_TM, MLAP_TM, XA_TM) == 0 and NSA_TK % NSA_TQ == 0
    n = b * s
    alpha = (2.0 * depth) ** 0.25

    lng = ln_g.reshape(depth, 4, 1, d)
    lnb = ln_b.reshape(depth, 4, 1, d)
    w1 = ffn_w1.astype(BF16)
    w3 = ffn_w3.astype(BF16)
    w2 = (0.5 * ffn_w2).astype(BF16)
    wo = w_out.astype(BF16)
    wbr = w_branch.astype(BF16)
    xq = xa_wq.astype(BF16)
    xkv = jnp.concatenate([xa_wk, xa_wv], axis=2).astype(BF16)
    xo = xa_wo.astype(BF16)
    p_lb = jax.nn.softmax(hgrn_lb_logits.astype(F32), axis=0)
    lower = (jnp.cumsum(p_lb, axis=0) - p_lb[0:1]).reshape(depth, HB, 1, HK)

    rope_a = _rope_tables(positions, NSA_ROT, 0)
    rope_c = _rope_tables(positions, ROPE_D, NOPE)
    ovl, place = _overlap_matrix(s)
    mem2 = mem.reshape(b * mem.shape[1], d)
    w = _stacked_weights(w_in, nsa_cmp_pos, nsa_cmp_w1, nsa_cmp_w2, mla_w_uq, mla_w_ukv)
    norm_b = hgrn_norm_g.reshape(depth, 1, HV)
    norm_q = mla_q_norm_g.reshape(depth, 1, Q_RANK)
    norm_kv = mla_kv_norm_g.reshape(depth, 1, KV_RANK)

    x2 = x.reshape(n, d)
    for l in range(depth):
        x2 = _ffn_ln(x2, w1, w3, w2, lng, lnb, l, 0, 0, alpha)
        x3 = x2.reshape(b, s, d)

        q, qr, kc, vc, ks, vs, kw, vw, gates = _proja(x3, w["wa"], l, *rope_a)
        kcmp, vcmp = _cmp(kc, vc, w["w1p"], w["pe8"], w["w2p"], l)
        ya = _nsa_attn(q, qr, kcmp, vcmp, ks, vs, kw, vw, gates, ovl, place)

        yb = _hgrn(x3, w["wb"], lower, norm_b, l)

        mq, mk, mv = _mlap(x3, w["wc"], norm_q, norm_kv, w["wuq"], w["wuk"], w["wuv"], l, *rope_c)
        yc = _mla_attn(mq, mk, mv)

        x2 = _merge(x2, ya.reshape(n, MIX_W), yb.reshape(n, MIX_W), yc.reshape(n, MIX_W),
                    w["wm"], wbr, wbr, wbr, wo, lng, lnb, l, alpha)

        kv_l = _proj(mem2, xkv, l, BF16).reshape(b, -1, 2 * d)
        x2 = _xattn(x2.reshape(b, s, d), xq, kv_l, xo, lng, lnb, l, alpha).reshape(n, d)

        x2 = _ffn_ln(x2, w1, w3, w2, lng, lnb, l, 1, 3, alpha)
    return x2.reshape(b, s, d)
```

```python
import functools

import numpy as np
import jax
import jax.numpy as jnp
from jax import lax
from jax.experimental import pallas as pl
from jax.experimental.pallas import tpu as pltpu

F32 = jnp.float32
BF16 = jnp.bfloat16

D_MODEL = 1024
MIX_W = D_MODEL // 2
NSA_DH = 64
NSA_HEADS = 8
NSA_KV = 2
NSA_R = 4
NSA_ROT = 16
CMP_L = 32
CMP_D = 16
CMP_HID = 256
SEL_L = 64
N_SEL = 8
WINDOW = 256
HB = 4
HK = 128
HV = 128
HGRN_CHUNK = 64
HGRN_SUB = 4
HGRN_PARTS = 4
HC = 8
NOPE = 64
ROPE_D = 32
VD = 64
Q_RANK = 384
KV_RANK = 256
XA_HEADS = 4
XA_DH = D_MODEL // XA_HEADS
D_FF = 2816
ROPE_THETA = 500000.0
LN_EPS = 1e-5
RMS_EPS = 1e-6
NEG = -1e30
BIG = 1e9
F_MIN = 1e-20

LOG2E = 1.4426950408889634
SQRT_2_OVER_PI = 0.7978845608028654
LANE = 128
SUBLANES = 8
SEL_LANE0 = NSA_DH
DEN_LANE = 64
VMEM_LIMIT = 56 * 1024 * 1024

IN_SIZES = (512, 128, 128, 128, 128, 128, 128, 24, 512, 512, 512, 512, 384, 256, 32, 3072)
IN_OFF = tuple(int(v) for v in np.concatenate([[0], np.cumsum(IN_SIZES)]))


def _dot(a, b):
    return jnp.dot(a, b, preferred_element_type=F32)


def _dot_nt(a, b):
    return lax.dot_general(a, b, (((1,), (1,)), ((), ())), preferred_element_type=F32)


def _ln(y, g, b):
    mu = jnp.mean(y, axis=-1, keepdims=True)
    yc = y - mu
    var = jnp.mean(yc * yc, axis=-1, keepdims=True)
    return yc * lax.rsqrt(var + LN_EPS) * g + b


def _pack_pair(a, b):
    lane = lax.broadcasted_iota(jnp.int32, a.shape, a.ndim - 1)
    return jnp.where(lane < LANE // 2, a, pltpu.roll(b, LANE // 2, axis=a.ndim - 1))


def _params(*sem):
    return pltpu.CompilerParams(dimension_semantics=sem, vmem_limit_bytes=VMEM_LIMIT)


def _resident(shape, index_map):
    return pl.BlockSpec(shape, index_map, pipeline_mode=pl.Buffered(1))


FFN_TM = 2048
FFN_TF = 256


def _ffn_kernel(x_ref, w1_ref, w3_ref, w2h_ref, g_ref, b_ref, o_ref, acc_ref, xb_ref, *, alpha):
    j = pl.program_id(1)

    @pl.when(j == 0)
    def _():
        xb_ref[...] = x_ref[...].astype(BF16)
        acc_ref[...] = alpha * x_ref[...]

    xb = xb_ref[...]
    h1 = _dot(xb, w1_ref[...])
    h3 = _dot(xb, w3_ref[...])
    h = (h1 * jax.nn.sigmoid(h1)) * h3
    acc_ref[...] += _dot(h.astype(BF16), w2h_ref[...])

    @pl.when(j == pl.num_programs(1) - 1)
    def _():
        o_ref[...] = _ln(acc_ref[...], g_ref[...], b_ref[...])


def _ffn_ln(x2, w1, w3, w2h, lng, lnb, l, which, ln_idx, alpha):
    n = x2.shape[0]
    tm = min(FFN_TM, n)
    grid = (n // tm, D_FF // FFN_TF)
    return pl.pallas_call(
        functools.partial(_ffn_kernel, alpha=alpha),
        grid=grid,
        in_specs=[
            pl.BlockSpec((tm, D_MODEL), lambda m, j: (m, 0)),
            pl.BlockSpec((None, None, D_MODEL, FFN_TF), lambda m, j: (l, which, 0, j)),
            pl.BlockSpec((None, None, D_MODEL, FFN_TF), lambda m, j: (l, which, 0, j)),
            pl.BlockSpec((None, None, FFN_TF, D_MODEL), lambda m, j: (l, which, j, 0)),
            pl.BlockSpec((None, None, 1, D_MODEL), lambda m, j: (l, ln_idx, 0, 0)),
            pl.BlockSpec((None, None, 1, D_MODEL), lambda m, j: (l, ln_idx, 0, 0)),
        ],
        out_specs=pl.BlockSpec((tm, D_MODEL), lambda m, j: (m, 0)),
        out_shape=jax.ShapeDtypeStruct((n, D_MODEL), F32),
        scratch_shapes=[pltpu.VMEM((tm, D_MODEL), F32), pltpu.VMEM((tm, D_MODEL), BF16)],
        compiler_params=_params("parallel", "arbitrary"),
        name="ffn_ln",
    )(x2, w1, w3, w2h, lng, lnb)


def _proj_kernel(x_ref, w_ref, o_ref):
    o_ref[...] = _dot(x_ref[...].astype(BF16), w_ref[...]).astype(o_ref.dtype)


def _proj_layers(x2, w, out_dtype, tm=1024):
    n, k = x2.shape
    depth, _, c = w.shape
    tm = min(tm, n)
    return pl.pallas_call(
        _proj_kernel,
        grid=(depth, n // tm),
        in_specs=[pl.BlockSpec((tm, k), lambda l, m: (m, 0)),
                  pl.BlockSpec((None, k, c), lambda l, m: (l, 0, 0))],
        out_specs=pl.BlockSpec((None, tm, c), lambda l, m: (l, m, 0)),
        out_shape=jax.ShapeDtypeStruct((depth, n, c), out_dtype),
        compiler_params=_params("parallel", "parallel"),
        name="proj",
    )(x2, w)


NSA_QW = NSA_HEADS * LANE
NSA_KW = NSA_KV * LANE
A_Q = 0
A_KV = NSA_QW
A_KV_ORDER = ("kc", "vc", "ks", "vs", "kw", "vw")
A_GATE = A_KV + len(A_KV_ORDER) * LANE
A_COLS = A_GATE + NSA_KW
PROJA_TM = 1024


def _rope_lanes(y, cos, sin_lo, sin_hi, half):
    return (y * cos + pltpu.roll(y, LANE - half, axis=1) * sin_lo
            + pltpu.roll(y, half, axis=1) * sin_hi)


def _proja_kernel(x_ref, w_ref, cos_ref, slo_ref, shi_ref,
                  q_ref, qr_ref, kc_ref, vc_ref, ks_ref, vs_ref, kw_ref, vw_ref, g_ref):
    xb = x_ref[...].astype(BF16)
    cos = cos_ref[...]
    sin_lo = slo_ref[...]
    sin_hi = shi_ref[...]
    half = NSA_ROT // 2
    tm = x_ref.shape[0]

    def mm(c0, width):
        return _dot(xb, w_ref[:, c0:c0 + width])

    qs = NSA_DH ** -0.5 * LOG2E
    y = mm(A_Q, NSA_QW)
    q_ref[...] = (y * qs).astype(BF16)
    for h in range(NSA_HEADS):
        hs = slice(h * LANE, (h + 1) * LANE)
        qr_ref[:, hs] = (_rope_lanes(y[:, hs], cos, sin_lo, sin_hi, half) * qs).astype(BF16)

    lane = lax.broadcasted_iota(jnp.int32, (tm, LANE), 1)
    first = lane < NSA_DH
    tok = pl.program_id(1) * tm + lax.broadcasted_iota(jnp.int32, (tm, LANE), 0)
    blk_flag = jnp.where(lane - SEL_LANE0 == tok // SEL_L, 1.0, 0.0)
    den_flag = jnp.where(lane == DEN_LANE, 1.0, 0.0)

    def packed(table):
        return jnp.where(first, table, pltpu.roll(table, NSA_DH, axis=1))

    def put(ref, y, flag=None):
        for g, yg in enumerate((y, pltpu.roll(y, NSA_DH, axis=1))):
            yg = jnp.where(first, yg, 0.0)
            ref[g] = (yg if flag is None else yg + flag).astype(ref.dtype)

    kv = mm(A_KV, len(A_KV_ORDER) * LANE)
    part = {name: kv[:, i * LANE:(i + 1) * LANE] for i, name in enumerate(A_KV_ORDER)}
    cos_p, lo_p, hi_p = packed(cos), packed(sin_lo), packed(sin_hi)
    put(kc_ref, part["kc"])
    put(vc_ref, part["vc"])
    put(ks_ref, _rope_lanes(part["ks"], cos_p, lo_p, hi_p, half), blk_flag)
    put(vs_ref, part["vs"], den_flag)
    put(kw_ref, _rope_lanes(part["kw"], cos_p, lo_p, hi_p, half))
    put(vw_ref, part["vw"], den_flag)
    g_ref[...] = mm(A_GATE, NSA_KW)


def _proja(x3, wa, l, cos_a, slo_a, shi_a):
    b, s, _ = x3.shape
    tm = PROJA_TM
    kv_shape = jax.ShapeDtypeStruct((b, NSA_KV, s, LANE), BF16)
    cmp_shape = jax.ShapeDtypeStruct((b, NSA_KV, s, LANE), F32)
    kv_spec = pl.BlockSpec((None, NSA_KV, tm, LANE), lambda bi, m: (bi, 0, m, 0))
    return pl.pallas_call(
        _proja_kernel,
        grid=(b, s // tm),
        in_specs=[
            pl.BlockSpec((None, tm, D_MODEL), lambda bi, m: (bi, m, 0)),
            _resident((None, D_MODEL, A_COLS), lambda bi, m: (l, 0, 0)),
            pl.BlockSpec((None, tm, LANE), lambda bi, m: (bi, m, 0)),
            pl.BlockSpec((None, tm, LANE), lambda bi, m: (bi, m, 0)),
            pl.BlockSpec((None, tm, LANE), lambda bi, m: (bi, m, 0)),
        ],
        out_specs=[
            pl.BlockSpec((None, tm, NSA_QW), lambda bi, m: (bi, m, 0)),
            pl.BlockSpec((None, tm, NSA_QW), lambda bi, m: (bi, m, 0)),
            kv_spec, kv_spec, kv_spec, kv_spec, kv_spec, kv_spec,
            pl.BlockSpec((None, tm, NSA_KW), lambda bi, m: (bi, m, 0)),
        ],
        out_shape=[
            jax.ShapeDtypeStruct((b, s, NSA_QW), BF16),
            jax.ShapeDtypeStruct((b, s, NSA_QW), BF16),
            cmp_shape, cmp_shape, kv_shape, kv_shape, kv_shape, kv_shape,
            jax.ShapeDtypeStruct((b, s, NSA_KW), F32),
        ],
        compiler_params=_params("parallel", "parallel"),
        name="nsa_proj",
    )(x3, wa, cos_a, slo_a, shi_a)


N_CMP = 128
CMP_HALF = CMP_D * LANE


def _gelu_tanh(x):
    return 0.5 * x * (1.0 + jnp.tanh(SQRT_2_OVER_PI * (x + 0.044715 * x * x * x)))


def _cmp_kernel(zk_ref, zv_ref, w1_ref, pe_ref, w2_ref, ok_ref, ov_ref):
    for which, (z_ref, o_ref) in enumerate(((zk_ref, ok_ref), (zv_ref, ov_ref))):
        first = jnp.zeros((N_CMP, CMP_HID), F32)
        second = jnp.zeros((N_CMP, CMP_HID), F32)
        for j in range(CMP_D):
            zj = z_ref[pl.ds(j, N_CMP, stride=CMP_D), :].astype(BF16)
            first = first + _dot(zj, w1_ref[which, j * LANE:(j + 1) * LANE, :])
            second = second + _dot(zj, w1_ref[which, CMP_HALF + j * LANE:CMP_HALF + (j + 1) * LANE, :])
        bias = _dot(pe_ref[which], w1_ref[which])[0:1]
        pre = first + pltpu.roll(second, N_CMP - 1, axis=0) + bias
        h = _gelu_tanh(pre)
        o_ref[...] = _dot(h.astype(BF16), w2_ref[which]).astype(BF16)


def _cmp(zk, zv, w1p, pe8, w2p, l):
    b, _, s, _ = zk.shape
    z_spec = pl.BlockSpec((None, None, s, LANE), lambda bi, g: (bi, g, 0, 0))
    o_spec = pl.BlockSpec((None, None, N_CMP, LANE), lambda bi, g: (bi, g, 0, 0))
    o_shape = jax.ShapeDtypeStruct((b, NSA_KV, N_CMP, LANE), BF16)
    return pl.pallas_call(
        _cmp_kernel,
        grid=(b, NSA_KV),
        in_specs=[
            z_spec, z_spec,
            _resident((None, 2, 2 * CMP_HALF, CMP_HID), lambda bi, g: (l, 0, 0, 0)),
            _resident((None, 2, 8, 2 * CMP_HALF), lambda bi, g: (l, 0, 0, 0)),
            _resident((None, 2, CMP_HID, LANE), lambda bi, g: (l, 0, 0, 0)),
        ],
        out_specs=[o_spec, o_spec],
        out_shape=[o_shape, o_shape],
        compiler_params=_params("parallel", "parallel"),
        name="nsa_cmp",
    )(zk, zv, w1p, pe8, w2p)


NSA_TQ = 256
NSA_TK = 512
N_SLC = 32
WIN_SLAB = WINDOW + NSA_TQ


def _nsa_kernel(q_ref, qr_ref, kc_ref, vc_ref, ks_ref, vs_ref, kw_ref, vw_ref, gt_ref, ovl_ref,
                place_ref, o_ref):
    tq, tk, r_heads = NSA_TQ, NSA_TK, NSA_R
    i = pl.program_id(2)
    t0 = i * tq
    tpos = t0 + lax.broadcasted_iota(jnp.int32, (tq, 1), 0)
    lane = lax.broadcasted_iota(jnp.int32, (1, LANE), 1)

    def stack(ref):
        return jnp.concatenate([ref[:, r * LANE:(r + 1) * LANE] for r in range(r_heads)], axis=0)

    gt = jax.nn.sigmoid(gt_ref[...])

    def gate(c):
        return jnp.stack([gt[:, 3 * r + c:3 * r + c + 1] for r in range(r_heads)], axis=0)

    q4w = stack(qr_ref)
    w0 = pl.multiple_of(jnp.maximum(t0 - WINDOW, 0), LANE)
    kw = kw_ref[pl.ds(w0, WIN_SLAB), :]
    vw = vw_ref[pl.ds(w0, WIN_SLAB), :]
    kpos = w0 + lax.broadcasted_iota(jnp.int32, (1, WIN_SLAB), 1)
    w_bias = jnp.where((kpos <= tpos) & (kpos > tpos - WINDOW), 0.0, NEG)
    s = _dot_nt(q4w, kw).reshape(r_heads, tq, WIN_SLAB) + w_bias[None]
    p = jnp.exp2(s - jnp.max(s, axis=-1, keepdims=True))
    o_win = _dot(p.reshape(r_heads * tq, WIN_SLAB).astype(BF16), vw).reshape(r_heads, tq, LANE)
    o_part = gate(2) * (o_win / o_win[:, :, DEN_LANE:DEN_LANE + 1])

    cmask = ((lane * CMP_D + (CMP_L - 1) <= tpos) & (lane < N_CMP - 1))[None]
    s = _dot_nt(stack(q_ref), kc_ref[...]).reshape(r_heads, tq, N_CMP)
    s = jnp.where(cmask, s, NEG)
    e = jnp.where(cmask, jnp.exp2(s - jnp.max(s, axis=-1, keepdims=True)), 0.0)
    den = jnp.sum(e, axis=-1, keepdims=True)
    p = e / jnp.where(den > 0.0, den, 1.0)
    psum = jnp.sum(p, axis=0)
    o_cmp = _dot(p.reshape(r_heads * tq, N_CMP).astype(BF16), vc_ref[...])
    o_part = o_part + gate(0) * o_cmp.reshape(r_heads, tq, LANE)

    imp = lax.dot_general(ovl_ref[...], psum, (((1,), (1,)), ((), ())),
                          precision=lax.Precision.HIGHEST, preferred_element_type=F32)
    blk = lax.broadcasted_iota(jnp.int32, (N_SLC, 1), 0)
    tpos_l = t0 + lax.broadcasted_iota(jnp.int32, (1, tq), 1)
    cur = tpos_l // SEL_L
    valid = blk * SEL_L <= tpos_l
    forced = (blk == 0) | (blk == cur) | (blk == cur - 1)
    score = jnp.where(valid & forced, BIG, jnp.where(valid, imp, -BIG))
    beats = []
    for j in range(N_SLC):
        sj = score[j:j + 1, :]
        beats.append(jnp.where((sj > score) | ((sj == score) & (j < blk)), 1.0, 0.0))
    while len(beats) > 1:
        beats = [a + b for a, b in zip(beats[0::2], beats[1::2])]
    sel_t = jnp.where((beats[0] < N_SEL) & valid, 1.0, 0.0)
    sel_q = lax.dot_general(sel_t, place_ref[...], (((0,), (0,)), ((), ())),
                            preferred_element_type=F32)
    in_flags = jnp.where((lane >= SEL_LANE0) & (lane < SEL_LANE0 + N_SLC), 1.0, 0.0)
    q_bias = (sel_q - in_flags) * (-NEG)
    q4 = (q4w.astype(F32).reshape(r_heads, tq, LANE) + q_bias[None]).astype(BF16)
    q4 = q4.reshape(r_heads * tq, LANE)

    def sel_step(c, carry, causal):
        m, acc = carry
        k0 = pl.multiple_of(c * tk, tk)
        k = ks_ref[pl.ds(k0, tk), :]
        v = vs_ref[pl.ds(k0, tk), :]
        s = _dot_nt(q4, k).reshape(r_heads, tq, tk)
        if causal:
            kpos = k0 + lax.broadcasted_iota(jnp.int32, (1, tk), 1)
            s = jnp.where((kpos <= tpos)[None], s, NEG)
        m_new = jnp.maximum(m, jnp.max(s, axis=-1, keepdims=True))
        p = jnp.exp2(s - m_new)
        pv = _dot(p.reshape(r_heads * tq, tk).astype(BF16), v).reshape(r_heads, tq, LANE)
        return m_new, jnp.exp2(m - m_new) * acc + pv

    last = (t0 + tq - 1) // tk
    carry = (jnp.full((r_heads, tq, 1), NEG, F32), jnp.zeros((r_heads, tq, LANE), F32))
    carry = lax.fori_loop(0, last, functools.partial(sel_step, causal=False), carry)
    _, acc_s = sel_step(last, carry, True)
    o = o_part + gate(1) * (acc_s / acc_s[:, :, DEN_LANE:DEN_LANE + 1])
    for r in range(0, r_heads, 2):
        o_ref[:, (r // 2) * LANE:(r // 2 + 1) * LANE] = _pack_pair(o[r], o[r + 1]).astype(BF16)


def _nsa_attn(q, qr, kcmp, vcmp, ks, vs, kw, vw, gates, ovl, place):
    b, s, _ = q.shape
    tq = NSA_TQ
    gw = NSA_R * LANE
    q_spec = pl.BlockSpec((None, tq, gw), lambda bi, g, i: (bi, i, g))
    c_spec = pl.BlockSpec((None, None, N_CMP, LANE), lambda bi, g, i: (bi, g, 0, 0))
    kv_spec = pl.BlockSpec((None, None, s, LANE), lambda bi, g, i: (bi, g, 0, 0))
    return pl.pallas_call(
        _nsa_kernel,
        grid=(b, NSA_KV, s // tq),
        in_specs=[q_spec, q_spec, c_spec, c_spec, kv_spec, kv_spec, kv_spec, kv_spec,
                  pl.BlockSpec((None, tq, LANE), lambda bi, g, i: (bi, i, g)),
                  _resident((N_SLC, N_CMP), lambda bi, g, i: (0, 0)),
                  _resident((N_SLC, LANE), lambda bi, g, i: (0, 0))],
        out_specs=pl.BlockSpec((None, tq, gw // 2), lambda bi, g, i: (bi, i, g)),
        out_shape=jax.ShapeDtypeStruct((b, s, MIX_W), BF16),
        compiler_params=_params("parallel", "parallel", "arbitrary"),
        name="nsa_attn",
    )(q, qr, kcmp, vcmp, ks, vs, kw, vw, gates, ovl, place)


def _hgrn_kernel(x_ref, wq_ref, wf_ref, wi_ref, wg_ref, lb_ref, ng_ref, o_ref, st_ref):
    w_head = jnp.concatenate([wq_ref[...], wf_ref[...], wi_ref[...], wg_ref[...]], axis=1)
    s_len = x_ref.shape[0]
    c_len, sub_len = HGRN_CHUNK, HGRN_SUB
    p_len = s_len // HGRN_PARTS
    nc = p_len // c_len
    lb = lb_ref[...]
    t = lax.broadcasted_iota(jnp.int32, (p_len, 1), 0)
    t_sub = t % sub_len
    row = lax.broadcasted_iota(jnp.int32, (nc, c_len, c_len), 1)
    col = lax.broadcasted_iota(jnp.int32, (nc, c_len, c_len), 2)
    tri = jnp.where(col <= row, 1.0, 0.0).astype(BF16)

    local = []
    for part in range(HGRN_PARTS):
        proj = _dot(x_ref[pl.ds(part * p_len, p_len), :].astype(BF16), w_head)
        q = proj[:, 0:HK]
        z = proj[:, HK:2 * HK]
        v = proj[:, 2 * HK:2 * HK + HV]
        go = proj[:, 2 * HK + HV:2 * HK + 2 * HV]

        sig = jax.nn.sigmoid(z)
        f = lb + (1.0 - lb) * sig
        lf = jnp.log(jnp.maximum(f, F_MIN))
        k = (1.0 - lb) * (1.0 - sig)

        lf3 = lf.reshape(nc, c_len, HK)
        hi = lf3.astype(BF16)
        rest = lf3 - hi.astype(F32)
        mid = rest.astype(BF16)
        low = (rest - mid.astype(F32)).astype(BF16)
        sums = jnp.einsum('cts,csd->ctd', tri, jnp.concatenate([hi, mid, low], axis=-1),
                          preferred_element_type=F32)
        b3 = sums[..., 0:HK] + sums[..., HK:2 * HK] + sums[..., 2 * HK:3 * HK]
        b = b3.reshape(p_len, HK)
        k3 = k.reshape(nc, c_len, HK)
        vb3 = v.astype(BF16).reshape(nc, c_len, HV)

        a_intra = None
        h = c_len // 2
        while h >= sub_len:
            blk = b.reshape(p_len // (2 * h), 2 * h, HK)
            e = jnp.exp(-jnp.abs(blk[:, h - 1:h, :] - blk)).reshape(p_len, HK)
            upper = (t // h) % 2 == 1
            lq = jnp.where(upper, q * e, 0.0).astype(BF16).reshape(nc, c_len, HK)
            rk = jnp.where(upper, 0.0, k * e).astype(BF16).reshape(nc, c_len, HK)
            a = jnp.einsum('ctk,csk->cts', lq, rk, preferred_element_type=F32)
            if 2 * h < c_len:
                a = jnp.where(row // (2 * h) == col // (2 * h), a, 0.0)
            a_intra = a if a_intra is None else a_intra + a
            h //= 2
        o = jnp.einsum('cts,csd->ctd', a_intra.astype(BF16), vb3,
                       preferred_element_type=F32).reshape(p_len, HV)

        o = o + jnp.sum(q * k, axis=-1, keepdims=True) * v

        def back(x, d):
            x3 = x.reshape(p_len // SUBLANES, SUBLANES, x.shape[-1])
            return pltpu.roll(x3, d, axis=1).reshape(x.shape)

        for d in range(1, sub_len):
            e = jnp.exp(jnp.where(t_sub >= d, b - back(b, d), NEG))
            w = jnp.sum(q * back(k, d) * e, axis=-1, keepdims=True)
            o = o + w * back(v, d)

        b_last = b3[:, c_len - 1:c_len, :]
        kd3 = (k3 * jnp.exp(b_last - b3)).astype(BF16)
        upd = jnp.einsum('csv,csk->cvk', vb3, kd3, preferred_element_type=F32)
        qe3 = (q * jnp.exp(b)).astype(BF16).reshape(nc, c_len, HK)
        local.append((o, go, qe3, upd, jnp.exp(b_last)))

    state = jnp.zeros((HV, HK), F32)
    for part, (_, _, _, upd, dec) in enumerate(local):
        for c in range(nc):
            st_ref[part * nc + c] = state.astype(BF16)
            state = state * dec[c] + upd[c]

    for part, (o, go, qe3, _, _) in enumerate(local):
        o = o + jnp.einsum('ctk,cvk->ctv', qe3, st_ref[pl.ds(part * nc, nc)],
                           preferred_element_type=F32).reshape(p_len, HV)
        o = o * lax.rsqrt(jnp.mean(o * o, axis=-1, keepdims=True) + RMS_EPS) * ng_ref[...]
        o_ref[pl.ds(part * p_len, p_len), :] = (o * (go * jax.nn.sigmoid(go))).astype(BF16)


def _hgrn(x3, wb, lb, ng, l):
    b, s, d = x3.shape

    def cols(p):
        return pl.BlockSpec((None, d, HK), lambda bi, h: (l, 0, p * HB + h))

    return pl.pallas_call(
        _hgrn_kernel,
        grid=(b, HB),
        in_specs=[pl.BlockSpec((None, s, d), lambda bi, h: (bi, 0, 0)),
                  cols(0), cols(1), cols(2), cols(3),
                  pl.BlockSpec((None, None, 1, HK), lambda bi, h: (l, h, 0, 0)),
                  pl.BlockSpec((None, 1, HV), lambda bi, h: (l, 0, 0))],
        out_specs=pl.BlockSpec((None, s, HV), lambda bi, h: (bi, 0, h)),
        out_shape=jax.ShapeDtypeStruct((b, s, HB * HV), BF16),
        scratch_shapes=[pltpu.VMEM((s // HGRN_CHUNK, HV, HK), BF16)],
        compiler_params=_params("parallel", "arbitrary"),
        name="hgrn",
    )(x3, wb, wb, wb, wb, lb, ng)


MLA_W = HC * LANE
C_CQ, C_CKV, C_KR = 0, Q_RANK, Q_RANK + KV_RANK
C_COLS = C_KR + LANE
MLAP_TM = 1024


def _rms(x, g):
    return x * lax.rsqrt(jnp.mean(x * x, axis=-1, keepdims=True) + RMS_EPS) * g


def _mlap_kernel(x_ref, wc_ref, qg_ref, kvg_ref, wuq_ref, wuk_ref, wuv_ref,
                 cos_ref, slo_ref, shi_ref, q_ref, k_ref, v_ref):
    half = ROPE_D // 2
    qs = (NOPE + ROPE_D) ** -0.5 * LOG2E
    den_flag = jnp.where(lax.broadcasted_iota(jnp.int32, (1, LANE), 1) == DEN_LANE, 1.0, 0.0)
    cos, lo, hi = cos_ref[...], slo_ref[...], shi_ref[...]
    c = _dot(x_ref[...].astype(BF16), wc_ref[...])
    nq = _rms(c[:, C_CQ:C_CKV], qg_ref[...]).astype(BF16)
    nkv = _rms(c[:, C_CKV:C_KR], kvg_ref[...]).astype(BF16)
    k_pe = _rope_lanes(c[:, C_KR:C_COLS], cos, lo, hi, half)
    yq = _dot(nq, wuq_ref[...])
    yk = _dot(nkv, wuk_ref[...])
    yv = _dot(nkv, wuv_ref[...])
    cos_q, lo_q, hi_q = cos * qs, lo * qs, hi * qs
    for h in range(HC):
        hs = slice(h * LANE, (h + 1) * LANE)
        q_ref[:, hs] = _rope_lanes(yq[:, hs], cos_q, lo_q, hi_q, half).astype(BF16)
        k_ref[:, hs] = (yk[:, hs] + k_pe).astype(BF16)
        v_ref[:, hs] = (yv[:, hs] + den_flag).astype(BF16)


def _mlap(x3, wc, qg, kvg, wuq, wuk, wuv, l, cos_c, slo_c, shi_c):
    b, s, _ = x3.shape
    tm = MLAP_TM
    o_spec = pl.BlockSpec((None, tm, MLA_W), lambda bi, m: (bi, m, 0))
    o_shape = jax.ShapeDtypeStruct((b, s, MLA_W), BF16)
    t_spec = pl.BlockSpec((None, tm, LANE), lambda bi, m: (bi, m, 0))
    return pl.pallas_call(
        _mlap_kernel,
        grid=(b, s // tm),
        in_specs=[
            pl.BlockSpec((None, tm, D_MODEL), lambda bi, m: (bi, m, 0)),
            _resident((None, D_MODEL, C_COLS), lambda bi, m: (l, 0, 0)),
            _resident((None, 1, Q_RANK), lambda bi, m: (l, 0, 0)),
            _resident((None, 1, KV_RANK), lambda bi, m: (l, 0, 0)),
            _resident((None, Q_RANK, MLA_W), lambda bi, m: (l, 0, 0)),
            _resident((None, KV_RANK, MLA_W), lambda bi, m: (l, 0, 0)),
            _resident((None, KV_RANK, MLA_W), lambda bi, m: (l, 0, 0)),
            t_spec, t_spec, t_spec,
        ],
        out_specs=[o_spec, o_spec, o_spec],
        out_shape=[o_shape, o_shape, o_shape],
        compiler_params=_params("parallel", "parallel"),
        name="mla_proj",
    )(x3, wc, qg, kvg, wuq, wuk, wuv, cos_c, slo_c, shi_c)


MLA_TQ = 512
MLA_TK = MLA_TQ


MLA_HPS = 4


def _mla_attn_kernel(q_ref, k_ref, v_ref, o_ref):
    tq, tk = MLA_TQ, MLA_TK
    i = pl.program_id(2)
    t0 = i * tq
    heads = [slice(h * LANE, (h + 1) * LANE) for h in range(MLA_HPS)]
    qs = [q_ref[:, hs] for hs in heads]

    def attend(k0, streams, causal):
        scores = [_dot_nt(q, k_ref[pl.ds(k0, nk), hs]) for q, hs, _, _, nk, _ in streams]
        m_new, probs = [], []
        for s, (q, _, m, _, nk, r0) in zip(scores, streams):
            if causal:
                kpos = k0 + lax.broadcasted_iota(jnp.int32, (1, nk), 1)
                qpos = t0 + r0 + lax.broadcasted_iota(jnp.int32, (q.shape[0], 1), 0)
                s = jnp.where(kpos <= qpos, s, NEG)
            m_new.append(jnp.maximum(m, jnp.max(s, axis=-1, keepdims=True)))
            probs.append(jnp.exp2(s - m_new[-1]).astype(BF16))
        pvs = [_dot(p, v_ref[pl.ds(k0, nk), hs]) for p, (_, hs, _, _, nk, _) in zip(probs, streams)]
        return [(mn, jnp.exp2(m - mn) * acc + pv)
                for mn, pv, (_, _, m, acc, _, _) in zip(m_new, pvs, streams)]

    def full_chunk(c, carry):
        streams = [(qs[h], hs, *carry[h], tk, 0) for h, hs in enumerate(heads)]
        return tuple(attend(pl.multiple_of(c * tk, tk), streams, False))

    carry = tuple((jnp.full((tq, 1), NEG, F32), jnp.zeros((tq, LANE), F32)) for _ in heads)
    carry = lax.fori_loop(0, i, full_chunk, carry)
    hq = tq // 2
    streams = [(qs[h][r0:r0 + hq], hs, carry[h][0][r0:r0 + hq], carry[h][1][r0:r0 + hq], r0 + hq, r0)
               for h, hs in enumerate(heads) for r0 in (0, hq)]
    done = attend(pl.multiple_of(t0, tq), streams, True)
    outs = []
    for h in range(len(heads)):
        acc = jnp.concatenate([done[2 * h][1], done[2 * h + 1][1]], axis=0)
        outs.append(acc / acc[:, DEN_LANE:DEN_LANE + 1])
    for h in range(0, MLA_HPS, 2):
        o_ref[:, (h // 2) * LANE:(h // 2 + 1) * LANE] = _pack_pair(outs[h], outs[h + 1]).astype(BF16)


def _mla_attn(q, k, v):
    b, s, _ = q.shape
    tq = MLA_TQ
    gw = MLA_HPS * LANE
    return pl.pallas_call(
        _mla_attn_kernel,
        grid=(b, HC // MLA_HPS, s // tq),
        in_specs=[pl.BlockSpec((None, tq, gw), lambda bi, h, i: (bi, i, h)),
                  pl.BlockSpec((None, s, gw), lambda bi, h, i: (bi, 0, h)),
                  pl.BlockSpec((None, s, gw), lambda bi, h, i: (bi, 0, h))],
        out_specs=pl.BlockSpec((None, tq, gw // 2), lambda bi, h, i: (bi, i, h)),
        out_shape=jax.ShapeDtypeStruct((b, s, MIX_W), BF16),
        compiler_params=_params("parallel", "parallel", "arbitrary"),
        name="mla_attn",
    )(q, k, v)


MERGE_TM = 1024


def _merge_kernel(x_ref, ya_ref, yb_ref, yc_ref, wm_ref, wa_ref, wb_ref, wc_ref, wo_ref,
                  g_ref, b_ref, o_ref, *, alpha):
    x = x_ref[...]
    xb = x.astype(BF16)
    mixed = jnp.zeros(x.shape, F32)
    for idx, (y_ref, w_ref) in enumerate(((ya_ref, wa_ref), (yb_ref, wb_ref), (yc_ref, wc_ref))):
        gate = jax.nn.sigmoid(_dot(xb, wm_ref[:, idx * D_MODEL:(idx + 1) * D_MODEL]))
        mixed = mixed + gate * _dot(y_ref[...], w_ref[...])
    y = alpha * x + _dot(mixed.astype(BF16), wo_ref[...])
    o_ref[...] = _ln(y, g_ref[...], b_ref[...])


def _merge(x2, ya, yb, yc, wm, wa, wb, wc, wo, lng, lnb, l, alpha):
    n = x2.shape[0]
    tm = min(MERGE_TM, n)

    def rows(width):
        return pl.BlockSpec((tm, width), lambda m: (m, 0))

    return pl.pallas_call(
        functools.partial(_merge_kernel, alpha=alpha),
        grid=(n // tm,),
        in_specs=[rows(D_MODEL), rows(MIX_W), rows(MIX_W), rows(MIX_W),
                  _resident((None, D_MODEL, 3 * D_MODEL), lambda m: (l, 0, 0)),
                  _resident((None, None, MIX_W, D_MODEL), lambda m: (l, 0, 0, 0)),
                  _resident((None, None, MIX_W, D_MODEL), lambda m: (l, 1, 0, 0)),
                  _resident((None, None, MIX_W, D_MODEL), lambda m: (l, 2, 0, 0)),
                  _resident((None, D_MODEL, D_MODEL), lambda m: (l, 0, 0)),
                  _resident((None, None, 1, D_MODEL), lambda m: (l, 1, 0, 0)),
                  _resident((None, None, 1, D_MODEL), lambda m: (l, 1, 0, 0))],
        out_specs=rows(D_MODEL),
        out_shape=jax.ShapeDtypeStruct((n, D_MODEL), F32),
        compiler_params=_params("parallel"),
        name="merge",
    )(x2, ya, yb, yc, wm, wa, wb, wc, wo, lng, lnb)


XA_TM = 1024


def _xattn_kernel(x_ref, wq_ref, k_ref, v_ref, wo_ref, g_ref, b_ref, o_ref, *, alpha):
    x = x_ref[...]
    xb = x.astype(BF16)
    q = (_dot(xb, wq_ref[...]) * (XA_DH ** -0.5 * LOG2E)).astype(BF16)
    ones = jnp.ones((k_ref.shape[0], LANE), BF16)
    heads = []
    for h in range(XA_HEADS):
        hs = slice(h * XA_DH, (h + 1) * XA_DH)
        s = _dot_nt(q[:, hs], k_ref[:, hs])
        e = jnp.exp2(s - jnp.max(s, axis=-1, keepdims=True)).astype(BF16)
        den = _dot(e, ones)
        den = jnp.concatenate([den] * (XA_DH // LANE), axis=1)
        heads.append((_dot(e, v_ref[:, hs]) / den).astype(BF16))
    att = jnp.concatenate(heads, axis=1)
    y = alpha * x + _dot(att, wo_ref[...])
    o_ref[...] = _ln(y, g_ref[...], b_ref[...])


def _xattn(x3, wq, kv, wo, lng, lnb, l, alpha):
    b, s, _ = x3.shape
    m_len = kv.shape[2]
    tm = XA_TM
    return pl.pallas_call(
        functools.partial(_xattn_kernel, alpha=alpha),
        grid=(b, s // tm),
        in_specs=[pl.BlockSpec((None, tm, D_MODEL), lambda bi, m: (bi, m, 0)),
                  _resident((None, D_MODEL, D_MODEL), lambda bi, m: (l, 0, 0)),
                  pl.BlockSpec((None, None, m_len, D_MODEL), lambda bi, m: (l, bi, 0, 0)),
                  pl.BlockSpec((None, None, m_len, D_MODEL), lambda bi, m: (l, bi, 0, 1)),
                  _resident((None, D_MODEL, D_MODEL), lambda bi, m: (l, 0, 0)),
                  _resident((None, None, 1, D_MODEL), lambda bi, m: (l, 2, 0, 0)),
                  _resident((None, None, 1, D_MODEL), lambda bi, m: (l, 2, 0, 0))],
        out_specs=pl.BlockSpec((None, tm, D_MODEL), lambda bi, m: (bi, m, 0)),
        out_shape=jax.ShapeDtypeStruct((b, s, D_MODEL), F32),
        compiler_params=_params("parallel", "parallel"),
        name="xattn",
    )(x3, wq, kv, kv, wo, lng, lnb)


def _pad_heads(w, n_heads, dh):
    depth, k, _ = w.shape
    w = w.reshape(depth, k, n_heads, dh)
    return jnp.pad(w, ((0, 0), (0, 0), (0, 0), (0, LANE - dh))).reshape(depth, k, n_heads * LANE)


def _rope_tables(positions, rot, off):
    half = rot // 2
    inv = ROPE_THETA ** (-jnp.arange(half, dtype=F32) / half)
    ang = positions.astype(F32)[..., None] * inv
    cos, sin = jnp.cos(ang), jnp.sin(ang)
    shape = positions.shape
    zeros = lambda n: jnp.zeros(shape + (n,), F32)
    cos_t = jnp.concatenate([jnp.ones(shape + (off,), F32), cos, cos,
                             jnp.ones(shape + (LANE - off - rot,), F32)], axis=-1)
    sin_lo = jnp.concatenate([zeros(off), -sin, zeros(LANE - off - half)], axis=-1)
    sin_hi = jnp.concatenate([zeros(off + half), sin, zeros(LANE - off - rot)], axis=-1)
    return cos_t, sin_lo, sin_hi


def _overlap_matrix(s_len):
    n_cmp = (s_len - CMP_L) // CMP_D + 1
    n_slc = s_len // SEL_L
    start = np.arange(n_cmp) * CMP_D
    j = np.arange(n_slc)
    ov = np.clip(np.minimum(start[:, None] + CMP_L, (j[None, :] + 1) * SEL_L)
                 - np.maximum(start[:, None], j[None, :] * SEL_L), 0, None) / CMP_L
    out = np.zeros((N_SLC, N_CMP), np.float32)
    out[:n_slc, :n_cmp] = ov.T
    place = np.zeros((N_SLC, LANE), np.float32)
    place[np.arange(N_SLC), SEL_LANE0 + np.arange(N_SLC)] = 1.0
    return jnp.asarray(out), jnp.asarray(place)


def _stacked_weights(w_in, nsa_cmp_pos, nsa_cmp_w1, nsa_cmp_w2, mla_w_uq, mla_w_ukv):
    depth = w_in.shape[0]
    w_in = w_in.astype(BF16)
    part = [w_in[:, :, IN_OFF[i]:IN_OFF[i + 1]] for i in range(len(IN_SIZES))]
    (a_q, a_kc, a_vc, a_ks, a_vs, a_kw, a_vw, a_gate, b_q, b_f, b_i, b_g,
     c_q, c_kv, c_kr, merge) = part
    gate = _pad_heads(a_gate, NSA_KV, NSA_R * 3)
    kv_parts = dict(kc=a_kc, vc=a_vc, ks=a_ks, vs=a_vs, kw=a_kw, vw=a_vw)
    wa = jnp.concatenate([_pad_heads(a_q, NSA_HEADS, NSA_DH)]
                         + [kv_parts[name] for name in A_KV_ORDER] + [gate], axis=2).astype(BF16)
    wb = jnp.concatenate([b_q, b_f, b_i, b_g], axis=2).astype(BF16)

    kr_pad = jnp.pad(c_kr, ((0, 0), (0, 0), (NOPE, LANE - NOPE - ROPE_D)))
    wc = jnp.concatenate([c_q, c_kv, kr_pad], axis=2).astype(BF16)
    wuq = _pad_heads(mla_w_uq, HC, NOPE + ROPE_D).astype(BF16)
    ukv = mla_w_ukv.reshape(depth, KV_RANK, HC, NOPE + VD)
    wuk = _pad_heads(ukv[..., :NOPE].reshape(depth, KV_RANK, HC * NOPE), HC, NOPE).astype(BF16)
    wuv = _pad_heads(ukv[..., NOPE:].reshape(depth, KV_RANK, HC * VD), HC, VD).astype(BF16)

    w1 = nsa_cmp_w1.reshape(depth, 2, CMP_L, NSA_DH, CMP_HID)
    w1p = jnp.pad(w1, ((0, 0), (0, 0), (0, 0), (0, LANE - NSA_DH), (0, 0)))
    w1p = w1p.reshape(depth, 2, CMP_L * LANE, CMP_HID)
    pe = jnp.pad(nsa_cmp_pos, ((0, 0), (0, 0), (0, 0), (0, LANE - NSA_DH)))
    pe8 = jnp.broadcast_to(pe.reshape(depth, 2, 1, CMP_L * LANE), (depth, 2, 8, CMP_L * LANE))
    w2p = jnp.pad(nsa_cmp_w2, ((0, 0), (0, 0), (0, 0), (0, LANE - NSA_DH)))
    return dict(wa=wa, wb=wb, wc=wc, wuq=wuq, wuk=wuk, wuv=wuv,
                w1p=w1p.astype(BF16), pe8=pe8.astype(BF16), w2p=w2p.astype(BF16),
                wm=merge.astype(BF16))


def kernel(x, mem, positions, ln_g, ln_b, ffn_w1, ffn_w3, ffn_w2, w_in, nsa_cmp_pos,
           nsa_cmp_w1, nsa_cmp_w2, hgrn_lb_logits, hgrn_norm_g, mla_q_norm_g, mla_w_uq,
           mla_kv_norm_g, mla_w_ukv, w_branch, w_out, xa_wq, xa_wk, xa_wv, xa_wo):
    b, s, d = x.shape
    depth = ln_g.shape[0]
    assert d == D_MODEL and s // SEL_L == N_SLC and (s - CMP_L) // CMP_D + 2 == N_CMP
    assert s % max(NSA_TQ, NSA_TK, MLA_TQ, PROJA_TM, MLAP_TM, XA_TM) == 0 and NSA_TK % NSA_TQ == 0
    n = b * s
    alpha = (2.0 * depth) ** 0.25

    lng = ln_g.reshape(depth, 4, 1, d)
    lnb = ln_b.reshape(depth, 4, 1, d)
    w1 = ffn_w1.astype(BF16)
    w3 = ffn_w3.astype(BF16)
    w2 = (0.5 * ffn_w2).astype(BF16)
    wo = w_out.astype(BF16)
    wbr = w_branch.astype(BF16)
    xq = xa_wq.astype(BF16)
    xkv = jnp.concatenate([xa_wk, xa_wv], axis=2).astype(BF16)
    xo = xa_wo.astype(BF16)
    p_lb = jax.nn.softmax(hgrn_lb_logits.astype(F32), axis=0)
    lower = (jnp.cumsum(p_lb, axis=0) - p_lb[0:1]).reshape(depth, HB, 1, HK)

    rope_a = _rope_tables(positions, NSA_ROT, 0)
    rope_c = _rope_tables(positions, ROPE_D, NOPE)
    ovl, place = _overlap_matrix(s)
    mem2 = mem.reshape(b * mem.shape[1], d)
    kv_all = _proj_layers(mem2, xkv, BF16).reshape(depth, b, mem.shape[1], 2 * d)
    w = _stacked_weights(w_in, nsa_cmp_pos, nsa_cmp_w1, nsa_cmp_w2, mla_w_uq, mla_w_ukv)
    norm_b = hgrn_norm_g.reshape(depth, 1, HV)
    norm_q = mla_q_norm_g.reshape(depth, 1, Q_RANK)
    norm_kv = mla_kv_norm_g.reshape(depth, 1, KV_RANK)

    x2 = x.reshape(n, d)
    for l in range(depth):
        x2 = _ffn_ln(x2, w1, w3, w2, lng, lnb, l, 0, 0, alpha)
        x3 = x2.reshape(b, s, d)

        q, qr, kc, vc, ks, vs, kw, vw, gates = _proja(x3, w["wa"], l, *rope_a)
        kcmp, vcmp = _cmp(kc, vc, w["w1p"], w["pe8"], w["w2p"], l)
        ya = _nsa_attn(q, qr, kcmp, vcmp, ks, vs, kw, vw, gates, ovl, place)

        yb = _hgrn(x3, w["wb"], lower, norm_b, l)

        mq, mk, mv = _mlap(x3, w["wc"], norm_q, norm_kv, w["wuq"], w["wuk"], w["wuv"], l, *rope_c)
        yc = _mla_attn(mq, mk, mv)

        x2 = _merge(x2, ya.reshape(n, MIX_W), yb.reshape(n, MIX_W), yc.reshape(n, MIX_W),
                    w["wm"], wbr, wbr, wbr, wo, lng, lnb, l, alpha)

        x2 = _xattn(x2.reshape(b, s, d), xq, kv_all, xo, lng, lnb, l, alpha).reshape(n, d)

        x2 = _ffn_ln(x2, w1, w3, w2, lng, lnb, l, 1, 3, alpha)
    return x2.reshape(b, s, d)
```
